```python
import math
import jax, jax.numpy as jnp
from jax import lax
import numpy as np

D_MODEL = 2048
BATCH = 4
SEQ = 4096
DEPTH = 2

MEM_LEN = 256
HEAD_DIM = 128
A_HEADS = 8
A_PATTERNS = ((128, 1), (512, 4), (2048, 16))
A_BLOCK = 128
N_BUCKETS = 32
MAX_DISTANCE = 2048
B_HEADS = 4
B_DK = 512
B_DV = 1024
B_GATE_RANK = 16
B_GATE_TAU = 16.0
B_CHUNK = 64
C_HEADS = 8
C_Q_RANK = 512
C_KV_RANK = 512
C_NOPE = 128
C_ROPE = 64
C_V = 128
ROPE_THETA = 10000.0
C_BLOCK = 128
X_HEADS = 4
X_HEAD_DIM = D_MODEL // X_HEADS
D_FF = -(-8 * D_MODEL // (3 * 256)) * 256
N_BRANCH = 3
IN_SIZES = (A_HEADS * HEAD_DIM, A_HEADS * HEAD_DIM, A_HEADS * HEAD_DIM,
            B_DK, B_DK, B_DV, B_GATE_RANK, B_DV,
            C_Q_RANK, C_KV_RANK, C_ROPE,
            N_BRANCH * D_MODEL)
D_IN = sum(IN_SIZES)
F32 = jnp.float32
NEG = -1e30
EPS = 1e-6

kernel_name = 'hybrid_gated_dilated_gla_mla_block'


def rmsnorm(x, g):
    xf = x.astype(F32)
    y = xf * lax.rsqrt(jnp.mean(xf * xf, axis=-1, keepdims=True) + EPS)
    return (y * g.astype(F32)).astype(x.dtype)


def split_columns(z):
    idx, acc = [], 0
    for size in IN_SIZES[:-1]:
        acc += size
        idx.append(acc)
    return jnp.split(z, idx, axis=-1)


def t5_bucket(dist):
    max_exact = N_BUCKETS // 2
    n = jnp.maximum(dist, 1).astype(F32)
    large = max_exact + (jnp.log(n / max_exact) / math.log(MAX_DISTANCE / max_exact)
                         * (N_BUCKETS - max_exact)).astype(jnp.int32)
    large = jnp.minimum(large, N_BUCKETS - 1)
    return jnp.where(dist < max_exact, dist, large)


def dilated_attention(q, k, v, rel_bias):
    b, s, h, d = q.shape
    scale = d ** -0.5
    qi = jnp.arange(A_BLOCK)[:, None]
    kj = jnp.arange(2 * A_BLOCK)[None, :]
    steps_back = qi + A_BLOCK - kj
    outs, lses = [], []
    for window, dil in A_PATTERNS:
        n_steps = window // dil
        L = s // dil
        nb = -(-L // A_BLOCK)
        Lp = nb * A_BLOCK

        def to_blocks(t):
            t = t.reshape(b, L, dil, h, d).transpose(0, 2, 1, 3, 4)
            t = jnp.pad(t, ((0, 0), (0, 0), (0, Lp - L), (0, 0), (0, 0)))
            return t.reshape(b, dil, nb, A_BLOCK, h, d)

        def with_prev(t):
            prev = jnp.pad(t[:, :, :-1], ((0, 0), (0, 0), (1, 0), (0, 0), (0, 0), (0, 0)))
            return jnp.concatenate([prev, t], axis=3)

        qb = to_blocks(q)
        kb = with_prev(to_blocks(k))
        vb = with_prev(to_blocks(v))
        key_idx = jnp.arange(nb)[:, None, None] * A_BLOCK + kj[None] - A_BLOCK
        valid = (steps_back >= 0) & (steps_back <= n_steps) & (key_idx >= 0)
        dist = jnp.clip(steps_back, 0, n_steps) * dil
        bias = rel_bias[t5_bucket(dist)].astype(F32).transpose(2, 0, 1)
        logits = jnp.einsum('brnqhd,brnkhd->brnhqk', qb, kb).astype(F32) * scale + bias
        logits = jnp.where(valid[None, None, :, None], logits, NEG)
        lse = jax.nn.logsumexp(logits, axis=-1)
        p = jnp.exp(logits - lse[..., None]).astype(v.dtype)
        o = jnp.einsum('brnhqk,brnkhd->brnqhd', p, vb)
        o = o.reshape(b, dil, Lp, h, d)[:, :, :L].transpose(0, 2, 1, 3, 4).reshape(b, s, h, d)
        lse = lse.transpose(0, 1, 2, 4, 3).reshape(b, dil, Lp, h)[:, :, :L]
        lse = lse.transpose(0, 2, 1, 3).reshape(b, s, h)
        outs.append(o)
        lses.append(lse)
    wts = jax.nn.softmax(jnp.stack(lses, axis=0), axis=0)
    out = jnp.sum(wts[..., None] * jnp.stack(outs, axis=0).astype(F32), axis=0)
    return out.astype(q.dtype)


def gated_linear_attention(q, k, v, log_a, r, norm_g):
    b, s, h, dk = q.shape
    dv = v.shape[-1]
    c = B_CHUNK
    n = s // c
    qc = (q.astype(F32) * dk ** -0.5).reshape(b, n, c, h, dk)
    kc = k.astype(F32).reshape(b, n, c, h, dk)
    vc = v.astype(F32).reshape(b, n, c, h, dv)
    cum = jnp.cumsum(log_a.astype(F32).reshape(b, n, c, h, dk), axis=2)
    cum_last = cum[:, :, -1:]
    q_dec = qc * jnp.exp(cum)
    k_inv = kc * jnp.exp(-cum)
    k_state = kc * jnp.exp(cum_last - cum)
    causal = jnp.tril(jnp.ones((c, c), dtype=bool))
    att = jnp.where(causal, jnp.einsum('bnihk,bnjhk->bnhij', q_dec, k_inv), 0.0)
    o_intra = jnp.einsum('bnhij,bnjhv->bnihv', att, vc)
    d_state = jnp.einsum('bnjhk,bnjhv->nbhkv', k_state, vc)
    decay = jnp.exp(cum_last[:, :, 0]).transpose(1, 0, 2, 3)

    def step(state, inp):
        dec, ds = inp
        return dec[..., None] * state + ds, state

    _, prev_states = lax.scan(step, jnp.zeros((b, h, dk, dv), F32), (decay, d_state))
    o_inter = jnp.einsum('bnihk,nbhkv->bnihv', q_dec, prev_states)
    o = (o_intra + o_inter).reshape(b, s, h, dv)
    o = rmsnorm(o, norm_g) * jax.nn.silu(r.astype(F32))
    return o.reshape(b, s, h * dv).astype(v.dtype)


def rope_tables(s):
    pos = jnp.arange(s, dtype=F32)
    inv = ROPE_THETA ** (-jnp.arange(0, C_ROPE, 2, dtype=F32) / C_ROPE)
    ang = pos[:, None] * inv[None, :]
    return jnp.cos(ang), jnp.sin(ang)


def apply_rope(x, cos, sin):
    half = x.shape[-1] // 2
    x1 = x[..., :half].astype(F32)
    x2 = x[..., half:].astype(F32)
    return jnp.concatenate([x1 * cos - x2 * sin, x1 * sin + x2 * cos], axis=-1).astype(x.dtype)


def latent_attention(c_qa, c_kva, c_kr, q_a_norm, w_qb, kv_a_norm, w_kvb):
    b, s, _ = c_qa.shape
    q = (rmsnorm(c_qa, q_a_norm) @ w_qb).reshape(b, s, C_HEADS, C_NOPE + C_ROPE)
    kv = (rmsnorm(c_kva, kv_a_norm) @ w_kvb).reshape(b, s, C_HEADS, C_NOPE + C_V)
    cos, sin = rope_tables(s)
    q = jnp.concatenate([q[..., :C_NOPE], apply_rope(q[..., C_NOPE:], cos[:, None], sin[:, None])], axis=-1)
    k_rope = apply_rope(c_kr, cos, sin)
    k = jnp.concatenate([kv[..., :C_NOPE],
                         jnp.broadcast_to(k_rope[:, :, None], (b, s, C_HEADS, C_ROPE))], axis=-1)
    v = kv[..., C_NOPE:]
    scale = (C_NOPE + C_ROPE) ** -0.5
    nb = s // C_BLOCK
    q_blocks = q.reshape(b, nb, C_BLOCK, C_HEADS, C_NOPE + C_ROPE).transpose(1, 0, 2, 3, 4)
    k_pos = jnp.arange(s)

    def one_block(args):
        qblk, start = args
        logits = jnp.einsum('bqhd,bkhd->bhqk', qblk, k).astype(F32) * scale
        q_pos = start + jnp.arange(C_BLOCK)
        logits = jnp.where(k_pos[None, :] <= q_pos[:, None], logits, NEG)
        p = jax.nn.softmax(logits, axis=-1).astype(v.dtype)
        return jnp.einsum('bhqk,bkhd->bqhd', p, v)

    o = lax.map(one_block, (q_blocks, jnp.arange(nb) * C_BLOCK))
    return o.transpose(1, 0, 2, 3, 4).reshape(b, s, C_HEADS * C_V)


def memory_cross_attention(hx, mem_n, w_xq, w_xkv, w_xo):
    b, s, _ = hx.shape
    m = mem_n.shape[1]
    q = (hx @ w_xq).reshape(b, s, X_HEADS, X_HEAD_DIM)
    kv = (mem_n @ w_xkv).reshape(b, m, 2, X_HEADS, X_HEAD_DIM)
    logits = jnp.einsum('bshd,bmhd->bhsm', q, kv[:, :, 0]).astype(F32) * X_HEAD_DIM ** -0.5
    p = jax.nn.softmax(logits, axis=-1).astype(hx.dtype)
    o = jnp.einsum('bhsm,bmhd->bshd', p, kv[:, :, 1]).reshape(b, s, X_HEADS * X_HEAD_DIM)
    return o @ w_xo


def setup_inputs(seed: int = 0) -> dict:
    key = jax.random.key(seed)
    ks = iter(jax.random.split(key, 40))
    L, D = DEPTH, D_MODEL

    def w(shape, fan_in):
        return jax.random.normal(next(ks), shape, F32) * fan_in ** -0.5

    def gain(shape):
        return 1.0 + 0.02 * jax.random.normal(next(ks), shape, F32)

    def small(shape, sc):
        return sc * jax.random.normal(next(ks), shape, F32)

    return {
        'x': jax.random.normal(next(ks), (BATCH, SEQ, D), F32),
        'mem': jax.random.normal(next(ks), (BATCH, MEM_LEN, D), F32),
        'rel_bias': small((N_BUCKETS, A_HEADS), 0.5),
        'norm_mix': gain((L, D)),
        'w_in': w((L, D, D_IN), D),
        'b_gate': small((L, N_BRANCH * D), 0.02),
        'w_alpha': w((L, B_GATE_RANK, B_DK), B_GATE_RANK),
        'b_alpha': small((L, B_DK), 0.1),
        'gla_norm': gain((L, B_HEADS, B_DV // B_HEADS)),
        'q_a_norm': gain((L, C_Q_RANK)),
        'w_qb': w((L, C_Q_RANK, C_HEADS * (C_NOPE + C_ROPE)), C_Q_RANK),
        'kv_a_norm': gain((L, C_KV_RANK)),
        'w_kvb': w((L, C_KV_RANK, C_HEADS * (C_NOPE + C_V)), C_KV_RANK),
        'w_up_a': w((L, A_HEADS * HEAD_DIM, D), A_HEADS * HEAD_DIM),
        'w_up_b': w((L, B_DV, D), B_DV),
        'w_up_c': w((L, C_HEADS * C_V, D), C_HEADS * C_V),
        'w_o': w((L, D, D), D),
        'norm_x': gain((L, D)),
        'norm_mem': gain((L, D)),
        'w_xq': w((L, D, X_HEADS * X_HEAD_DIM), D),
        'w_xkv': w((L, D, 2 * X_HEADS * X_HEAD_DIM), D),
        'w_xo': w((L, X_HEADS * X_HEAD_DIM, D), X_HEADS * X_HEAD_DIM),
        'norm_ffn': gain((L, D)),
        'w_ffn_gate': w((L, D, D_FF), D),
        'w_ffn_up': w((L, D, D_FF), D),
        'w_ffn_down': w((L, D_FF, D), D_FF),
        'norm_final': gain((D,)),
    }


def reference(x, mem, rel_bias, norm_mix, w_in, b_gate, w_alpha, b_alpha, gla_norm,
              q_a_norm, w_qb, kv_a_norm, w_kvb, w_up_a, w_up_b, w_up_c, w_o,
              norm_x, norm_mem, w_xq, w_xkv, w_xo,
              norm_ffn, w_ffn_gate, w_ffn_up, w_ffn_down, norm_final):
    b, s, d = x.shape
    for l in range(DEPTH):
        h = rmsnorm(x, norm_mix[l])
        z = h @ w_in[l]
        (aq, ak, av, bq, bk, bv, b_lr, br, c_qa, c_kva, c_kr, gates) = split_columns(z)
        o_a = dilated_attention(aq.reshape(b, s, A_HEADS, HEAD_DIM), ak.reshape(b, s, A_HEADS, HEAD_DIM),
                                av.reshape(b, s, A_HEADS, HEAD_DIM), rel_bias).reshape(b, s, A_HEADS * HEAD_DIM)
        log_a = jax.nn.log_sigmoid((b_lr @ w_alpha[l] + b_alpha[l]).astype(F32)) / B_GATE_TAU
        o_b = gated_linear_attention(bq.reshape(b, s, B_HEADS, B_DK // B_HEADS),
                                     bk.reshape(b, s, B_HEADS, B_DK // B_HEADS),
                                     bv.reshape(b, s, B_HEADS, B_DV // B_HEADS),
                                     log_a.reshape(b, s, B_HEADS, B_DK // B_HEADS),
                                     br.reshape(b, s, B_HEADS, B_DV // B_HEADS), gla_norm[l])
        o_c = latent_attention(c_qa, c_kva, c_kr, q_a_norm[l], w_qb[l], kv_a_norm[l], w_kvb[l])
        g = jax.nn.sigmoid((gates + b_gate[l]).astype(F32)).astype(x.dtype).reshape(b, s, N_BRANCH, d)
        merged = (g[:, :, 0] * (o_a @ w_up_a[l]) + g[:, :, 1] * (o_b @ w_up_b[l])
                  + g[:, :, 2] * (o_c @ w_up_c[l]))
        x = x + (merged @ w_o[l]).astype(x.dtype)
        x = x + memory_cross_attention(rmsnorm(x, norm_x[l]), rmsnorm(mem, norm_mem[l]),
                                       w_xq[l], w_xkv[l], w_xo[l]).astype(x.dtype)
        hf = rmsnorm(x, norm_ffn[l])
        x = x + ((jax.nn.silu(hf @ w_ffn_gate[l]) * (hf @ w_ffn_up[l])) @ w_ffn_down[l]).astype(x.dtype)
    return rmsnorm(x, norm_final)
```

```python
import functools
import math

import numpy as np
import jax
import jax.numpy as jnp
from jax import lax
from jax.experimental import pallas as pl
from jax.experimental.pallas import tpu as pltpu

F32 = jnp.float32
BF16 = jnp.bfloat16
EPS = 1e-6
NEG = -1e30

HEAD_DIM = 128
A_HEADS = 8
A_DILATIONS = (1, 4, 16)
A_BLOCK = 128
N_BUCKETS = 32
MAX_DISTANCE = 2048
B_HEADS = 4
B_DK = 512
B_DV = 1024
B_GATE_RANK = 16
B_GATE_TAU = 16.0
B_CHUNK = 64
C_HEADS = 8
C_Q_RANK = 512
C_KV_RANK = 512
C_NOPE = 128
C_ROPE = 64
C_V = 128
ROPE_THETA = 10000.0
X_HEADS = 4
N_BRANCH = 3

VMEM_LIMIT_BYTES = 56 * 1024 * 1024
LANES = 128

Z_AQ, Z_AK, Z_AV = 0, 1024, 2048
Z_BV, Z_BR = 3072, 4096
Z_CQA, Z_CKVA = 5120, 5632
Z_BQ, Z_BK = 6144, 6656
Z_GATES = 7168
Z_LR = 13312
Z_CKR = 13440
Z_WIDTH = 13824
A_TILE = 2048


def _cparams(*sem):
    return pltpu.CompilerParams(dimension_semantics=sem, vmem_limit_bytes=VMEM_LIMIT_BYTES)


def _rmsnorm_rows(x_ref, g_ref, h_ref):
    rows = x_ref.shape[0]

    def body(i, carry):
        r = pl.multiple_of(i * 16, 16)
        x = x_ref[pl.ds(r, 16), :]
        ms = jnp.mean(x * x, axis=-1, keepdims=True)
        h_ref[pl.ds(r, 16), :] = (x * lax.rsqrt(ms + EPS) * g_ref[...]).astype(BF16)
        return carry

    lax.fori_loop(0, rows // 16, body, 0)


def _norm_matmul_kernel(x_ref, g_ref, w_ref, o_ref, h_ref, *, scale):
    @pl.when(pl.program_id(1) == 0)
    def _():
        _rmsnorm_rows(x_ref, g_ref, h_ref)

    acc = jnp.dot(h_ref[...], w_ref[...], preferred_element_type=F32)
    if scale != 1.0:
        acc = acc * scale
    o_ref[...] = acc.astype(o_ref.dtype)


def norm_matmul(x, g, w, *, out_dtype, tm, tn, scale=1.0):
    m, k = x.shape
    n = w.shape[1]
    return pl.pallas_call(
        functools.partial(_norm_matmul_kernel, scale=scale),
        grid=(m // tm, n // tn),
        in_specs=[
            pl.BlockSpec((tm, k), lambda i, j: (i, 0)),
            pl.BlockSpec((1, k), lambda i, j: (0, 0)),
            pl.BlockSpec((k, tn), lambda i, j: (0, j)),
        ],
        out_specs=pl.BlockSpec((tm, tn), lambda i, j: (i, j)),
        out_shape=jax.ShapeDtypeStruct((m, n), out_dtype),
        scratch_shapes=[pltpu.VMEM((tm, k), BF16)],
        compiler_params=_cparams("parallel", "arbitrary"),
        name="norm_matmul",
    )(x, g.reshape(1, k), w)


def _norm_swiglu_kernel(x_ref, g_ref, wg_ref, wu_ref, o_ref, h_ref):
    @pl.when(pl.program_id(1) == 0)
    def _():
        _rmsnorm_rows(x_ref, g_ref, h_ref)

    h = h_ref[...]
    a = jnp.dot(h, wg_ref[...], preferred_element_type=F32)
    b = jnp.dot(h, wu_ref[...], preferred_element_type=F32)
    o_ref[...] = (a * jax.nn.sigmoid(a) * b).astype(o_ref.dtype)


def norm_swiglu(x, g, wg, wu, *, tm, tn):
    m, k = x.shape
    n = wg.shape[1]
    return pl.pallas_call(
        _norm_swiglu_kernel,
        grid=(m // tm, n // tn),
        in_specs=[
            pl.BlockSpec((tm, k), lambda i, j: (i, 0)),
            pl.BlockSpec((1, k), lambda i, j: (0, 0)),
            pl.BlockSpec((k, tn), lambda i, j: (0, j)),
            pl.BlockSpec((k, tn), lambda i, j: (0, j)),
        ],
        out_specs=pl.BlockSpec((tm, tn), lambda i, j: (i, j)),
        out_shape=jax.ShapeDtypeStruct((m, n), BF16),
        scratch_shapes=[pltpu.VMEM((tm, k), BF16)],
        compiler_params=_cparams("parallel", "arbitrary"),
        name="norm_swiglu",
    )(x, g.reshape(1, k), wg, wu)


def _matmul_residual_kernel(a_ref, w_ref, r_ref, o_ref):
    o_ref[...] = r_ref[...] + jnp.dot(a_ref[...], w_ref[...], preferred_element_type=F32)


def matmul_residual(a, w, res, *, tm, tn):
    m, k = a.shape
    n = w.shape[1]
    return pl.pallas_call(
        _matmul_residual_kernel,
        grid=(m // tm, n // tn),
        in_specs=[
            pl.BlockSpec((tm, k), lambda i, j: (i, 0)),
            pl.BlockSpec((k, tn), lambda i, j: (0, j)),
            pl.BlockSpec((tm, tn), lambda i, j: (i, j)),
        ],
        out_specs=pl.BlockSpec((tm, tn), lambda i, j: (i, j)),
        out_shape=jax.ShapeDtypeStruct((m, n), F32),
        compiler_params=_cparams("parallel", "parallel"),
        name="matmul_residual",
    )(a, w, res)


def _rmsnorm_kernel(x_ref, g_ref, o_ref):
    x = x_ref[...]
    ms = jnp.mean(x * x, axis=-1, keepdims=True)
    o_ref[...] = x * lax.rsqrt(ms + EPS) * g_ref[...]


def rmsnorm_rows(x, g, *, tm):
    m, k = x.shape
    return pl.pallas_call(
        _rmsnorm_kernel,
        grid=(m // tm,),
        in_specs=[pl.BlockSpec((tm, k), lambda i: (i, 0)), pl.BlockSpec((1, k), lambda i: (0, 0))],
        out_specs=pl.BlockSpec((tm, k), lambda i: (i, 0)),
        out_shape=jax.ShapeDtypeStruct((m, k), F32),
        compiler_params=_cparams("parallel"),
        name="final_rmsnorm",
    )(x, g.reshape(1, k))


def _t5_bucket_np(dist):
    max_exact = N_BUCKETS // 2
    n = np.maximum(dist, 1).astype(np.float64)
    large = max_exact + (np.log(n / max_exact) / math.log(MAX_DISTANCE / max_exact)
                         * (N_BUCKETS - max_exact)).astype(np.int32)
    large = np.minimum(large, N_BUCKETS - 1)
    return np.where(dist < max_exact, dist, large).astype(np.int32)


def _dilated_bucket_table():
    qi = np.arange(A_BLOCK)[:, None]
    kj = np.arange(2 * A_BLOCK)[None, :]
    steps_back = qi + A_BLOCK - kj
    valid = (steps_back >= 0) & (steps_back <= A_BLOCK)
    tabs = []
    for dil in A_DILATIONS:
        bucket = _t5_bucket_np(np.clip(steps_back, 0, A_BLOCK) * dil)
        tabs.append(np.where(valid, bucket, -1))
    return np.stack(tabs).astype(np.int32)


def _bias_table_kernel(idx_ref, rb_ref, o_ref):
    h = pl.program_id(1)
    idx = idx_ref[...]
    acc = jnp.full(idx.shape, NEG, F32)
    for b in range(N_BUCKETS):
        acc = jnp.where(idx == b, rb_ref[b, h], acc)
    o_ref[...] = acc


def dilated_bias_table(rel_bias):
    idx = jnp.asarray(_dilated_bucket_table())
    npat = len(A_DILATIONS)
    return pl.pallas_call(
        _bias_table_kernel,
        grid=(npat, A_HEADS),
        in_specs=[
            pl.BlockSpec((None, A_BLOCK, 2 * A_BLOCK), lambda p, h: (p, 0, 0)),
            pl.BlockSpec(memory_space=pltpu.SMEM),
        ],
        out_specs=pl.BlockSpec((None, None, A_BLOCK, 2 * A_BLOCK), lambda p, h: (p, h, 0, 0)),
        out_shape=jax.ShapeDtypeStruct((npat, A_HEADS, A_BLOCK, 2 * A_BLOCK), F32),
        compiler_params=_cparams("parallel", "parallel"),
        name="dilated_bias_table",
    )(idx, rel_bias)


def _dilated_kernel(q_ref, kc_ref, kp_ref, vc_ref, vp_ref, bias_ref, o_ref,
                    kcat, vcat, bm0, o_scr, lse_scr, *, scale):
    t = pl.program_id(2)
    tile = q_ref.shape[0]
    blk = A_BLOCK
    kcat[0:tile, :] = kp_ref[...]
    kcat[tile:2 * tile, :] = kc_ref[...]
    vcat[0:tile, :] = vp_ref[...]
    vcat[tile:2 * tile, :] = vc_ref[...]
    col = lax.broadcasted_iota(jnp.int32, (blk, 2 * blk), 1)
    no_prev = jnp.logical_and(col < blk, t == 0)
    for p in range(len(A_DILATIONS)):
        bm0[p] = jnp.where(no_prev, NEG, bias_ref[p])

    def block(p, dil, q_start, use_first):
        k_start = tile + q_start - dil * blk
        if dil == 1:
            q_idx = pl.ds(q_start, blk)
            k_idx = pl.ds(k_start, 2 * blk)
        else:
            q_idx = pl.ds(q_start, blk, stride=dil)
            k_idx = pl.ds(k_start, 2 * blk, stride=dil)
        q = (q_ref[q_idx, :] * scale).astype(BF16)
        k = kcat[k_idx, :].astype(BF16)
        v = vcat[k_idx, :].astype(BF16)
        bm = bm0[p] if use_first else bias_ref[p]
        s = lax.dot_general(q, k, (((1,), (1,)), ((), ())), preferred_element_type=F32)
        s = jnp.where(bm > 0.5 * NEG, s + bm, NEG)
        m = jnp.max(s, axis=-1, keepdims=True)
        e = jnp.exp(s - m)
        l = jnp.sum(e, axis=-1, keepdims=True)
        o = jnp.dot(e.astype(BF16), v, preferred_element_type=F32) / l
        lse = m + jnp.log(l)
        o_scr[p, q_idx, :] = o
        lse_scr[p, q_idx, :] = jnp.broadcast_to(lse, (blk, HEAD_DIM))

    for p, dil in enumerate(A_DILATIONS):
        nblk = tile // (blk * dil)

        def first_body(r, carry, p=p, dil=dil):
            block(p, dil, r, True)
            return carry

        lax.fori_loop(0, dil, first_body, 0)
        if nblk > 1:
            def rest_body(c, carry, p=p, dil=dil, nblk=nblk):
                r = c // (nblk - 1)
                n = c % (nblk - 1) + 1
                block(p, dil, r + dil * blk * n, False)
                return carry

            lax.fori_loop(0, dil * (nblk - 1), rest_body, 0)

    def merge(i, carry):
        r = pl.multiple_of(i * blk, blk)
        rows = pl.ds(r, blk)
        l0, l1, l2 = lse_scr[0, rows, :], lse_scr[1, rows, :], lse_scr[2, rows, :]
        mx = jnp.maximum(jnp.maximum(l0, l1), l2)
        w0, w1, w2 = jnp.exp(l0 - mx), jnp.exp(l1 - mx), jnp.exp(l2 - mx)
        num = w0 * o_scr[0, rows, :] + w1 * o_scr[1, rows, :] + w2 * o_scr[2, rows, :]
        o_ref[rows, :] = (num / (w0 + w1 + w2)).astype(o_ref.dtype)
        return carry

    lax.fori_loop(0, tile // blk, merge, 0)


def dilated_attention(z, bias_tab, *, batch, seq):
    z3 = z.reshape(batch, seq, z.shape[1])
    nt = seq // A_TILE
    hq, hk, hv = Z_AQ // HEAD_DIM, Z_AK // HEAD_DIM, Z_AV // HEAD_DIM
    tile_spec = lambda col0, prev: pl.BlockSpec(
        (None, A_TILE, HEAD_DIM),
        (lambda b, h, t: (b, jnp.maximum(t - 1, 0), col0 + h)) if prev else (lambda b, h, t: (b, t, col0 + h)))
    npat = len(A_DILATIONS)
    out = pl.pallas_call(
        functools.partial(_dilated_kernel, scale=HEAD_DIM ** -0.5),
        grid=(batch, A_HEADS, nt),
        in_specs=[
            tile_spec(hq, False),
            tile_spec(hk, False), tile_spec(hk, True),
            tile_spec(hv, False), tile_spec(hv, True),
            pl.BlockSpec((npat, None, A_BLOCK, 2 * A_BLOCK), lambda b, h, t: (0, h, 0, 0)),
        ],
        out_specs=pl.BlockSpec((None, A_TILE, HEAD_DIM), lambda b, h, t: (b, t, h)),
        out_shape=jax.ShapeDtypeStruct((batch, seq, A_HEADS * HEAD_DIM), BF16),
        scratch_shapes=[
            pltpu.VMEM((2 * A_TILE, HEAD_DIM), F32),
            pltpu.VMEM((2 * A_TILE, HEAD_DIM), F32),
            pltpu.VMEM((npat, A_BLOCK, 2 * A_BLOCK), F32),
            pltpu.VMEM((npat, A_TILE, HEAD_DIM), F32),
            pltpu.VMEM((npat, A_TILE, HEAD_DIM), F32),
        ],
        compiler_params=_cparams("parallel", "parallel", "arbitrary"),
        name="dilated_attention",
    )(z3, z3, z3, z3, z3, bias_tab)
    return out.reshape(batch * seq, A_HEADS * HEAD_DIM)


def _gla_kernel(q_ref, k_ref, v_ref, r_ref, lr_ref, wa_ref, ba_ref, gn_ref, o_ref, st_ref, *, scale):
    @pl.when(pl.program_id(2) == 0)
    def _():
        st_ref[...] = jnp.zeros_like(st_ref)

    c = B_CHUNK
    tc, dk = q_ref.shape
    pre = jnp.dot(lr_ref[...], wa_ref[...], precision=lax.Precision.HIGHEST,
                  preferred_element_type=F32) + ba_ref[...]
    log_a = (jnp.minimum(pre, 0.0) - jnp.log(1.0 + jnp.exp(-jnp.abs(pre)))) * (1.0 / B_GATE_TAU)
    row = lax.broadcasted_iota(jnp.int32, (c, dk), 0)
    tril = lax.broadcasted_iota(jnp.int32, (c, c), 0) >= lax.broadcasted_iota(jnp.int32, (c, c), 1)
    for ci in range(tc // c):
        sl = slice(ci * c, (ci + 1) * c)
        cum = log_a[sl, :]
        for sh in (1, 2, 4, 8, 16, 32):
            cum = cum + jnp.where(row >= sh, pltpu.roll(cum, sh, axis=0), 0.0)
        cum_last = cum[c - 1:c, :]
        q = q_ref[sl, :] * scale
        k = k_ref[sl, :]
        v = v_ref[sl, :]
        q_dec = (q * jnp.exp(cum)).astype(BF16)
        k_inv = (k * jnp.exp(-cum)).astype(BF16)
        k_state = (k * jnp.exp(cum_last - cum)).astype(BF16)
        att = lax.dot_general(q_dec, k_inv, (((1,), (1,)), ((), ())), preferred_element_type=F32)
        att = jnp.where(tril, att, 0.0).astype(BF16)
        st = st_ref[...]
        o = (jnp.dot(att, v.astype(BF16), preferred_element_type=F32)
             + lax.dot_general(q_dec, st.astype(BF16), (((1,), (1,)), ((), ())), preferred_element_type=F32))
        st_ref[...] = st * jnp.exp(cum_last) + jnp.dot(v.T.astype(BF16), k_state, preferred_element_type=F32)
        ms = jnp.mean(o * o, axis=-1, keepdims=True)
        o = o * lax.rsqrt(ms + EPS) * gn_ref[...]
        r = r_ref[sl, :]
        o_ref[sl, :] = (o * (r * jax.nn.sigmoid(r))).astype(o_ref.dtype)


def gated_linear_attention(z, w_alpha_p, b_alpha, gla_norm, *, batch, seq, tc):
    dk = B_DK // B_HEADS
    dv = B_DV // B_HEADS
    nt = seq // tc
    row = lambda b, h, t: b * nt + t
    return pl.pallas_call(
        functools.partial(_gla_kernel, scale=dk ** -0.5),
        grid=(batch, B_HEADS, nt),
        in_specs=[
            pl.BlockSpec((tc, dk), lambda b, h, t: (row(b, h, t), Z_BQ // dk + h)),
            pl.BlockSpec((tc, dk), lambda b, h, t: (row(b, h, t), Z_BK // dk + h)),
            pl.BlockSpec((tc, dv), lambda b, h, t: (row(b, h, t), Z_BV // dv + h)),
            pl.BlockSpec((tc, dv), lambda b, h, t: (row(b, h, t), Z_BR // dv + h)),
            pl.BlockSpec((tc, LANES), lambda b, h, t: (row(b, h, t), Z_LR // LANES)),
            pl.BlockSpec((LANES, dk), lambda b, h, t: (0, h)),
            pl.BlockSpec((1, dk), lambda b, h, t: (0, h)),
            pl.BlockSpec((None, 1, dv), lambda b, h, t: (h, 0, 0)),
        ],
        out_specs=pl.BlockSpec((tc, dv), lambda b, h, t: (row(b, h, t), h)),
        out_shape=jax.ShapeDtypeStruct((batch * seq, B_DV), BF16),
        scratch_shapes=[pltpu.VMEM((dv, dk), F32)],
        compiler_params=_cparams("parallel", "parallel", "arbitrary"),
        name="gated_linear_attention",
    )(z, z, z, z, z, w_alpha_p, b_alpha.reshape(1, B_DK), gla_norm.reshape(B_HEADS, 1, dv))


C_QW = 2 * C_NOPE


def _rope_tables(seq):
    pos = jnp.arange(seq, dtype=F32)
    inv = ROPE_THETA ** (-jnp.arange(0, C_ROPE, 2, dtype=F32) / C_ROPE)
    ang = pos[:, None] * inv[None, :]
    cos, sin = jnp.cos(ang), jnp.sin(ang)
    half = C_ROPE // 2
    z = lambda w: jnp.zeros((seq, w), F32)
    k_tab = jnp.stack([
        jnp.concatenate([cos, cos, z(LANES - C_ROPE)], axis=1),
        jnp.concatenate([-sin, z(half), z(LANES - C_ROPE)], axis=1),
        jnp.concatenate([z(half), sin, z(LANES - C_ROPE)], axis=1)])
    q_tab = jnp.concatenate([
        jnp.stack([jnp.ones((seq, C_NOPE), F32), z(C_NOPE), z(C_NOPE)]), k_tab], axis=2)
    return q_tab, k_tab


def _mla_proj_kernel(cq_ref, ckv_ref, ckr_ref, gq_ref, gkv_ref, wq_ref, wk_ref, wv_ref, qt_ref, kt_ref,
                     q_out, kn_out, kr_out, v_out, *, scale):
    half = C_ROPE // 2

    def rms(x, g):
        ms = jnp.mean(x * x, axis=-1, keepdims=True)
        return (x * lax.rsqrt(ms + EPS) * g).astype(BF16)

    cq = rms(cq_ref[...], gq_ref[...])
    q = jnp.dot(cq, wq_ref[...], preferred_element_type=F32)
    t0, t1, t2 = qt_ref[0], qt_ref[1], qt_ref[2]
    for h in range(C_HEADS):
        qh = q[:, h * C_QW:(h + 1) * C_QW]
        qh = qh * t0 + pltpu.roll(qh, C_QW - half, axis=1) * t1 + pltpu.roll(qh, half, axis=1) * t2
        q_out[:, h * C_QW:(h + 1) * C_QW] = (qh * scale).astype(q_out.dtype)
    ckv = rms(ckv_ref[...], gkv_ref[...])
    kn_out[...] = jnp.dot(ckv, wk_ref[...], preferred_element_type=F32).astype(kn_out.dtype)
    v_out[...] = jnp.dot(ckv, wv_ref[...], preferred_element_type=F32).astype(v_out.dtype)
    kr = ckr_ref[...]
    kr = kr * kt_ref[0] + pltpu.roll(kr, LANES - half, axis=1) * kt_ref[1] + pltpu.roll(kr, half, axis=1) * kt_ref[2]
    kr_out[...] = kr.astype(kr_out.dtype)


def mla_project(z, q_a_norm, kv_a_norm, wq_p, wk_p, wv_p, q_tab, k_tab, *, seq, tm):
    m = z.shape[0]
    nt = seq // tm
    const = lambda i: (0, 0)
    return pl.pallas_call(
        functools.partial(_mla_proj_kernel, scale=(C_NOPE + C_ROPE) ** -0.5),
        grid=(m // tm,),
        in_specs=[
            pl.BlockSpec((tm, C_Q_RANK), lambda i: (i, Z_CQA // C_Q_RANK)),
            pl.BlockSpec((tm, C_KV_RANK), lambda i: (i, Z_CKVA // C_KV_RANK)),
            pl.BlockSpec((tm, LANES), lambda i: (i, Z_CKR // LANES)),
            pl.BlockSpec((1, C_Q_RANK), const),
            pl.BlockSpec((1, C_KV_RANK), const),
            pl.BlockSpec((C_Q_RANK, C_HEADS * C_QW), const),
            pl.BlockSpec((C_KV_RANK, C_HEADS * C_NOPE), const),
            pl.BlockSpec((C_KV_RANK, C_HEADS * C_V), const),
            pl.BlockSpec((3, tm, C_QW), lambda i: (0, i % nt, 0)),
            pl.BlockSpec((3, tm, LANES), lambda i: (0, i % nt, 0)),
        ],
        out_specs=[
            pl.BlockSpec((tm, C_HEADS * C_QW), lambda i: (i, 0)),
            pl.BlockSpec((tm, C_HEADS * C_NOPE), lambda i: (i, 0)),
            pl.BlockSpec((tm, LANES), lambda i: (i, 0)),
            pl.BlockSpec((tm, C_HEADS * C_V), lambda i: (i, 0)),
        ],
        out_shape=[
            jax.ShapeDtypeStruct((m, C_HEADS * C_QW), BF16),
            jax.ShapeDtypeStruct((m, C_HEADS * C_NOPE), BF16),
            jax.ShapeDtypeStruct((m, LANES), BF16),
            jax.ShapeDtypeStruct((m, C_HEADS * C_V), BF16),
        ],
        compiler_params=_cparams("parallel"),
        name="mla_project",
    )(z, z, z, q_a_norm.reshape(1, -1), kv_a_norm.reshape(1, -1), wq_p, wk_p, wv_p, q_tab, k_tab)


def _mla_flash_kernel(q_ref, kn_ref, kr_ref, v_ref, o_ref, m_ref, l_ref, acc_ref):
    qi = pl.program_id(2)
    ki = pl.program_id(3)
    tq = q_ref.shape[0]
    tk = kn_ref.shape[0]

    @pl.when(ki == 0)
    def _():
        m_ref[...] = jnp.full_like(m_ref, NEG)
        l_ref[...] = jnp.zeros_like(l_ref)
        acc_ref[...] = jnp.zeros_like(acc_ref)

    def step(masked):
        k = jnp.concatenate([kn_ref[...], kr_ref[...]], axis=1)
        s = lax.dot_general(q_ref[...], k, (((1,), (1,)), ((), ())), preferred_element_type=F32)
        if masked:
            rows = lax.broadcasted_iota(jnp.int32, (tq, tk), 0)
            cols = lax.broadcasted_iota(jnp.int32, (tq, tk), 1)
            s = jnp.where(cols <= rows, s, NEG)
        m_prev = m_ref[...]
        m_new = jnp.maximum(m_prev, jnp.max(s, axis=-1, keepdims=True))
        alpha = jnp.exp(m_prev - m_new)
        p = jnp.exp(s - pltpu.repeat(m_new, tk // LANES, axis=1))
        l_ref[...] = alpha * l_ref[...] + jnp.sum(p, axis=-1, keepdims=True)
        acc_ref[...] = alpha * acc_ref[...] + jnp.dot(p.astype(BF16), v_ref[...], preferred_element_type=F32)
        m_ref[...] = m_new

    @pl.when(ki < qi)
    def _():
        step(False)

    @pl.when(ki == qi)
    def _():
        step(True)
        o_ref[...] = (acc_ref[...] / l_ref[...]).astype(o_ref.dtype)


def mla_flash(q, kn, kr, v, *, batch, seq, tq):
    nq = seq // tq
    qrow = lambda b, h, i, j: b * nq + i
    krow = lambda b, h, i, j: b * nq + jnp.minimum(i, j)
    return pl.pallas_call(
        _mla_flash_kernel,
        grid=(batch, C_HEADS, nq, nq),
        in_specs=[
            pl.BlockSpec((tq, C_QW), lambda b, h, i, j: (qrow(b, h, i, j), h)),
            pl.BlockSpec((tq, C_NOPE), lambda b, h, i, j: (krow(b, h, i, j), h)),
            pl.BlockSpec((tq, LANES), lambda b, h, i, j: (krow(b, h, i, j), 0)),
            pl.BlockSpec((tq, C_V), lambda b, h, i, j: (krow(b, h, i, j), h)),
        ],
        out_specs=pl.BlockSpec((tq, C_V), lambda b, h, i, j: (qrow(b, h, i, j), h)),
        out_shape=jax.ShapeDtypeStruct((batch * seq, C_HEADS * C_V), BF16),
        scratch_shapes=[
            pltpu.VMEM((tq, LANES), F32),
            pltpu.VMEM((tq, LANES), F32),
            pltpu.VMEM((tq, C_V), F32),
        ],
        compiler_params=_cparams("parallel", "parallel", "parallel", "arbitrary"),
        name="mla_flash",
    )(q, kn, kr, v)


def _merge_kernel(oa_ref, ob_ref, oc_ref, wa_ref, wb_ref, wc_ref, g0_ref, g1_ref, g2_ref,
                  b0_ref, b1_ref, b2_ref, o_ref):
    def branch(o, w, g, b):
        return jax.nn.sigmoid(g[...] + b[...]) * jnp.dot(o[...], w[...], preferred_element_type=F32)

    acc = branch(oa_ref, wa_ref, g0_ref, b0_ref)
    acc = acc + branch(ob_ref, wb_ref, g1_ref, b1_ref)
    acc = acc + branch(oc_ref, wc_ref, g2_ref, b2_ref)
    o_ref[...] = acc.astype(o_ref.dtype)


def gated_merge(o_a, o_b, o_c, w_a, w_b, w_c, z, b_gate, *, tm, tn):
    m, k = o_a.shape
    d = w_a.shape[1]
    nj = d // tn
    act = pl.BlockSpec((tm, k), lambda i, j: (i, 0))
    wgt = pl.BlockSpec((k, tn), lambda i, j: (0, j))
    gate = lambda br: pl.BlockSpec((tm, tn), lambda i, j: (i, Z_GATES // tn + br * nj + j))
    gbias = lambda br: pl.BlockSpec((1, tn), lambda i, j: (0, br * nj + j))
    return pl.pallas_call(
        _merge_kernel,
        grid=(m // tm, nj),
        in_specs=[act, act, act, wgt, wgt, wgt, gate(0), gate(1), gate(2), gbias(0), gbias(1), gbias(2)],
        out_specs=pl.BlockSpec((tm, tn), lambda i, j: (i, j)),
        out_shape=jax.ShapeDtypeStruct((m, d), BF16),
        compiler_params=_cparams("parallel", "parallel"),
        name="gated_merge",
    )(o_a, o_b, o_c, w_a, w_b, w_c, z, z, z, *([b_gate.reshape(1, -1)] * 3))


def _cross_attn_kernel(q_ref, k_ref, v_ref, o_ref):
    s = lax.dot_general(q_ref[...], k_ref[...], (((1,), (1,)), ((), ())), preferred_element_type=F32)
    m = jnp.max(s, axis=-1, keepdims=True)
    e = jnp.exp(s - m)
    l = jnp.sum(e, axis=-1, keepdims=True)
    o_ref[...] = (jnp.dot(e.astype(BF16), v_ref[...], preferred_element_type=F32) / l).astype(o_ref.dtype)


def cross_attention(q, kv, *, batch, seq, mem_len, tq):
    d = q.shape[1]
    hd = d // X_HEADS
    nq = seq // tq
    return pl.pallas_call(
        _cross_attn_kernel,
        grid=(batch, nq, X_HEADS),
        in_specs=[
            pl.BlockSpec((tq, hd), lambda b, i, h: (b * nq + i, h)),
            pl.BlockSpec((mem_len, hd), lambda b, i, h: (b, h)),
            pl.BlockSpec((mem_len, hd), lambda b, i, h: (b, X_HEADS + h)),
        ],
        out_specs=pl.BlockSpec((tq, hd), lambda b, i, h: (b * nq + i, h)),
        out_shape=jax.ShapeDtypeStruct((batch * seq, d), BF16),
        compiler_params=_cparams("parallel", "parallel", "parallel"),
        name="cross_attention",
    )(q, kv, kv)


def _pack_w_in(w):
    d = w.shape[0]
    sizes = (1024, 1024, 1024, B_DK, B_DK, B_DV, B_GATE_RANK, B_DV, C_Q_RANK, C_KV_RANK, C_ROPE, N_BRANCH * d)
    offs = np.concatenate([[0], np.cumsum(sizes)])
    aq, ak, av, bq, bk, bv, lr, br, cqa, ckva, ckr, gates = [w[:, offs[i]:offs[i + 1]] for i in range(len(sizes))]
    zeros = lambda n: jnp.zeros((d, n), w.dtype)
    packed = jnp.concatenate(
        [aq, ak, av, bv, br, cqa, ckva, bq, bk, gates,
         lr, zeros(LANES - B_GATE_RANK), ckr, zeros(LANES - C_ROPE)], axis=1)
    packed = jnp.concatenate([packed, zeros(Z_WIDTH - packed.shape[1])], axis=1)
    return packed.astype(BF16)


def _pack_w_qb(w):
    r = w.shape[0]
    w = w.reshape(r, C_HEADS, C_NOPE + C_ROPE)
    w = jnp.pad(w, ((0, 0), (0, 0), (0, C_QW - C_NOPE - C_ROPE)))
    return w.reshape(r, C_HEADS * C_QW).astype(BF16)


def _pack_w_kvb(w):
    r = w.shape[0]
    w = w.reshape(r, C_HEADS, C_NOPE + C_V)
    wk = w[:, :, :C_NOPE].reshape(r, C_HEADS * C_NOPE)
    wv = w[:, :, C_NOPE:].reshape(r, C_HEADS * C_V)
    return wk.astype(BF16), wv.astype(BF16)


def kernel(x, mem, rel_bias, norm_mix, w_in, b_gate, w_alpha, b_alpha, gla_norm, q_a_norm, w_qb, kv_a_norm, w_kvb, w_up_a, w_up_b, w_up_c, w_o, norm_x, norm_mem, w_xq, w_xkv, w_xo, norm_ffn, w_ffn_gate, w_ffn_up, w_ffn_down, norm_final):
    batch, seq, d = x.shape
    mem_len = mem.shape[1]
    depth = w_in.shape[0]
    t = batch * seq
    assert d == 2048 and seq % A_TILE == 0, "tiling is derived for the stated shapes"

    xf = x.reshape(t, d)
    memf = mem.reshape(batch * mem_len, d)
    bias_tab = dilated_bias_table(rel_bias)
    q_tab, k_tab = _rope_tables(seq)
    x_scale = (d // X_HEADS) ** -0.5
    bf = lambda a: a.astype(BF16)

    for l in range(depth):
        z = norm_matmul(xf, norm_mix[l], _pack_w_in(w_in[l]), out_dtype=F32, tm=1024, tn=512)
        o_a = dilated_attention(z, bias_tab, batch=batch, seq=seq)
        w_alpha_p = jnp.pad(w_alpha[l], ((0, LANES - B_GATE_RANK), (0, 0)))
        o_b = gated_linear_attention(z, w_alpha_p, b_alpha[l], gla_norm[l], batch=batch, seq=seq, tc=512)
        wk_p, wv_p = _pack_w_kvb(w_kvb[l])
        cq, ckn, ckr, cv = mla_project(z, q_a_norm[l], kv_a_norm[l], _pack_w_qb(w_qb[l]), wk_p, wv_p,
                                       q_tab, k_tab, seq=seq, tm=512)
        o_c = mla_flash(cq, ckn, ckr, cv, batch=batch, seq=seq, tq=512)
        merged = gated_merge(o_a, o_b, o_c, bf(w_up_a[l]), bf(w_up_b[l]), bf(w_up_c[l]), z, b_gate[l],
                             tm=512, tn=512)
        xf = matmul_residual(merged, bf(w_o[l]), xf, tm=1024, tn=512)
        xq = norm_matmul(xf, norm_x[l], bf(w_xq[l]), out_dtype=BF16, tm=1024, tn=512, scale=x_scale)
        xkv = norm_matmul(memf, norm_mem[l], bf(w_xkv[l]), out_dtype=BF16, tm=256, tn=512)
        xo = cross_attention(xq, xkv, batch=batch, seq=seq, mem_len=mem_len, tq=1024)
        xf = matmul_residual(xo, bf(w_xo[l]), xf, tm=1024, tn=512)
        act = norm_swiglu(xf, norm_ffn[l], bf(w_ffn_gate[l]), bf(w_ffn_up[l]), tm=1024, tn=512)
        xf = matmul_residual(act, bf(w_ffn_down[l]), xf, tm=512, tn=512)
    return rmsnorm_rows(xf, norm_final, tm=512).reshape(batch, seq, d)
```

```python
import functools
import math

import numpy as np
import jax
import jax.numpy as jnp
from jax import lax
from jax.experimental import pallas as pl
from jax.experimental.pallas import tpu as pltpu

F32 = jnp.float32
BF16 = jnp.bfloat16
EPS = 1e-6
NEG = -1e30

HEAD_DIM = 128
A_HEADS = 8
A_DILATIONS = (1, 4, 16)
A_BLOCK = 128
N_BUCKETS = 32
MAX_DISTANCE = 2048
B_HEADS = 4
B_DK = 512
B_DV = 1024
B_GATE_RANK = 16
B_GATE_TAU = 16.0
B_CHUNK = 64
C_HEADS = 8
C_Q_RANK = 512
C_KV_RANK = 512
C_NOPE = 128
C_ROPE = 64
C_V = 128
ROPE_THETA = 10000.0
X_HEADS = 4
N_BRANCH = 3

VMEM_LIMIT_BYTES = 56 * 1024 * 1024
LANES = 128

Z_AQ, Z_AK, Z_AV = 0, 1024, 2048
Z_BV, Z_BR = 3072, 4096
Z_CQA, Z_CKVA = 5120, 5632
Z_BQ, Z_BK = 6144, 6656
Z_GATES = 7168
Z_LR = 13312
Z_CKR = 13440
Z_WIDTH = 13824
A_TILE = 2048


def _cparams(*sem):
    return pltpu.CompilerParams(dimension_semantics=sem, vmem_limit_bytes=VMEM_LIMIT_BYTES)


def _rmsnorm_rows(x_ref, g_ref, h_ref):
    rows = x_ref.shape[0]

    def body(i, carry):
        r = pl.multiple_of(i * 16, 16)
        x = x_ref[pl.ds(r, 16), :]
        ms = jnp.mean(x * x, axis=-1, keepdims=True)
        h_ref[pl.ds(r, 16), :] = (x * lax.rsqrt(ms + EPS) * g_ref[...]).astype(BF16)
        return carry

    lax.fori_loop(0, rows // 16, body, 0, unroll=8)


def _norm_matmul_kernel(x_ref, g_ref, w_ref, o_ref, h_ref, *, scale):
    @pl.when(pl.program_id(1) == 0)
    def _():
        _rmsnorm_rows(x_ref, g_ref, h_ref)

    acc = jnp.dot(h_ref[...], w_ref[...], preferred_element_type=F32)
    if scale != 1.0:
        acc = acc * scale
    o_ref[...] = acc.astype(o_ref.dtype)


def norm_matmul(x, g, w, *, out_dtype, tm, tn, scale=1.0):
    m, k = x.shape
    n = w.shape[1]
    return pl.pallas_call(
        functools.partial(_norm_matmul_kernel, scale=scale),
        grid=(m // tm, n // tn),
        in_specs=[
            pl.BlockSpec((tm, k), lambda i, j: (i, 0)),
            pl.BlockSpec((1, k), lambda i, j: (0, 0)),
            pl.BlockSpec((k, tn), lambda i, j: (0, j)),
        ],
        out_specs=pl.BlockSpec((tm, tn), lambda i, j: (i, j)),
        out_shape=jax.ShapeDtypeStruct((m, n), out_dtype),
        scratch_shapes=[pltpu.VMEM((tm, k), BF16)],
        compiler_params=_cparams("parallel", "arbitrary"),
        name="norm_matmul",
    )(x, g.reshape(1, k), w)


def _norm_swiglu_kernel(x_ref, g_ref, wg_ref, wu_ref, o_ref, h_ref):
    @pl.when(pl.program_id(1) == 0)
    def _():
        _rmsnorm_rows(x_ref, g_ref, h_ref)

    h = h_ref[...]
    a = jnp.dot(h, wg_ref[...], preferred_element_type=F32)
    b = jnp.dot(h, wu_ref[...], preferred_element_type=F32)
    o_ref[...] = (a * jax.nn.sigmoid(a) * b).astype(o_ref.dtype)


def norm_swiglu(x, g, wg, wu, *, tm, tn):
    m, k = x.shape
    n = wg.shape[1]
    return pl.pallas_call(
        _norm_swiglu_kernel,
        grid=(m // tm, n // tn),
        in_specs=[
            pl.BlockSpec((tm, k), lambda i, j: (i, 0)),
            pl.BlockSpec((1, k), lambda i, j: (0, 0)),
            pl.BlockSpec((k, tn), lambda i, j: (0, j)),
            pl.BlockSpec((k, tn), lambda i, j: (0, j)),
        ],
        out_specs=pl.BlockSpec((tm, tn), lambda i, j: (i, j)),
        out_shape=jax.ShapeDtypeStruct((m, n), BF16),
        scratch_shapes=[pltpu.VMEM((tm, k), BF16)],
        compiler_params=_cparams("parallel", "arbitrary"),
        name="norm_swiglu",
    )(x, g.reshape(1, k), wg, wu)


def _matmul_residual_kernel(a_ref, w_ref, r_ref, o_ref):
    o_ref[...] = r_ref[...] + jnp.dot(a_ref[...], w_ref[...], preferred_element_type=F32)


def matmul_residual(a, w, res, *, tm, tn):
    m, k = a.shape
    n = w.shape[1]
    return pl.pallas_call(
        _matmul_residual_kernel,
        grid=(m // tm, n // tn),
        in_specs=[
            pl.BlockSpec((tm, k), lambda i, j: (i, 0)),
            pl.BlockSpec((k, tn), lambda i, j: (0, j)),
            pl.BlockSpec((tm, tn), lambda i, j: (i, j)),
        ],
        out_specs=pl.BlockSpec((tm, tn), lambda i, j: (i, j)),
        out_shape=jax.ShapeDtypeStruct((m, n), F32),
        compiler_params=_cparams("parallel", "parallel"),
        name="matmul_residual",
    )(a, w, res)


def _rmsnorm_kernel(x_ref, g_ref, o_ref):
    x = x_ref[...]
    ms = jnp.mean(x * x, axis=-1, keepdims=True)
    o_ref[...] = x * lax.rsqrt(ms + EPS) * g_ref[...]


def rmsnorm_rows(x, g, *, tm):
    m, k = x.shape
    return pl.pallas_call(
        _rmsnorm_kernel,
        grid=(m // tm,),
        in_specs=[pl.BlockSpec((tm, k), lambda i: (i, 0)), pl.BlockSpec((1, k), lambda i: (0, 0))],
        out_specs=pl.BlockSpec((tm, k), lambda i: (i, 0)),
        out_shape=jax.ShapeDtypeStruct((m, k), F32),
        compiler_params=_cparams("parallel"),
        name="final_rmsnorm",
    )(x, g.reshape(1, k))


def _t5_bucket_np(dist):
    max_exact = N_BUCKETS // 2
    n = np.maximum(dist, 1).astype(np.float64)
    large = max_exact + (np.log(n / max_exact) / math.log(MAX_DISTANCE / max_exact)
                         * (N_BUCKETS - max_exact)).astype(np.int32)
    large = np.minimum(large, N_BUCKETS - 1)
    return np.where(dist < max_exact, dist, large).astype(np.int32)


def _dilated_bucket_table():
    qi = np.arange(A_BLOCK)[:, None]
    kj = np.arange(2 * A_BLOCK)[None, :]
    steps_back = qi + A_BLOCK - kj
    valid = (steps_back >= 0) & (steps_back <= A_BLOCK)
    tabs = []
    for dil in A_DILATIONS:
        bucket = _t5_bucket_np(np.clip(steps_back, 0, A_BLOCK) * dil)
        tabs.append(np.where(valid, bucket, -1))
    return np.stack(tabs).astype(np.int32)


def _bias_table_kernel(idx_ref, rb_ref, o_ref):
    h = pl.program_id(1)
    idx = idx_ref[...]
    acc = jnp.full(idx.shape, NEG, F32)
    for b in range(N_BUCKETS):
        acc = jnp.where(idx == b, rb_ref[b, h], acc)
    o_ref[...] = acc


def dilated_bias_table(rel_bias):
    idx = jnp.asarray(_dilated_bucket_table())
    npat = len(A_DILATIONS)
    return pl.pallas_call(
        _bias_table_kernel,
        grid=(npat, A_HEADS),
        in_specs=[
            pl.BlockSpec((None, A_BLOCK, 2 * A_BLOCK), lambda p, h: (p, 0, 0)),
            pl.BlockSpec(memory_space=pltpu.SMEM),
        ],
        out_specs=pl.BlockSpec((None, None, A_BLOCK, 2 * A_BLOCK), lambda p, h: (p, h, 0, 0)),
        out_shape=jax.ShapeDtypeStruct((npat, A_HEADS, A_BLOCK, 2 * A_BLOCK), F32),
        compiler_params=_cparams("parallel", "parallel"),
        name="dilated_bias_table",
    )(idx, rel_bias)


def _dilated_kernel(q_ref, kc_ref, kp_ref, vc_ref, vp_ref, bias_ref, o_ref,
                    kcat, vcat, bm0, o_scr, lse_scr, *, scale):
    t = pl.program_id(2)
    tile = q_ref.shape[0]
    blk = A_BLOCK
    kcat[0:tile, :] = kp_ref[...]
    kcat[tile:2 * tile, :] = kc_ref[...]
    vcat[0:tile, :] = vp_ref[...]
    vcat[tile:2 * tile, :] = vc_ref[...]
    col = lax.broadcasted_iota(jnp.int32, (blk, 2 * blk), 1)
    no_prev = jnp.logical_and(col < blk, t == 0)
    for p in range(len(A_DILATIONS)):
        bm0[2 * p] = bias_ref[p]
        bm0[2 * p + 1] = jnp.where(no_prev, NEG, bias_ref[p])

    def block(p, dil, q_start, first):
        k_start = tile + q_start - dil * blk
        if dil == 1:
            q_idx = pl.ds(pl.multiple_of(q_start, blk), blk)
            k_idx = pl.ds(pl.multiple_of(k_start, blk), 2 * blk)
        else:
            q_idx = pl.ds(q_start, blk, stride=dil)
            k_idx = pl.ds(k_start, 2 * blk, stride=dil)
        q = (q_ref[q_idx, :] * scale).astype(BF16)
        k = kcat[k_idx, :].astype(BF16)
        v = vcat[k_idx, :].astype(BF16)
        bm = bm0[2 * p + first]
        s = lax.dot_general(q, k, (((1,), (1,)), ((), ())), preferred_element_type=F32)
        s = jnp.where(bm > 0.5 * NEG, s + bm, NEG)
        m = jnp.max(s, axis=-1, keepdims=True)
        e = jnp.exp(s - m)
        l = jnp.sum(e, axis=-1, keepdims=True)
        o = jnp.dot(e.astype(BF16), v, preferred_element_type=F32) / l
        lse = m + jnp.log(l)
        o_scr[p, q_idx, :] = o
        lse_scr[p, q_idx, :] = jnp.broadcast_to(lse, (blk, HEAD_DIM))

    for p, dil in enumerate(A_DILATIONS):
        nblk = tile // (blk * dil)

        def body(c, carry, p=p, dil=dil, nblk=nblk):
            r = lax.shift_right_logical(c, int(math.log2(nblk)))
            n = jnp.bitwise_and(c, nblk - 1)
            block(p, dil, r + dil * blk * n, (n == 0).astype(jnp.int32))
            return carry

        lax.fori_loop(0, dil * nblk, body, 0, unroll=8)

    def merge(i, carry):
        r = pl.multiple_of(i * blk, blk)
        rows = pl.ds(r, blk)
        l0, l1, l2 = lse_scr[0, rows, :], lse_scr[1, rows, :], lse_scr[2, rows, :]
        mx = jnp.maximum(jnp.maximum(l0, l1), l2)
        w0, w1, w2 = jnp.exp(l0 - mx), jnp.exp(l1 - mx), jnp.exp(l2 - mx)
        num = w0 * o_scr[0, rows, :] + w1 * o_scr[1, rows, :] + w2 * o_scr[2, rows, :]
        o_ref[rows, :] = (num / (w0 + w1 + w2)).astype(o_ref.dtype)
        return carry

    lax.fori_loop(0, tile // blk, merge, 0)


def dilated_attention(z, bias_tab, *, batch, seq):
    z3 = z.reshape(batch, seq, z.shape[1])
    nt = seq // A_TILE
    hq, hk, hv = Z_AQ // HEAD_DIM, Z_AK // HEAD_DIM, Z_AV // HEAD_DIM
    tile_spec = lambda col0, prev: pl.BlockSpec(
        (None, A_TILE, HEAD_DIM),
        (lambda b, h, t: (b, jnp.maximum(t - 1, 0), col0 + h)) if prev else (lambda b, h, t: (b, t, col0 + h)))
    npat = len(A_DILATIONS)
    out = pl.pallas_call(
        functools.partial(_dilated_kernel, scale=HEAD_DIM ** -0.5),
        grid=(batch, A_HEADS, nt),
        in_specs=[
            tile_spec(hq, False),
            tile_spec(hk, False), tile_spec(hk, True),
            tile_spec(hv, False), tile_spec(hv, True),
            pl.BlockSpec((npat, None, A_BLOCK, 2 * A_BLOCK), lambda b, h, t: (0, h, 0, 0)),
        ],
        out_specs=pl.BlockSpec((None, A_TILE, HEAD_DIM), lambda b, h, t: (b, t, h)),
        out_shape=jax.ShapeDtypeStruct((batch, seq, A_HEADS * HEAD_DIM), BF16),
        scratch_shapes=[
            pltpu.VMEM((2 * A_TILE, HEAD_DIM), F32),
            pltpu.VMEM((2 * A_TILE, HEAD_DIM), F32),
            pltpu.VMEM((2 * npat, A_BLOCK, 2 * A_BLOCK), F32),
            pltpu.VMEM((npat, A_TILE, HEAD_DIM), F32),
            pltpu.VMEM((npat, A_TILE, HEAD_DIM), F32),
        ],
        compiler_params=_cparams("parallel", "parallel", "arbitrary"),
        name="dilated_attention",
    )(z3, z3, z3, z3, z3, bias_tab)
    return out.reshape(batch * seq, A_HEADS * HEAD_DIM)


def _gla_kernel(q_ref, k_ref, v_ref, r_ref, lr_ref, wa_ref, ba_ref, gn_ref, o_ref, st_ref, *, scale):
    @pl.when(pl.program_id(2) == 0)
    def _():
        st_ref[...] = jnp.zeros_like(st_ref)

    c = B_CHUNK
    tc, dk = q_ref.shape
    pre = jnp.dot(lr_ref[...], wa_ref[...], precision=lax.Precision.HIGHEST,
                  preferred_element_type=F32) + ba_ref[...]
    log_a = (jnp.minimum(pre, 0.0) - jnp.log(1.0 + jnp.exp(-jnp.abs(pre)))) * (1.0 / B_GATE_TAU)
    row = lax.broadcasted_iota(jnp.int32, (c, dk), 0)
    tril = lax.broadcasted_iota(jnp.int32, (c, c), 0) >= lax.broadcasted_iota(jnp.int32, (c, c), 1)
    for ci in range(tc // c):
        sl = slice(ci * c, (ci + 1) * c)
        cum = log_a[sl, :]
        for sh in (1, 2, 4, 8, 16, 32):
            cum = cum + jnp.where(row >= sh, pltpu.roll(cum, sh, axis=0), 0.0)
        cum_last = cum[c - 1:c, :]
        q = q_ref[sl, :] * scale
        k = k_ref[sl, :]
        v = v_ref[sl, :]
        q_dec = (q * jnp.exp(cum)).astype(BF16)
        k_inv = (k * jnp.exp(-cum)).astype(BF16)
        k_state = (k * jnp.exp(cum_last - cum)).astype(BF16)
        att = lax.dot_general(q_dec, k_inv, (((1,), (1,)), ((), ())), preferred_element_type=F32)
        att = jnp.where(tril, att, 0.0).astype(BF16)
        st = st_ref[...]
        o = (jnp.dot(att, v.astype(BF16), preferred_element_type=F32)
             + lax.dot_general(q_dec, st.astype(BF16), (((1,), (1,)), ((), ())), preferred_element_type=F32))
        st_ref[...] = st * jnp.exp(cum_last) + jnp.dot(v.T.astype(BF16), k_state, preferred_element_type=F32)
        ms = jnp.mean(o * o, axis=-1, keepdims=True)
        o = o * lax.rsqrt(ms + EPS) * gn_ref[...]
        r = r_ref[sl, :]
        o_ref[sl, :] = (o * (r * jax.nn.sigmoid(r))).astype(o_ref.dtype)


def gated_linear_attention(z, w_alpha_p, b_alpha, gla_norm, *, batch, seq, tc):
    dk = B_DK // B_HEADS
    dv = B_DV // B_HEADS
    nt = seq // tc
    row = lambda b, h, t: b * nt + t
    return pl.pallas_call(
        functools.partial(_gla_kernel, scale=dk ** -0.5),
        grid=(batch, B_HEADS, nt),
        in_specs=[
            pl.BlockSpec((tc, dk), lambda b, h, t: (row(b, h, t), Z_BQ // dk + h)),
            pl.BlockSpec((tc, dk), lambda b, h, t: (row(b, h, t), Z_BK // dk + h)),
            pl.BlockSpec((tc, dv), lambda b, h, t: (row(b, h, t), Z_BV // dv + h)),
            pl.BlockSpec((tc, dv), lambda b, h, t: (row(b, h, t), Z_BR // dv + h)),
            pl.BlockSpec((tc, LANES), lambda b, h, t: (row(b, h, t), Z_LR // LANES)),
            pl.BlockSpec((LANES, dk), lambda b, h, t: (0, h)),
            pl.BlockSpec((1, dk), lambda b, h, t: (0, h)),
            pl.BlockSpec((None, 1, dv), lambda b, h, t: (h, 0, 0)),
        ],
        out_specs=pl.BlockSpec((tc, dv), lambda b, h, t: (row(b, h, t), h)),
        out_shape=jax.ShapeDtypeStruct((batch * seq, B_DV), BF16),
        scratch_shapes=[pltpu.VMEM((dv, dk), F32)],
        compiler_params=_cparams("parallel", "parallel", "arbitrary"),
        name="gated_linear_attention",
    )(z, z, z, z, z, w_alpha_p, b_alpha.reshape(1, B_DK), gla_norm.reshape(B_HEADS, 1, dv))


C_QW = 2 * C_NOPE


def _rope_tables(seq):
    pos = jnp.arange(seq, dtype=F32)
    inv = ROPE_THETA ** (-jnp.arange(0, C_ROPE, 2, dtype=F32) / C_ROPE)
    ang = pos[:, None] * inv[None, :]
    cos, sin = jnp.cos(ang), jnp.sin(ang)
    half = C_ROPE // 2
    z = lambda w: jnp.zeros((seq, w), F32)
    k_tab = jnp.stack([
        jnp.concatenate([cos, cos, z(LANES - C_ROPE)], axis=1),
        jnp.concatenate([-sin, z(half), z(LANES - C_ROPE)], axis=1),
        jnp.concatenate([z(half), sin, z(LANES - C_ROPE)], axis=1)])
    q_tab = jnp.concatenate([
        jnp.stack([jnp.ones((seq, C_NOPE), F32), z(C_NOPE), z(C_NOPE)]), k_tab], axis=2)
    return q_tab, k_tab


def _mla_proj_kernel(cq_ref, ckv_ref, ckr_ref, gq_ref, gkv_ref, wq_ref, wk_ref, wv_ref, qt_ref, kt_ref,
                     q_out, kn_out, kr_out, v_out, *, scale):
    half = C_ROPE // 2

    def rms(x, g):
        ms = jnp.mean(x * x, axis=-1, keepdims=True)
        return (x * lax.rsqrt(ms + EPS) * g).astype(BF16)

    cq = rms(cq_ref[...], gq_ref[...])
    q = jnp.dot(cq, wq_ref[...], preferred_element_type=F32)
    t0, t1, t2 = qt_ref[0], qt_ref[1], qt_ref[2]
    for h in range(C_HEADS):
        qh = q[:, h * C_QW:(h + 1) * C_QW]
        qh = qh * t0 + pltpu.roll(qh, C_QW - half, axis=1) * t1 + pltpu.roll(qh, half, axis=1) * t2
        q_out[:, h * C_QW:(h + 1) * C_QW] = (qh * scale).astype(q_out.dtype)
    ckv = rms(ckv_ref[...], gkv_ref[...])
    kn_out[...] = jnp.dot(ckv, wk_ref[...], preferred_element_type=F32).astype(kn_out.dtype)
    v_out[...] = jnp.dot(ckv, wv_ref[...], preferred_element_type=F32).astype(v_out.dtype)
    kr = ckr_ref[...]
    kr = kr * kt_ref[0] + pltpu.roll(kr, LANES - half, axis=1) * kt_ref[1] + pltpu.roll(kr, half, axis=1) * kt_ref[2]
    kr_out[...] = kr.astype(kr_out.dtype)


def mla_project(z, q_a_norm, kv_a_norm, wq_p, wk_p, wv_p, q_tab, k_tab, *, seq, tm):
    m = z.shape[0]
    nt = seq // tm
    const = lambda i: (0, 0)
    return pl.pallas_call(
        functools.partial(_mla_proj_kernel, scale=(C_NOPE + C_ROPE) ** -0.5 * math.log2(math.e)),
        grid=(m // tm,),
        in_specs=[
            pl.BlockSpec((tm, C_Q_RANK), lambda i: (i, Z_CQA // C_Q_RANK)),
            pl.BlockSpec((tm, C_KV_RANK), lambda i: (i, Z_CKVA // C_KV_RANK)),
            pl.BlockSpec((tm, LANES), lambda i: (i, Z_CKR // LANES)),
            pl.BlockSpec((1, C_Q_RANK), const),
            pl.BlockSpec((1, C_KV_RANK), const),
            pl.BlockSpec((C_Q_RANK, C_HEADS * C_QW), const),
            pl.BlockSpec((C_KV_RANK, C_HEADS * C_NOPE), const),
            pl.BlockSpec((C_KV_RANK, C_HEADS * C_V), const),
            pl.BlockSpec((3, tm, C_QW), lambda i: (0, i % nt, 0)),
            pl.BlockSpec((3, tm, LANES), lambda i: (0, i % nt, 0)),
        ],
        out_specs=[
            pl.BlockSpec((tm, C_HEADS * C_QW), lambda i: (i, 0)),
            pl.BlockSpec((tm, C_HEADS * C_NOPE), lambda i: (i, 0)),
            pl.BlockSpec((tm, LANES), lambda i: (i, 0)),
            pl.BlockSpec((tm, C_HEADS * C_V), lambda i: (i, 0)),
        ],
        out_shape=[
            jax.ShapeDtypeStruct((m, C_HEADS * C_QW), BF16),
            jax.ShapeDtypeStruct((m, C_HEADS * C_NOPE), BF16),
            jax.ShapeDtypeStruct((m, LANES), BF16),
            jax.ShapeDtypeStruct((m, C_HEADS * C_V), BF16),
        ],
        compiler_params=_cparams("parallel"),
        name="mla_project",
    )(z, z, z, q_a_norm.reshape(1, -1), kv_a_norm.reshape(1, -1), wq_p, wk_p, wv_p, q_tab, k_tab)


def _mla_flash_kernel(q_ref, kn_ref, kr_ref, v_ref, o_ref, m_ref, l_ref, acc_ref):
    qi = pl.program_id(2)
    tq = q_ref.shape[0]
    tk = tq // 2
    m_ref[...] = jnp.full_like(m_ref, NEG)
    l_ref[...] = jnp.zeros_like(l_ref)
    acc_ref[...] = jnp.zeros_like(acc_ref)

    def step(rows, k_start, diagonal):
        ks = pl.ds(pl.multiple_of(k_start, tk), tk)
        k = jnp.concatenate([kn_ref[ks, :], kr_ref[ks, :]], axis=1)
        s = lax.dot_general(q_ref[rows, :], k, (((1,), (1,)), ((), ())), preferred_element_type=F32)
        if diagonal:
            shape = s.shape
            s = jnp.where(lax.broadcasted_iota(jnp.int32, shape, 1) <= lax.broadcasted_iota(jnp.int32, shape, 0),
                          s, NEG)
        m_prev = m_ref[rows, :]
        m_new = jnp.maximum(m_prev, jnp.max(s, axis=-1, keepdims=True))
        alpha = jnp.exp2(m_prev - m_new)
        p = jnp.exp2(s - pltpu.repeat(m_new, tk // LANES, axis=1))
        l_ref[rows, :] = alpha * l_ref[rows, :] + jnp.sum(p, axis=-1, keepdims=True)
        acc_ref[rows, :] = alpha * acc_ref[rows, :] + jnp.dot(p.astype(BF16), v_ref[ks, :],
                                                              preferred_element_type=F32)
        m_ref[rows, :] = m_new

    def body(kb, carry):
        step(slice(0, tq), kb * tk, False)
        return carry

    lax.fori_loop(0, 2 * qi, body, 0)
    q0 = qi * tq
    step(slice(0, tk), q0, True)
    step(slice(tk, tq), q0, False)
    step(slice(tk, tq), q0 + tk, True)
    o_ref[...] = (acc_ref[...] / l_ref[...]).astype(o_ref.dtype)


def mla_flash(q, kn, kr, v, *, batch, seq, tq):
    nq = seq // tq
    return pl.pallas_call(
        _mla_flash_kernel,
        grid=(batch, C_HEADS, nq),
        in_specs=[
            pl.BlockSpec((tq, C_QW), lambda b, h, i: (b * nq + i, h)),
            pl.BlockSpec((seq, C_NOPE), lambda b, h, i: (b, h)),
            pl.BlockSpec((seq, LANES), lambda b, h, i: (b, 0)),
            pl.BlockSpec((seq, C_V), lambda b, h, i: (b, h)),
        ],
        out_specs=pl.BlockSpec((tq, C_V), lambda b, h, i: (b * nq + i, h)),
        out_shape=jax.ShapeDtypeStruct((batch * seq, C_HEADS * C_V), BF16),
        scratch_shapes=[
            pltpu.VMEM((tq, LANES), F32),
            pltpu.VMEM((tq, LANES), F32),
            pltpu.VMEM((tq, C_V), F32),
        ],
        compiler_params=_cparams("parallel", "parallel", "parallel"),
        name="mla_flash",
    )(q, kn, kr, v)


def _merge_kernel(oa_ref, ob_ref, oc_ref, wa_ref, wb_ref, wc_ref, g0_ref, g1_ref, g2_ref,
                  b0_ref, b1_ref, b2_ref, o_ref):
    def branch(o, w, g, b):
        return jax.nn.sigmoid(g[...] + b[...]) * jnp.dot(o[...], w[...], preferred_element_type=F32)

    acc = branch(oa_ref, wa_ref, g0_ref, b0_ref)
    acc = acc + branch(ob_ref, wb_ref, g1_ref, b1_ref)
    acc = acc + branch(oc_ref, wc_ref, g2_ref, b2_ref)
    o_ref[...] = acc.astype(o_ref.dtype)


def gated_merge(o_a, o_b, o_c, w_a, w_b, w_c, z, b_gate, *, tm, tn):
    m, k = o_a.shape
    d = w_a.shape[1]
    nj = d // tn
    act = pl.BlockSpec((tm, k), lambda i, j: (i, 0))
    wgt = pl.BlockSpec((k, tn), lambda i, j: (0, j))
    gate = lambda br: pl.BlockSpec((tm, tn), lambda i, j: (i, Z_GATES // tn + br * nj + j))
    gbias = lambda br: pl.BlockSpec((1, tn), lambda i, j: (0, br * nj + j))
    return pl.pallas_call(
        _merge_kernel,
        grid=(m // tm, nj),
        in_specs=[act, act, act, wgt, wgt, wgt, gate(0), gate(1), gate(2), gbias(0), gbias(1), gbias(2)],
        out_specs=pl.BlockSpec((tm, tn), lambda i, j: (i, j)),
        out_shape=jax.ShapeDtypeStruct((m, d), BF16),
        compiler_params=_cparams("parallel", "parallel"),
        name="gated_merge",
    )(o_a, o_b, o_c, w_a, w_b, w_c, z, z, z, *([b_gate.reshape(1, -1)] * 3))


def _cross_attn_kernel(q_ref, k_ref, v_ref, o_ref):
    s = lax.dot_general(q_ref[...], k_ref[...], (((1,), (1,)), ((), ())), preferred_element_type=F32)
    m = jnp.max(s, axis=-1, keepdims=True)
    e = jnp.exp(s - m)
    l = jnp.sum(e, axis=-1, keepdims=True)
    o_ref[...] = (jnp.dot(e.astype(BF16), v_ref[...], preferred_element_type=F32) / l).astype(o_ref.dtype)


def cross_attention(q, kv, *, batch, seq, mem_len, tq):
    d = q.shape[1]
    hd = d // X_HEADS
    nq = seq // tq
    return pl.pallas_call(
        _cross_attn_kernel,
        grid=(batch, nq, X_HEADS),
        in_specs=[
            pl.BlockSpec((tq, hd), lambda b, i, h: (b * nq + i, h)),
            pl.BlockSpec((mem_len, hd), lambda b, i, h: (b, h)),
            pl.BlockSpec((mem_len, hd), lambda b, i, h: (b, X_HEADS + h)),
        ],
        out_specs=pl.BlockSpec((tq, hd), lambda b, i, h: (b * nq + i, h)),
        out_shape=jax.ShapeDtypeStruct((batch * seq, d), BF16),
        compiler_params=_cparams("parallel", "parallel", "parallel"),
        name="cross_attention",
    )(q, kv, kv)


def _pack_w_in(w):
    d = w.shape[0]
    sizes = (1024, 1024, 1024, B_DK, B_DK, B_DV, B_GATE_RANK, B_DV, C_Q_RANK, C_KV_RANK, C_ROPE, N_BRANCH * d)
    offs = np.concatenate([[0], np.cumsum(sizes)])
    aq, ak, av, bq, bk, bv, lr, br, cqa, ckva, ckr, gates = [w[:, offs[i]:offs[i + 1]] for i in range(len(sizes))]
    zeros = lambda n: jnp.zeros((d, n), w.dtype)
    packed = jnp.concatenate(
        [aq, ak, av, bv, br, cqa, ckva, bq, bk, gates,
         lr, zeros(LANES - B_GATE_RANK), ckr, zeros(LANES - C_ROPE)], axis=1)
    packed = jnp.concatenate([packed, zeros(Z_WIDTH - packed.shape[1])], axis=1)
    return packed.astype(BF16)


def _pack_w_qb(w):
    r = w.shape[0]
    w = w.reshape(r, C_HEADS, C_NOPE + C_ROPE)
    w = jnp.pad(w, ((0, 0), (0, 0), (0, C_QW - C_NOPE - C_ROPE)))
    return w.reshape(r, C_HEADS * C_QW).astype(BF16)


def _pack_w_kvb(w):
    r = w.shape[0]
    w = w.reshape(r, C_HEADS, C_NOPE + C_V)
    wk = w[:, :, :C_NOPE].reshape(r, C_HEADS * C_NOPE)
    wv = w[:, :, C_NOPE:].reshape(r, C_HEADS * C_V)
    return wk.astype(BF16), wv.astype(BF16)


def kernel(x, mem, rel_bias, norm_mix, w_in, b_gate, w_alpha, b_alpha, gla_norm, q_a_norm, w_qb, kv_a_norm, w_kvb, w_up_a, w_up_b, w_up_c, w_o, norm_x, norm_mem, w_xq, w_xkv, w_xo, norm_ffn, w_ffn_gate, w_ffn_up, w_ffn_down, norm_final):
    batch, seq, d = x.shape
    mem_len = mem.shape[1]
    depth = w_in.shape[0]
    t = batch * seq
    assert d == 2048 and seq % A_TILE == 0, "tiling is derived for the stated shapes"

    xf = x.reshape(t, d)
    memf = mem.reshape(batch * mem_len, d)
    bias_tab = dilated_bias_table(rel_bias)
    q_tab, k_tab = _rope_tables(seq)
    x_scale = (d // X_HEADS) ** -0.5
    bf = lambda a: a.astype(BF16)

    for l in range(depth):
        z = norm_matmul(xf, norm_mix[l], _pack_w_in(w_in[l]), out_dtype=F32, tm=1024, tn=512)
        o_a = dilated_attention(z, bias_tab, batch=batch, seq=seq)
        w_alpha_p = jnp.pad(w_alpha[l], ((0, LANES - B_GATE_RANK), (0, 0)))
        o_b = gated_linear_attention(z, w_alpha_p, b_alpha[l], gla_norm[l], batch=batch, seq=seq, tc=512)
        wk_p, wv_p = _pack_w_kvb(w_kvb[l])
        cq, ckn, ckr, cv = mla_project(z, q_a_norm[l], kv_a_norm[l], _pack_w_qb(w_qb[l]), wk_p, wv_p,
                                       q_tab, k_tab, seq=seq, tm=512)
        o_c = mla_flash(cq, ckn, ckr, cv, batch=batch, seq=seq, tq=1024)
        merged = gated_merge(o_a, o_b, o_c, bf(w_up_a[l]), bf(w_up_b[l]), bf(w_up_c[l]), z, b_gate[l],
                             tm=512, tn=512)
        xf = matmul_residual(merged, bf(w_o[l]), xf, tm=1024, tn=512)
        xq = norm_matmul(xf, norm_x[l], bf(w_xq[l]), out_dtype=BF16, tm=1024, tn=512, scale=x_scale)
        xkv = norm_matmul(memf, norm_mem[l], bf(w_xkv[l]), out_dtype=BF16, tm=256, tn=512)
        xo = cross_attention(xq, xkv, batch=batch, seq=seq, mem_len=mem_len, tq=1024)
        xf = matmul_residual(xo, bf(w_xo[l]), xf, tm=1024, tn=512)
        act = norm_swiglu(xf, norm_ffn[l], bf(w_ffn_gate[l]), bf(w_ffn_up[l]), tm=1024, tn=512)
        xf = matmul_residual(act, bf(w_ffn_down[l]), xf, tm=512, tn=512)
    return rmsnorm_rows(xf, norm_final, tm=512).reshape(batch, seq, d)
```

```python
import functools
import math

import numpy as np
import jax
import jax.numpy as jnp
from jax import lax
from jax.experimental import pallas as pl
from jax.experimental.pallas import tpu as pltpu

F32 = jnp.float32
BF16 = jnp.bfloat16
EPS = 1e-6
NEG = -1e30

HEAD_DIM = 128
A_HEADS = 8
A_DILATIONS = (1, 4, 16)
A_BLOCK = 128
N_BUCKETS = 32
MAX_DISTANCE = 2048
B_HEADS = 4
B_DK = 512
B_DV = 1024
B_GATE_RANK = 16
B_GATE_TAU = 16.0
B_CHUNK = 64
C_HEADS = 8
C_Q_RANK = 512
C_KV_RANK = 512
C_NOPE = 128
C_ROPE = 64
C_V = 128
ROPE_THETA = 10000.0
X_HEADS = 4
N_BRANCH = 3

VMEM_LIMIT_BYTES = 56 * 1024 * 1024
LANES = 128

Z_AQ, Z_AK, Z_AV = 0, 1024, 2048
Z_BV, Z_BR = 3072, 4096
Z_CQA, Z_CKVA = 5120, 5632
Z_BQ, Z_BK = 6144, 6656
Z_LR = 7168
Z_CKR = 7296
Z_WIDTH = 7680
G_WIDTH = 3 * 2048
A_TILE = 2048


def _cparams(*sem):
    return pltpu.CompilerParams(dimension_semantics=sem, vmem_limit_bytes=VMEM_LIMIT_BYTES)


def _rmsnorm_rows(x_ref, g_ref, h_ref):
    rows = x_ref.shape[0]

    def body(i, carry):
        r = pl.multiple_of(i * 16, 16)
        x = x_ref[pl.ds(r, 16), :]
        ms = jnp.mean(x * x, axis=-1, keepdims=True)
        h_ref[pl.ds(r, 16), :] = (x * lax.rsqrt(ms + EPS) * g_ref[...]).astype(BF16)
        return carry

    lax.fori_loop(0, rows // 16, body, 0, unroll=8)


def _norm_matmul_kernel(x_ref, g_ref, w_ref, o_ref, h_ref, *, scale):
    @pl.when(pl.program_id(1) == 0)
    def _():
        _rmsnorm_rows(x_ref, g_ref, h_ref)

    acc = jnp.dot(h_ref[...], w_ref[...].astype(BF16), preferred_element_type=F32)
    if scale != 1.0:
        acc = acc * scale
    o_ref[...] = acc.astype(o_ref.dtype)


def _weight_spec(w, layer, tn):
    if w.ndim == 2:
        return pl.BlockSpec((w.shape[0], tn), lambda i, j: (0, j))
    return pl.BlockSpec((None, w.shape[1], tn), lambda i, j: (layer, 0, j))


def norm_matmul(x, g, w, *, out_dtype, tm, tn, scale=1.0, layer=None):
    m, k = x.shape
    n = w.shape[-1]
    return pl.pallas_call(
        functools.partial(_norm_matmul_kernel, scale=scale),
        grid=(m // tm, n // tn),
        in_specs=[
            pl.BlockSpec((tm, k), lambda i, j: (i, 0)),
            pl.BlockSpec((1, k), lambda i, j: (0, 0)),
            _weight_spec(w, layer, tn),
        ],
        out_specs=pl.BlockSpec((tm, tn), lambda i, j: (i, j)),
        out_shape=jax.ShapeDtypeStruct((m, n), out_dtype),
        scratch_shapes=[pltpu.VMEM((tm, k), BF16)],
        compiler_params=_cparams("parallel", "arbitrary"),
        name="norm_matmul",
    )(x, g.reshape(1, k), w)


def _w_in_kernel(x_ref, g_ref, w_ref, z_ref, gate_ref, h_ref, *, nz):
    j = pl.program_id(1)

    @pl.when(j == 0)
    def _():
        _rmsnorm_rows(x_ref, g_ref, h_ref)

    acc = jnp.dot(h_ref[...], w_ref[...], preferred_element_type=F32)

    @pl.when(j < nz)
    def _():
        z_ref[...] = acc

    @pl.when(j >= nz)
    def _():
        gate_ref[...] = acc.astype(gate_ref.dtype)


def input_projection(x, g, w, *, tm, tn):
    m, k = x.shape
    nz = Z_WIDTH // tn
    ng = G_WIDTH // tn
    return pl.pallas_call(
        functools.partial(_w_in_kernel, nz=nz),
        grid=(m // tm, nz + ng),
        in_specs=[
            pl.BlockSpec((tm, k), lambda i, j: (i, 0)),
            pl.BlockSpec((1, k), lambda i, j: (0, 0)),
            pl.BlockSpec((k, tn), lambda i, j: (0, j)),
        ],
        out_specs=[
            pl.BlockSpec((tm, tn), lambda i, j: (i, jnp.minimum(j, nz - 1))),
            pl.BlockSpec((tm, tn), lambda i, j: (i, jnp.maximum(j - nz, 0))),
        ],
        out_shape=[jax.ShapeDtypeStruct((m, Z_WIDTH), F32), jax.ShapeDtypeStruct((m, G_WIDTH), BF16)],
        scratch_shapes=[pltpu.VMEM((tm, k), BF16)],
        compiler_params=_cparams("parallel", "arbitrary"),
        name="input_projection",
    )(x, g.reshape(1, k), w)


def _norm_swiglu_kernel(x_ref, g_ref, wg_ref, wu_ref, o_ref, h_ref):
    @pl.when(pl.program_id(1) == 0)
    def _():
        _rmsnorm_rows(x_ref, g_ref, h_ref)

    h = h_ref[...]
    a = jnp.dot(h, wg_ref[...].astype(BF16), preferred_element_type=F32)
    b = jnp.dot(h, wu_ref[...].astype(BF16), preferred_element_type=F32)
    o_ref[...] = (a * jax.nn.sigmoid(a) * b).astype(o_ref.dtype)


def norm_swiglu(x, g, wg, wu, *, tm, tn, layer=None):
    m, k = x.shape
    n = wg.shape[-1]
    return pl.pallas_call(
        _norm_swiglu_kernel,
        grid=(m // tm, n // tn),
        in_specs=[
            pl.BlockSpec((tm, k), lambda i, j: (i, 0)),
            pl.BlockSpec((1, k), lambda i, j: (0, 0)),
            _weight_spec(wg, layer, tn),
            _weight_spec(wu, layer, tn),
        ],
        out_specs=pl.BlockSpec((tm, tn), lambda i, j: (i, j)),
        out_shape=jax.ShapeDtypeStruct((m, n), BF16),
        scratch_shapes=[pltpu.VMEM((tm, k), BF16)],
        compiler_params=_cparams("parallel", "arbitrary"),
        name="norm_swiglu",
    )(x, g.reshape(1, k), wg, wu)


def _matmul_residual_kernel(a_ref, w_ref, r_ref, o_ref):
    o_ref[...] = r_ref[...] + jnp.dot(a_ref[...], w_ref[...].astype(BF16), preferred_element_type=F32)


def matmul_residual(a, w, res, *, tm, tn, layer=None):
    m, k = a.shape
    n = w.shape[-1]
    return pl.pallas_call(
        _matmul_residual_kernel,
        grid=(m // tm, n // tn),
        in_specs=[
            pl.BlockSpec((tm, k), lambda i, j: (i, 0)),
            _weight_spec(w, layer, tn),
            pl.BlockSpec((tm, tn), lambda i, j: (i, j)),
        ],
        out_specs=pl.BlockSpec((tm, tn), lambda i, j: (i, j)),
        out_shape=jax.ShapeDtypeStruct((m, n), F32),
        compiler_params=_cparams("parallel", "parallel"),
        name="matmul_residual",
    )(a, w, res)


def _rmsnorm_kernel(x_ref, g_ref, o_ref):
    x = x_ref[...]
    ms = jnp.mean(x * x, axis=-1, keepdims=True)
    o_ref[...] = x * lax.rsqrt(ms + EPS) * g_ref[...]


def rmsnorm_rows(x, g, *, tm):
    m, k = x.shape
    return pl.pallas_call(
        _rmsnorm_kernel,
        grid=(m // tm,),
        in_specs=[pl.BlockSpec((tm, k), lambda i: (i, 0)), pl.BlockSpec((1, k), lambda i: (0, 0))],
        out_specs=pl.BlockSpec((tm, k), lambda i: (i, 0)),
        out_shape=jax.ShapeDtypeStruct((m, k), F32),
        compiler_params=_cparams("parallel"),
        name="final_rmsnorm",
    )(x, g.reshape(1, k))


def _t5_bucket_np(dist):
    max_exact = N_BUCKETS // 2
    n = np.maximum(dist, 1).astype(np.float64)
    large = max_exact + (np.log(n / max_exact) / math.log(MAX_DISTANCE / max_exact)
                         * (N_BUCKETS - max_exact)).astype(np.int32)
    large = np.minimum(large, N_BUCKETS - 1)
    return np.where(dist < max_exact, dist, large).astype(np.int32)


def _dilated_bucket_table():
    qi = np.arange(A_BLOCK)[:, None]
    kj = np.arange(2 * A_BLOCK)[None, :]
    steps_back = qi + A_BLOCK - kj
    valid = (steps_back >= 0) & (steps_back <= A_BLOCK)
    tabs = []
    for dil in A_DILATIONS:
        bucket = _t5_bucket_np(np.clip(steps_back, 0, A_BLOCK) * dil)
        tabs.append(np.where(valid, bucket, -1))
    return np.stack(tabs).astype(np.int32)


def _bias_table_kernel(idx_ref, rb_ref, o_ref):
    h = pl.program_id(1)
    idx = idx_ref[...]
    acc = jnp.full(idx.shape, NEG, F32)
    for b in range(N_BUCKETS):
        acc = jnp.where(idx == b, rb_ref[b, h], acc)
    o_ref[...] = acc


def dilated_bias_table(rel_bias):
    idx = jnp.asarray(_dilated_bucket_table())
    npat = len(A_DILATIONS)
    return pl.pallas_call(
        _bias_table_kernel,
        grid=(npat, A_HEADS),
        in_specs=[
            pl.BlockSpec((None, A_BLOCK, 2 * A_BLOCK), lambda p, h: (p, 0, 0)),
            pl.BlockSpec(memory_space=pltpu.SMEM),
        ],
        out_specs=pl.BlockSpec((None, None, A_BLOCK, 2 * A_BLOCK), lambda p, h: (p, h, 0, 0)),
        out_shape=jax.ShapeDtypeStruct((npat, A_HEADS, A_BLOCK, 2 * A_BLOCK), F32),
        compiler_params=_cparams("parallel", "parallel"),
        name="dilated_bias_table",
    )(idx, rel_bias)


def _dilated_kernel(q_ref, kc_ref, kp_ref, vc_ref, vp_ref, bias_ref, o_ref,
                    kcat, vcat, bm0, o_scr, lse_scr, *, scale):
    t = pl.program_id(2)
    tile = q_ref.shape[0]
    blk = A_BLOCK
    kcat[0:tile, :] = kp_ref[...]
    kcat[tile:2 * tile, :] = kc_ref[...]
    vcat[0:tile, :] = vp_ref[...]
    vcat[tile:2 * tile, :] = vc_ref[...]
    col = lax.broadcasted_iota(jnp.int32, (blk, 2 * blk), 1)
    no_prev = jnp.logical_and(col < blk, t == 0)
    for p in range(len(A_DILATIONS)):
        bm0[2 * p] = bias_ref[p]
        bm0[2 * p + 1] = jnp.where(no_prev, NEG, bias_ref[p])

    def block(p, dil, q_start, first):
        k_start = tile + q_start - dil * blk
        if dil == 1:
            q_idx = pl.ds(pl.multiple_of(q_start, blk), blk)
            k_idx = pl.ds(pl.multiple_of(k_start, blk), 2 * blk)
        else:
            q_idx = pl.ds(q_start, blk, stride=dil)
            k_idx = pl.ds(k_start, 2 * blk, stride=dil)
        q = (q_ref[q_idx, :] * scale).astype(BF16)
        k = kcat[k_idx, :].astype(BF16)
        v = vcat[k_idx, :].astype(BF16)
        bm = bm0[2 * p + first]
        s = lax.dot_general(q, k, (((1,), (1,)), ((), ())), preferred_element_type=F32)
        s = jnp.where(bm > 0.5 * NEG, s + bm, NEG)
        m = jnp.max(s, axis=-1, keepdims=True)
        e = jnp.exp(s - m)
        l = jnp.sum(e, axis=-1, keepdims=True)
        o = jnp.dot(e.astype(BF16), v, preferred_element_type=F32) / l
        lse = m + jnp.log(l)
        o_scr[p, q_idx, :] = o
        lse_scr[p, q_idx, :] = jnp.broadcast_to(lse, (blk, HEAD_DIM))

    for p, dil in enumerate(A_DILATIONS):
        nblk = tile // (blk * dil)

        def body(c, carry, p=p, dil=dil, nblk=nblk):
            r = lax.shift_right_logical(c, int(math.log2(nblk)))
            n = jnp.bitwise_and(c, nblk - 1)
            block(p, dil, r + dil * blk * n, (n == 0).astype(jnp.int32))
            return carry

        lax.fori_loop(0, dil * nblk, body, 0, unroll=8)

    def merge(i, carry):
        r = pl.multiple_of(i * blk, blk)
        rows = pl.ds(r, blk)
        l0, l1, l2 = lse_scr[0, rows, :], lse_scr[1, rows, :], lse_scr[2, rows, :]
        mx = jnp.maximum(jnp.maximum(l0, l1), l2)
        w0, w1, w2 = jnp.exp(l0 - mx), jnp.exp(l1 - mx), jnp.exp(l2 - mx)
        num = w0 * o_scr[0, rows, :] + w1 * o_scr[1, rows, :] + w2 * o_scr[2, rows, :]
        o_ref[rows, :] = (num / (w0 + w1 + w2)).astype(o_ref.dtype)
        return carry

    lax.fori_loop(0, tile // blk, merge, 0)


def dilated_attention(z, bias_tab, *, batch, seq):
    z3 = z.reshape(batch, seq, z.shape[1])
    nt = seq // A_TILE
    hq, hk, hv = Z_AQ // HEAD_DIM, Z_AK // HEAD_DIM, Z_AV // HEAD_DIM
    tile_spec = lambda col0, prev: pl.BlockSpec(
        (None, A_TILE, HEAD_DIM),
        (lambda b, h, t: (b, jnp.maximum(t - 1, 0), col0 + h)) if prev else (lambda b, h, t: (b, t, col0 + h)))
    npat = len(A_DILATIONS)
    out = pl.pallas_call(
        functools.partial(_dilated_kernel, scale=HEAD_DIM ** -0.5),
        grid=(batch, A_HEADS, nt),
        in_specs=[
            tile_spec(hq, False),
            tile_spec(hk, False), tile_spec(hk, True),
            tile_spec(hv, False), tile_spec(hv, True),
            pl.BlockSpec((npat, None, A_BLOCK, 2 * A_BLOCK), lambda b, h, t: (0, h, 0, 0)),
        ],
        out_specs=pl.BlockSpec((None, A_TILE, HEAD_DIM), lambda b, h, t: (b, t, h)),
        out_shape=jax.ShapeDtypeStruct((batch, seq, A_HEADS * HEAD_DIM), BF16),
        scratch_shapes=[
            pltpu.VMEM((2 * A_TILE, HEAD_DIM), F32),
            pltpu.VMEM((2 * A_TILE, HEAD_DIM), F32),
            pltpu.VMEM((2 * npat, A_BLOCK, 2 * A_BLOCK), F32),
            pltpu.VMEM((npat, A_TILE, HEAD_DIM), F32),
            pltpu.VMEM((npat, A_TILE, HEAD_DIM), F32),
        ],
        compiler_params=_cparams("parallel", "parallel", "arbitrary"),
        name="dilated_attention",
    )(z3, z3, z3, z3, z3, bias_tab)
    return out.reshape(batch * seq, A_HEADS * HEAD_DIM)


def _gla_kernel(q_ref, k_ref, v_ref, r_ref, lr_ref, wa_ref, ba_ref, gn_ref, o_ref, st_ref, *, scale):
    @pl.when(pl.program_id(1) == 0)
    def _():
        st_ref[...] = jnp.zeros_like(st_ref)

    c = B_CHUNK
    tc = q_ref.shape[0]
    dk = B_DK // B_HEADS
    dv = B_DV // B_HEADS
    pre = jnp.dot(lr_ref[...], wa_ref[...], precision=lax.Precision.HIGHEST,
                  preferred_element_type=F32) + ba_ref[...]
    log_a = (jnp.minimum(pre, 0.0) - jnp.log(1.0 + jnp.exp(-jnp.abs(pre)))) * (1.0 / B_GATE_TAU)
    tril = lax.broadcasted_iota(jnp.int32, (c, c), 0) >= lax.broadcasted_iota(jnp.int32, (c, c), 1)
    ones_tril = jnp.where(tril, 1.0, 0.0).astype(F32)
    for ci in range(tc // c):
        sl = slice(ci * c, (ci + 1) * c)
        cum_all = jnp.dot(ones_tril, log_a[sl, :], precision=lax.Precision.HIGHEST, preferred_element_type=F32)
        for h in range(B_HEADS):
            kcols = slice(h * dk, (h + 1) * dk)
            vcols = slice(h * dv, (h + 1) * dv)
            cum = cum_all[:, kcols]
            cum_last = cum[c - 1:c, :]
            q = q_ref[sl, kcols] * scale
            k = k_ref[sl, kcols]
            v = v_ref[sl, vcols]
            q_dec = (q * jnp.exp(cum)).astype(BF16)
            k_inv = (k * jnp.exp(-cum)).astype(BF16)
            k_state = (k * jnp.exp(cum_last - cum)).astype(BF16)
            att = lax.dot_general(q_dec, k_inv, (((1,), (1,)), ((), ())), preferred_element_type=F32)
            att = jnp.where(tril, att, 0.0).astype(BF16)
            st = st_ref[h]
            o = (jnp.dot(att, v.astype(BF16), preferred_element_type=F32)
                 + lax.dot_general(q_dec, st.astype(BF16), (((1,), (1,)), ((), ())), preferred_element_type=F32))
            st_ref[h] = st * jnp.exp(cum_last) + jnp.dot(v.T.astype(BF16), k_state, preferred_element_type=F32)
            ms = jnp.mean(o * o, axis=-1, keepdims=True)
            o = o * lax.rsqrt(ms + EPS) * gn_ref[:, vcols]
            r = r_ref[sl, vcols]
            o_ref[sl, vcols] = (o * (r * jax.nn.sigmoid(r))).astype(o_ref.dtype)


def gated_linear_attention(z, w_alpha_p, b_alpha, gla_norm, *, batch, seq, tc):
    dk = B_DK // B_HEADS
    dv = B_DV // B_HEADS
    nt = seq // tc
    row = lambda b, t: b * nt + t
    const = lambda b, t: (0, 0)
    return pl.pallas_call(
        functools.partial(_gla_kernel, scale=dk ** -0.5),
        grid=(batch, nt),
        in_specs=[
            pl.BlockSpec((tc, B_DK), lambda b, t: (row(b, t), Z_BQ // B_DK)),
            pl.BlockSpec((tc, B_DK), lambda b, t: (row(b, t), Z_BK // B_DK)),
            pl.BlockSpec((tc, B_DV), lambda b, t: (row(b, t), Z_BV // B_DV)),
            pl.BlockSpec((tc, B_DV), lambda b, t: (row(b, t), Z_BR // B_DV)),
            pl.BlockSpec((tc, LANES), lambda b, t: (row(b, t), Z_LR // LANES)),
            pl.BlockSpec((LANES, B_DK), const),
            pl.BlockSpec((1, B_DK), const),
            pl.BlockSpec((1, B_DV), const),
        ],
        out_specs=pl.BlockSpec((tc, B_DV), lambda b, t: (row(b, t), 0)),
        out_shape=jax.ShapeDtypeStruct((batch * seq, B_DV), BF16),
        scratch_shapes=[pltpu.VMEM((B_HEADS, dv, dk), F32)],
        compiler_params=_cparams("parallel", "arbitrary"),
        name="gated_linear_attention",
    )(z, z, z, z, z, w_alpha_p, b_alpha.reshape(1, B_DK), gla_norm.reshape(1, B_DV))


C_QW = 2 * C_NOPE


def _rope_tables(seq):
    pos = jnp.arange(seq, dtype=F32)
    inv = ROPE_THETA ** (-jnp.arange(0, C_ROPE, 2, dtype=F32) / C_ROPE)
    ang = pos[:, None] * inv[None, :]
    cos, sin = jnp.cos(ang), jnp.sin(ang)
    half = C_ROPE // 2
    z = lambda w: jnp.zeros((seq, w), F32)
    k_tab = jnp.stack([
        jnp.concatenate([cos, cos, z(LANES - C_ROPE)], axis=1),
        jnp.concatenate([-sin, z(half), z(LANES - C_ROPE)], axis=1),
        jnp.concatenate([z(half), sin, z(LANES - C_ROPE)], axis=1)])
    q_tab = jnp.concatenate([
        jnp.stack([jnp.ones((seq, C_NOPE), F32), z(C_NOPE), z(C_NOPE)]), k_tab], axis=2)
    return q_tab, k_tab


def _mla_proj_kernel(cq_ref, ckv_ref, ckr_ref, gq_ref, gkv_ref, wq_ref, wk_ref, wv_ref, qt_ref, kt_ref,
                     q_out, kn_out, kr_out, v_out, *, scale):
    half = C_ROPE // 2

    def rms(x, g):
        ms = jnp.mean(x * x, axis=-1, keepdims=True)
        return (x * lax.rsqrt(ms + EPS) * g).astype(BF16)

    cq = rms(cq_ref[...], gq_ref[...])
    q = jnp.dot(cq, wq_ref[...], preferred_element_type=F32)
    t0, t1, t2 = qt_ref[0], qt_ref[1], qt_ref[2]
    for h in range(C_HEADS):
        qh = q[:, h * C_QW:(h + 1) * C_QW]
        qh = qh * t0 + pltpu.roll(qh, C_QW - half, axis=1) * t1 + pltpu.roll(qh, half, axis=1) * t2
        q_out[:, h * C_QW:(h + 1) * C_QW] = (qh * scale).astype(q_out.dtype)
    ckv = rms(ckv_ref[...], gkv_ref[...])
    kn_out[...] = jnp.dot(ckv, wk_ref[...], preferred_element_type=F32).astype(kn_out.dtype)
    v_out[...] = jnp.dot(ckv, wv_ref[...], preferred_element_type=F32).astype(v_out.dtype)
    kr = ckr_ref[...]
    kr = kr * kt_ref[0] + pltpu.roll(kr, LANES - half, axis=1) * kt_ref[1] + pltpu.roll(kr, half, axis=1) * kt_ref[2]
    kr_out[...] = kr.astype(kr_out.dtype)


def mla_project(z, q_a_norm, kv_a_norm, wq_p, wk_p, wv_p, q_tab, k_tab, *, seq, tm):
    m = z.shape[0]
    nt = seq // tm
    const = lambda i: (0, 0)
    return pl.pallas_call(
        functools.partial(_mla_proj_kernel, scale=(C_NOPE + C_ROPE) ** -0.5 * math.log2(math.e)),
        grid=(m // tm,),
        in_specs=[
            pl.BlockSpec((tm, C_Q_RANK), lambda i: (i, Z_CQA // C_Q_RANK)),
            pl.BlockSpec((tm, C_KV_RANK), lambda i: (i, Z_CKVA // C_KV_RANK)),
            pl.BlockSpec((tm, LANES), lambda i: (i, Z_CKR // LANES)),
            pl.BlockSpec((1, C_Q_RANK), const),
            pl.BlockSpec((1, C_KV_RANK), const),
            pl.BlockSpec((C_Q_RANK, C_HEADS * C_QW), const),
            pl.BlockSpec((C_KV_RANK, C_HEADS * C_NOPE), const),
            pl.BlockSpec((C_KV_RANK, C_HEADS * C_V), const),
            pl.BlockSpec((3, tm, C_QW), lambda i: (0, i % nt, 0)),
            pl.BlockSpec((3, tm, LANES), lambda i: (0, i % nt, 0)),
        ],
        out_specs=[
            pl.BlockSpec((tm, C_HEADS * C_QW), lambda i: (i, 0)),
            pl.BlockSpec((tm, C_HEADS * C_NOPE), lambda i: (i, 0)),
            pl.BlockSpec((tm, LANES), lambda i: (i, 0)),
            pl.BlockSpec((tm, C_HEADS * C_V), lambda i: (i, 0)),
        ],
        out_shape=[
            jax.ShapeDtypeStruct((m, C_HEADS * C_QW), BF16),
            jax.ShapeDtypeStruct((m, C_HEADS * C_NOPE), BF16),
            jax.ShapeDtypeStruct((m, LANES), BF16),
            jax.ShapeDtypeStruct((m, C_HEADS * C_V), BF16),
        ],
        compiler_params=_cparams("parallel"),
        name="mla_project",
    )(z, z, z, q_a_norm.reshape(1, -1), kv_a_norm.reshape(1, -1), wq_p, wk_p, wv_p, q_tab, k_tab)


def _mla_flash_kernel(q_ref, kn_ref, kr_ref, v_ref, o_ref, m_ref, l_ref, acc_ref):
    qi = pl.program_id(2)
    tq = q_ref.shape[0]
    tk = tq // 2
    m_ref[...] = jnp.full_like(m_ref, NEG)
    l_ref[...] = jnp.zeros_like(l_ref)
    acc_ref[...] = jnp.zeros_like(acc_ref)

    def step(rows, k_start, diagonal):
        ks = pl.ds(pl.multiple_of(k_start, tk), tk)
        k = jnp.concatenate([kn_ref[ks, :], kr_ref[ks, :]], axis=1)
        s = lax.dot_general(q_ref[rows, :], k, (((1,), (1,)), ((), ())), preferred_element_type=F32)
        if diagonal:
            shape = s.shape
            s = jnp.where(lax.broadcasted_iota(jnp.int32, shape, 1) <= lax.broadcasted_iota(jnp.int32, shape, 0),
                          s, NEG)
        m_prev = m_ref[rows, :]
        m_new = jnp.maximum(m_prev, jnp.max(s, axis=-1, keepdims=True))
        alpha = jnp.exp2(m_prev - m_new)
        p = jnp.exp2(s - jnp.concatenate([m_new] * (tk // LANES), axis=1))
        l_ref[rows, :] = alpha * l_ref[rows, :] + jnp.sum(p, axis=-1, keepdims=True)
        acc_ref[rows, :] = alpha * acc_ref[rows, :] + jnp.dot(p.astype(BF16), v_ref[ks, :],
                                                              preferred_element_type=F32)
        m_ref[rows, :] = m_new

    def body(kb, carry):
        step(slice(0, tq), kb * tk, False)
        return carry

    lax.fori_loop(0, 2 * qi, body, 0)
    q0 = qi * tq
    step(slice(0, tk), q0, True)
    step(slice(tk, tq), q0, False)
    step(slice(tk, tq), q0 + tk, True)
    o_ref[...] = (acc_ref[...] / l_ref[...]).astype(o_ref.dtype)


def mla_flash(q, kn, kr, v, *, batch, seq, tq):
    nq = seq // tq
    return pl.pallas_call(
        _mla_flash_kernel,
        grid=(batch, C_HEADS, nq),
        in_specs=[
            pl.BlockSpec((tq, C_QW), lambda b, h, i: (b * nq + i, h)),
            pl.BlockSpec((seq, C_NOPE), lambda b, h, i: (b, h)),
            pl.BlockSpec((seq, LANES), lambda b, h, i: (b, 0)),
            pl.BlockSpec((seq, C_V), lambda b, h, i: (b, h)),
        ],
        out_specs=pl.BlockSpec((tq, C_V), lambda b, h, i: (b * nq + i, h)),
        out_shape=jax.ShapeDtypeStruct((batch * seq, C_HEADS * C_V), BF16),
        scratch_shapes=[
            pltpu.VMEM((tq, LANES), F32),
            pltpu.VMEM((tq, LANES), F32),
            pltpu.VMEM((tq, C_V), F32),
        ],
        compiler_params=_cparams("parallel", "parallel", "parallel"),
        name="mla_flash",
    )(q, kn, kr, v)


def _merge_kernel(oa_ref, ob_ref, oc_ref, wa_ref, wb_ref, wc_ref, g0_ref, g1_ref, g2_ref,
                  b0_ref, b1_ref, b2_ref, o_ref):
    def branch(o, w, g, b):
        return jax.nn.sigmoid(g[...] + b[...]) * jnp.dot(o[...], w[...], preferred_element_type=F32)

    acc = branch(oa_ref, wa_ref, g0_ref, b0_ref)
    acc = acc + branch(ob_ref, wb_ref, g1_ref, b1_ref)
    acc = acc + branch(oc_ref, wc_ref, g2_ref, b2_ref)
    o_ref[...] = acc.astype(o_ref.dtype)


def gated_merge(o_a, o_b, o_c, w_a, w_b, w_c, gates, b_gate, *, tm, tn):
    m, k = o_a.shape
    d = w_a.shape[1]
    nj = d // tn
    act = pl.BlockSpec((tm, k), lambda i, j: (i, 0))
    wgt = pl.BlockSpec((k, tn), lambda i, j: (0, j))
    gate = lambda br: pl.BlockSpec((tm, tn), lambda i, j: (i, br * nj + j))
    gbias = lambda br: pl.BlockSpec((1, tn), lambda i, j: (0, br * nj + j))
    return pl.pallas_call(
        _merge_kernel,
        grid=(m // tm, nj),
        in_specs=[act, act, act, wgt, wgt, wgt, gate(0), gate(1), gate(2), gbias(0), gbias(1), gbias(2)],
        out_specs=pl.BlockSpec((tm, tn), lambda i, j: (i, j)),
        out_shape=jax.ShapeDtypeStruct((m, d), BF16),
        compiler_params=_cparams("parallel", "parallel"),
        name="gated_merge",
    )(o_a, o_b, o_c, w_a, w_b, w_c, gates, gates, gates, *([b_gate.reshape(1, -1)] * 3))


def _cross_attn_kernel(q_ref, k_ref, v_ref, o_ref):
    s = lax.dot_general(q_ref[...], k_ref[...], (((1,), (1,)), ((), ())), preferred_element_type=F32)
    m = jnp.max(s, axis=-1, keepdims=True)
    e = jnp.exp(s - m)
    l = jnp.sum(e, axis=-1, keepdims=True)
    o_ref[...] = (jnp.dot(e.astype(BF16), v_ref[...], preferred_element_type=F32) / l).astype(o_ref.dtype)


def cross_attention(q, kv, *, batch, seq, mem_len, tq):
    d = q.shape[1]
    hd = d // X_HEADS
    nq = seq // tq
    return pl.pallas_call(
        _cross_attn_kernel,
        grid=(batch, nq, X_HEADS),
        in_specs=[
            pl.BlockSpec((tq, hd), lambda b, i, h: (b * nq + i, h)),
            pl.BlockSpec((mem_len, hd), lambda b, i, h: (b, h)),
            pl.BlockSpec((mem_len, hd), lambda b, i, h: (b, X_HEADS + h)),
        ],
        out_specs=pl.BlockSpec((tq, hd), lambda b, i, h: (b * nq + i, h)),
        out_shape=jax.ShapeDtypeStruct((batch * seq, d), BF16),
        compiler_params=_cparams("parallel", "parallel", "parallel"),
        name="cross_attention",
    )(q, kv, kv)


def _pack_w_in(w):
    d = w.shape[0]
    sizes = (1024, 1024, 1024, B_DK, B_DK, B_DV, B_GATE_RANK, B_DV, C_Q_RANK, C_KV_RANK, C_ROPE, N_BRANCH * d)
    offs = np.concatenate([[0], np.cumsum(sizes)])
    aq, ak, av, bq, bk, bv, lr, br, cqa, ckva, ckr, gates = [w[:, offs[i]:offs[i + 1]] for i in range(len(sizes))]
    zeros = lambda n: jnp.zeros((d, n), w.dtype)
    used = Z_CKR + LANES
    packed = jnp.concatenate(
        [aq, ak, av, bv, br, cqa, ckva, bq, bk,
         lr, zeros(LANES - B_GATE_RANK), ckr, zeros(LANES - C_ROPE), zeros(Z_WIDTH - used), gates], axis=1)
    return packed.astype(BF16)


def _pack_w_qb(w):
    r = w.shape[0]
    w = w.reshape(r, C_HEADS, C_NOPE + C_ROPE)
    w = jnp.pad(w, ((0, 0), (0, 0), (0, C_QW - C_NOPE - C_ROPE)))
    return w.reshape(r, C_HEADS * C_QW).astype(BF16)


def _pack_w_kvb(w):
    r = w.shape[0]
    w = w.reshape(r, C_HEADS, C_NOPE + C_V)
    wk = w[:, :, :C_NOPE].reshape(r, C_HEADS * C_NOPE)
    wv = w[:, :, C_NOPE:].reshape(r, C_HEADS * C_V)
    return wk.astype(BF16), wv.astype(BF16)


def kernel(x, mem, rel_bias, norm_mix, w_in, b_gate, w_alpha, b_alpha, gla_norm, q_a_norm, w_qb, kv_a_norm, w_kvb, w_up_a, w_up_b, w_up_c, w_o, norm_x, norm_mem, w_xq, w_xkv, w_xo, norm_ffn, w_ffn_gate, w_ffn_up, w_ffn_down, norm_final):
    batch, seq, d = x.shape
    mem_len = mem.shape[1]
    depth = w_in.shape[0]
    t = batch * seq
    assert d == 2048 and seq % A_TILE == 0, "tiling is derived for the stated shapes"

    xf = x.reshape(t, d)
    memf = mem.reshape(batch * mem_len, d)
    bias_tab = dilated_bias_table(rel_bias)
    q_tab, k_tab = _rope_tables(seq)
    x_scale = (d // X_HEADS) ** -0.5
    bf = lambda a: a.astype(BF16)

    for l in range(depth):
        z, gates = input_projection(xf, norm_mix[l], _pack_w_in(w_in[l]), tm=1024, tn=768)
        o_a = dilated_attention(z, bias_tab, batch=batch, seq=seq)
        w_alpha_p = jnp.pad(w_alpha[l], ((0, LANES - B_GATE_RANK), (0, 0)))
        o_b = gated_linear_attention(z, w_alpha_p, b_alpha[l], gla_norm[l], batch=batch, seq=seq, tc=512)
        wk_p, wv_p = _pack_w_kvb(w_kvb[l])
        cq, ckn, ckr, cv = mla_project(z, q_a_norm[l], kv_a_norm[l], _pack_w_qb(w_qb[l]), wk_p, wv_p,
                                       q_tab, k_tab, seq=seq, tm=512)
        o_c = mla_flash(cq, ckn, ckr, cv, batch=batch, seq=seq, tq=1024)
        merged = gated_merge(o_a, o_b, o_c, bf(w_up_a[l]), bf(w_up_b[l]), bf(w_up_c[l]), gates, b_gate[l],
                             tm=1024, tn=512)
        xf = matmul_residual(merged, w_o, xf, tm=1024, tn=512, layer=l)
        xq = norm_matmul(xf, norm_x[l], w_xq, out_dtype=BF16, tm=1024, tn=512, scale=x_scale, layer=l)
        xkv = norm_matmul(memf, norm_mem[l], w_xkv, out_dtype=BF16, tm=batch * mem_len, tn=512, layer=l)
        xo = cross_attention(xq, xkv, batch=batch, seq=seq, mem_len=mem_len, tq=1024)
        xf = matmul_residual(xo, w_xo, xf, tm=1024, tn=512, layer=l)
        act = norm_swiglu(xf, norm_ffn[l], w_ffn_gate, w_ffn_up, tm=1024, tn=512, layer=l)
        xf = matmul_residual(act, bf(w_ffn_down[l]), xf, tm=1024, tn=512)
    return rmsnorm_rows(xf, norm_final, tm=512).reshape(batch, seq, d)
```

```python
import functools
import math

import numpy as np
import jax
import jax.numpy as jnp
from jax import lax
from jax.experimental import pallas as pl
from jax.experimental.pallas import tpu as pltpu

F32 = jnp.float32
BF16 = jnp.bfloat16
EPS = 1e-6
NEG = -1e30

HEAD_DIM = 128
A_HEADS = 8
A_DILATIONS = (1, 4, 16)
A_BLOCK = 128
N_BUCKETS = 32
MAX_DISTANCE = 2048
B_HEADS = 4
B_DK = 512
B_DV = 1024
B_GATE_RANK = 16
B_GATE_TAU = 16.0
B_CHUNK = 64
C_HEADS = 8
C_Q_RANK = 512
C_KV_RANK = 512
C_NOPE = 128
C_ROPE = 64
C_V = 128
ROPE_THETA = 10000.0
X_HEADS = 4
N_BRANCH = 3

VMEM_LIMIT_BYTES = 56 * 1024 * 1024
LANES = 128

Z_AQ, Z_AK, Z_AV = 0, 1024, 2048
Z_BV, Z_BR = 3072, 4096
Z_CQA, Z_CKVA = 5120, 5632
Z_BQ, Z_BK = 6144, 6656
Z_LR = 7168
Z_CKR = 7296
Z_WIDTH = 7680
G_WIDTH = 3 * 2048
A_TILE = 2048


def _cparams(*sem):
    return pltpu.CompilerParams(dimension_semantics=sem, vmem_limit_bytes=VMEM_LIMIT_BYTES)


def _rmsnorm_rows(x_ref, g_ref, h_ref):
    rows = x_ref.shape[0]

    def body(i, carry):
        r = pl.multiple_of(i * 16, 16)
        x = x_ref[pl.ds(r, 16), :]
        ms = jnp.mean(x * x, axis=-1, keepdims=True)
        h_ref[pl.ds(r, 16), :] = (x * lax.rsqrt(ms + EPS) * g_ref[...]).astype(BF16)
        return carry

    lax.fori_loop(0, rows // 16, body, 0, unroll=8)


def _norm_matmul_kernel(x_ref, g_ref, w_ref, o_ref, h_ref, *, scale):
    @pl.when(pl.program_id(1) == 0)
    def _():
        _rmsnorm_rows(x_ref, g_ref, h_ref)

    acc = jnp.dot(h_ref[...], w_ref[...].astype(BF16), preferred_element_type=F32)
    if scale != 1.0:
        acc = acc * scale
    o_ref[...] = acc.astype(o_ref.dtype)


def _weight_spec(w, layer, tn):
    if w.ndim == 2:
        return pl.BlockSpec((w.shape[0], tn), lambda i, j: (0, j))
    return pl.BlockSpec((None, w.shape[1], tn), lambda i, j: (layer, 0, j))


def norm_matmul(x, g, w, *, out_dtype, tm, tn, scale=1.0, layer=None):
    m, k = x.shape
    n = w.shape[-1]
    return pl.pallas_call(
        functools.partial(_norm_matmul_kernel, scale=scale),
        grid=(m // tm, n // tn),
        in_specs=[
            pl.BlockSpec((tm, k), lambda i, j: (i, 0)),
            pl.BlockSpec((1, k), lambda i, j: (0, 0)),
            _weight_spec(w, layer, tn),
        ],
        out_specs=pl.BlockSpec((tm, tn), lambda i, j: (i, j)),
        out_shape=jax.ShapeDtypeStruct((m, n), out_dtype),
        scratch_shapes=[pltpu.VMEM((tm, k), BF16)],
        compiler_params=_cparams("parallel", "arbitrary"),
        name="norm_matmul",
    )(x, g.reshape(1, k), w)


def _w_in_kernel(x_ref, g_ref, w_ref, z_ref, gate_ref, h_ref, *, nz):
    j = pl.program_id(1)

    @pl.when(j == 0)
    def _():
        _rmsnorm_rows(x_ref, g_ref, h_ref)

    acc = jnp.dot(h_ref[...], w_ref[...], preferred_element_type=F32)

    @pl.when(j < nz)
    def _():
        z_ref[...] = acc

    @pl.when(j >= nz)
    def _():
        gate_ref[...] = acc.astype(gate_ref.dtype)


def input_projection(x, g, w, *, tm, tn):
    m, k = x.shape
    nz = Z_WIDTH // tn
    ng = G_WIDTH // tn
    return pl.pallas_call(
        functools.partial(_w_in_kernel, nz=nz),
        grid=(m // tm, nz + ng),
        in_specs=[
            pl.BlockSpec((tm, k), lambda i, j: (i, 0)),
            pl.BlockSpec((1, k), lambda i, j: (0, 0)),
            pl.BlockSpec((k, tn), lambda i, j: (0, j)),
        ],
        out_specs=[
            pl.BlockSpec((tm, tn), lambda i, j: (i, jnp.minimum(j, nz - 1))),
            pl.BlockSpec((tm, tn), lambda i, j: (i, jnp.maximum(j - nz, 0))),
        ],
        out_shape=[jax.ShapeDtypeStruct((m, Z_WIDTH), F32), jax.ShapeDtypeStruct((m, G_WIDTH), BF16)],
        scratch_shapes=[pltpu.VMEM((tm, k), BF16)],
        compiler_params=_cparams("parallel", "arbitrary"),
        name="input_projection",
    )(x, g.reshape(1, k), w)


def _norm_swiglu_kernel(x_ref, g_ref, wg_ref, wu_ref, o_ref, h_ref):
    @pl.when(pl.program_id(1) == 0)
    def _():
        _rmsnorm_rows(x_ref, g_ref, h_ref)

    h = h_ref[...]
    a = jnp.dot(h, wg_ref[...].astype(BF16), preferred_element_type=F32)
    b = jnp.dot(h, wu_ref[...].astype(BF16), preferred_element_type=F32)
    o_ref[...] = (a * jax.nn.sigmoid(a) * b).astype(o_ref.dtype)


def norm_swiglu(x, g, wg, wu, *, tm, tn, layer=None):
    m, k = x.shape
    n = wg.shape[-1]
    return pl.pallas_call(
        _norm_swiglu_kernel,
        grid=(m // tm, n // tn),
        in_specs=[
            pl.BlockSpec((tm, k), lambda i, j: (i, 0)),
            pl.BlockSpec((1, k), lambda i, j: (0, 0)),
            _weight_spec(wg, layer, tn),
            _weight_spec(wu, layer, tn),
        ],
        out_specs=pl.BlockSpec((tm, tn), lambda i, j: (i, j)),
        out_shape=jax.ShapeDtypeStruct((m, n), BF16),
        scratch_shapes=[pltpu.VMEM((tm, k), BF16)],
        compiler_params=_cparams("parallel", "arbitrary"),
        name="norm_swiglu",
    )(x, g.reshape(1, k), wg, wu)


def _matmul_residual_kernel(a_ref, w_ref, r_ref, o_ref):
    o_ref[...] = r_ref[...] + jnp.dot(a_ref[...], w_ref[...].astype(BF16), preferred_element_type=F32)


def matmul_residual(a, w, res, *, tm, tn, layer=None):
    m, k = a.shape
    n = w.shape[-1]
    return pl.pallas_call(
        _matmul_residual_kernel,
        grid=(m // tm, n // tn),
        in_specs=[
            pl.BlockSpec((tm, k), lambda i, j: (i, 0)),
            _weight_spec(w, layer, tn),
            pl.BlockSpec((tm, tn), lambda i, j: (i, j)),
        ],
        out_specs=pl.BlockSpec((tm, tn), lambda i, j: (i, j)),
        out_shape=jax.ShapeDtypeStruct((m, n), F32),
        compiler_params=_cparams("parallel", "parallel"),
        name="matmul_residual",
    )(a, w, res)


def _rmsnorm_kernel(x_ref, g_ref, o_ref):
    x = x_ref[...]
    ms = jnp.mean(x * x, axis=-1, keepdims=True)
    o_ref[...] = x * lax.rsqrt(ms + EPS) * g_ref[...]


def rmsnorm_rows(x, g, *, tm):
    m, k = x.shape
    return pl.pallas_call(
        _rmsnorm_kernel,
        grid=(m // tm,),
        in_specs=[pl.BlockSpec((tm, k), lambda i: (i, 0)), pl.BlockSpec((1, k), lambda i: (0, 0))],
        out_specs=pl.BlockSpec((tm, k), lambda i: (i, 0)),
        out_shape=jax.ShapeDtypeStruct((m, k), F32),
        compiler_params=_cparams("parallel"),
        name="final_rmsnorm",
    )(x, g.reshape(1, k))


def _t5_bucket_np(dist):
    max_exact = N_BUCKETS // 2
    n = np.maximum(dist, 1).astype(np.float64)
    large = max_exact + (np.log(n / max_exact) / math.log(MAX_DISTANCE / max_exact)
                         * (N_BUCKETS - max_exact)).astype(np.int32)
    large = np.minimum(large, N_BUCKETS - 1)
    return np.where(dist < max_exact, dist, large).astype(np.int32)


def _dilated_bucket_table():
    qi = np.arange(A_BLOCK)[:, None]
    kj = np.arange(2 * A_BLOCK)[None, :]
    steps_back = qi + A_BLOCK - kj
    valid = (steps_back >= 0) & (steps_back <= A_BLOCK)
    tabs = []
    for dil in A_DILATIONS:
        bucket = _t5_bucket_np(np.clip(steps_back, 0, A_BLOCK) * dil)
        tabs.append(np.where(valid, bucket, -1))
    return np.stack(tabs).astype(np.int32)


def _bias_table_kernel(idx_ref, rb_ref, o_ref):
    h = pl.program_id(1)
    idx = idx_ref[...]
    acc = jnp.full(idx.shape, NEG, F32)
    for b in range(N_BUCKETS):
        acc = jnp.where(idx == b, rb_ref[b, h], acc)
    o_ref[...] = acc


def dilated_bias_table(rel_bias):
    idx = jnp.asarray(_dilated_bucket_table())
    npat = len(A_DILATIONS)
    return pl.pallas_call(
        _bias_table_kernel,
        grid=(npat, A_HEADS),
        in_specs=[
            pl.BlockSpec((None, A_BLOCK, 2 * A_BLOCK), lambda p, h: (p, 0, 0)),
            pl.BlockSpec(memory_space=pltpu.SMEM),
        ],
        out_specs=pl.BlockSpec((None, None, A_BLOCK, 2 * A_BLOCK), lambda p, h: (p, h, 0, 0)),
        out_shape=jax.ShapeDtypeStruct((npat, A_HEADS, A_BLOCK, 2 * A_BLOCK), F32),
        compiler_params=_cparams("parallel", "parallel"),
        name="dilated_bias_table",
    )(idx, rel_bias)


def _dilated_kernel(q_ref, kc_ref, kp_ref, vc_ref, vp_ref, bias_ref, o_ref,
                    kcat, vcat, bm0, o_scr, lse_scr, *, scale):
    t = pl.program_id(2)
    tile = q_ref.shape[0]
    blk = A_BLOCK
    kcat[0:tile, :] = kp_ref[...]
    kcat[tile:2 * tile, :] = kc_ref[...]
    vcat[0:tile, :] = vp_ref[...]
    vcat[tile:2 * tile, :] = vc_ref[...]
    col = lax.broadcasted_iota(jnp.int32, (blk, 2 * blk), 1)
    no_prev = jnp.logical_and(col < blk, t == 0)
    for p in range(len(A_DILATIONS)):
        bm0[2 * p] = bias_ref[p]
        bm0[2 * p + 1] = jnp.where(no_prev, NEG, bias_ref[p])

    def block(p, dil, q_start, first):
        k_start = tile + q_start - dil * blk
        if dil == 1:
            q_idx = pl.ds(pl.multiple_of(q_start, blk), blk)
            k_idx = pl.ds(pl.multiple_of(k_start, blk), 2 * blk)
        else:
            q_idx = pl.ds(q_start, blk, stride=dil)
            k_idx = pl.ds(k_start, 2 * blk, stride=dil)
        q = (q_ref[q_idx, :] * scale).astype(BF16)
        k = kcat[k_idx, :].astype(BF16)
        v = vcat[k_idx, :].astype(BF16)
        bm = bm0[2 * p + first]
        s = lax.dot_general(q, k, (((1,), (1,)), ((), ())), preferred_element_type=F32)
        s = jnp.where(bm > 0.5 * NEG, s + bm, NEG)
        m = jnp.max(s, axis=-1, keepdims=True)
        e = jnp.exp(s - m)
        l = jnp.sum(e, axis=-1, keepdims=True)
        o = jnp.dot(e.astype(BF16), v, preferred_element_type=F32) / l
        lse = m + jnp.log(l)
        o_scr[p, q_idx, :] = o
        lse_scr[p, q_idx, :] = jnp.broadcast_to(lse, (blk, HEAD_DIM))

    for p, dil in enumerate(A_DILATIONS):
        nblk = tile // (blk * dil)

        def body(c, carry, p=p, dil=dil, nblk=nblk):
            r = lax.shift_right_logical(c, int(math.log2(nblk)))
            n = jnp.bitwise_and(c, nblk - 1)
            block(p, dil, r + dil * blk * n, (n == 0).astype(jnp.int32))
            return carry

        lax.fori_loop(0, dil * nblk, body, 0, unroll=8)

    def merge(i, carry):
        r = pl.multiple_of(i * blk, blk)
        rows = pl.ds(r, blk)
        l0, l1, l2 = lse_scr[0, rows, :], lse_scr[1, rows, :], lse_scr[2, rows, :]
        mx = jnp.maximum(jnp.maximum(l0, l1), l2)
        w0, w1, w2 = jnp.exp(l0 - mx), jnp.exp(l1 - mx), jnp.exp(l2 - mx)
        num = w0 * o_scr[0, rows, :] + w1 * o_scr[1, rows, :] + w2 * o_scr[2, rows, :]
        o_ref[rows, :] = (num / (w0 + w1 + w2)).astype(o_ref.dtype)
        return carry

    lax.fori_loop(0, tile // blk, merge, 0)


def dilated_attention(z, bias_tab, *, batch, seq):
    z3 = z.reshape(batch, seq, z.shape[1])
    nt = seq // A_TILE
    hq, hk, hv = Z_AQ // HEAD_DIM, Z_AK // HEAD_DIM, Z_AV // HEAD_DIM
    tile_spec = lambda col0, prev: pl.BlockSpec(
        (None, A_TILE, HEAD_DIM),
        (lambda b, h, t: (b, jnp.maximum(t - 1, 0), col0 + h)) if prev else (lambda b, h, t: (b, t, col0 + h)))
    npat = len(A_DILATIONS)
    out = pl.pallas_call(
        functools.partial(_dilated_kernel, scale=HEAD_DIM ** -0.5),
        grid=(batch, A_HEADS, nt),
        in_specs=[
            tile_spec(hq, False),
            tile_spec(hk, False), tile_spec(hk, True),
            tile_spec(hv, False), tile_spec(hv, True),
            pl.BlockSpec((npat, None, A_BLOCK, 2 * A_BLOCK), lambda b, h, t: (0, h, 0, 0)),
        ],
        out_specs=pl.BlockSpec((None, A_TILE, HEAD_DIM), lambda b, h, t: (b, t, h)),
        out_shape=jax.ShapeDtypeStruct((batch, seq, A_HEADS * HEAD_DIM), BF16),
        scratch_shapes=[
            pltpu.VMEM((2 * A_TILE, HEAD_DIM), F32),
            pltpu.VMEM((2 * A_TILE, HEAD_DIM), F32),
            pltpu.VMEM((2 * npat, A_BLOCK, 2 * A_BLOCK), F32),
            pltpu.VMEM((npat, A_TILE, HEAD_DIM), F32),
            pltpu.VMEM((npat, A_TILE, HEAD_DIM), F32),
        ],
        compiler_params=_cparams("parallel", "parallel", "arbitrary"),
        name="dilated_attention",
    )(z3, z3, z3, z3, z3, bias_tab)
    return out.reshape(batch * seq, A_HEADS * HEAD_DIM)


def _gla_kernel(q_ref, k_ref, v_ref, r_ref, lr_ref, wa_ref, ba_ref, gn_ref, o_ref, st_ref, *, scale):
    @pl.when(pl.program_id(1) == 0)
    def _():
        st_ref[...] = jnp.zeros_like(st_ref)

    c = B_CHUNK
    tc = q_ref.shape[0]
    dk = B_DK // B_HEADS
    dv = B_DV // B_HEADS
    pre = jnp.dot(lr_ref[...], wa_ref[...], precision=lax.Precision.HIGHEST,
                  preferred_element_type=F32) + ba_ref[...]
    log_a = (jnp.minimum(pre, 0.0) - jnp.log(1.0 + jnp.exp(-jnp.abs(pre)))) * (1.0 / B_GATE_TAU)
    tril = lax.broadcasted_iota(jnp.int32, (c, c), 0) >= lax.broadcasted_iota(jnp.int32, (c, c), 1)
    ones_tril = jnp.where(tril, 1.0, 0.0).astype(F32)
    for ci in range(tc // c):
        sl = slice(ci * c, (ci + 1) * c)
        cum_all = jnp.dot(ones_tril, log_a[sl, :], precision=lax.Precision.HIGHEST, preferred_element_type=F32)
        for h in range(B_HEADS):
            kcols = slice(h * dk, (h + 1) * dk)
            vcols = slice(h * dv, (h + 1) * dv)
            cum = cum_all[:, kcols]
            cum_last = cum[c - 1:c, :]
            q = q_ref[sl, kcols] * scale
            k = k_ref[sl, kcols]
            v = v_ref[sl, vcols]
            q_dec = (q * jnp.exp(cum)).astype(BF16)
            k_inv = (k * jnp.exp(-cum)).astype(BF16)
            k_state = (k * jnp.exp(cum_last - cum)).astype(BF16)
            att = lax.dot_general(q_dec, k_inv, (((1,), (1,)), ((), ())), preferred_element_type=F32)
            att = jnp.where(tril, att, 0.0).astype(BF16)
            st = st_ref[h]
            o = (jnp.dot(att, v.astype(BF16), preferred_element_type=F32)
                 + lax.dot_general(q_dec, st.astype(BF16), (((1,), (1,)), ((), ())), preferred_element_type=F32))
            st_ref[h] = st * jnp.exp(cum_last) + jnp.dot(v.T.astype(BF16), k_state, preferred_element_type=F32)
            ms = jnp.mean(o * o, axis=-1, keepdims=True)
            o = o * lax.rsqrt(ms + EPS) * gn_ref[:, vcols]
            r = r_ref[sl, vcols]
            o_ref[sl, vcols] = (o * (r * jax.nn.sigmoid(r))).astype(o_ref.dtype)


def gated_linear_attention(z, w_alpha_p, b_alpha, gla_norm, *, batch, seq, tc):
    dk = B_DK // B_HEADS
    dv = B_DV // B_HEADS
    nt = seq // tc
    row = lambda b, t: b * nt + t
    const = lambda b, t: (0, 0)
    return pl.pallas_call(
        functools.partial(_gla_kernel, scale=dk ** -0.5),
        grid=(batch, nt),
        in_specs=[
            pl.BlockSpec((tc, B_DK), lambda b, t: (row(b, t), Z_BQ // B_DK)),
            pl.BlockSpec((tc, B_DK), lambda b, t: (row(b, t), Z_BK // B_DK)),
            pl.BlockSpec((tc, B_DV), lambda b, t: (row(b, t), Z_BV // B_DV)),
            pl.BlockSpec((tc, B_DV), lambda b, t: (row(b, t), Z_BR // B_DV)),
            pl.BlockSpec((tc, LANES), lambda b, t: (row(b, t), Z_LR // LANES)),
            pl.BlockSpec((LANES, B_DK), const),
            pl.BlockSpec((1, B_DK), const),
            pl.BlockSpec((1, B_DV), const),
        ],
        out_specs=pl.BlockSpec((tc, B_DV), lambda b, t: (row(b, t), 0)),
        out_shape=jax.ShapeDtypeStruct((batch * seq, B_DV), BF16),
        scratch_shapes=[pltpu.VMEM((B_HEADS, dv, dk), F32)],
        compiler_params=_cparams("parallel", "arbitrary"),
        name="gated_linear_attention",
    )(z, z, z, z, z, w_alpha_p, b_alpha.reshape(1, B_DK), gla_norm.reshape(1, B_DV))


C_QW = 2 * C_NOPE


def _rope_tables(seq):
    pos = jnp.arange(seq, dtype=F32)
    inv = ROPE_THETA ** (-jnp.arange(0, C_ROPE, 2, dtype=F32) / C_ROPE)
    ang = pos[:, None] * inv[None, :]
    cos, sin = jnp.cos(ang), jnp.sin(ang)
    half = C_ROPE // 2
    z = lambda w: jnp.zeros((seq, w), F32)
    k_tab = jnp.stack([
        jnp.concatenate([cos, cos, z(LANES - C_ROPE)], axis=1),
        jnp.concatenate([-sin, z(half), z(LANES - C_ROPE)], axis=1),
        jnp.concatenate([z(half), sin, z(LANES - C_ROPE)], axis=1)])
    q_tab = jnp.concatenate([
        jnp.stack([jnp.ones((seq, C_NOPE), F32), z(C_NOPE), z(C_NOPE)]), k_tab], axis=2)
    return q_tab, k_tab


def _mla_proj_kernel(cq_ref, ckv_ref, ckr_ref, gq_ref, gkv_ref, wq_ref, wk_ref, wv_ref, qt_ref, kt_ref,
                     q_out, kn_out, kr_out, v_out, *, scale):
    half = C_ROPE // 2

    def rms(x, g):
        ms = jnp.mean(x * x, axis=-1, keepdims=True)
        return (x * lax.rsqrt(ms + EPS) * g).astype(BF16)

    cq = rms(cq_ref[...], gq_ref[...])
    q = jnp.dot(cq, wq_ref[...], preferred_element_type=F32)
    t0, t1, t2 = qt_ref[0], qt_ref[1], qt_ref[2]
    for h in range(C_HEADS):
        qh = q[:, h * C_QW:(h + 1) * C_QW]
        qh = qh * t0 + pltpu.roll(qh, C_QW - half, axis=1) * t1 + pltpu.roll(qh, half, axis=1) * t2
        q_out[:, h * C_QW:(h + 1) * C_QW] = (qh * scale).astype(q_out.dtype)
    ckv = rms(ckv_ref[...], gkv_ref[...])
    kn_out[...] = jnp.dot(ckv, wk_ref[...], preferred_element_type=F32).astype(kn_out.dtype)
    v_out[...] = jnp.dot(ckv, wv_ref[...], preferred_element_type=F32).astype(v_out.dtype)
    kr = ckr_ref[...]
    kr = kr * kt_ref[0] + pltpu.roll(kr, LANES - half, axis=1) * kt_ref[1] + pltpu.roll(kr, half, axis=1) * kt_ref[2]
    kr_out[...] = kr.astype(kr_out.dtype)


def mla_project(z, q_a_norm, kv_a_norm, wq_p, wk_p, wv_p, q_tab, k_tab, *, seq, tm):
    m = z.shape[0]
    nt = seq // tm
    const = lambda i: (0, 0)
    return pl.pallas_call(
        functools.partial(_mla_proj_kernel, scale=(C_NOPE + C_ROPE) ** -0.5 * math.log2(math.e)),
        grid=(m // tm,),
        in_specs=[
            pl.BlockSpec((tm, C_Q_RANK), lambda i: (i, Z_CQA // C_Q_RANK)),
            pl.BlockSpec((tm, C_KV_RANK), lambda i: (i, Z_CKVA // C_KV_RANK)),
            pl.BlockSpec((tm, LANES), lambda i: (i, Z_CKR // LANES)),
            pl.BlockSpec((1, C_Q_RANK), const),
            pl.BlockSpec((1, C_KV_RANK), const),
            pl.BlockSpec((C_Q_RANK, C_HEADS * C_QW), const),
            pl.BlockSpec((C_KV_RANK, C_HEADS * C_NOPE), const),
            pl.BlockSpec((C_KV_RANK, C_HEADS * C_V), const),
            pl.BlockSpec((3, tm, C_QW), lambda i: (0, i % nt, 0)),
            pl.BlockSpec((3, tm, LANES), lambda i: (0, i % nt, 0)),
        ],
        out_specs=[
            pl.BlockSpec((tm, C_HEADS * C_QW), lambda i: (i, 0)),
            pl.BlockSpec((tm, C_HEADS * C_NOPE), lambda i: (i, 0)),
            pl.BlockSpec((tm, LANES), lambda i: (i, 0)),
            pl.BlockSpec((tm, C_HEADS * C_V), lambda i: (i, 0)),
        ],
        out_shape=[
            jax.ShapeDtypeStruct((m, C_HEADS * C_QW), BF16),
            jax.ShapeDtypeStruct((m, C_HEADS * C_NOPE), BF16),
            jax.ShapeDtypeStruct((m, LANES), BF16),
            jax.ShapeDtypeStruct((m, C_HEADS * C_V), BF16),
        ],
        compiler_params=_cparams("parallel"),
        name="mla_project",
    )(z, z, z, q_a_norm.reshape(1, -1), kv_a_norm.reshape(1, -1), wq_p, wk_p, wv_p, q_tab, k_tab)


def _mla_flash_kernel(q_ref, kn_ref, kr_ref, v_ref, o_ref, m_ref, l_ref, acc_ref):
    qi = pl.program_id(2)
    tq = q_ref.shape[0]
    tk = tq // 2
    m_ref[...] = jnp.full_like(m_ref, NEG)
    l_ref[...] = jnp.zeros_like(l_ref)
    acc_ref[...] = jnp.zeros_like(acc_ref)

    def step(rows, k_start, diagonal):
        ks = pl.ds(pl.multiple_of(k_start, tk), tk)
        k = jnp.concatenate([kn_ref[ks, :], kr_ref[ks, :]], axis=1)
        s = lax.dot_general(q_ref[rows, :], k, (((1,), (1,)), ((), ())), preferred_element_type=F32)
        if diagonal:
            shape = s.shape
            s = jnp.where(lax.broadcasted_iota(jnp.int32, shape, 1) <= lax.broadcasted_iota(jnp.int32, shape, 0),
                          s, NEG)
        m_prev = m_ref[rows, :]
        m_new = jnp.maximum(m_prev, jnp.max(s, axis=-1, keepdims=True))
        alpha = jnp.exp2(m_prev - m_new)
        p = jnp.exp2(s - jnp.concatenate([m_new] * (tk // LANES), axis=1))
        l_ref[rows, :] = alpha * l_ref[rows, :] + jnp.sum(p, axis=-1, keepdims=True)
        acc_ref[rows, :] = alpha * acc_ref[rows, :] + jnp.dot(p.astype(BF16), v_ref[ks, :],
                                                              preferred_element_type=F32)
        m_ref[rows, :] = m_new

    def body(kb, carry):
        step(slice(0, tq), 2 * kb * tk, False)
        step(slice(0, tq), (2 * kb + 1) * tk, False)
        return carry

    lax.fori_loop(0, qi, body, 0)
    q0 = qi * tq
    step(slice(0, tk), q0, True)
    step(slice(tk, tq), q0, False)
    step(slice(tk, tq), q0 + tk, True)
    o_ref[...] = (acc_ref[...] / l_ref[...]).astype(o_ref.dtype)


def mla_flash(q, kn, kr, v, *, batch, seq, tq):
    nq = seq // tq
    return pl.pallas_call(
        _mla_flash_kernel,
        grid=(batch, C_HEADS, nq),
        in_specs=[
            pl.BlockSpec((tq, C_QW), lambda b, h, i: (b * nq + i, h)),
            pl.BlockSpec((seq, C_NOPE), lambda b, h, i: (b, h)),
            pl.BlockSpec((seq, LANES), lambda b, h, i: (b, 0)),
            pl.BlockSpec((seq, C_V), lambda b, h, i: (b, h)),
        ],
        out_specs=pl.BlockSpec((tq, C_V), lambda b, h, i: (b * nq + i, h)),
        out_shape=jax.ShapeDtypeStruct((batch * seq, C_HEADS * C_V), BF16),
        scratch_shapes=[
            pltpu.VMEM((tq, LANES), F32),
            pltpu.VMEM((tq, LANES), F32),
            pltpu.VMEM((tq, C_V), F32),
        ],
        compiler_params=_cparams("parallel", "parallel", "parallel"),
        name="mla_flash",
    )(q, kn, kr, v)


def _mixer_out_kernel(oa_ref, ob_ref, oc_ref, wa_ref, wb_ref, wc_ref, g0_ref, g1_ref, g2_ref,
                      b0_ref, b1_ref, b2_ref, wo_ref, xr_ref, o_ref, mg_ref):
    j = pl.program_id(1)
    nj, _, tn = mg_ref.shape

    def branch(o, w, g, b):
        return jax.nn.sigmoid(g[...] + b[...]) * jnp.dot(o[...], w[...], preferred_element_type=F32)

    @pl.when(j < nj)
    def _():
        acc = branch(oa_ref, wa_ref, g0_ref, b0_ref)
        acc = acc + branch(ob_ref, wb_ref, g1_ref, b1_ref)
        acc = acc + branch(oc_ref, wc_ref, g2_ref, b2_ref)
        mg_ref[j] = acc.astype(mg_ref.dtype)

    @pl.when(j >= nj)
    def _():
        acc = xr_ref[...]
        for c in range(nj):
            acc = acc + jnp.dot(mg_ref[c], wo_ref[c * tn:(c + 1) * tn, :].astype(BF16),
                                preferred_element_type=F32)
        o_ref[...] = acc


def mixer_output_block(x, o_a, o_b, o_c, w_a, w_b, w_c, gates, b_gate, w_o, *, tm, tn):
    m, k = o_a.shape
    d = w_a.shape[1]
    nj = d // tn
    mcol = lambda j: jnp.minimum(j, nj - 1)
    ocol = lambda j: jnp.maximum(j - nj, 0)
    act = pl.BlockSpec((tm, k), lambda i, j: (i, 0))
    wgt = pl.BlockSpec((k, tn), lambda i, j: (0, mcol(j)))
    gate = lambda br: pl.BlockSpec((tm, tn), lambda i, j: (i, br * nj + mcol(j)))
    gbias = lambda br: pl.BlockSpec((1, tn), lambda i, j: (0, br * nj + mcol(j)))
    return pl.pallas_call(
        _mixer_out_kernel,
        grid=(m // tm, 2 * nj),
        in_specs=[act, act, act, wgt, wgt, wgt, gate(0), gate(1), gate(2), gbias(0), gbias(1), gbias(2),
                  pl.BlockSpec((d, tn), lambda i, j: (0, ocol(j))),
                  pl.BlockSpec((tm, tn), lambda i, j: (i, ocol(j)))],
        out_specs=pl.BlockSpec((tm, tn), lambda i, j: (i, ocol(j))),
        out_shape=jax.ShapeDtypeStruct((m, d), F32),
        scratch_shapes=[pltpu.VMEM((nj, tm, tn), BF16)],
        compiler_params=_cparams("parallel", "arbitrary"),
        name="mixer_output_block",
    )(o_a, o_b, o_c, w_a, w_b, w_c, gates, gates, gates, *([b_gate.reshape(1, -1)] * 3), w_o, x)


def _cross_block_kernel(x_ref, g_ref, wq_ref, k_ref, v_ref, wo_ref, xr_ref, o_ref, h_ref, ao_ref, *, scale):
    j = pl.program_id(1)
    hd = k_ref.shape[1]

    @pl.when(j == 0)
    def _():
        _rmsnorm_rows(x_ref, g_ref, h_ref)

    @pl.when(j < X_HEADS)
    def _():
        q = jnp.dot(h_ref[...], wq_ref[...].astype(BF16), preferred_element_type=F32) * scale
        s = lax.dot_general(q.astype(BF16), k_ref[...], (((1,), (1,)), ((), ())), preferred_element_type=F32)
        m = jnp.max(s, axis=-1, keepdims=True)
        e = jnp.exp(s - m)
        l = jnp.sum(e, axis=-1, keepdims=True)
        o = jnp.dot(e.astype(BF16), v_ref[...], preferred_element_type=F32) / l
        ao_ref[j] = o.astype(ao_ref.dtype)

    @pl.when(j >= X_HEADS)
    def _():
        acc = xr_ref[...]
        for h in range(X_HEADS):
            acc = acc + jnp.dot(ao_ref[h], wo_ref[h * hd:(h + 1) * hd, :].astype(BF16),
                                preferred_element_type=F32)
        o_ref[...] = acc


def cross_attention_block(x, g, w_q, kv, w_o, *, batch, seq, mem_len, tm):
    m, d = x.shape
    hd = d // X_HEADS
    tiles_per_batch = seq // tm
    head = lambda j: jnp.minimum(j, X_HEADS - 1)
    col = lambda j: jnp.maximum(j - X_HEADS, 0)
    return pl.pallas_call(
        functools.partial(_cross_block_kernel, scale=hd ** -0.5),
        grid=(m // tm, 2 * X_HEADS),
        in_specs=[
            pl.BlockSpec((tm, d), lambda i, j: (i, 0)),
            pl.BlockSpec((1, d), lambda i, j: (0, 0)),
            pl.BlockSpec((d, hd), lambda i, j: (0, head(j))),
            pl.BlockSpec((mem_len, hd), lambda i, j: (i // tiles_per_batch, head(j))),
            pl.BlockSpec((mem_len, hd), lambda i, j: (i // tiles_per_batch, X_HEADS + head(j))),
            pl.BlockSpec((d, hd), lambda i, j: (0, col(j))),
            pl.BlockSpec((tm, hd), lambda i, j: (i, col(j))),
        ],
        out_specs=pl.BlockSpec((tm, hd), lambda i, j: (i, col(j))),
        out_shape=jax.ShapeDtypeStruct((m, d), F32),
        scratch_shapes=[pltpu.VMEM((tm, d), BF16), pltpu.VMEM((X_HEADS, tm, hd), BF16)],
        compiler_params=_cparams("parallel", "arbitrary"),
        name="cross_attention_block",
    )(x, g.reshape(1, d), w_q, kv, kv, w_o, x)


def _pack_w_in(w):
    d = w.shape[0]
    sizes = (1024, 1024, 1024, B_DK, B_DK, B_DV, B_GATE_RANK, B_DV, C_Q_RANK, C_KV_RANK, C_ROPE, N_BRANCH * d)
    offs = np.concatenate([[0], np.cumsum(sizes)])
    aq, ak, av, bq, bk, bv, lr, br, cqa, ckva, ckr, gates = [w[:, offs[i]:offs[i + 1]] for i in range(len(sizes))]
    zeros = lambda n: jnp.zeros((d, n), w.dtype)
    used = Z_CKR + LANES
    packed = jnp.concatenate(
        [aq, ak, av, bv, br, cqa, ckva, bq, bk,
         lr, zeros(LANES - B_GATE_RANK), ckr, zeros(LANES - C_ROPE), zeros(Z_WIDTH - used), gates], axis=1)
    return packed.astype(BF16)


def _pack_w_qb(w):
    r = w.shape[0]
    w = w.reshape(r, C_HEADS, C_NOPE + C_ROPE)
    w = jnp.pad(w, ((0, 0), (0, 0), (0, C_QW - C_NOPE - C_ROPE)))
    return w.reshape(r, C_HEADS * C_QW).astype(BF16)


def _pack_w_kvb(w):
    r = w.shape[0]
    w = w.reshape(r, C_HEADS, C_NOPE + C_V)
    wk = w[:, :, :C_NOPE].reshape(r, C_HEADS * C_NOPE)
    wv = w[:, :, C_NOPE:].reshape(r, C_HEADS * C_V)
    return wk.astype(BF16), wv.astype(BF16)


def kernel(x, mem, rel_bias, norm_mix, w_in, b_gate, w_alpha, b_alpha, gla_norm, q_a_norm, w_qb, kv_a_norm, w_kvb, w_up_a, w_up_b, w_up_c, w_o, norm_x, norm_mem, w_xq, w_xkv, w_xo, norm_ffn, w_ffn_gate, w_ffn_up, w_ffn_down, norm_final):
    batch, seq, d = x.shape
    mem_len = mem.shape[1]
    depth = w_in.shape[0]
    t = batch * seq
    assert d == 2048 and seq % A_TILE == 0, "tiling is derived for the stated shapes"

    xf = x.reshape(t, d)
    memf = mem.reshape(batch * mem_len, d)
    bias_tab = dilated_bias_table(rel_bias)
    q_tab, k_tab = _rope_tables(seq)
    bf = lambda a: a.astype(BF16)

    for l in range(depth):
        z, gates = input_projection(xf, norm_mix[l], _pack_w_in(w_in[l]), tm=1024, tn=768)
        o_a = dilated_attention(z, bias_tab, batch=batch, seq=seq)
        w_alpha_p = jnp.pad(w_alpha[l], ((0, LANES - B_GATE_RANK), (0, 0)))
        o_b = gated_linear_attention(z, w_alpha_p, b_alpha[l], gla_norm[l], batch=batch, seq=seq, tc=512)
        wk_p, wv_p = _pack_w_kvb(w_kvb[l])
        cq, ckn, ckr, cv = mla_project(z, q_a_norm[l], kv_a_norm[l], _pack_w_qb(w_qb[l]), wk_p, wv_p,
                                       q_tab, k_tab, seq=seq, tm=512)
        o_c = mla_flash(cq, ckn, ckr, cv, batch=batch, seq=seq, tq=1024)
        xf = mixer_output_block(xf, o_a, o_b, o_c, bf(w_up_a[l]), bf(w_up_b[l]), bf(w_up_c[l]), gates, b_gate[l],
                                bf(w_o[l]), tm=1024, tn=512)
        xkv = norm_matmul(memf, norm_mem[l], w_xkv, out_dtype=BF16, tm=batch * mem_len, tn=512, layer=l)
        xf = cross_attention_block(xf, norm_x[l], bf(w_xq[l]), xkv, bf(w_xo[l]),
                                   batch=batch, seq=seq, mem_len=mem_len, tm=1024)
        act = norm_swiglu(xf, norm_ffn[l], w_ffn_gate, w_ffn_up, tm=1024, tn=512, layer=l)
        xf = matmul_residual(act, bf(w_ffn_down[l]), xf, tm=1024, tn=512)
    return rmsnorm_rows(xf, norm_final, tm=512).reshape(batch, seq, d)
```

```python
import functools
import math

import numpy as np
import jax
import jax.numpy as jnp
from jax import lax
from jax.experimental import pallas as pl
from jax.experimental.pallas import tpu as pltpu

F32 = jnp.float32
BF16 = jnp.bfloat16
EPS = 1e-6
NEG = -1e30

HEAD_DIM = 128
A_HEADS = 8
A_DILATIONS = (1, 4, 16)
A_BLOCK = 128
N_BUCKETS = 32
MAX_DISTANCE = 2048
B_HEADS = 4
B_DK = 512
B_DV = 1024
B_GATE_RANK = 16
B_GATE_TAU = 16.0
B_CHUNK = 64
C_HEADS = 8
C_Q_RANK = 512
C_KV_RANK = 512
C_NOPE = 128
C_ROPE = 64
C_V = 128
ROPE_THETA = 10000.0
X_HEADS = 4
N_BRANCH = 3

VMEM_LIMIT_BYTES = 56 * 1024 * 1024
LANES = 128

Z_AQ, Z_AK, Z_AV = 0, 1024, 2048
Z_BV, Z_BR = 3072, 4096
Z_CQA, Z_CKVA = 5120, 5632
Z_BQ, Z_BK = 6144, 6656
Z_LR = 7168
Z_CKR = 7296
Z_WIDTH = 7680
G_WIDTH = 3 * 2048
A_TILE = 2048


def _cparams(*sem):
    return pltpu.CompilerParams(dimension_semantics=sem, vmem_limit_bytes=VMEM_LIMIT_BYTES)


def _rmsnorm_rows(x_ref, g_ref, h_ref):
    rows = x_ref.shape[0]

    def body(i, carry):
        r = pl.multiple_of(i * 16, 16)
        x = x_ref[pl.ds(r, 16), :]
        ms = jnp.mean(x * x, axis=-1, keepdims=True)
        h_ref[pl.ds(r, 16), :] = (x * lax.rsqrt(ms + EPS) * g_ref[...]).astype(BF16)
        return carry

    lax.fori_loop(0, rows // 16, body, 0, unroll=8)


def _norm_matmul_kernel(x_ref, g_ref, w_ref, o_ref, h_ref, *, scale):
    @pl.when(pl.program_id(1) == 0)
    def _():
        _rmsnorm_rows(x_ref, g_ref, h_ref)

    acc = jnp.dot(h_ref[...], w_ref[...].astype(BF16), preferred_element_type=F32)
    if scale != 1.0:
        acc = acc * scale
    o_ref[...] = acc.astype(o_ref.dtype)


def _weight_spec(w, layer, tn):
    if w.ndim == 2:
        return pl.BlockSpec((w.shape[0], tn), lambda i, j: (0, j))
    return pl.BlockSpec((None, w.shape[1], tn), lambda i, j: (layer, 0, j))


def norm_matmul(x, g, w, *, out_dtype, tm, tn, scale=1.0, layer=None):
    m, k = x.shape
    n = w.shape[-1]
    return pl.pallas_call(
        functools.partial(_norm_matmul_kernel, scale=scale),
        grid=(m // tm, n // tn),
        in_specs=[
            pl.BlockSpec((tm, k), lambda i, j: (i, 0)),
            pl.BlockSpec((1, k), lambda i, j: (0, 0)),
            _weight_spec(w, layer, tn),
        ],
        out_specs=pl.BlockSpec((tm, tn), lambda i, j: (i, j)),
        out_shape=jax.ShapeDtypeStruct((m, n), out_dtype),
        scratch_shapes=[pltpu.VMEM((tm, k), BF16)],
        compiler_params=_cparams("parallel", "arbitrary"),
        name="norm_matmul",
    )(x, g.reshape(1, k), w)


def _w_in_kernel(x_ref, g_ref, w_ref, z_ref, gate_ref, h_ref, *, nz):
    j = pl.program_id(1)

    @pl.when(j == 0)
    def _():
        _rmsnorm_rows(x_ref, g_ref, h_ref)

    acc = jnp.dot(h_ref[...], w_ref[...], preferred_element_type=F32)

    @pl.when(j < nz)
    def _():
        z_ref[...] = acc

    @pl.when(j >= nz)
    def _():
        gate_ref[...] = acc.astype(gate_ref.dtype)


def input_projection(x, g, w, *, tm, tn):
    m, k = x.shape
    nz = Z_WIDTH // tn
    ng = G_WIDTH // tn
    return pl.pallas_call(
        functools.partial(_w_in_kernel, nz=nz),
        grid=(m // tm, nz + ng),
        in_specs=[
            pl.BlockSpec((tm, k), lambda i, j: (i, 0)),
            pl.BlockSpec((1, k), lambda i, j: (0, 0)),
            pl.BlockSpec((k, tn), lambda i, j: (0, j)),
        ],
        out_specs=[
            pl.BlockSpec((tm, tn), lambda i, j: (i, jnp.minimum(j, nz - 1))),
            pl.BlockSpec((tm, tn), lambda i, j: (i, jnp.maximum(j - nz, 0))),
        ],
        out_shape=[jax.ShapeDtypeStruct((m, Z_WIDTH), F32), jax.ShapeDtypeStruct((m, G_WIDTH), BF16)],
        scratch_shapes=[pltpu.VMEM((tm, k), BF16)],
        compiler_params=_cparams("parallel", "arbitrary"),
        name="input_projection",
    )(x, g.reshape(1, k), w)


def _norm_swiglu_kernel(x_ref, g_ref, wg_ref, wu_ref, o_ref, h_ref):
    @pl.when(pl.program_id(1) == 0)
    def _():
        _rmsnorm_rows(x_ref, g_ref, h_ref)

    h = h_ref[...]
    a = jnp.dot(h, wg_ref[...].astype(BF16), preferred_element_type=F32)
    b = jnp.dot(h, wu_ref[...].astype(BF16), preferred_element_type=F32)
    o_ref[...] = (a * jax.nn.sigmoid(a) * b).astype(o_ref.dtype)


def norm_swiglu(x, g, wg, wu, *, tm, tn, layer=None):
    m, k = x.shape
    n = wg.shape[-1]
    return pl.pallas_call(
        _norm_swiglu_kernel,
        grid=(m // tm, n // tn),
        in_specs=[
            pl.BlockSpec((tm, k), lambda i, j: (i, 0)),
            pl.BlockSpec((1, k), lambda i, j: (0, 0)),
            _weight_spec(wg, layer, tn),
            _weight_spec(wu, layer, tn),
        ],
        out_specs=pl.BlockSpec((tm, tn), lambda i, j: (i, j)),
        out_shape=jax.ShapeDtypeStruct((m, n), BF16),
        scratch_shapes=[pltpu.VMEM((tm, k), BF16)],
        compiler_params=_cparams("parallel", "arbitrary"),
        name="norm_swiglu",
    )(x, g.reshape(1, k), wg, wu)


def _matmul_residual_kernel(a_ref, w_ref, r_ref, o_ref):
    o_ref[...] = r_ref[...] + jnp.dot(a_ref[...], w_ref[...].astype(BF16), preferred_element_type=F32)


def matmul_residual(a, w, res, *, tm, tn, layer=None):
    m, k = a.shape
    n = w.shape[-1]
    return pl.pallas_call(
        _matmul_residual_kernel,
        grid=(m // tm, n // tn),
        in_specs=[
            pl.BlockSpec((tm, k), lambda i, j: (i, 0)),
            _weight_spec(w, layer, tn),
            pl.BlockSpec((tm, tn), lambda i, j: (i, j)),
        ],
        out_specs=pl.BlockSpec((tm, tn), lambda i, j: (i, j)),
        out_shape=jax.ShapeDtypeStruct((m, n), F32),
        compiler_params=_cparams("parallel", "parallel"),
        name="matmul_residual",
    )(a, w, res)


def _rmsnorm_kernel(x_ref, g_ref, o_ref):
    x = x_ref[...]
    ms = jnp.mean(x * x, axis=-1, keepdims=True)
    o_ref[...] = x * lax.rsqrt(ms + EPS) * g_ref[...]


def rmsnorm_rows(x, g, *, tm):
    m, k = x.shape
    return pl.pallas_call(
        _rmsnorm_kernel,
        grid=(m // tm,),
        in_specs=[pl.BlockSpec((tm, k), lambda i: (i, 0)), pl.BlockSpec((1, k), lambda i: (0, 0))],
        out_specs=pl.BlockSpec((tm, k), lambda i: (i, 0)),
        out_shape=jax.ShapeDtypeStruct((m, k), F32),
        compiler_params=_cparams("parallel"),
        name="final_rmsnorm",
    )(x, g.reshape(1, k))


def _t5_bucket_np(dist):
    max_exact = N_BUCKETS // 2
    n = np.maximum(dist, 1).astype(np.float64)
    large = max_exact + (np.log(n / max_exact) / math.log(MAX_DISTANCE / max_exact)
                         * (N_BUCKETS - max_exact)).astype(np.int32)
    large = np.minimum(large, N_BUCKETS - 1)
    return np.where(dist < max_exact, dist, large).astype(np.int32)


def _dilated_bucket_table():
    qi = np.arange(A_BLOCK)[:, None]
    kj = np.arange(2 * A_BLOCK)[None, :]
    steps_back = qi + A_BLOCK - kj
    valid = (steps_back >= 0) & (steps_back <= A_BLOCK)
    tabs = []
    for dil in A_DILATIONS:
        bucket = _t5_bucket_np(np.clip(steps_back, 0, A_BLOCK) * dil)
        tabs.append(np.where(valid, bucket, -1))
    return np.stack(tabs).astype(np.int32)


def _bias_table_kernel(idx_ref, rb_ref, o_ref):
    h = pl.program_id(1)
    idx = idx_ref[...]
    acc = jnp.full(idx.shape, NEG, F32)
    for b in range(N_BUCKETS):
        acc = jnp.where(idx == b, rb_ref[b, h], acc)
    o_ref[...] = acc


def dilated_bias_table(rel_bias):
    idx = jnp.asarray(_dilated_bucket_table())
    npat = len(A_DILATIONS)
    return pl.pallas_call(
        _bias_table_kernel,
        grid=(npat, A_HEADS),
        in_specs=[
            pl.BlockSpec((None, A_BLOCK, 2 * A_BLOCK), lambda p, h: (p, 0, 0)),
            pl.BlockSpec(memory_space=pltpu.SMEM),
        ],
        out_specs=pl.BlockSpec((None, None, A_BLOCK, 2 * A_BLOCK), lambda p, h: (p, h, 0, 0)),
        out_shape=jax.ShapeDtypeStruct((npat, A_HEADS, A_BLOCK, 2 * A_BLOCK), F32),
        compiler_params=_cparams("parallel", "parallel"),
        name="dilated_bias_table",
    )(idx, rel_bias)


def _dilated_kernel(q_ref, kc_ref, kp_ref, vc_ref, vp_ref, bias_ref, o_ref, bm0, o_scr, lse_scr, *, scale):
    t = pl.program_id(2)
    tile = q_ref.shape[0]
    blk = A_BLOCK
    col = lax.broadcasted_iota(jnp.int32, (blk, 2 * blk), 1)
    no_prev = jnp.logical_and(col < blk, t == 0)
    for p in range(len(A_DILATIONS)):
        bm0[p] = jnp.where(no_prev, NEG, bias_ref[p])

    def rows(start, size, dil):
        return pl.ds(start, size) if dil == 1 else pl.ds(start, size, stride=dil)

    def block(p, dil, r, n):
        q_idx = rows(r + dil * blk * n, blk, dil)
        if n == 0:
            prev = rows(tile - dil * blk + r, blk, dil)
            cur = rows(r, blk, dil)
            k = jnp.concatenate([kp_ref[prev, :], kc_ref[cur, :]], axis=0)
            v = jnp.concatenate([vp_ref[prev, :], vc_ref[cur, :]], axis=0)
            bm = bm0[p]
        else:
            k_idx = rows(r + dil * blk * (n - 1), 2 * blk, dil)
            k, v = kc_ref[k_idx, :], vc_ref[k_idx, :]
            bm = bias_ref[p]
        q = (q_ref[q_idx, :] * scale).astype(BF16)
        s = lax.dot_general(q, k.astype(BF16), (((1,), (1,)), ((), ())), preferred_element_type=F32)
        s = jnp.where(bm > 0.5 * NEG, s + bm, NEG)
        m = jnp.max(s, axis=-1, keepdims=True)
        e = jnp.exp(s - m)
        l = jnp.sum(e, axis=-1, keepdims=True)
        o = jnp.dot(e.astype(BF16), v.astype(BF16), preferred_element_type=F32) / l
        lse = m + jnp.log(l)
        o_scr[p, q_idx, :] = o
        lse_scr[p, q_idx, :] = jnp.broadcast_to(lse, (blk, HEAD_DIM))

    for p, dil in enumerate(A_DILATIONS):
        for r in range(dil):
            for n in range(tile // (blk * dil)):
                block(p, dil, r, n)

    def merge(i, carry):
        r = pl.multiple_of(i * blk, blk)
        rows = pl.ds(r, blk)
        l0, l1, l2 = lse_scr[0, rows, :], lse_scr[1, rows, :], lse_scr[2, rows, :]
        mx = jnp.maximum(jnp.maximum(l0, l1), l2)
        w0, w1, w2 = jnp.exp(l0 - mx), jnp.exp(l1 - mx), jnp.exp(l2 - mx)
        num = w0 * o_scr[0, rows, :] + w1 * o_scr[1, rows, :] + w2 * o_scr[2, rows, :]
        o_ref[rows, :] = (num / (w0 + w1 + w2)).astype(o_ref.dtype)
        return carry

    lax.fori_loop(0, tile // blk, merge, 0)


def dilated_attention(z, bias_tab, *, batch, seq):
    z3 = z.reshape(batch, seq, z.shape[1])
    nt = seq // A_TILE
    hq, hk, hv = Z_AQ // HEAD_DIM, Z_AK // HEAD_DIM, Z_AV // HEAD_DIM
    tile_spec = lambda col0, prev: pl.BlockSpec(
        (None, A_TILE, HEAD_DIM),
        (lambda b, h, t: (b, jnp.maximum(t - 1, 0), col0 + h)) if prev else (lambda b, h, t: (b, t, col0 + h)))
    npat = len(A_DILATIONS)
    out = pl.pallas_call(
        functools.partial(_dilated_kernel, scale=HEAD_DIM ** -0.5),
        grid=(batch, A_HEADS, nt),
        in_specs=[
            tile_spec(hq, False),
            tile_spec(hk, False), tile_spec(hk, True),
            tile_spec(hv, False), tile_spec(hv, True),
            pl.BlockSpec((npat, None, A_BLOCK, 2 * A_BLOCK), lambda b, h, t: (0, h, 0, 0)),
        ],
        out_specs=pl.BlockSpec((None, A_TILE, HEAD_DIM), lambda b, h, t: (b, t, h)),
        out_shape=jax.ShapeDtypeStruct((batch, seq, A_HEADS * HEAD_DIM), BF16),
        scratch_shapes=[
            pltpu.VMEM((npat, A_BLOCK, 2 * A_BLOCK), F32),
            pltpu.VMEM((npat, A_TILE, HEAD_DIM), F32),
            pltpu.VMEM((npat, A_TILE, HEAD_DIM), F32),
        ],
        compiler_params=_cparams("parallel", "parallel", "arbitrary"),
        name="dilated_attention",
    )(z3, z3, z3, z3, z3, bias_tab)
    return out.reshape(batch * seq, A_HEADS * HEAD_DIM)


def _gla_kernel(q_ref, k_ref, v_ref, r_ref, lr_ref, wa_ref, ba_ref, gn_ref, o_ref, st_ref, *, scale):
    @pl.when(pl.program_id(1) == 0)
    def _():
        st_ref[...] = jnp.zeros_like(st_ref)

    c = B_CHUNK
    tc = q_ref.shape[0]
    dk = B_DK // B_HEADS
    dv = B_DV // B_HEADS
    pre = jnp.dot(lr_ref[...], wa_ref[...], precision=lax.Precision.HIGHEST,
                  preferred_element_type=F32) + ba_ref[...]
    log_a = (jnp.minimum(pre, 0.0) - jnp.log(1.0 + jnp.exp(-jnp.abs(pre)))) * (1.0 / B_GATE_TAU)
    tril = lax.broadcasted_iota(jnp.int32, (c, c), 0) >= lax.broadcasted_iota(jnp.int32, (c, c), 1)
    ones_tril = jnp.where(tril, 1.0, 0.0).astype(F32)
    for ci in range(tc // c):
        sl = slice(ci * c, (ci + 1) * c)
        cum_all = jnp.dot(ones_tril, log_a[sl, :], precision=lax.Precision.HIGHEST, preferred_element_type=F32)
        for h in range(B_HEADS):
            kcols = slice(h * dk, (h + 1) * dk)
            vcols = slice(h * dv, (h + 1) * dv)
            cum = cum_all[:, kcols]
            cum_last = cum[c - 1:c, :]
            q = q_ref[sl, kcols] * scale
            k = k_ref[sl, kcols]
            v = v_ref[sl, vcols]
            q_dec = (q * jnp.exp(cum)).astype(BF16)
            k_inv = (k * jnp.exp(-cum)).astype(BF16)
            k_state = (k * jnp.exp(cum_last - cum)).astype(BF16)
            att = lax.dot_general(q_dec, k_inv, (((1,), (1,)), ((), ())), preferred_element_type=F32)
            att = jnp.where(tril, att, 0.0).astype(BF16)
            st = st_ref[h]
            o = (jnp.dot(att, v.astype(BF16), preferred_element_type=F32)
                 + lax.dot_general(q_dec, st.astype(BF16), (((1,), (1,)), ((), ())), preferred_element_type=F32))
            st_ref[h] = st * jnp.exp(cum_last) + jnp.dot(v.T.astype(BF16), k_state, preferred_element_type=F32)
            ms = jnp.mean(o * o, axis=-1, keepdims=True)
            o = o * lax.rsqrt(ms + EPS) * gn_ref[:, vcols]
            r = r_ref[sl, vcols]
            o_ref[sl, vcols] = (o * (r * jax.nn.sigmoid(r))).astype(o_ref.dtype)


def gated_linear_attention(z, w_alpha_p, b_alpha, gla_norm, *, batch, seq, tc):
    dk = B_DK // B_HEADS
    dv = B_DV // B_HEADS
    nt = seq // tc
    row = lambda b, t: b * nt + t
    const = lambda b, t: (0, 0)
    return pl.pallas_call(
        functools.partial(_gla_kernel, scale=dk ** -0.5),
        grid=(batch, nt),
        in_specs=[
            pl.BlockSpec((tc, B_DK), lambda b, t: (row(b, t), Z_BQ // B_DK)),
            pl.BlockSpec((tc, B_DK), lambda b, t: (row(b, t), Z_BK // B_DK)),
            pl.BlockSpec((tc, B_DV), lambda b, t: (row(b, t), Z_BV // B_DV)),
            pl.BlockSpec((tc, B_DV), lambda b, t: (row(b, t), Z_BR // B_DV)),
            pl.BlockSpec((tc, LANES), lambda b, t: (row(b, t), Z_LR // LANES)),
            pl.BlockSpec((LANES, B_DK), const),
            pl.BlockSpec((1, B_DK), const),
            pl.BlockSpec((1, B_DV), const),
        ],
        out_specs=pl.BlockSpec((tc, B_DV), lambda b, t: (row(b, t), 0)),
        out_shape=jax.ShapeDtypeStruct((batch * seq, B_DV), BF16),
        scratch_shapes=[pltpu.VMEM((B_HEADS, dv, dk), F32)],
        compiler_params=_cparams("parallel", "arbitrary"),
        name="gated_linear_attention",
    )(z, z, z, z, z, w_alpha_p, b_alpha.reshape(1, B_DK), gla_norm.reshape(1, B_DV))


C_QW = 2 * C_NOPE


def _rope_tables(seq):
    pos = jnp.arange(seq, dtype=F32)
    inv = ROPE_THETA ** (-jnp.arange(0, C_ROPE, 2, dtype=F32) / C_ROPE)
    ang = pos[:, None] * inv[None, :]
    cos, sin = jnp.cos(ang), jnp.sin(ang)
    half = C_ROPE // 2
    z = lambda w: jnp.zeros((seq, w), F32)
    k_tab = jnp.stack([
        jnp.concatenate([cos, cos, z(LANES - C_ROPE)], axis=1),
        jnp.concatenate([-sin, z(half), z(LANES - C_ROPE)], axis=1),
        jnp.concatenate([z(half), sin, z(LANES - C_ROPE)], axis=1)])
    q_tab = jnp.concatenate([
        jnp.stack([jnp.ones((seq, C_NOPE), F32), z(C_NOPE), z(C_NOPE)]), k_tab], axis=2)
    return q_tab, k_tab


def _mla_proj_kernel(cq_ref, ckv_ref, ckr_ref, gq_ref, gkv_ref, wq_ref, wk_ref, wv_ref, qt_ref, kt_ref,
                     q_out, kn_out, kr_out, v_out, *, scale):
    half = C_ROPE // 2

    def rms(x, g):
        ms = jnp.mean(x * x, axis=-1, keepdims=True)
        return (x * lax.rsqrt(ms + EPS) * g).astype(BF16)

    cq = rms(cq_ref[...], gq_ref[...])
    q = jnp.dot(cq, wq_ref[...], preferred_element_type=F32)
    t0, t1, t2 = qt_ref[0], qt_ref[1], qt_ref[2]
    for h in range(C_HEADS):
        qh = q[:, h * C_QW:(h + 1) * C_QW]
        qh = qh * t0 + pltpu.roll(qh, C_QW - half, axis=1) * t1 + pltpu.roll(qh, half, axis=1) * t2
        q_out[:, h * C_QW:(h + 1) * C_QW] = (qh * scale).astype(q_out.dtype)
    ckv = rms(ckv_ref[...], gkv_ref[...])
    kn_out[...] = jnp.dot(ckv, wk_ref[...], preferred_element_type=F32).astype(kn_out.dtype)
    v_out[...] = jnp.dot(ckv, wv_ref[...], preferred_element_type=F32).astype(v_out.dtype)
    kr = ckr_ref[...]
    kr = kr * kt_ref[0] + pltpu.roll(kr, LANES - half, axis=1) * kt_ref[1] + pltpu.roll(kr, half, axis=1) * kt_ref[2]
    kr_out[...] = kr.astype(kr_out.dtype)


def mla_project(z, q_a_norm, kv_a_norm, wq_p, wk_p, wv_p, q_tab, k_tab, *, seq, tm):
    m = z.shape[0]
    nt = seq // tm
    const = lambda i: (0, 0)
    return pl.pallas_call(
        functools.partial(_mla_proj_kernel, scale=(C_NOPE + C_ROPE) ** -0.5 * math.log2(math.e)),
        grid=(m // tm,),
        in_specs=[
            pl.BlockSpec((tm, C_Q_RANK), lambda i: (i, Z_CQA // C_Q_RANK)),
            pl.BlockSpec((tm, C_KV_RANK), lambda i: (i, Z_CKVA // C_KV_RANK)),
            pl.BlockSpec((tm, LANES), lambda i: (i, Z_CKR // LANES)),
            pl.BlockSpec((1, C_Q_RANK), const),
            pl.BlockSpec((1, C_KV_RANK), const),
            pl.BlockSpec((C_Q_RANK, C_HEADS * C_QW), const),
            pl.BlockSpec((C_KV_RANK, C_HEADS * C_NOPE), const),
            pl.BlockSpec((C_KV_RANK, C_HEADS * C_V), const),
            pl.BlockSpec((3, tm, C_QW), lambda i: (0, i % nt, 0)),
            pl.BlockSpec((3, tm, LANES), lambda i: (0, i % nt, 0)),
        ],
        out_specs=[
            pl.BlockSpec((tm, C_HEADS * C_QW), lambda i: (i, 0)),
            pl.BlockSpec((tm, C_HEADS * C_NOPE), lambda i: (i, 0)),
            pl.BlockSpec((tm, LANES), lambda i: (i, 0)),
            pl.BlockSpec((tm, C_HEADS * C_V), lambda i: (i, 0)),
        ],
        out_shape=[
            jax.ShapeDtypeStruct((m, C_HEADS * C_QW), BF16),
            jax.ShapeDtypeStruct((m, C_HEADS * C_NOPE), BF16),
            jax.ShapeDtypeStruct((m, LANES), BF16),
            jax.ShapeDtypeStruct((m, C_HEADS * C_V), BF16),
        ],
        compiler_params=_cparams("parallel"),
        name="mla_project",
    )(z, z, z, q_a_norm.reshape(1, -1), kv_a_norm.reshape(1, -1), wq_p, wk_p, wv_p, q_tab, k_tab)


def _mla_flash_kernel(q_ref, kn_ref, kr_ref, v_ref, o_ref, m_ref, l_ref, acc_ref):
    qi = pl.program_id(2)
    tq = q_ref.shape[0]
    tk = tq // 2
    m_ref[...] = jnp.full_like(m_ref, NEG)
    l_ref[...] = jnp.zeros_like(l_ref)
    acc_ref[...] = jnp.zeros_like(acc_ref)

    def step(rows, k_start, diagonal):
        ks = pl.ds(pl.multiple_of(k_start, tk), tk)
        k = jnp.concatenate([kn_ref[ks, :], kr_ref[ks, :]], axis=1)
        s = lax.dot_general(q_ref[rows, :], k, (((1,), (1,)), ((), ())), preferred_element_type=F32)
        if diagonal:
            shape = s.shape
            s = jnp.where(lax.broadcasted_iota(jnp.int32, shape, 1) <= lax.broadcasted_iota(jnp.int32, shape, 0),
                          s, NEG)
        m_prev = m_ref[rows, :]
        m_new = jnp.maximum(m_prev, jnp.max(s, axis=-1, keepdims=True))
        alpha = jnp.exp2(m_prev - m_new)
        p = jnp.exp2(s - jnp.concatenate([m_new] * (tk // LANES), axis=1))
        l_ref[rows, :] = alpha * l_ref[rows, :] + jnp.sum(p, axis=-1, keepdims=True)
        acc_ref[rows, :] = alpha * acc_ref[rows, :] + jnp.dot(p.astype(BF16), v_ref[ks, :],
                                                              preferred_element_type=F32)
        m_ref[rows, :] = m_new

    def body(kb, carry):
        step(slice(0, tq), 2 * kb * tk, False)
        step(slice(0, tq), (2 * kb + 1) * tk, False)
        return carry

    lax.fori_loop(0, qi, body, 0)
    q0 = qi * tq
    step(slice(0, tk), q0, True)
    step(slice(tk, tq), q0, False)
    step(slice(tk, tq), q0 + tk, True)
    o_ref[...] = (acc_ref[...] / l_ref[...]).astype(o_ref.dtype)


def mla_flash(q, kn, kr, v, *, batch, seq, tq):
    nq = seq // tq
    return pl.pallas_call(
        _mla_flash_kernel,
        grid=(batch, C_HEADS, nq),
        in_specs=[
            pl.BlockSpec((tq, C_QW), lambda b, h, i: (b * nq + i, h)),
            pl.BlockSpec((seq, C_NOPE), lambda b, h, i: (b, h)),
            pl.BlockSpec((seq, LANES), lambda b, h, i: (b, 0)),
            pl.BlockSpec((seq, C_V), lambda b, h, i: (b, h)),
        ],
        out_specs=pl.BlockSpec((tq, C_V), lambda b, h, i: (b * nq + i, h)),
        out_shape=jax.ShapeDtypeStruct((batch * seq, C_HEADS * C_V), BF16),
        scratch_shapes=[
            pltpu.VMEM((tq, LANES), F32),
            pltpu.VMEM((tq, LANES), F32),
            pltpu.VMEM((tq, C_V), F32),
        ],
        compiler_params=_cparams("parallel", "parallel", "parallel"),
        name="mla_flash",
    )(q, kn, kr, v)


def _mixer_out_kernel(oa_ref, ob_ref, oc_ref, wa_ref, wb_ref, wc_ref, g0_ref, g1_ref, g2_ref,
                      b0_ref, b1_ref, b2_ref, wo_ref, xr_ref, o_ref, mg_ref):
    j = pl.program_id(1)
    nj, _, tn = mg_ref.shape

    def branch(o, w, g, b):
        return jax.nn.sigmoid(g[...] + b[...]) * jnp.dot(o[...], w[j], preferred_element_type=F32)

    @pl.when(j < nj)
    def _():
        acc = branch(oa_ref, wa_ref, g0_ref, b0_ref)
        acc = acc + branch(ob_ref, wb_ref, g1_ref, b1_ref)
        acc = acc + branch(oc_ref, wc_ref, g2_ref, b2_ref)
        mg_ref[j] = acc.astype(mg_ref.dtype)

    @pl.when(j >= nj)
    def _():
        acc = xr_ref[...]
        for c in range(nj):
            acc = acc + jnp.dot(mg_ref[c], wo_ref[j - nj, c * tn:(c + 1) * tn, :], preferred_element_type=F32)
        o_ref[...] = acc


def _column_tiles(w, tn):
    k, n = w.shape
    return w.reshape(k, n // tn, tn).transpose(1, 0, 2).astype(BF16)


def _resident(shape):
    return pl.BlockSpec(shape, lambda i, j: (0,) * len(shape), pipeline_mode=pl.Buffered(1))


def mixer_output_block(x, o_a, o_b, o_c, w_a, w_b, w_c, gates, b_gate, w_o, *, tm, tn):
    m, k = o_a.shape
    nj = w_a.shape[0]
    d = nj * tn
    mcol = lambda j: jnp.minimum(j, nj - 1)
    ocol = lambda j: jnp.maximum(j - nj, 0)
    act = pl.BlockSpec((tm, k), lambda i, j: (i, 0))
    wgt = _resident((nj, k, tn))
    gate = lambda br: pl.BlockSpec((tm, tn), lambda i, j: (i, br * nj + mcol(j)))
    gbias = lambda br: pl.BlockSpec((1, tn), lambda i, j: (0, br * nj + mcol(j)))
    return pl.pallas_call(
        _mixer_out_kernel,
        grid=(m // tm, 2 * nj),
        in_specs=[act, act, act, wgt, wgt, wgt, gate(0), gate(1), gate(2), gbias(0), gbias(1), gbias(2),
                  _resident((nj, d, tn)),
                  pl.BlockSpec((tm, tn), lambda i, j: (i, ocol(j)))],
        out_specs=pl.BlockSpec((tm, tn), lambda i, j: (i, ocol(j))),
        out_shape=jax.ShapeDtypeStruct((m, d), F32),
        scratch_shapes=[pltpu.VMEM((nj, tm, tn), BF16)],
        compiler_params=_cparams("parallel", "arbitrary"),
        name="mixer_output_block",
    )(o_a, o_b, o_c, w_a, w_b, w_c, gates, gates, gates, *([b_gate.reshape(1, -1)] * 3), w_o, x)


def _cross_block_kernel(x_ref, g_ref, wq_ref, k_ref, v_ref, wo_ref, xr_ref, o_ref, h_ref, ao_ref, *, scale):
    j = pl.program_id(1)
    hd = k_ref.shape[1]

    @pl.when(j == 0)
    def _():
        _rmsnorm_rows(x_ref, g_ref, h_ref)

    @pl.when(j < X_HEADS)
    def _():
        q = jnp.dot(h_ref[...], wq_ref[j], preferred_element_type=F32) * scale
        s = lax.dot_general(q.astype(BF16), k_ref[...], (((1,), (1,)), ((), ())), preferred_element_type=F32)
        m = jnp.max(s, axis=-1, keepdims=True)
        e = jnp.exp(s - m)
        l = jnp.sum(e, axis=-1, keepdims=True)
        o = jnp.dot(e.astype(BF16), v_ref[...], preferred_element_type=F32) / l
        ao_ref[j] = o.astype(ao_ref.dtype)

    @pl.when(j >= X_HEADS)
    def _():
        acc = xr_ref[...]
        for h in range(X_HEADS):
            acc = acc + jnp.dot(ao_ref[h], wo_ref[j - X_HEADS, h * hd:(h + 1) * hd, :],
                                preferred_element_type=F32)
        o_ref[...] = acc


def cross_attention_block(x, g, w_q, kv, w_o, *, batch, seq, mem_len, tm):
    m, d = x.shape
    hd = d // X_HEADS
    tiles_per_batch = seq // tm
    head = lambda j: jnp.minimum(j, X_HEADS - 1)
    col = lambda j: jnp.maximum(j - X_HEADS, 0)
    return pl.pallas_call(
        functools.partial(_cross_block_kernel, scale=hd ** -0.5),
        grid=(m // tm, 2 * X_HEADS),
        in_specs=[
            pl.BlockSpec((tm, d), lambda i, j: (i, 0)),
            pl.BlockSpec((1, d), lambda i, j: (0, 0)),
            _resident((X_HEADS, d, hd)),
            pl.BlockSpec((mem_len, hd), lambda i, j: (i // tiles_per_batch, head(j))),
            pl.BlockSpec((mem_len, hd), lambda i, j: (i // tiles_per_batch, X_HEADS + head(j))),
            _resident((X_HEADS, d, hd)),
            pl.BlockSpec((tm, hd), lambda i, j: (i, col(j))),
        ],
        out_specs=pl.BlockSpec((tm, hd), lambda i, j: (i, col(j))),
        out_shape=jax.ShapeDtypeStruct((m, d), F32),
        scratch_shapes=[pltpu.VMEM((tm, d), BF16), pltpu.VMEM((X_HEADS, tm, hd), BF16)],
        compiler_params=_cparams("parallel", "arbitrary"),
        name="cross_attention_block",
    )(x, g.reshape(1, d), w_q, kv, kv, w_o, x)


def _pack_w_in(w):
    d = w.shape[0]
    sizes = (1024, 1024, 1024, B_DK, B_DK, B_DV, B_GATE_RANK, B_DV, C_Q_RANK, C_KV_RANK, C_ROPE, N_BRANCH * d)
    offs = np.concatenate([[0], np.cumsum(sizes)])
    aq, ak, av, bq, bk, bv, lr, br, cqa, ckva, ckr, gates = [w[:, offs[i]:offs[i + 1]] for i in range(len(sizes))]
    zeros = lambda n: jnp.zeros((d, n), w.dtype)
    used = Z_CKR + LANES
    packed = jnp.concatenate(
        [aq, ak, av, bv, br, cqa, ckva, bq, bk,
         lr, zeros(LANES - B_GATE_RANK), ckr, zeros(LANES - C_ROPE), zeros(Z_WIDTH - used), gates], axis=1)
    return packed.astype(BF16)


def _pack_w_qb(w):
    r = w.shape[0]
    w = w.reshape(r, C_HEADS, C_NOPE + C_ROPE)
    w = jnp.pad(w, ((0, 0), (0, 0), (0, C_QW - C_NOPE - C_ROPE)))
    return w.reshape(r, C_HEADS * C_QW).astype(BF16)


def _pack_w_kvb(w):
    r = w.shape[0]
    w = w.reshape(r, C_HEADS, C_NOPE + C_V)
    wk = w[:, :, :C_NOPE].reshape(r, C_HEADS * C_NOPE)
    wv = w[:, :, C_NOPE:].reshape(r, C_HEADS * C_V)
    return wk.astype(BF16), wv.astype(BF16)


def kernel(x, mem, rel_bias, norm_mix, w_in, b_gate, w_alpha, b_alpha, gla_norm, q_a_norm, w_qb, kv_a_norm, w_kvb, w_up_a, w_up_b, w_up_c, w_o, norm_x, norm_mem, w_xq, w_xkv, w_xo, norm_ffn, w_ffn_gate, w_ffn_up, w_ffn_down, norm_final):
    batch, seq, d = x.shape
    mem_len = mem.shape[1]
    depth = w_in.shape[0]
    t = batch * seq
    assert d == 2048 and seq % A_TILE == 0, "tiling is derived for the stated shapes"

    xf = x.reshape(t, d)
    memf = mem.reshape(batch * mem_len, d)
    bias_tab = dilated_bias_table(rel_bias)
    q_tab, k_tab = _rope_tables(seq)
    bf = lambda a: a.astype(BF16)

    for l in range(depth):
        z, gates = input_projection(xf, norm_mix[l], _pack_w_in(w_in[l]), tm=1024, tn=768)
        o_a = dilated_attention(z, bias_tab, batch=batch, seq=seq)
        w_alpha_p = jnp.pad(w_alpha[l], ((0, LANES - B_GATE_RANK), (0, 0)))
        o_b = gated_linear_attention(z, w_alpha_p, b_alpha[l], gla_norm[l], batch=batch, seq=seq, tc=512)
        wk_p, wv_p = _pack_w_kvb(w_kvb[l])
        cq, ckn, ckr, cv = mla_project(z, q_a_norm[l], kv_a_norm[l], _pack_w_qb(w_qb[l]), wk_p, wv_p,
                                       q_tab, k_tab, seq=seq, tm=512)
        o_c = mla_flash(cq, ckn, ckr, cv, batch=batch, seq=seq, tq=1024)
        ct = lambda w: _column_tiles(w, 512)
        xf = mixer_output_block(xf, o_a, o_b, o_c, ct(w_up_a[l]), ct(w_up_b[l]), ct(w_up_c[l]), gates, b_gate[l],
                                ct(w_o[l]), tm=1024, tn=512)
        xkv = norm_matmul(memf, norm_mem[l], w_xkv, out_dtype=BF16, tm=batch * mem_len, tn=512, layer=l)
        xf = cross_attention_block(xf, norm_x[l], _column_tiles(w_xq[l], d // X_HEADS), xkv,
                                   _column_tiles(w_xo[l], d // X_HEADS),
                                   batch=batch, seq=seq, mem_len=mem_len, tm=1024)
        act = norm_swiglu(xf, norm_ffn[l], w_ffn_gate, w_ffn_up, tm=1024, tn=512, layer=l)
        xf = matmul_residual(act, bf(w_ffn_down[l]), xf, tm=1024, tn=512)
    return rmsnorm_rows(xf, norm_final, tm=512).reshape(batch, seq, d)
```

```python
import functools
import math

import numpy as np
import jax
import jax.numpy as jnp
from jax import lax
from jax.experimental import pallas as pl
from jax.experimental.pallas import tpu as pltpu

F32 = jnp.float32
BF16 = jnp.bfloat16
EPS = 1e-6
NEG = -1e30

HEAD_DIM = 128
A_HEADS = 8
A_DILATIONS = (1, 4, 16)
A_BLOCK = 128
N_BUCKETS = 32
MAX_DISTANCE = 2048
B_HEADS = 4
B_DK = 512
B_DV = 1024
B_GATE_RANK = 16
B_GATE_TAU = 16.0
B_CHUNK = 64
C_HEADS = 8
C_Q_RANK = 512
C_KV_RANK = 512
C_NOPE = 128
C_ROPE = 64
C_V = 128
ROPE_THETA = 10000.0
X_HEADS = 4
N_BRANCH = 3

VMEM_LIMIT_BYTES = 56 * 1024 * 1024
LANES = 128

Z_AQ, Z_AK, Z_AV = 0, 1024, 2048
Z_BV, Z_BR = 3072, 4096
Z_CQA, Z_CKVA = 5120, 5632
Z_BQ, Z_BK = 6144, 6656
Z_LR = 7168
Z_CKR = 7296
Z_WIDTH = 7680
G_WIDTH = 3 * 2048
A_TILE = 2048


def _cparams(*sem):
    return pltpu.CompilerParams(dimension_semantics=sem, vmem_limit_bytes=VMEM_LIMIT_BYTES)


def _rmsnorm_rows(x_ref, g_ref, h_ref):
    rows = x_ref.shape[0]

    def body(i, carry):
        r = pl.multiple_of(i * 16, 16)
        x = x_ref[pl.ds(r, 16), :]
        ms = jnp.mean(x * x, axis=-1, keepdims=True)
        h_ref[pl.ds(r, 16), :] = (x * lax.rsqrt(ms + EPS) * g_ref[...]).astype(BF16)
        return carry

    lax.fori_loop(0, rows // 16, body, 0, unroll=8)


def _norm_matmul_kernel(x_ref, g_ref, w_ref, o_ref, h_ref, *, scale):
    @pl.when(pl.program_id(1) == 0)
    def _():
        _rmsnorm_rows(x_ref, g_ref, h_ref)

    acc = jnp.dot(h_ref[...], w_ref[...].astype(BF16), preferred_element_type=F32)
    if scale != 1.0:
        acc = acc * scale
    o_ref[...] = acc.astype(o_ref.dtype)


def _weight_spec(w, layer, tn):
    if w.ndim == 2:
        return pl.BlockSpec((w.shape[0], tn), lambda i, j: (0, j))
    return pl.BlockSpec((None, w.shape[1], tn), lambda i, j: (layer, 0, j))


def norm_matmul(x, g, w, *, out_dtype, tm, tn, scale=1.0, layer=None):
    m, k = x.shape
    n = w.shape[-1]
    return pl.pallas_call(
        functools.partial(_norm_matmul_kernel, scale=scale),
        grid=(m // tm, n // tn),
        in_specs=[
            pl.BlockSpec((tm, k), lambda i, j: (i, 0)),
            pl.BlockSpec((1, k), lambda i, j: (0, 0)),
            _weight_spec(w, layer, tn),
        ],
        out_specs=pl.BlockSpec((tm, tn), lambda i, j: (i, j)),
        out_shape=jax.ShapeDtypeStruct((m, n), out_dtype),
        scratch_shapes=[pltpu.VMEM((tm, k), BF16)],
        compiler_params=_cparams("parallel", "arbitrary"),
        name="norm_matmul",
    )(x, g.reshape(1, k), w)


def _w_in_kernel(x_ref, g_ref, w_ref, z_ref, gate_ref, h_ref, *, nz):
    j = pl.program_id(1)

    @pl.when(j == 0)
    def _():
        _rmsnorm_rows(x_ref, g_ref, h_ref)

    acc = jnp.dot(h_ref[...], w_ref[...], preferred_element_type=F32)

    @pl.when(j < nz)
    def _():
        z_ref[...] = acc

    @pl.when(j >= nz)
    def _():
        gate_ref[...] = acc.astype(gate_ref.dtype)


def input_projection(x, g, w, *, tm, tn, layer=None):
    m, k = x.shape
    nz = Z_WIDTH // tn
    ng = G_WIDTH // tn
    return pl.pallas_call(
        functools.partial(_w_in_kernel, nz=nz),
        grid=(m // tm, nz + ng),
        in_specs=[
            pl.BlockSpec((tm, k), lambda i, j: (i, 0)),
            pl.BlockSpec((1, k), lambda i, j: (0, 0)),
            _weight_spec(w, layer, tn),
        ],
        out_specs=[
            pl.BlockSpec((tm, tn), lambda i, j: (i, jnp.minimum(j, nz - 1))),
            pl.BlockSpec((tm, tn), lambda i, j: (i, jnp.maximum(j - nz, 0))),
        ],
        out_shape=[jax.ShapeDtypeStruct((m, Z_WIDTH), F32), jax.ShapeDtypeStruct((m, G_WIDTH), BF16)],
        scratch_shapes=[pltpu.VMEM((tm, k), BF16)],
        compiler_params=_cparams("parallel", "arbitrary"),
        name="input_projection",
    )(x, g.reshape(1, k), w)


def _norm_swiglu_kernel(x_ref, g_ref, wg_ref, wu_ref, o_ref, h_ref):
    @pl.when(pl.program_id(1) == 0)
    def _():
        _rmsnorm_rows(x_ref, g_ref, h_ref)

    h = h_ref[...]
    a = jnp.dot(h, wg_ref[...].astype(BF16), preferred_element_type=F32)
    b = jnp.dot(h, wu_ref[...].astype(BF16), preferred_element_type=F32)
    o_ref[...] = (a * jax.nn.sigmoid(a) * b).astype(o_ref.dtype)


def norm_swiglu(x, g, wg, wu, *, tm, tn, layer=None):
    m, k = x.shape
    n = wg.shape[-1]
    return pl.pallas_call(
        _norm_swiglu_kernel,
        grid=(m // tm, n // tn),
        in_specs=[
            pl.BlockSpec((tm, k), lambda i, j: (i, 0)),
            pl.BlockSpec((1, k), lambda i, j: (0, 0)),
            _weight_spec(wg, layer, tn),
            _weight_spec(wu, layer, tn),
        ],
        out_specs=pl.BlockSpec((tm, tn), lambda i, j: (i, j)),
        out_shape=jax.ShapeDtypeStruct((m, n), BF16),
        scratch_shapes=[pltpu.VMEM((tm, k), BF16)],
        compiler_params=_cparams("parallel", "arbitrary"),
        name="norm_swiglu",
    )(x, g.reshape(1, k), wg, wu)


def _matmul_residual_kernel(a_ref, w_ref, r_ref, o_ref):
    o_ref[...] = r_ref[...] + jnp.dot(a_ref[...], w_ref[...].astype(BF16), preferred_element_type=F32)


def matmul_residual(a, w, res, *, tm, tn, layer=None):
    m, k = a.shape
    n = w.shape[-1]
    return pl.pallas_call(
        _matmul_residual_kernel,
        grid=(m // tm, n // tn),
        in_specs=[
            pl.BlockSpec((tm, k), lambda i, j: (i, 0)),
            _weight_spec(w, layer, tn),
            pl.BlockSpec((tm, tn), lambda i, j: (i, j)),
        ],
        out_specs=pl.BlockSpec((tm, tn), lambda i, j: (i, j)),
        out_shape=jax.ShapeDtypeStruct((m, n), F32),
        compiler_params=_cparams("parallel", "parallel"),
        name="matmul_residual",
    )(a, w, res)


def _rmsnorm_kernel(x_ref, g_ref, o_ref):
    x = x_ref[...]
    ms = jnp.mean(x * x, axis=-1, keepdims=True)
    o_ref[...] = x * lax.rsqrt(ms + EPS) * g_ref[...]


def rmsnorm_rows(x, g, *, tm):
    m, k = x.shape
    return pl.pallas_call(
        _rmsnorm_kernel,
        grid=(m // tm,),
        in_specs=[pl.BlockSpec((tm, k), lambda i: (i, 0)), pl.BlockSpec((1, k), lambda i: (0, 0))],
        out_specs=pl.BlockSpec((tm, k), lambda i: (i, 0)),
        out_shape=jax.ShapeDtypeStruct((m, k), F32),
        compiler_params=_cparams("parallel"),
        name="final_rmsnorm",
    )(x, g.reshape(1, k))


def _t5_bucket_np(dist):
    max_exact = N_BUCKETS // 2
    n = np.maximum(dist, 1).astype(np.float64)
    large = max_exact + (np.log(n / max_exact) / math.log(MAX_DISTANCE / max_exact)
                         * (N_BUCKETS - max_exact)).astype(np.int32)
    large = np.minimum(large, N_BUCKETS - 1)
    return np.where(dist < max_exact, dist, large).astype(np.int32)


def _dilated_bucket_table():
    qi = np.arange(A_BLOCK)[:, None]
    kj = np.arange(2 * A_BLOCK)[None, :]
    steps_back = qi + A_BLOCK - kj
    valid = (steps_back >= 0) & (steps_back <= A_BLOCK)
    tabs = []
    for dil in A_DILATIONS:
        bucket = _t5_bucket_np(np.clip(steps_back, 0, A_BLOCK) * dil)
        tabs.append(np.where(valid, bucket, -1))
    return np.stack(tabs).astype(np.int32)


def _bias_table_kernel(idx_ref, rb_ref, o_ref):
    h = pl.program_id(1)
    idx = idx_ref[...]
    acc = jnp.full(idx.shape, NEG, F32)
    for b in range(N_BUCKETS):
        acc = jnp.where(idx == b, rb_ref[b, h], acc)
    o_ref[...] = acc


def dilated_bias_table(rel_bias):
    idx = jnp.asarray(_dilated_bucket_table())
    npat = len(A_DILATIONS)
    return pl.pallas_call(
        _bias_table_kernel,
        grid=(npat, A_HEADS),
        in_specs=[
            pl.BlockSpec((None, A_BLOCK, 2 * A_BLOCK), lambda p, h: (p, 0, 0)),
            pl.BlockSpec(memory_space=pltpu.SMEM),
        ],
        out_specs=pl.BlockSpec((None, None, A_BLOCK, 2 * A_BLOCK), lambda p, h: (p, h, 0, 0)),
        out_shape=jax.ShapeDtypeStruct((npat, A_HEADS, A_BLOCK, 2 * A_BLOCK), F32),
        compiler_params=_cparams("parallel", "parallel"),
        name="dilated_bias_table",
    )(idx, rel_bias)


def _dilated_kernel(q_ref, kc_ref, kp_ref, vc_ref, vp_ref, bias_ref, o_ref, bm0, o_scr, lse_scr, *, scale):
    t = pl.program_id(2)
    tile = q_ref.shape[0]
    blk = A_BLOCK
    col = lax.broadcasted_iota(jnp.int32, (blk, 2 * blk), 1)
    no_prev = jnp.logical_and(col < blk, t == 0)
    for p in range(len(A_DILATIONS)):
        bm0[p] = jnp.where(no_prev, NEG, bias_ref[p])

    def rows(start, size, dil):
        return pl.ds(start, size) if dil == 1 else pl.ds(start, size, stride=dil)

    def block(p, dil, r, n):
        q_idx = rows(r + dil * blk * n, blk, dil)
        if n == 0:
            prev = rows(tile - dil * blk + r, blk, dil)
            cur = rows(r, blk, dil)
            k = jnp.concatenate([kp_ref[prev, :], kc_ref[cur, :]], axis=0)
            v = jnp.concatenate([vp_ref[prev, :], vc_ref[cur, :]], axis=0)
            bm = bm0[p]
        else:
            k_idx = rows(r + dil * blk * (n - 1), 2 * blk, dil)
            k, v = kc_ref[k_idx, :], vc_ref[k_idx, :]
            bm = bias_ref[p]
        q = (q_ref[q_idx, :] * scale).astype(BF16)
        s = lax.dot_general(q, k.astype(BF16), (((1,), (1,)), ((), ())), preferred_element_type=F32)
        s = jnp.where(bm > 0.5 * NEG, s + bm, NEG)
        m = jnp.max(s, axis=-1, keepdims=True)
        e = jnp.exp(s - m)
        l = jnp.sum(e, axis=-1, keepdims=True)
        o = jnp.dot(e.astype(BF16), v.astype(BF16), preferred_element_type=F32) / l
        lse = m + jnp.log(l)
        o_scr[p, q_idx, :] = o
        lse_scr[p, q_idx, :] = jnp.broadcast_to(lse, (blk, HEAD_DIM))

    for p, dil in enumerate(A_DILATIONS):
        for r in range(dil):
            for n in range(tile // (blk * dil)):
                block(p, dil, r, n)

    def merge(i, carry):
        r = pl.multiple_of(i * blk, blk)
        rows = pl.ds(r, blk)
        l0, l1, l2 = lse_scr[0, rows, :], lse_scr[1, rows, :], lse_scr[2, rows, :]
        mx = jnp.maximum(jnp.maximum(l0, l1), l2)
        w0, w1, w2 = jnp.exp(l0 - mx), jnp.exp(l1 - mx), jnp.exp(l2 - mx)
        num = w0 * o_scr[0, rows, :] + w1 * o_scr[1, rows, :] + w2 * o_scr[2, rows, :]
        o_ref[rows, :] = (num / (w0 + w1 + w2)).astype(o_ref.dtype)
        return carry

    lax.fori_loop(0, tile // blk, merge, 0)


def dilated_attention(z, bias_tab, *, batch, seq):
    z3 = z.reshape(batch, seq, z.shape[1])
    nt = seq // A_TILE
    hq, hk, hv = Z_AQ // HEAD_DIM, Z_AK // HEAD_DIM, Z_AV // HEAD_DIM
    tile_spec = lambda col0, prev: pl.BlockSpec(
        (None, A_TILE, HEAD_DIM),
        (lambda b, h, t: (b, jnp.maximum(t - 1, 0), col0 + h)) if prev else (lambda b, h, t: (b, t, col0 + h)))
    npat = len(A_DILATIONS)
    out = pl.pallas_call(
        functools.partial(_dilated_kernel, scale=HEAD_DIM ** -0.5),
        grid=(batch, A_HEADS, nt),
        in_specs=[
            tile_spec(hq, False),
            tile_spec(hk, False), tile_spec(hk, True),
            tile_spec(hv, False), tile_spec(hv, True),
            pl.BlockSpec((npat, None, A_BLOCK, 2 * A_BLOCK), lambda b, h, t: (0, h, 0, 0)),
        ],
        out_specs=pl.BlockSpec((None, A_TILE, HEAD_DIM), lambda b, h, t: (b, t, h)),
        out_shape=jax.ShapeDtypeStruct((batch, seq, A_HEADS * HEAD_DIM), BF16),
        scratch_shapes=[
            pltpu.VMEM((npat, A_BLOCK, 2 * A_BLOCK), F32),
            pltpu.VMEM((npat, A_TILE, HEAD_DIM), F32),
            pltpu.VMEM((npat, A_TILE, HEAD_DIM), F32),
        ],
        compiler_params=_cparams("parallel", "parallel", "arbitrary"),
        name="dilated_attention",
    )(z3, z3, z3, z3, z3, bias_tab)
    return out.reshape(batch * seq, A_HEADS * HEAD_DIM)


def _gla_kernel(q_ref, k_ref, v_ref, r_ref, lr_ref, wa_ref, ba_ref, gn_ref, o_ref, st_ref, *, scale):
    @pl.when(pl.program_id(1) == 0)
    def _():
        st_ref[...] = jnp.zeros_like(st_ref)

    c = B_CHUNK
    tc = q_ref.shape[0]
    dk = B_DK // B_HEADS
    dv = B_DV // B_HEADS
    pre = jnp.dot(lr_ref[...], wa_ref[...], precision=lax.Precision.HIGHEST,
                  preferred_element_type=F32) + ba_ref[...]
    log_a = (jnp.minimum(pre, 0.0) - jnp.log(1.0 + jnp.exp(-jnp.abs(pre)))) * (1.0 / B_GATE_TAU)
    tril = lax.broadcasted_iota(jnp.int32, (c, c), 0) >= lax.broadcasted_iota(jnp.int32, (c, c), 1)
    ones_tril = jnp.where(tril, 1.0, 0.0).astype(F32)
    for ci in range(tc // c):
        sl = slice(ci * c, (ci + 1) * c)
        cum_all = jnp.dot(ones_tril, log_a[sl, :], precision=lax.Precision.HIGHEST, preferred_element_type=F32)
        for h in range(B_HEADS):
            kcols = slice(h * dk, (h + 1) * dk)
            vcols = slice(h * dv, (h + 1) * dv)
            cum = cum_all[:, kcols]
            cum_last = cum[c - 1:c, :]
            q = q_ref[sl, kcols] * scale
            k = k_ref[sl, kcols]
            v = v_ref[sl, vcols]
            q_dec = (q * jnp.exp(cum)).astype(BF16)
            k_inv = (k * jnp.exp(-cum)).astype(BF16)
            k_state = (k * jnp.exp(cum_last - cum)).astype(BF16)
            att = lax.dot_general(q_dec, k_inv, (((1,), (1,)), ((), ())), preferred_element_type=F32)
            att = jnp.where(tril, att, 0.0).astype(BF16)
            st = st_ref[h]
            o = (jnp.dot(att, v.astype(BF16), preferred_element_type=F32)
                 + lax.dot_general(q_dec, st.astype(BF16), (((1,), (1,)), ((), ())), preferred_element_type=F32))
            st_ref[h] = st * jnp.exp(cum_last) + jnp.dot(v.T.astype(BF16), k_state, preferred_element_type=F32)
            ms = jnp.mean(o * o, axis=-1, keepdims=True)
            o = o * lax.rsqrt(ms + EPS) * gn_ref[:, vcols]
            r = r_ref[sl, vcols]
            o_ref[sl, vcols] = (o * (r * jax.nn.sigmoid(r))).astype(o_ref.dtype)


def gated_linear_attention(z, w_alpha_p, b_alpha, gla_norm, *, batch, seq, tc):
    dk = B_DK // B_HEADS
    dv = B_DV // B_HEADS
    nt = seq // tc
    row = lambda b, t: b * nt + t
    const = lambda b, t: (0, 0)
    return pl.pallas_call(
        functools.partial(_gla_kernel, scale=dk ** -0.5),
        grid=(batch, nt),
        in_specs=[
            pl.BlockSpec((tc, B_DK), lambda b, t: (row(b, t), Z_BQ // B_DK)),
            pl.BlockSpec((tc, B_DK), lambda b, t: (row(b, t), Z_BK // B_DK)),
            pl.BlockSpec((tc, B_DV), lambda b, t: (row(b, t), Z_BV // B_DV)),
            pl.BlockSpec((tc, B_DV), lambda b, t: (row(b, t), Z_BR // B_DV)),
            pl.BlockSpec((tc, LANES), lambda b, t: (row(b, t), Z_LR // LANES)),
            pl.BlockSpec((LANES, B_DK), const),
            pl.BlockSpec((1, B_DK), const),
            pl.BlockSpec((1, B_DV), const),
        ],
        out_specs=pl.BlockSpec((tc, B_DV), lambda b, t: (row(b, t), 0)),
        out_shape=jax.ShapeDtypeStruct((batch * seq, B_DV), BF16),
        scratch_shapes=[pltpu.VMEM((B_HEADS, dv, dk), F32)],
        compiler_params=_cparams("parallel", "arbitrary"),
        name="gated_linear_attention",
    )(z, z, z, z, z, w_alpha_p, b_alpha.reshape(1, B_DK), gla_norm.reshape(1, B_DV))


C_QW = 2 * C_NOPE


def _rope_tables(seq):
    pos = jnp.arange(seq, dtype=F32)
    inv = ROPE_THETA ** (-jnp.arange(0, C_ROPE, 2, dtype=F32) / C_ROPE)
    ang = pos[:, None] * inv[None, :]
    cos, sin = jnp.cos(ang), jnp.sin(ang)
    half = C_ROPE // 2
    z = lambda w: jnp.zeros((seq, w), F32)
    k_tab = jnp.stack([
        jnp.concatenate([cos, cos, z(LANES - C_ROPE)], axis=1),
        jnp.concatenate([-sin, z(half), z(LANES - C_ROPE)], axis=1),
        jnp.concatenate([z(half), sin, z(LANES - C_ROPE)], axis=1)])
    q_tab = jnp.concatenate([
        jnp.stack([jnp.ones((seq, C_NOPE), F32), z(C_NOPE), z(C_NOPE)]), k_tab], axis=2)
    return q_tab, k_tab


def _mla_proj_kernel(cq_ref, ckv_ref, ckr_ref, gq_ref, gkv_ref, wq_ref, wk_ref, wv_ref, qt_ref, kt_ref,
                     q_out, kn_out, kr_out, v_out, *, scale):
    half = C_ROPE // 2

    def rms(x, g):
        ms = jnp.mean(x * x, axis=-1, keepdims=True)
        return (x * lax.rsqrt(ms + EPS) * g).astype(BF16)

    cq = rms(cq_ref[...], gq_ref[...])
    q = jnp.dot(cq, wq_ref[...], preferred_element_type=F32)
    t0, t1, t2 = qt_ref[0], qt_ref[1], qt_ref[2]
    for h in range(C_HEADS):
        qh = q[:, h * C_QW:(h + 1) * C_QW]
        qh = qh * t0 + pltpu.roll(qh, C_QW - half, axis=1) * t1 + pltpu.roll(qh, half, axis=1) * t2
        q_out[:, h * C_QW:(h + 1) * C_QW] = (qh * scale).astype(q_out.dtype)
    ckv = rms(ckv_ref[...], gkv_ref[...])
    kn_out[...] = jnp.dot(ckv, wk_ref[...], preferred_element_type=F32).astype(kn_out.dtype)
    v_out[...] = jnp.dot(ckv, wv_ref[...], preferred_element_type=F32).astype(v_out.dtype)
    kr = ckr_ref[...]
    kr = kr * kt_ref[0] + pltpu.roll(kr, LANES - half, axis=1) * kt_ref[1] + pltpu.roll(kr, half, axis=1) * kt_ref[2]
    kr_out[...] = kr.astype(kr_out.dtype)


def mla_project(z, q_a_norm, kv_a_norm, wq_p, wk_p, wv_p, q_tab, k_tab, *, seq, tm):
    m = z.shape[0]
    nt = seq // tm
    const = lambda i: (0, 0)
    return pl.pallas_call(
        functools.partial(_mla_proj_kernel, scale=(C_NOPE + C_ROPE) ** -0.5 * math.log2(math.e)),
        grid=(m // tm,),
        in_specs=[
            pl.BlockSpec((tm, C_Q_RANK), lambda i: (i, Z_CQA // C_Q_RANK)),
            pl.BlockSpec((tm, C_KV_RANK), lambda i: (i, Z_CKVA // C_KV_RANK)),
            pl.BlockSpec((tm, LANES), lambda i: (i, Z_CKR // LANES)),
            pl.BlockSpec((1, C_Q_RANK), const),
            pl.BlockSpec((1, C_KV_RANK), const),
            pl.BlockSpec((C_Q_RANK, C_HEADS * C_QW), const),
            pl.BlockSpec((C_KV_RANK, C_HEADS * C_NOPE), const),
            pl.BlockSpec((C_KV_RANK, C_HEADS * C_V), const),
            pl.BlockSpec((3, tm, C_QW), lambda i: (0, i % nt, 0)),
            pl.BlockSpec((3, tm, LANES), lambda i: (0, i % nt, 0)),
        ],
        out_specs=[
            pl.BlockSpec((tm, C_HEADS * C_QW), lambda i: (i, 0)),
            pl.BlockSpec((tm, C_HEADS * C_NOPE), lambda i: (i, 0)),
            pl.BlockSpec((tm, LANES), lambda i: (i, 0)),
            pl.BlockSpec((tm, C_HEADS * C_V), lambda i: (i, 0)),
        ],
        out_shape=[
            jax.ShapeDtypeStruct((m, C_HEADS * C_QW), BF16),
            jax.ShapeDtypeStruct((m, C_HEADS * C_NOPE), BF16),
            jax.ShapeDtypeStruct((m, LANES), BF16),
            jax.ShapeDtypeStruct((m, C_HEADS * C_V), BF16),
        ],
        compiler_params=_cparams("parallel"),
        name="mla_project",
    )(z, z, z, q_a_norm.reshape(1, -1), kv_a_norm.reshape(1, -1), wq_p, wk_p, wv_p, q_tab, k_tab)


def _mla_flash_kernel(q_ref, kn_ref, kr_ref, v_ref, o_ref, m_ref, l_ref, acc_ref):
    tq = m_ref.shape[0]
    tk = tq // 2

    def q_tile(qi, carry):
        q0 = pl.multiple_of(qi * tq, tq)
        m_ref[...] = jnp.full_like(m_ref, NEG)
        l_ref[...] = jnp.zeros_like(l_ref)
        acc_ref[...] = jnp.zeros_like(acc_ref)

        def step(r0, nr, k_start, nk, mask_offset=None):
            rows = slice(r0, r0 + nr)
            ks = pl.ds(pl.multiple_of(k_start, tk), nk)
            k = jnp.concatenate([kn_ref[ks, :], kr_ref[ks, :]], axis=1)
            q = q_ref[pl.ds(pl.multiple_of(q0 + r0, tk), nr), :]
            s = lax.dot_general(q, k, (((1,), (1,)), ((), ())), preferred_element_type=F32)
            if mask_offset is not None:
                visible = (lax.broadcasted_iota(jnp.int32, s.shape, 1)
                           <= lax.broadcasted_iota(jnp.int32, s.shape, 0) + mask_offset)
                s = jnp.where(visible, s, NEG)
            m_prev = m_ref[rows, :]
            m_new = jnp.maximum(m_prev, jnp.max(s, axis=-1, keepdims=True))
            alpha = jnp.exp2(m_prev - m_new)
            p = jnp.exp2(s - jnp.concatenate([m_new] * (nk // LANES), axis=1))
            l_ref[rows, :] = alpha * l_ref[rows, :] + jnp.sum(p, axis=-1, keepdims=True)
            acc_ref[rows, :] = alpha * acc_ref[rows, :] + jnp.dot(p.astype(BF16), v_ref[ks, :],
                                                                  preferred_element_type=F32)
            m_ref[rows, :] = m_new

        def body(kb, c):
            step(0, tq, 2 * kb * tk, tk)
            step(0, tq, (2 * kb + 1) * tk, tk)
            return c

        lax.fori_loop(0, qi, body, 0)
        step(0, tk, q0, tk, 0)
        step(tk, tk, q0, tq, tk)
        o_ref[pl.ds(q0, tq), :] = (acc_ref[...] / l_ref[...]).astype(o_ref.dtype)
        return carry

    lax.fori_loop(0, q_ref.shape[0] // tq, q_tile, 0)


def mla_flash(q, kn, kr, v, *, batch, seq, tq):
    return pl.pallas_call(
        _mla_flash_kernel,
        grid=(batch, C_HEADS),
        in_specs=[
            pl.BlockSpec((seq, C_QW), lambda b, h: (b, h)),
            pl.BlockSpec((seq, C_NOPE), lambda b, h: (b, h)),
            pl.BlockSpec((seq, LANES), lambda b, h: (b, 0)),
            pl.BlockSpec((seq, C_V), lambda b, h: (b, h)),
        ],
        out_specs=pl.BlockSpec((seq, C_V), lambda b, h: (b, h)),
        out_shape=jax.ShapeDtypeStruct((batch * seq, C_HEADS * C_V), BF16),
        scratch_shapes=[
            pltpu.VMEM((tq, LANES), F32),
            pltpu.VMEM((tq, LANES), F32),
            pltpu.VMEM((tq, C_V), F32),
        ],
        compiler_params=_cparams("parallel", "parallel"),
        name="mla_flash",
    )(q, kn, kr, v)


def _mixer_out_kernel(oa_ref, ob_ref, oc_ref, wa_ref, wb_ref, wc_ref, g0_ref, g1_ref, g2_ref,
                      b0_ref, b1_ref, b2_ref, wo_ref, xr_ref, o_ref, mg_ref):
    j = pl.program_id(1)
    nj, _, tn = mg_ref.shape

    def branch(o, w, g, b):
        return jax.nn.sigmoid(g[...] + b[...]) * jnp.dot(o[...], w[j], preferred_element_type=F32)

    @pl.when(j < nj)
    def _():
        acc = branch(oa_ref, wa_ref, g0_ref, b0_ref)
        acc = acc + branch(ob_ref, wb_ref, g1_ref, b1_ref)
        acc = acc + branch(oc_ref, wc_ref, g2_ref, b2_ref)
        mg_ref[j] = acc.astype(mg_ref.dtype)

    @pl.when(j >= nj)
    def _():
        acc = xr_ref[...]
        for c in range(nj):
            acc = acc + jnp.dot(mg_ref[c], wo_ref[j - nj, c * tn:(c + 1) * tn, :], preferred_element_type=F32)
        o_ref[...] = acc


def _column_tiles(w, tn):
    k, n = w.shape
    return w.reshape(k, n // tn, tn).transpose(1, 0, 2).astype(BF16)


def _resident(shape):
    return pl.BlockSpec(shape, lambda i, j: (0,) * len(shape), pipeline_mode=pl.Buffered(1))


def mixer_output_block(x, o_a, o_b, o_c, w_a, w_b, w_c, gates, b_gate, w_o, *, tm, tn):
    m, k = o_a.shape
    nj = w_a.shape[0]
    d = nj * tn
    mcol = lambda j: jnp.minimum(j, nj - 1)
    ocol = lambda j: jnp.maximum(j - nj, 0)
    act = pl.BlockSpec((tm, k), lambda i, j: (i, 0))
    wgt = _resident((nj, k, tn))
    gate = lambda br: pl.BlockSpec((tm, tn), lambda i, j: (i, br * nj + mcol(j)))
    gbias = lambda br: pl.BlockSpec((1, tn), lambda i, j: (0, br * nj + mcol(j)))
    return pl.pallas_call(
        _mixer_out_kernel,
        grid=(m // tm, 2 * nj),
        in_specs=[act, act, act, wgt, wgt, wgt, gate(0), gate(1), gate(2), gbias(0), gbias(1), gbias(2),
                  _resident((nj, d, tn)),
                  pl.BlockSpec((tm, tn), lambda i, j: (i, ocol(j)))],
        out_specs=pl.BlockSpec((tm, tn), lambda i, j: (i, ocol(j))),
        out_shape=jax.ShapeDtypeStruct((m, d), F32),
        scratch_shapes=[pltpu.VMEM((nj, tm, tn), BF16)],
        compiler_params=_cparams("parallel", "arbitrary"),
        name="mixer_output_block",
    )(o_a, o_b, o_c, w_a, w_b, w_c, gates, gates, gates, *([b_gate.reshape(1, -1)] * 3), w_o, x)


def _cross_block_kernel(x_ref, g_ref, wq_ref, k_ref, v_ref, wo_ref, xr_ref, o_ref, h_ref, ao_ref, *, scale):
    j = pl.program_id(1)
    hd = k_ref.shape[1]

    @pl.when(j == 0)
    def _():
        _rmsnorm_rows(x_ref, g_ref, h_ref)

    @pl.when(j < X_HEADS)
    def _():
        q = jnp.dot(h_ref[...], wq_ref[j], preferred_element_type=F32) * scale
        s = lax.dot_general(q.astype(BF16), k_ref[...], (((1,), (1,)), ((), ())), preferred_element_type=F32)
        m = jnp.max(s, axis=-1, keepdims=True)
        e = jnp.exp(s - m)
        l = jnp.sum(e, axis=-1, keepdims=True)
        o = jnp.dot(e.astype(BF16), v_ref[...], preferred_element_type=F32) / l
        ao_ref[j] = o.astype(ao_ref.dtype)

    @pl.when(j >= X_HEADS)
    def _():
        acc = xr_ref[...]
        for h in range(X_HEADS):
            acc = acc + jnp.dot(ao_ref[h], wo_ref[j - X_HEADS, h * hd:(h + 1) * hd, :],
                                preferred_element_type=F32)
        o_ref[...] = acc


def cross_attention_block(x, g, w_q, kv, w_o, *, batch, seq, mem_len, tm):
    m, d = x.shape
    hd = d // X_HEADS
    tiles_per_batch = seq // tm
    head = lambda j: jnp.minimum(j, X_HEADS - 1)
    col = lambda j: jnp.maximum(j - X_HEADS, 0)
    return pl.pallas_call(
        functools.partial(_cross_block_kernel, scale=hd ** -0.5),
        grid=(m // tm, 2 * X_HEADS),
        in_specs=[
            pl.BlockSpec((tm, d), lambda i, j: (i, 0)),
            pl.BlockSpec((1, d), lambda i, j: (0, 0)),
            _resident((X_HEADS, d, hd)),
            pl.BlockSpec((mem_len, hd), lambda i, j: (i // tiles_per_batch, head(j))),
            pl.BlockSpec((mem_len, hd), lambda i, j: (i // tiles_per_batch, X_HEADS + head(j))),
            _resident((X_HEADS, d, hd)),
            pl.BlockSpec((tm, hd), lambda i, j: (i, col(j))),
        ],
        out_specs=pl.BlockSpec((tm, hd), lambda i, j: (i, col(j))),
        out_shape=jax.ShapeDtypeStruct((m, d), F32),
        scratch_shapes=[pltpu.VMEM((tm, d), BF16), pltpu.VMEM((X_HEADS, tm, hd), BF16)],
        compiler_params=_cparams("parallel", "arbitrary"),
        name="cross_attention_block",
    )(x, g.reshape(1, d), w_q, kv, kv, w_o, x)


def _pack_w_in(w):
    nl, d, _ = w.shape
    sizes = (1024, 1024, 1024, B_DK, B_DK, B_DV, B_GATE_RANK, B_DV, C_Q_RANK, C_KV_RANK, C_ROPE, N_BRANCH * d)
    offs = np.concatenate([[0], np.cumsum(sizes)])
    aq, ak, av, bq, bk, bv, lr, br, cqa, ckva, ckr, gates = [
        w[:, :, offs[i]:offs[i + 1]].astype(BF16) for i in range(len(sizes))]
    zeros = lambda n: jnp.zeros((nl, d, n), BF16)
    used = Z_CKR + LANES
    return jnp.concatenate(
        [aq, ak, av, bv, br, cqa, ckva, bq, bk,
         lr, zeros(LANES - B_GATE_RANK), ckr, zeros(LANES - C_ROPE), zeros(Z_WIDTH - used), gates], axis=2)


def _pack_w_qb(w):
    r = w.shape[0]
    w = w.reshape(r, C_HEADS, C_NOPE + C_ROPE)
    w = jnp.pad(w, ((0, 0), (0, 0), (0, C_QW - C_NOPE - C_ROPE)))
    return w.reshape(r, C_HEADS * C_QW).astype(BF16)


def _pack_w_kvb(w):
    r = w.shape[0]
    w = w.reshape(r, C_HEADS, C_NOPE + C_V)
    wk = w[:, :, :C_NOPE].reshape(r, C_HEADS * C_NOPE)
    wv = w[:, :, C_NOPE:].reshape(r, C_HEADS * C_V)
    return wk.astype(BF16), wv.astype(BF16)


def kernel(x, mem, rel_bias, norm_mix, w_in, b_gate, w_alpha, b_alpha, gla_norm, q_a_norm, w_qb, kv_a_norm, w_kvb, w_up_a, w_up_b, w_up_c, w_o, norm_x, norm_mem, w_xq, w_xkv, w_xo, norm_ffn, w_ffn_gate, w_ffn_up, w_ffn_down, norm_final):
    batch, seq, d = x.shape
    mem_len = mem.shape[1]
    depth = w_in.shape[0]
    t = batch * seq
    assert d == 2048 and seq % A_TILE == 0, "tiling is derived for the stated shapes"

    xf = x.reshape(t, d)
    memf = mem.reshape(batch * mem_len, d)
    bias_tab = dilated_bias_table(rel_bias)
    q_tab, k_tab = _rope_tables(seq)
    bf = lambda a: a.astype(BF16)
    w_in_p = _pack_w_in(w_in)

    for l in range(depth):
        z, gates = input_projection(xf, norm_mix[l], w_in_p, tm=1024, tn=768, layer=l)
        o_a = dilated_attention(z, bias_tab, batch=batch, seq=seq)
        w_alpha_p = jnp.pad(w_alpha[l], ((0, LANES - B_GATE_RANK), (0, 0)))
        o_b = gated_linear_attention(z, w_alpha_p, b_alpha[l], gla_norm[l], batch=batch, seq=seq, tc=512)
        wk_p, wv_p = _pack_w_kvb(w_kvb[l])
        cq, ckn, ckr, cv = mla_project(z, q_a_norm[l], kv_a_norm[l], _pack_w_qb(w_qb[l]), wk_p, wv_p,
                                       q_tab, k_tab, seq=seq, tm=512)
        o_c = mla_flash(cq, ckn, ckr, cv, batch=batch, seq=seq, tq=1024)
        ct = lambda w: _column_tiles(w, 512)
        xf = mixer_output_block(xf, o_a, o_b, o_c, ct(w_up_a[l]), ct(w_up_b[l]), ct(w_up_c[l]), gates, b_gate[l],
                                ct(w_o[l]), tm=1024, tn=512)
        xkv = norm_matmul(memf, norm_mem[l], w_xkv, out_dtype=BF16, tm=batch * mem_len, tn=512, layer=l)
        xf = cross_attention_block(xf, norm_x[l], _column_tiles(w_xq[l], d // X_HEADS), xkv,
                                   _column_tiles(w_xo[l], d // X_HEADS),
                                   batch=batch, seq=seq, mem_len=mem_len, tm=1024)
        act = norm_swiglu(xf, norm_ffn[l], w_ffn_gate, w_ffn_up, tm=1024, tn=512, layer=l)
        xf = matmul_residual(act, bf(w_ffn_down[l]), xf, tm=1024, tn=512)
    return rmsnorm_rows(xf, norm_final, tm=512).reshape(batch, seq, d)
```

```python
import functools
import math

import numpy as np
import jax
import jax.numpy as jnp
from jax import lax
from jax.experimental import pallas as pl
from jax.experimental.pallas import tpu as pltpu

F32 = jnp.float32
BF16 = jnp.bfloat16
EPS = 1e-6
NEG = -1e30

HEAD_DIM = 128
A_HEADS = 8
A_DILATIONS = (1, 4, 16)
A_BLOCK = 128
N_BUCKETS = 32
MAX_DISTANCE = 2048
B_HEADS = 4
B_DK = 512
B_DV = 1024
B_GATE_RANK = 16
B_GATE_TAU = 16.0
B_CHUNK = 64
C_HEADS = 8
C_Q_RANK = 512
C_KV_RANK = 512
C_NOPE = 128
C_ROPE = 64
C_V = 128
ROPE_THETA = 10000.0
X_HEADS = 4
N_BRANCH = 3

VMEM_LIMIT_BYTES = 56 * 1024 * 1024
LANES = 128

Z_AQ, Z_AK, Z_AV = 0, 1024, 2048
Z_BV, Z_BR = 3072, 4096
Z_CQA, Z_CKVA = 5120, 5632
Z_BQ, Z_BK = 6144, 6656
Z_LR = 7168
Z_CKR = 7296
Z_WIDTH = 7680
G_WIDTH = 3 * 2048
W_IN_TN = 1536
A_TILE = 2048


def _cparams(*sem):
    return pltpu.CompilerParams(dimension_semantics=sem, vmem_limit_bytes=VMEM_LIMIT_BYTES)


def _rmsnorm_rows(x_ref, g_ref, h_ref):
    rows = x_ref.shape[0]

    def body(i, carry):
        r = pl.multiple_of(i * 16, 16)
        x = x_ref[pl.ds(r, 16), :]
        ms = jnp.mean(x * x, axis=-1, keepdims=True)
        h_ref[pl.ds(r, 16), :] = (x * lax.rsqrt(ms + EPS) * g_ref[...]).astype(BF16)
        return carry

    lax.fori_loop(0, rows // 16, body, 0, unroll=8)


def _norm_matmul_kernel(x_ref, g_ref, w_ref, o_ref, h_ref, *, scale):
    @pl.when(pl.program_id(1) == 0)
    def _():
        _rmsnorm_rows(x_ref, g_ref, h_ref)

    acc = jnp.dot(h_ref[...], w_ref[...].astype(BF16), preferred_element_type=F32)
    if scale != 1.0:
        acc = acc * scale
    o_ref[...] = acc.astype(o_ref.dtype)


def _weight_spec(w, layer, tn, col0=0):
    if w.ndim == 2:
        return pl.BlockSpec((w.shape[0], tn), lambda i, j: (0, col0 + j))
    return pl.BlockSpec((None, w.shape[1], tn), lambda i, j: (layer, 0, col0 + j))


def norm_matmul(x, g, w, *, out_dtype, tm, tn, scale=1.0, layer=None, col0=0, n=None):
    m, k = x.shape
    n = w.shape[-1] if n is None else n
    return pl.pallas_call(
        functools.partial(_norm_matmul_kernel, scale=scale),
        grid=(m // tm, n // tn),
        in_specs=[
            pl.BlockSpec((tm, k), lambda i, j: (i, 0)),
            pl.BlockSpec((1, k), lambda i, j: (0, 0)),
            _weight_spec(w, layer, tn, col0),
        ],
        out_specs=pl.BlockSpec((tm, tn), lambda i, j: (i, j)),
        out_shape=jax.ShapeDtypeStruct((m, n), out_dtype),
        scratch_shapes=[pltpu.VMEM((tm, k), BF16)],
        compiler_params=_cparams("parallel", "arbitrary"),
        name="norm_matmul",
    )(x, g.reshape(1, k), w)


def _norm_swiglu_kernel(x_ref, g_ref, wg_ref, wu_ref, o_ref, h_ref):
    @pl.when(pl.program_id(1) == 0)
    def _():
        _rmsnorm_rows(x_ref, g_ref, h_ref)

    h = h_ref[...]
    a = jnp.dot(h, wg_ref[...].astype(BF16), preferred_element_type=F32)
    b = jnp.dot(h, wu_ref[...].astype(BF16), preferred_element_type=F32)
    o_ref[...] = (a * jax.nn.sigmoid(a) * b).astype(o_ref.dtype)


def norm_swiglu(x, g, wg, wu, *, tm, tn, layer=None):
    m, k = x.shape
    n = wg.shape[-1]
    return pl.pallas_call(
        _norm_swiglu_kernel,
        grid=(m // tm, n // tn),
        in_specs=[
            pl.BlockSpec((tm, k), lambda i, j: (i, 0)),
            pl.BlockSpec((1, k), lambda i, j: (0, 0)),
            _weight_spec(wg, layer, tn),
            _weight_spec(wu, layer, tn),
        ],
        out_specs=pl.BlockSpec((tm, tn), lambda i, j: (i, j)),
        out_shape=jax.ShapeDtypeStruct((m, n), BF16),
        scratch_shapes=[pltpu.VMEM((tm, k), BF16)],
        compiler_params=_cparams("parallel", "arbitrary"),
        name="norm_swiglu",
    )(x, g.reshape(1, k), wg, wu)


def _matmul_residual_kernel(a_ref, w_ref, r_ref, o_ref):
    o_ref[...] = r_ref[...] + jnp.dot(a_ref[...], w_ref[...].astype(BF16), preferred_element_type=F32)


def matmul_residual(a, w, res, *, tm, tn, layer=None):
    m, k = a.shape
    n = w.shape[-1]
    return pl.pallas_call(
        _matmul_residual_kernel,
        grid=(m // tm, n // tn),
        in_specs=[
            pl.BlockSpec((tm, k), lambda i, j: (i, 0)),
            _weight_spec(w, layer, tn),
            pl.BlockSpec((tm, tn), lambda i, j: (i, j)),
        ],
        out_specs=pl.BlockSpec((tm, tn), lambda i, j: (i, j)),
        out_shape=jax.ShapeDtypeStruct((m, n), F32),
        compiler_params=_cparams("parallel", "parallel"),
        name="matmul_residual",
    )(a, w, res)


def _rmsnorm_kernel(x_ref, g_ref, o_ref):
    x = x_ref[...]
    ms = jnp.mean(x * x, axis=-1, keepdims=True)
    o_ref[...] = x * lax.rsqrt(ms + EPS) * g_ref[...]


def rmsnorm_rows(x, g, *, tm):
    m, k = x.shape
    return pl.pallas_call(
        _rmsnorm_kernel,
        grid=(m // tm,),
        in_specs=[pl.BlockSpec((tm, k), lambda i: (i, 0)), pl.BlockSpec((1, k), lambda i: (0, 0))],
        out_specs=pl.BlockSpec((tm, k), lambda i: (i, 0)),
        out_shape=jax.ShapeDtypeStruct((m, k), F32),
        compiler_params=_cparams("parallel"),
        name="final_rmsnorm",
    )(x, g.reshape(1, k))


def _t5_bucket_np(dist):
    max_exact = N_BUCKETS // 2
    n = np.maximum(dist, 1).astype(np.float64)
    large = max_exact + (np.log(n / max_exact) / math.log(MAX_DISTANCE / max_exact)
                         * (N_BUCKETS - max_exact)).astype(np.int32)
    large = np.minimum(large, N_BUCKETS - 1)
    return np.where(dist < max_exact, dist, large).astype(np.int32)


def _dilated_bucket_table():
    qi = np.arange(A_BLOCK)[:, None]
    kj = np.arange(2 * A_BLOCK)[None, :]
    steps_back = qi + A_BLOCK - kj
    valid = (steps_back >= 0) & (steps_back <= A_BLOCK)
    tabs = []
    for dil in A_DILATIONS:
        bucket = _t5_bucket_np(np.clip(steps_back, 0, A_BLOCK) * dil)
        tabs.append(np.where(valid, bucket, -1))
    return np.stack(tabs).astype(np.int32)


def _bias_table_kernel(idx_ref, rb_ref, o_ref):
    h = pl.program_id(1)
    idx = idx_ref[...]
    acc = jnp.full(idx.shape, NEG, F32)
    for b in range(N_BUCKETS):
        acc = jnp.where(idx == b, rb_ref[b, h], acc)
    o_ref[...] = acc


def dilated_bias_table(rel_bias):
    idx = jnp.asarray(_dilated_bucket_table())
    npat = len(A_DILATIONS)
    return pl.pallas_call(
        _bias_table_kernel,
        grid=(npat, A_HEADS),
        in_specs=[
            pl.BlockSpec((None, A_BLOCK, 2 * A_BLOCK), lambda p, h: (p, 0, 0)),
            pl.BlockSpec(memory_space=pltpu.SMEM),
        ],
        out_specs=pl.BlockSpec((None, None, A_BLOCK, 2 * A_BLOCK), lambda p, h: (p, h, 0, 0)),
        out_shape=jax.ShapeDtypeStruct((npat, A_HEADS, A_BLOCK, 2 * A_BLOCK), F32),
        compiler_params=_cparams("parallel", "parallel"),
        name="dilated_bias_table",
    )(idx, rel_bias)


def _dilated_kernel(q_ref, kc_ref, kp_ref, vc_ref, vp_ref, bias_ref, o_ref, bm0, o_scr, lse_scr, *, scale):
    t = pl.program_id(2)
    tile = q_ref.shape[0]
    blk = A_BLOCK
    col = lax.broadcasted_iota(jnp.int32, (blk, 2 * blk), 1)
    no_prev = jnp.logical_and(col < blk, t == 0)
    for p in range(len(A_DILATIONS)):
        bm0[p] = jnp.where(no_prev, NEG, bias_ref[p])

    def rows(start, size, dil):
        return pl.ds(start, size) if dil == 1 else pl.ds(start, size, stride=dil)

    def block(p, dil, r, n):
        q_idx = rows(r + dil * blk * n, blk, dil)
        if n == 0:
            prev = rows(tile - dil * blk + r, blk, dil)
            cur = rows(r, blk, dil)
            k = jnp.concatenate([kp_ref[prev, :], kc_ref[cur, :]], axis=0)
            v = jnp.concatenate([vp_ref[prev, :], vc_ref[cur, :]], axis=0)
            bm = bm0[p]
        else:
            k_idx = rows(r + dil * blk * (n - 1), 2 * blk, dil)
            k, v = kc_ref[k_idx, :], vc_ref[k_idx, :]
            bm = bias_ref[p]
        q = (q_ref[q_idx, :] * scale).astype(BF16)
        s = lax.dot_general(q, k.astype(BF16), (((1,), (1,)), ((), ())), preferred_element_type=F32)
        s = jnp.where(bm > 0.5 * NEG, s + bm, NEG)
        m = jnp.max(s, axis=-1, keepdims=True)
        e = jnp.exp(s - m)
        l = jnp.sum(e, axis=-1, keepdims=True)
        o = jnp.dot(e.astype(BF16), v.astype(BF16), preferred_element_type=F32) / l
        lse = m + jnp.log(l)
        o_scr[p, q_idx, :] = o
        lse_scr[p, q_idx, :] = jnp.broadcast_to(lse, (blk, HEAD_DIM))

    for p, dil in enumerate(A_DILATIONS):
        for r in range(dil):
            for n in range(tile // (blk * dil)):
                block(p, dil, r, n)

    def merge(i, carry):
        r = pl.multiple_of(i * blk, blk)
        rows = pl.ds(r, blk)
        l0, l1, l2 = lse_scr[0, rows, :], lse_scr[1, rows, :], lse_scr[2, rows, :]
        mx = jnp.maximum(jnp.maximum(l0, l1), l2)
        w0, w1, w2 = jnp.exp(l0 - mx), jnp.exp(l1 - mx), jnp.exp(l2 - mx)
        num = w0 * o_scr[0, rows, :] + w1 * o_scr[1, rows, :] + w2 * o_scr[2, rows, :]
        o_ref[rows, :] = (num / (w0 + w1 + w2)).astype(o_ref.dtype)
        return carry

    lax.fori_loop(0, tile // blk, merge, 0)


def dilated_attention(z, bias_tab, *, batch, seq):
    z3 = z.reshape(batch, seq, z.shape[1])
    nt = seq // A_TILE
    hq, hk, hv = Z_AQ // HEAD_DIM, Z_AK // HEAD_DIM, Z_AV // HEAD_DIM
    tile_spec = lambda col0, prev: pl.BlockSpec(
        (None, A_TILE, HEAD_DIM),
        (lambda b, h, t: (b, jnp.maximum(t - 1, 0), col0 + h)) if prev else (lambda b, h, t: (b, t, col0 + h)))
    npat = len(A_DILATIONS)
    out = pl.pallas_call(
        functools.partial(_dilated_kernel, scale=HEAD_DIM ** -0.5),
        grid=(batch, A_HEADS, nt),
        in_specs=[
            tile_spec(hq, False),
            tile_spec(hk, False), tile_spec(hk, True),
            tile_spec(hv, False), tile_spec(hv, True),
            pl.BlockSpec((npat, None, A_BLOCK, 2 * A_BLOCK), lambda b, h, t: (0, h, 0, 0)),
        ],
        out_specs=pl.BlockSpec((None, A_TILE, HEAD_DIM), lambda b, h, t: (b, t, h)),
        out_shape=jax.ShapeDtypeStruct((batch, seq, A_HEADS * HEAD_DIM), BF16),
        scratch_shapes=[
            pltpu.VMEM((npat, A_BLOCK, 2 * A_BLOCK), F32),
            pltpu.VMEM((npat, A_TILE, HEAD_DIM), F32),
            pltpu.VMEM((npat, A_TILE, HEAD_DIM), F32),
        ],
        compiler_params=_cparams("parallel", "parallel", "arbitrary"),
        name="dilated_attention",
    )(z3, z3, z3, z3, z3, bias_tab)
    return out.reshape(batch * seq, A_HEADS * HEAD_DIM)


def _gla_kernel(q_ref, k_ref, v_ref, r_ref, lr_ref, wa_ref, ba_ref, gn_ref, o_ref, st_ref, *, scale):
    @pl.when(pl.program_id(1) == 0)
    def _():
        st_ref[...] = jnp.zeros_like(st_ref)

    c = B_CHUNK
    tc = q_ref.shape[0]
    dk = B_DK // B_HEADS
    dv = B_DV // B_HEADS
    pre = jnp.dot(lr_ref[...], wa_ref[...], precision=lax.Precision.HIGHEST,
                  preferred_element_type=F32) + ba_ref[...]
    log_a = (jnp.minimum(pre, 0.0) - jnp.log(1.0 + jnp.exp(-jnp.abs(pre)))) * (1.0 / B_GATE_TAU)
    tril = lax.broadcasted_iota(jnp.int32, (c, c), 0) >= lax.broadcasted_iota(jnp.int32, (c, c), 1)
    ones_tril = jnp.where(tril, 1.0, 0.0).astype(F32)
    for ci in range(tc // c):
        sl = slice(ci * c, (ci + 1) * c)
        cum_all = jnp.dot(ones_tril, log_a[sl, :], precision=lax.Precision.HIGHEST, preferred_element_type=F32)
        for h in range(B_HEADS):
            kcols = slice(h * dk, (h + 1) * dk)
            vcols = slice(h * dv, (h + 1) * dv)
            cum = cum_all[:, kcols]
            cum_last = cum[c - 1:c, :]
            q = q_ref[sl, kcols] * scale
            k = k_ref[sl, kcols]
            v = v_ref[sl, vcols]
            q_dec = (q * jnp.exp(cum)).astype(BF16)
            k_inv = (k * jnp.exp(-cum)).astype(BF16)
            k_state = (k * jnp.exp(cum_last - cum)).astype(BF16)
            att = lax.dot_general(q_dec, k_inv, (((1,), (1,)), ((), ())), preferred_element_type=F32)
            att = jnp.where(tril, att, 0.0).astype(BF16)
            st = st_ref[h]
            o = (jnp.dot(att, v.astype(BF16), preferred_element_type=F32)
                 + lax.dot_general(q_dec, st.astype(BF16), (((1,), (1,)), ((), ())), preferred_element_type=F32))
            st_ref[h] = st * jnp.exp(cum_last) + jnp.dot(v.T.astype(BF16), k_state, preferred_element_type=F32)
            ms = jnp.mean(o * o, axis=-1, keepdims=True)
            o = o * lax.rsqrt(ms + EPS) * gn_ref[:, vcols]
            r = r_ref[sl, vcols]
            o_ref[sl, vcols] = (o * (r * jax.nn.sigmoid(r))).astype(o_ref.dtype)


def gated_linear_attention(z, w_alpha_p, b_alpha, gla_norm, *, batch, seq, tc):
    dk = B_DK // B_HEADS
    dv = B_DV // B_HEADS
    nt = seq // tc
    row = lambda b, t: b * nt + t
    const = lambda b, t: (0, 0)
    return pl.pallas_call(
        functools.partial(_gla_kernel, scale=dk ** -0.5),
        grid=(batch, nt),
        in_specs=[
            pl.BlockSpec((tc, B_DK), lambda b, t: (row(b, t), Z_BQ // B_DK)),
            pl.BlockSpec((tc, B_DK), lambda b, t: (row(b, t), Z_BK // B_DK)),
            pl.BlockSpec((tc, B_DV), lambda b, t: (row(b, t), Z_BV // B_DV)),
            pl.BlockSpec((tc, B_DV), lambda b, t: (row(b, t), Z_BR // B_DV)),
            pl.BlockSpec((tc, LANES), lambda b, t: (row(b, t), Z_LR // LANES)),
            pl.BlockSpec((LANES, B_DK), const),
            pl.BlockSpec((1, B_DK), const),
            pl.BlockSpec((1, B_DV), const),
        ],
        out_specs=pl.BlockSpec((tc, B_DV), lambda b, t: (row(b, t), 0)),
        out_shape=jax.ShapeDtypeStruct((batch * seq, B_DV), BF16),
        scratch_shapes=[pltpu.VMEM((B_HEADS, dv, dk), F32)],
        compiler_params=_cparams("parallel", "arbitrary"),
        name="gated_linear_attention",
    )(z, z, z, z, z, w_alpha_p, b_alpha.reshape(1, B_DK), gla_norm.reshape(1, B_DV))


C_QW = 2 * C_NOPE


def _rope_tables(seq):
    pos = jnp.arange(seq, dtype=F32)
    inv = ROPE_THETA ** (-jnp.arange(0, C_ROPE, 2, dtype=F32) / C_ROPE)
    ang = pos[:, None] * inv[None, :]
    cos, sin = jnp.cos(ang), jnp.sin(ang)
    half = C_ROPE // 2
    z = lambda w: jnp.zeros((seq, w), F32)
    k_tab = jnp.stack([
        jnp.concatenate([cos, cos, z(LANES - C_ROPE)], axis=1),
        jnp.concatenate([-sin, z(half), z(LANES - C_ROPE)], axis=1),
        jnp.concatenate([z(half), sin, z(LANES - C_ROPE)], axis=1)])
    q_tab = jnp.concatenate([
        jnp.stack([jnp.ones((seq, C_NOPE), F32), z(C_NOPE), z(C_NOPE)]), k_tab], axis=2)
    return q_tab, k_tab


def _mla_proj_kernel(cq_ref, ckv_ref, ckr_ref, gq_ref, gkv_ref, wq_ref, wk_ref, wv_ref, qt_ref, kt_ref,
                     q_out, kn_out, kr_out, v_out, *, scale):
    half = C_ROPE // 2

    def rms(x, g):
        ms = jnp.mean(x * x, axis=-1, keepdims=True)
        return (x * lax.rsqrt(ms + EPS) * g).astype(BF16)

    cq = rms(cq_ref[...], gq_ref[...])
    q = jnp.dot(cq, wq_ref[...], preferred_element_type=F32)
    t0, t1, t2 = qt_ref[0], qt_ref[1], qt_ref[2]
    for h in range(C_HEADS):
        qh = q[:, h * C_QW:(h + 1) * C_QW]
        qh = qh * t0 + pltpu.roll(qh, C_QW - half, axis=1) * t1 + pltpu.roll(qh, half, axis=1) * t2
        q_out[:, h * C_QW:(h + 1) * C_QW] = (qh * scale).astype(q_out.dtype)
    ckv = rms(ckv_ref[...], gkv_ref[...])
    kn_out[...] = jnp.dot(ckv, wk_ref[...], preferred_element_type=F32).astype(kn_out.dtype)
    v_out[...] = jnp.dot(ckv, wv_ref[...], preferred_element_type=F32).astype(v_out.dtype)
    kr = ckr_ref[...]
    kr = kr * kt_ref[0] + pltpu.roll(kr, LANES - half, axis=1) * kt_ref[1] + pltpu.roll(kr, half, axis=1) * kt_ref[2]
    kr_out[...] = kr.astype(kr_out.dtype)


def mla_project(z, q_a_norm, kv_a_norm, wq_p, wk_p, wv_p, q_tab, k_tab, *, seq, tm):
    m = z.shape[0]
    nt = seq // tm
    const = lambda i: (0, 0)
    return pl.pallas_call(
        functools.partial(_mla_proj_kernel, scale=(C_NOPE + C_ROPE) ** -0.5 * math.log2(math.e)),
        grid=(m // tm,),
        in_specs=[
            pl.BlockSpec((tm, C_Q_RANK), lambda i: (i, Z_CQA // C_Q_RANK)),
            pl.BlockSpec((tm, C_KV_RANK), lambda i: (i, Z_CKVA // C_KV_RANK)),
            pl.BlockSpec((tm, LANES), lambda i: (i, Z_CKR // LANES)),
            pl.BlockSpec((1, C_Q_RANK), const),
            pl.BlockSpec((1, C_KV_RANK), const),
            pl.BlockSpec((C_Q_RANK, C_HEADS * C_QW), const),
            pl.BlockSpec((C_KV_RANK, C_HEADS * C_NOPE), const),
            pl.BlockSpec((C_KV_RANK, C_HEADS * C_V), const),
            pl.BlockSpec((3, tm, C_QW), lambda i: (0, i % nt, 0)),
            pl.BlockSpec((3, tm, LANES), lambda i: (0, i % nt, 0)),
        ],
        out_specs=[
            pl.BlockSpec((tm, C_HEADS * C_QW), lambda i: (i, 0)),
            pl.BlockSpec((tm, C_HEADS * C_NOPE), lambda i: (i, 0)),
            pl.BlockSpec((tm, LANES), lambda i: (i, 0)),
            pl.BlockSpec((tm, C_HEADS * C_V), lambda i: (i, 0)),
        ],
        out_shape=[
            jax.ShapeDtypeStruct((m, C_HEADS * C_QW), BF16),
            jax.ShapeDtypeStruct((m, C_HEADS * C_NOPE), BF16),
            jax.ShapeDtypeStruct((m, LANES), BF16),
            jax.ShapeDtypeStruct((m, C_HEADS * C_V), BF16),
        ],
        compiler_params=_cparams("parallel"),
        name="mla_project",
    )(z, z, z, q_a_norm.reshape(1, -1), kv_a_norm.reshape(1, -1), wq_p, wk_p, wv_p, q_tab, k_tab)


def _mla_flash_kernel(q_ref, kn_ref, kr_ref, v_ref, o_ref, m_ref, l_ref, acc_ref):
    tq = m_ref.shape[0]
    tk = tq // 2

    def q_tile(qi, carry):
        q0 = pl.multiple_of(qi * tq, tq)
        m_ref[...] = jnp.full_like(m_ref, NEG)
        l_ref[...] = jnp.zeros_like(l_ref)
        acc_ref[...] = jnp.zeros_like(acc_ref)

        def step(r0, nr, k_start, nk, mask_offset=None):
            rows = slice(r0, r0 + nr)
            ks = pl.ds(pl.multiple_of(k_start, tk), nk)
            k = jnp.concatenate([kn_ref[ks, :], kr_ref[ks, :]], axis=1)
            q = q_ref[pl.ds(pl.multiple_of(q0 + r0, tk), nr), :]
            s = lax.dot_general(q, k, (((1,), (1,)), ((), ())), preferred_element_type=F32)
            if mask_offset is not None:
                visible = (lax.broadcasted_iota(jnp.int32, s.shape, 1)
                           <= lax.broadcasted_iota(jnp.int32, s.shape, 0) + mask_offset)
                s = jnp.where(visible, s, NEG)
            m_prev = m_ref[rows, :]
            m_new = jnp.maximum(m_prev, jnp.max(s, axis=-1, keepdims=True))
            alpha = jnp.exp2(m_prev - m_new)
            p = jnp.exp2(s - jnp.concatenate([m_new] * (nk // LANES), axis=1))
            l_ref[rows, :] = alpha * l_ref[rows, :] + jnp.sum(p, axis=-1, keepdims=True)
            acc_ref[rows, :] = alpha * acc_ref[rows, :] + jnp.dot(p.astype(BF16), v_ref[ks, :],
                                                                  preferred_element_type=F32)
            m_ref[rows, :] = m_new

        def body(kb, c):
            step(0, tq, 2 * kb * tk, tk)
            step(0, tq, (2 * kb + 1) * tk, tk)
            return c

        lax.fori_loop(0, qi, body, 0)
        step(0, tk, q0, tk, 0)
        step(tk, tk, q0, tq, tk)
        o_ref[pl.ds(q0, tq), :] = (acc_ref[...] / l_ref[...]).astype(o_ref.dtype)
        return carry

    lax.fori_loop(0, q_ref.shape[0] // tq, q_tile, 0)


def mla_flash(q, kn, kr, v, *, batch, seq, tq):
    return pl.pallas_call(
        _mla_flash_kernel,
        grid=(batch, C_HEADS),
        in_specs=[
            pl.BlockSpec((seq, C_QW), lambda b, h: (b, h)),
            pl.BlockSpec((seq, C_NOPE), lambda b, h: (b, h)),
            pl.BlockSpec((seq, LANES), lambda b, h: (b, 0)),
            pl.BlockSpec((seq, C_V), lambda b, h: (b, h)),
        ],
        out_specs=pl.BlockSpec((seq, C_V), lambda b, h: (b, h)),
        out_shape=jax.ShapeDtypeStruct((batch * seq, C_HEADS * C_V), BF16),
        scratch_shapes=[
            pltpu.VMEM((tq, LANES), F32),
            pltpu.VMEM((tq, LANES), F32),
            pltpu.VMEM((tq, C_V), F32),
        ],
        compiler_params=_cparams("parallel", "parallel"),
        name="mla_flash",
    )(q, kn, kr, v)


def _mixer_out_kernel(oa_ref, ob_ref, oc_ref, wa_ref, wb_ref, wc_ref, g0_ref, g1_ref, g2_ref,
                      b0_ref, b1_ref, b2_ref, wo_ref, xr_ref, o_ref, mg_ref):
    j = pl.program_id(1)
    nj, _, tn = mg_ref.shape

    def branch(o, w, g, b):
        return jax.nn.sigmoid(g[...] + b[...]) * jnp.dot(o[...], w[j], preferred_element_type=F32)

    @pl.when(j < nj)
    def _():
        acc = branch(oa_ref, wa_ref, g0_ref, b0_ref)
        acc = acc + branch(ob_ref, wb_ref, g1_ref, b1_ref)
        acc = acc + branch(oc_ref, wc_ref, g2_ref, b2_ref)
        mg_ref[j] = acc.astype(mg_ref.dtype)

    @pl.when(j >= nj)
    def _():
        acc = xr_ref[...]
        for c in range(nj):
            acc = acc + jnp.dot(mg_ref[c], wo_ref[j - nj, c * tn:(c + 1) * tn, :], preferred_element_type=F32)
        o_ref[...] = acc


def _column_tiles(w, tn):
    k, n = w.shape
    return w.reshape(k, n // tn, tn).transpose(1, 0, 2).astype(BF16)


def _resident(shape):
    return pl.BlockSpec(shape, lambda i, j: (0,) * len(shape), pipeline_mode=pl.Buffered(1))


def mixer_output_block(x, o_a, o_b, o_c, w_a, w_b, w_c, gates, b_gate, w_o, *, tm, tn):
    m, k = o_a.shape
    nj = w_a.shape[0]
    d = nj * tn
    mcol = lambda j: jnp.minimum(j, nj - 1)
    ocol = lambda j: jnp.maximum(j - nj, 0)
    act = pl.BlockSpec((tm, k), lambda i, j: (i, 0))
    wgt = _resident((nj, k, tn))
    gate = lambda br: pl.BlockSpec((tm, tn), lambda i, j: (i, br * nj + mcol(j)))
    gbias = lambda br: pl.BlockSpec((1, tn), lambda i, j: (0, br * nj + mcol(j)))
    return pl.pallas_call(
        _mixer_out_kernel,
        grid=(m // tm, 2 * nj),
        in_specs=[act, act, act, wgt, wgt, wgt, gate(0), gate(1), gate(2), gbias(0), gbias(1), gbias(2),
                  _resident((nj, d, tn)),
                  pl.BlockSpec((tm, tn), lambda i, j: (i, ocol(j)))],
        out_specs=pl.BlockSpec((tm, tn), lambda i, j: (i, ocol(j))),
        out_shape=jax.ShapeDtypeStruct((m, d), F32),
        scratch_shapes=[pltpu.VMEM((nj, tm, tn), BF16)],
        compiler_params=_cparams("parallel", "arbitrary"),
        name="mixer_output_block",
    )(o_a, o_b, o_c, w_a, w_b, w_c, gates, gates, gates, *([b_gate.reshape(1, -1)] * 3), w_o, x)


def _cross_block_kernel(x_ref, g_ref, wq_ref, k_ref, v_ref, wo_ref, xr_ref, o_ref, h_ref, ao_ref, *, scale):
    j = pl.program_id(1)
    hd = k_ref.shape[1]

    @pl.when(j == 0)
    def _():
        _rmsnorm_rows(x_ref, g_ref, h_ref)

    @pl.when(j < X_HEADS)
    def _():
        q = jnp.dot(h_ref[...], wq_ref[j], preferred_element_type=F32) * scale
        s = lax.dot_general(q.astype(BF16), k_ref[...], (((1,), (1,)), ((), ())), preferred_element_type=F32)
        m = jnp.max(s, axis=-1, keepdims=True)
        e = jnp.exp(s - m)
        l = jnp.sum(e, axis=-1, keepdims=True)
        o = jnp.dot(e.astype(BF16), v_ref[...], preferred_element_type=F32) / l
        ao_ref[j] = o.astype(ao_ref.dtype)

    @pl.when(j >= X_HEADS)
    def _():
        acc = xr_ref[...]
        for h in range(X_HEADS):
            acc = acc + jnp.dot(ao_ref[h], wo_ref[j - X_HEADS, h * hd:(h + 1) * hd, :],
                                preferred_element_type=F32)
        o_ref[...] = acc


def cross_attention_block(x, g, w_q, kv, w_o, *, batch, seq, mem_len, tm):
    m, d = x.shape
    hd = d // X_HEADS
    tiles_per_batch = seq // tm
    head = lambda j: jnp.minimum(j, X_HEADS - 1)
    col = lambda j: jnp.maximum(j - X_HEADS, 0)
    return pl.pallas_call(
        functools.partial(_cross_block_kernel, scale=hd ** -0.5),
        grid=(m // tm, 2 * X_HEADS),
        in_specs=[
            pl.BlockSpec((tm, d), lambda i, j: (i, 0)),
            pl.BlockSpec((1, d), lambda i, j: (0, 0)),
            _resident((X_HEADS, d, hd)),
            pl.BlockSpec((mem_len, hd), lambda i, j: (i // tiles_per_batch, head(j))),
            pl.BlockSpec((mem_len, hd), lambda i, j: (i // tiles_per_batch, X_HEADS + head(j))),
            _resident((X_HEADS, d, hd)),
            pl.BlockSpec((tm, hd), lambda i, j: (i, col(j))),
        ],
        out_specs=pl.BlockSpec((tm, hd), lambda i, j: (i, col(j))),
        out_shape=jax.ShapeDtypeStruct((m, d), F32),
        scratch_shapes=[pltpu.VMEM((tm, d), BF16), pltpu.VMEM((X_HEADS, tm, hd), BF16)],
        compiler_params=_cparams("parallel", "arbitrary"),
        name="cross_attention_block",
    )(x, g.reshape(1, d), w_q, kv, kv, w_o, x)


def _pack_w_in(w):
    nl, d, _ = w.shape
    sizes = (1024, 1024, 1024, B_DK, B_DK, B_DV, B_GATE_RANK, B_DV, C_Q_RANK, C_KV_RANK, C_ROPE, N_BRANCH * d)
    offs = np.concatenate([[0], np.cumsum(sizes)])
    aq, ak, av, bq, bk, bv, lr, br, cqa, ckva, ckr, gates = [
        w[:, :, offs[i]:offs[i + 1]].astype(BF16) for i in range(len(sizes))]
    zeros = lambda n: jnp.zeros((nl, d, n), BF16)
    used = Z_CKR + LANES
    return jnp.concatenate(
        [aq, ak, av, bv, br, cqa, ckva, bq, bk,
         lr, zeros(LANES - B_GATE_RANK), ckr, zeros(LANES - C_ROPE), zeros(Z_WIDTH - used), gates], axis=2)


def _pack_w_qb(w):
    r = w.shape[0]
    w = w.reshape(r, C_HEADS, C_NOPE + C_ROPE)
    w = jnp.pad(w, ((0, 0), (0, 0), (0, C_QW - C_NOPE - C_ROPE)))
    return w.reshape(r, C_HEADS * C_QW).astype(BF16)


def _pack_w_kvb(w):
    r = w.shape[0]
    w = w.reshape(r, C_HEADS, C_NOPE + C_V)
    wk = w[:, :, :C_NOPE].reshape(r, C_HEADS * C_NOPE)
    wv = w[:, :, C_NOPE:].reshape(r, C_HEADS * C_V)
    return wk.astype(BF16), wv.astype(BF16)


def kernel(x, mem, rel_bias, norm_mix, w_in, b_gate, w_alpha, b_alpha, gla_norm, q_a_norm, w_qb, kv_a_norm, w_kvb, w_up_a, w_up_b, w_up_c, w_o, norm_x, norm_mem, w_xq, w_xkv, w_xo, norm_ffn, w_ffn_gate, w_ffn_up, w_ffn_down, norm_final):
    batch, seq, d = x.shape
    mem_len = mem.shape[1]
    depth = w_in.shape[0]
    t = batch * seq
    assert d == 2048 and seq % A_TILE == 0, "tiling is derived for the stated shapes"

    xf = x.reshape(t, d)
    memf = mem.reshape(batch * mem_len, d)
    bias_tab = dilated_bias_table(rel_bias)
    q_tab, k_tab = _rope_tables(seq)
    bf = lambda a: a.astype(BF16)
    w_in_p = _pack_w_in(w_in)

    for l in range(depth):
        z = norm_matmul(xf, norm_mix[l], w_in_p, out_dtype=F32, tm=1024, tn=W_IN_TN, layer=l, n=Z_WIDTH)
        gates = norm_matmul(xf, norm_mix[l], w_in_p, out_dtype=BF16, tm=1024, tn=W_IN_TN, layer=l,
                            col0=Z_WIDTH // W_IN_TN, n=G_WIDTH)
        o_a = dilated_attention(z, bias_tab, batch=batch, seq=seq)
        w_alpha_p = jnp.pad(w_alpha[l], ((0, LANES - B_GATE_RANK), (0, 0)))
        o_b = gated_linear_attention(z, w_alpha_p, b_alpha[l], gla_norm[l], batch=batch, seq=seq, tc=512)
        wk_p, wv_p = _pack_w_kvb(w_kvb[l])
        cq, ckn, ckr, cv = mla_project(z, q_a_norm[l], kv_a_norm[l], _pack_w_qb(w_qb[l]), wk_p, wv_p,
                                       q_tab, k_tab, seq=seq, tm=512)
        o_c = mla_flash(cq, ckn, ckr, cv, batch=batch, seq=seq, tq=1024)
        ct = lambda w: _column_tiles(w, 512)
        xf = mixer_output_block(xf, o_a, o_b, o_c, ct(w_up_a[l]), ct(w_up_b[l]), ct(w_up_c[l]), gates, b_gate[l],
                                ct(w_o[l]), tm=1024, tn=512)
        xkv = norm_matmul(memf, norm_mem[l], w_xkv, out_dtype=BF16, tm=batch * mem_len, tn=512, layer=l)
        xf = cross_attention_block(xf, norm_x[l], _column_tiles(w_xq[l], d // X_HEADS), xkv,
                                   _column_tiles(w_xo[l], d // X_HEADS),
                                   batch=batch, seq=seq, mem_len=mem_len, tm=1024)
        act = norm_swiglu(xf, norm_ffn[l], w_ffn_gate, w_ffn_up, tm=1024, tn=512, layer=l)
        xf = matmul_residual(act, bf(w_ffn_down[l]), xf, tm=1024, tn=512)
    return rmsnorm_rows(xf, norm_final, tm=512).reshape(batch, seq, d)
```

```python
import functools
import math

import numpy as np
import jax
import jax.numpy as jnp
from jax import lax
from jax.experimental import pallas as pl
from jax.experimental.pallas import tpu as pltpu

F32 = jnp.float32
BF16 = jnp.bfloat16
EPS = 1e-6
NEG = -1e30

HEAD_DIM = 128
A_HEADS = 8
A_DILATIONS = (1, 4, 16)
A_BLOCK = 128
N_BUCKETS = 32
MAX_DISTANCE = 2048
B_HEADS = 4
B_DK = 512
B_DV = 1024
B_GATE_RANK = 16
B_GATE_TAU = 16.0
B_CHUNK = 64
C_HEADS = 8
C_Q_RANK = 512
C_KV_RANK = 512
C_NOPE = 128
C_ROPE = 64
C_V = 128
ROPE_THETA = 10000.0
X_HEADS = 4
N_BRANCH = 3

VMEM_LIMIT_BYTES = 56 * 1024 * 1024
LANES = 128

Z_AQ, Z_AK, Z_AV = 0, 1024, 2048
Z_BV, Z_BR = 3072, 4096
Z_CQA, Z_CKVA = 5120, 5632
Z_BQ, Z_BK = 6144, 6656
Z_LR = 7168
Z_CKR = 7296
Z_WIDTH = 7680
G_WIDTH = 3 * 2048
W_IN_TN = 1536
A_TILE = 2048


def _cparams(*sem):
    return pltpu.CompilerParams(dimension_semantics=sem, vmem_limit_bytes=VMEM_LIMIT_BYTES)


def _rmsnorm_rows(x_ref, g_ref, h_ref):
    rows = x_ref.shape[0]

    def body(i, carry):
        r = pl.multiple_of(i * 16, 16)
        x = x_ref[pl.ds(r, 16), :]
        ms = jnp.mean(x * x, axis=-1, keepdims=True)
        h_ref[pl.ds(r, 16), :] = (x * lax.rsqrt(ms + EPS) * g_ref[...]).astype(BF16)
        return carry

    lax.fori_loop(0, rows // 16, body, 0, unroll=8)


def _norm_matmul_kernel(x_ref, g_ref, w_ref, o_ref, h_ref, *, scale):
    @pl.when(pl.program_id(1) == 0)
    def _():
        _rmsnorm_rows(x_ref, g_ref, h_ref)

    acc = jnp.dot(h_ref[...], w_ref[...].astype(BF16), preferred_element_type=F32)
    if scale != 1.0:
        acc = acc * scale
    o_ref[...] = acc.astype(o_ref.dtype)


def _weight_spec(w, layer, tn):
    if w.ndim == 2:
        return pl.BlockSpec((w.shape[0], tn), lambda i, j: (0, j))
    return pl.BlockSpec((None, w.shape[1], tn), lambda i, j: (layer, 0, j))


def norm_matmul(x, g, w, *, out_dtype, tm, tn, scale=1.0, layer=None):
    m, k = x.shape
    n = w.shape[-1]
    return pl.pallas_call(
        functools.partial(_norm_matmul_kernel, scale=scale),
        grid=(m // tm, n // tn),
        in_specs=[
            pl.BlockSpec((tm, k), lambda i, j: (i, 0)),
            pl.BlockSpec((1, k), lambda i, j: (0, 0)),
            _weight_spec(w, layer, tn),
        ],
        out_specs=pl.BlockSpec((tm, tn), lambda i, j: (i, j)),
        out_shape=jax.ShapeDtypeStruct((m, n), out_dtype),
        scratch_shapes=[pltpu.VMEM((tm, k), BF16)],
        compiler_params=_cparams("parallel", "arbitrary"),
        name="norm_matmul",
    )(x, g.reshape(1, k), w)


def _norm_swiglu_kernel(x_ref, g_ref, wg_ref, wu_ref, o_ref, h_ref):
    @pl.when(pl.program_id(1) == 0)
    def _():
        _rmsnorm_rows(x_ref, g_ref, h_ref)

    h = h_ref[...]
    a = jnp.dot(h, wg_ref[...].astype(BF16), preferred_element_type=F32)
    b = jnp.dot(h, wu_ref[...].astype(BF16), preferred_element_type=F32)
    o_ref[...] = (a * jax.nn.sigmoid(a) * b).astype(o_ref.dtype)


def norm_swiglu(x, g, wg, wu, *, tm, tn, layer=None):
    m, k = x.shape
    n = wg.shape[-1]
    return pl.pallas_call(
        _norm_swiglu_kernel,
        grid=(m // tm, n // tn),
        in_specs=[
            pl.BlockSpec((tm, k), lambda i, j: (i, 0)),
            pl.BlockSpec((1, k), lambda i, j: (0, 0)),
            _weight_spec(wg, layer, tn),
            _weight_spec(wu, layer, tn),
        ],
        out_specs=pl.BlockSpec((tm, tn), lambda i, j: (i, j)),
        out_shape=jax.ShapeDtypeStruct((m, n), BF16),
        scratch_shapes=[pltpu.VMEM((tm, k), BF16)],
        compiler_params=_cparams("parallel", "arbitrary"),
        name="norm_swiglu",
    )(x, g.reshape(1, k), wg, wu)


def _matmul_residual_kernel(a_ref, w_ref, r_ref, o_ref):
    o_ref[...] = r_ref[...] + jnp.dot(a_ref[...], w_ref[...].astype(BF16), preferred_element_type=F32)


def matmul_residual(a, w, res, *, tm, tn, layer=None):
    m, k = a.shape
    n = w.shape[-1]
    return pl.pallas_call(
        _matmul_residual_kernel,
        grid=(m // tm, n // tn),
        in_specs=[
            pl.BlockSpec((tm, k), lambda i, j: (i, 0)),
            _weight_spec(w, layer, tn),
            pl.BlockSpec((tm, tn), lambda i, j: (i, j)),
        ],
        out_specs=pl.BlockSpec((tm, tn), lambda i, j: (i, j)),
        out_shape=jax.ShapeDtypeStruct((m, n), F32),
        compiler_params=_cparams("parallel", "parallel"),
        name="matmul_residual",
    )(a, w, res)


def _rmsnorm_kernel(x_ref, g_ref, o_ref):
    x = x_ref[...]
    ms = jnp.mean(x * x, axis=-1, keepdims=True)
    o_ref[...] = x * lax.rsqrt(ms + EPS) * g_ref[...]


def rmsnorm_rows(x, g, *, tm):
    m, k = x.shape
    return pl.pallas_call(
        _rmsnorm_kernel,
        grid=(m // tm,),
        in_specs=[pl.BlockSpec((tm, k), lambda i: (i, 0)), pl.BlockSpec((1, k), lambda i: (0, 0))],
        out_specs=pl.BlockSpec((tm, k), lambda i: (i, 0)),
        out_shape=jax.ShapeDtypeStruct((m, k), F32),
        compiler_params=_cparams("parallel"),
        name="final_rmsnorm",
    )(x, g.reshape(1, k))


def _t5_bucket_np(dist):
    max_exact = N_BUCKETS // 2
    n = np.maximum(dist, 1).astype(np.float64)
    large = max_exact + (np.log(n / max_exact) / math.log(MAX_DISTANCE / max_exact)
                         * (N_BUCKETS - max_exact)).astype(np.int32)
    large = np.minimum(large, N_BUCKETS - 1)
    return np.where(dist < max_exact, dist, large).astype(np.int32)


def _dilated_bucket_table():
    qi = np.arange(A_BLOCK)[:, None]
    kj = np.arange(2 * A_BLOCK)[None, :]
    steps_back = qi + A_BLOCK - kj
    valid = (steps_back >= 0) & (steps_back <= A_BLOCK)
    tabs = []
    for dil in A_DILATIONS:
        bucket = _t5_bucket_np(np.clip(steps_back, 0, A_BLOCK) * dil)
        tabs.append(np.where(valid, bucket, -1))
    return np.stack(tabs).astype(np.int32)


def _bias_table_kernel(idx_ref, rb_ref, o_ref):
    h = pl.program_id(1)
    idx = idx_ref[...]
    acc = jnp.full(idx.shape, NEG, F32)
    for b in range(N_BUCKETS):
        acc = jnp.where(idx == b, rb_ref[b, h], acc)
    o_ref[...] = acc


def dilated_bias_table(rel_bias):
    idx = jnp.asarray(_dilated_bucket_table())
    npat = len(A_DILATIONS)
    return pl.pallas_call(
        _bias_table_kernel,
        grid=(npat, A_HEADS),
        in_specs=[
            pl.BlockSpec((None, A_BLOCK, 2 * A_BLOCK), lambda p, h: (p, 0, 0)),
            pl.BlockSpec(memory_space=pltpu.SMEM),
        ],
        out_specs=pl.BlockSpec((None, None, A_BLOCK, 2 * A_BLOCK), lambda p, h: (p, h, 0, 0)),
        out_shape=jax.ShapeDtypeStruct((npat, A_HEADS, A_BLOCK, 2 * A_BLOCK), F32),
        compiler_params=_cparams("parallel", "parallel"),
        name="dilated_bias_table",
    )(idx, rel_bias)


def _dilated_kernel(q_ref, kc_ref, kp_ref, vc_ref, vp_ref, bias_ref, o_ref, bm0, o_scr, lse_scr, *, scale):
    t = pl.program_id(2)
    tile = q_ref.shape[0]
    blk = A_BLOCK
    col = lax.broadcasted_iota(jnp.int32, (blk, 2 * blk), 1)
    no_prev = jnp.logical_and(col < blk, t == 0)
    for p in range(len(A_DILATIONS)):
        bm0[p] = jnp.where(no_prev, NEG, bias_ref[p])

    def rows(start, size, dil):
        return pl.ds(start, size) if dil == 1 else pl.ds(start, size, stride=dil)

    def block(p, dil, r, n):
        q_idx = rows(r + dil * blk * n, blk, dil)
        if n == 0:
            prev = rows(tile - dil * blk + r, blk, dil)
            cur = rows(r, blk, dil)
            k = jnp.concatenate([kp_ref[prev, :], kc_ref[cur, :]], axis=0)
            v = jnp.concatenate([vp_ref[prev, :], vc_ref[cur, :]], axis=0)
            bm = bm0[p]
        else:
            k_idx = rows(r + dil * blk * (n - 1), 2 * blk, dil)
            k, v = kc_ref[k_idx, :], vc_ref[k_idx, :]
            bm = bias_ref[p]
        q = (q_ref[q_idx, :] * scale).astype(BF16)
        s = lax.dot_general(q, k.astype(BF16), (((1,), (1,)), ((), ())), preferred_element_type=F32)
        s = jnp.where(bm > 0.5 * NEG, s + bm, NEG)
        m = jnp.max(s, axis=-1, keepdims=True)
        e = jnp.exp(s - m)
        l = jnp.sum(e, axis=-1, keepdims=True)
        o = jnp.dot(e.astype(BF16), v.astype(BF16), preferred_element_type=F32) / l
        lse = m + jnp.log(l)
        o_scr[p, q_idx, :] = o
        lse_scr[p, q_idx, :] = jnp.broadcast_to(lse, (blk, HEAD_DIM))

    for p, dil in enumerate(A_DILATIONS):
        for r in range(dil):
            for n in range(tile // (blk * dil)):
                block(p, dil, r, n)

    def merge(i, carry):
        r = pl.multiple_of(i * blk, blk)
        rows = pl.ds(r, blk)
        l0, l1, l2 = lse_scr[0, rows, :], lse_scr[1, rows, :], lse_scr[2, rows, :]
        mx = jnp.maximum(jnp.maximum(l0, l1), l2)
        w0, w1, w2 = jnp.exp(l0 - mx), jnp.exp(l1 - mx), jnp.exp(l2 - mx)
        num = w0 * o_scr[0, rows, :] + w1 * o_scr[1, rows, :] + w2 * o_scr[2, rows, :]
        o_ref[rows, :] = (num / (w0 + w1 + w2)).astype(o_ref.dtype)
        return carry

    lax.fori_loop(0, tile // blk, merge, 0)


def dilated_attention(z, bias_tab, *, batch, seq):
    z3 = z.reshape(batch, seq, z.shape[1])
    nt = seq // A_TILE
    hq, hk, hv = Z_AQ // HEAD_DIM, Z_AK // HEAD_DIM, Z_AV // HEAD_DIM
    tile_spec = lambda col0, prev: pl.BlockSpec(
        (None, A_TILE, HEAD_DIM),
        (lambda b, h, t: (b, jnp.maximum(t - 1, 0), col0 + h)) if prev else (lambda b, h, t: (b, t, col0 + h)))
    npat = len(A_DILATIONS)
    out = pl.pallas_call(
        functools.partial(_dilated_kernel, scale=HEAD_DIM ** -0.5),
        grid=(batch, A_HEADS, nt),
        in_specs=[
            tile_spec(hq, False),
            tile_spec(hk, False), tile_spec(hk, True),
            tile_spec(hv, False), tile_spec(hv, True),
            pl.BlockSpec((npat, None, A_BLOCK, 2 * A_BLOCK), lambda b, h, t: (0, h, 0, 0)),
        ],
        out_specs=pl.BlockSpec((None, A_TILE, HEAD_DIM), lambda b, h, t: (b, t, h)),
        out_shape=jax.ShapeDtypeStruct((batch, seq, A_HEADS * HEAD_DIM), BF16),
        scratch_shapes=[
            pltpu.VMEM((npat, A_BLOCK, 2 * A_BLOCK), F32),
            pltpu.VMEM((npat, A_TILE, HEAD_DIM), F32),
            pltpu.VMEM((npat, A_TILE, HEAD_DIM), F32),
        ],
        compiler_params=_cparams("parallel", "parallel", "arbitrary"),
        name="dilated_attention",
    )(z3, z3, z3, z3, z3, bias_tab)
    return out.reshape(batch * seq, A_HEADS * HEAD_DIM)


def _gla_kernel(q_ref, k_ref, v_ref, r_ref, lr_ref, wa_ref, ba_ref, gn_ref, o_ref, st_ref, *, scale):
    @pl.when(pl.program_id(1) == 0)
    def _():
        st_ref[...] = jnp.zeros_like(st_ref)

    c = B_CHUNK
    tc = q_ref.shape[0]
    dk = B_DK // B_HEADS
    dv = B_DV // B_HEADS
    pre = jnp.dot(lr_ref[...], wa_ref[...], precision=lax.Precision.HIGHEST,
                  preferred_element_type=F32) + ba_ref[...]
    log_a = (jnp.minimum(pre, 0.0) - jnp.log(1.0 + jnp.exp(-jnp.abs(pre)))) * (1.0 / B_GATE_TAU)
    tril = lax.broadcasted_iota(jnp.int32, (c, c), 0) >= lax.broadcasted_iota(jnp.int32, (c, c), 1)
    ones_tril = jnp.where(tril, 1.0, 0.0).astype(F32)
    for ci in range(tc // c):
        sl = slice(ci * c, (ci + 1) * c)
        cum_all = jnp.dot(ones_tril, log_a[sl, :], precision=lax.Precision.HIGHEST, preferred_element_type=F32)
        for h in range(B_HEADS):
            kcols = slice(h * dk, (h + 1) * dk)
            vcols = slice(h * dv, (h + 1) * dv)
            cum = cum_all[:, kcols]
            cum_last = cum[c - 1:c, :]
            q = q_ref[sl, kcols] * scale
            k = k_ref[sl, kcols]
            v = v_ref[sl, vcols]
            q_dec = (q * jnp.exp(cum)).astype(BF16)
            k_inv = (k * jnp.exp(-cum)).astype(BF16)
            k_state = (k * jnp.exp(cum_last - cum)).astype(BF16)
            att = lax.dot_general(q_dec, k_inv, (((1,), (1,)), ((), ())), preferred_element_type=F32)
            att = jnp.where(tril, att, 0.0).astype(BF16)
            st = st_ref[h]
            o = (jnp.dot(att, v.astype(BF16), preferred_element_type=F32)
                 + lax.dot_general(q_dec, st.astype(BF16), (((1,), (1,)), ((), ())), preferred_element_type=F32))
            st_ref[h] = st * jnp.exp(cum_last) + jnp.dot(v.T.astype(BF16), k_state, preferred_element_type=F32)
            ms = jnp.mean(o * o, axis=-1, keepdims=True)
            o = o * lax.rsqrt(ms + EPS) * gn_ref[:, vcols]
            r = r_ref[sl, vcols]
            o_ref[sl, vcols] = (o * (r * jax.nn.sigmoid(r))).astype(o_ref.dtype)


def gated_linear_attention(z, w_alpha_p, b_alpha, gla_norm, *, batch, seq, tc):
    dk = B_DK // B_HEADS
    dv = B_DV // B_HEADS
    nt = seq // tc
    row = lambda b, t: b * nt + t
    const = lambda b, t: (0, 0)
    return pl.pallas_call(
        functools.partial(_gla_kernel, scale=dk ** -0.5),
        grid=(batch, nt),
        in_specs=[
            pl.BlockSpec((tc, B_DK), lambda b, t: (row(b, t), Z_BQ // B_DK)),
            pl.BlockSpec((tc, B_DK), lambda b, t: (row(b, t), Z_BK // B_DK)),
            pl.BlockSpec((tc, B_DV), lambda b, t: (row(b, t), Z_BV // B_DV)),
            pl.BlockSpec((tc, B_DV), lambda b, t: (row(b, t), Z_BR // B_DV)),
            pl.BlockSpec((tc, LANES), lambda b, t: (row(b, t), Z_LR // LANES)),
            pl.BlockSpec((LANES, B_DK), const),
            pl.BlockSpec((1, B_DK), const),
            pl.BlockSpec((1, B_DV), const),
        ],
        out_specs=pl.BlockSpec((tc, B_DV), lambda b, t: (row(b, t), 0)),
        out_shape=jax.ShapeDtypeStruct((batch * seq, B_DV), BF16),
        scratch_shapes=[pltpu.VMEM((B_HEADS, dv, dk), F32)],
        compiler_params=_cparams("parallel", "arbitrary"),
        name="gated_linear_attention",
    )(z, z, z, z, z, w_alpha_p, b_alpha.reshape(1, B_DK), gla_norm.reshape(1, B_DV))


C_QW = 2 * C_NOPE


def _rope_tables(seq):
    pos = jnp.arange(seq, dtype=F32)
    inv = ROPE_THETA ** (-jnp.arange(0, C_ROPE, 2, dtype=F32) / C_ROPE)
    ang = pos[:, None] * inv[None, :]
    cos, sin = jnp.cos(ang), jnp.sin(ang)
    half = C_ROPE // 2
    z = lambda w: jnp.zeros((seq, w), F32)
    k_tab = jnp.stack([
        jnp.concatenate([cos, cos, z(LANES - C_ROPE)], axis=1),
        jnp.concatenate([-sin, z(half), z(LANES - C_ROPE)], axis=1),
        jnp.concatenate([z(half), sin, z(LANES - C_ROPE)], axis=1)])
    q_tab = jnp.concatenate([
        jnp.stack([jnp.ones((seq, C_NOPE), F32), z(C_NOPE), z(C_NOPE)]), k_tab], axis=2)
    return q_tab, k_tab


def _mla_proj_kernel(cq_ref, ckv_ref, ckr_ref, gq_ref, gkv_ref, wq_ref, wk_ref, wv_ref, qt_ref, kt_ref,
                     q_out, kn_out, kr_out, v_out, *, scale):
    half = C_ROPE // 2

    def rms(x, g):
        ms = jnp.mean(x * x, axis=-1, keepdims=True)
        return (x * lax.rsqrt(ms + EPS) * g).astype(BF16)

    cq = rms(cq_ref[...], gq_ref[...])
    q = jnp.dot(cq, wq_ref[...], preferred_element_type=F32)
    t0, t1, t2 = qt_ref[0], qt_ref[1], qt_ref[2]
    for h in range(C_HEADS):
        qh = q[:, h * C_QW:(h + 1) * C_QW]
        qh = qh * t0 + pltpu.roll(qh, C_QW - half, axis=1) * t1 + pltpu.roll(qh, half, axis=1) * t2
        q_out[:, h * C_QW:(h + 1) * C_QW] = (qh * scale).astype(q_out.dtype)
    ckv = rms(ckv_ref[...], gkv_ref[...])
    kn_out[...] = jnp.dot(ckv, wk_ref[...], preferred_element_type=F32).astype(kn_out.dtype)
    v_out[...] = jnp.dot(ckv, wv_ref[...], preferred_element_type=F32).astype(v_out.dtype)
    kr = ckr_ref[...]
    kr = kr * kt_ref[0] + pltpu.roll(kr, LANES - half, axis=1) * kt_ref[1] + pltpu.roll(kr, half, axis=1) * kt_ref[2]
    kr_out[...] = kr.astype(kr_out.dtype)


def mla_project(z, q_a_norm, kv_a_norm, wq_p, wk_p, wv_p, q_tab, k_tab, *, seq, tm):
    m = z.shape[0]
    nt = seq // tm
    const = lambda i: (0, 0)
    return pl.pallas_call(
        functools.partial(_mla_proj_kernel, scale=(C_NOPE + C_ROPE) ** -0.5 * math.log2(math.e)),
        grid=(m // tm,),
        in_specs=[
            pl.BlockSpec((tm, C_Q_RANK), lambda i: (i, Z_CQA // C_Q_RANK)),
            pl.BlockSpec((tm, C_KV_RANK), lambda i: (i, Z_CKVA // C_KV_RANK)),
            pl.BlockSpec((tm, LANES), lambda i: (i, Z_CKR // LANES)),
            pl.BlockSpec((1, C_Q_RANK), const),
            pl.BlockSpec((1, C_KV_RANK), const),
            pl.BlockSpec((C_Q_RANK, C_HEADS * C_QW), const),
            pl.BlockSpec((C_KV_RANK, C_HEADS * C_NOPE), const),
            pl.BlockSpec((C_KV_RANK, C_HEADS * C_V), const),
            pl.BlockSpec((3, tm, C_QW), lambda i: (0, i % nt, 0)),
            pl.BlockSpec((3, tm, LANES), lambda i: (0, i % nt, 0)),
        ],
        out_specs=[
            pl.BlockSpec((tm, C_HEADS * C_QW), lambda i: (i, 0)),
            pl.BlockSpec((tm, C_HEADS * C_NOPE), lambda i: (i, 0)),
            pl.BlockSpec((tm, LANES), lambda i: (i, 0)),
            pl.BlockSpec((tm, C_HEADS * C_V), lambda i: (i, 0)),
        ],
        out_shape=[
            jax.ShapeDtypeStruct((m, C_HEADS * C_QW), BF16),
            jax.ShapeDtypeStruct((m, C_HEADS * C_NOPE), BF16),
            jax.ShapeDtypeStruct((m, LANES), BF16),
            jax.ShapeDtypeStruct((m, C_HEADS * C_V), BF16),
        ],
        compiler_params=_cparams("parallel"),
        name="mla_project",
    )(z, z, z, q_a_norm.reshape(1, -1), kv_a_norm.reshape(1, -1), wq_p, wk_p, wv_p, q_tab, k_tab)


def _mla_flash_kernel(q_ref, kn_ref, kr_ref, v_ref, o_ref, m_ref, l_ref, acc_ref):
    tq = m_ref.shape[0]
    tk = tq // 2

    def q_tile(qi, carry):
        q0 = pl.multiple_of(qi * tq, tq)
        m_ref[...] = jnp.full_like(m_ref, NEG)
        l_ref[...] = jnp.zeros_like(l_ref)
        acc_ref[...] = jnp.zeros_like(acc_ref)

        def step(r0, nr, k_start, nk, mask_offset=None):
            rows = slice(r0, r0 + nr)
            ks = pl.ds(pl.multiple_of(k_start, tk), nk)
            k = jnp.concatenate([kn_ref[ks, :], kr_ref[ks, :]], axis=1)
            q = q_ref[pl.ds(pl.multiple_of(q0 + r0, tk), nr), :]
            s = lax.dot_general(q, k, (((1,), (1,)), ((), ())), preferred_element_type=F32)
            if mask_offset is not None:
                visible = (lax.broadcasted_iota(jnp.int32, s.shape, 1)
                           <= lax.broadcasted_iota(jnp.int32, s.shape, 0) + mask_offset)
                s = jnp.where(visible, s, NEG)
            m_prev = m_ref[rows, :]
            m_new = jnp.maximum(m_prev, jnp.max(s, axis=-1, keepdims=True))
            alpha = jnp.exp2(m_prev - m_new)
            p = jnp.exp2(s - jnp.concatenate([m_new] * (nk // LANES), axis=1))
            l_ref[rows, :] = alpha * l_ref[rows, :] + jnp.sum(p, axis=-1, keepdims=True)
            acc_ref[rows, :] = alpha * acc_ref[rows, :] + jnp.dot(p.astype(BF16), v_ref[ks, :],
                                                                  preferred_element_type=F32)
            m_ref[rows, :] = m_new

        def body(kb, c):
            step(0, tq, 2 * kb * tk, tk)
            step(0, tq, (2 * kb + 1) * tk, tk)
            return c

        lax.fori_loop(0, qi, body, 0)
        step(0, tk, q0, tk, 0)
        step(tk, tk, q0, tq, tk)
        o_ref[pl.ds(q0, tq), :] = (acc_ref[...] / l_ref[...]).astype(o_ref.dtype)
        return carry

    lax.fori_loop(0, q_ref.shape[0] // tq, q_tile, 0)


def mla_flash(q, kn, kr, v, *, batch, seq, tq):
    return pl.pallas_call(
        _mla_flash_kernel,
        grid=(batch, C_HEADS),
        in_specs=[
            pl.BlockSpec((seq, C_QW), lambda b, h: (b, h)),
            pl.BlockSpec((seq, C_NOPE), lambda b, h: (b, h)),
            pl.BlockSpec((seq, LANES), lambda b, h: (b, 0)),
            pl.BlockSpec((seq, C_V), lambda b, h: (b, h)),
        ],
        out_specs=pl.BlockSpec((seq, C_V), lambda b, h: (b, h)),
        out_shape=jax.ShapeDtypeStruct((batch * seq, C_HEADS * C_V), BF16),
        scratch_shapes=[
            pltpu.VMEM((tq, LANES), F32),
            pltpu.VMEM((tq, LANES), F32),
            pltpu.VMEM((tq, C_V), F32),
        ],
        compiler_params=_cparams("parallel", "parallel"),
        name="mla_flash",
    )(q, kn, kr, v)


def _mixer_out_kernel(oa_ref, ob_ref, oc_ref, wa_ref, wb_ref, wc_ref, g0_ref, g1_ref, g2_ref,
                      b0_ref, b1_ref, b2_ref, wo_ref, xr_ref, o_ref, mg_ref, *, n_rows):
    i = pl.program_id(0)
    j = pl.program_id(1)
    _, nj, _, tn = mg_ref.shape
    cur = lax.rem(i, 2)

    @pl.when(i > 0)
    def _():
        acc = xr_ref[...]
        for c in range(nj):
            acc = acc + jnp.dot(mg_ref[1 - cur, c], wo_ref[c * tn:(c + 1) * tn, :], preferred_element_type=F32)
        o_ref[...] = acc

    def branch(o, w, g, b):
        return jax.nn.sigmoid(g[...] + b[...]) * jnp.dot(o[...], w[j], preferred_element_type=F32)

    @pl.when(i < n_rows)
    def _():
        acc = branch(oa_ref, wa_ref, g0_ref, b0_ref)
        acc = acc + branch(ob_ref, wb_ref, g1_ref, b1_ref)
        acc = acc + branch(oc_ref, wc_ref, g2_ref, b2_ref)
        mg_ref[cur, j] = acc.astype(mg_ref.dtype)


def _column_tiles(w, tn):
    k, n = w.shape
    return w.reshape(k, n // tn, tn).transpose(1, 0, 2).astype(BF16)


def _resident(shape):
    return pl.BlockSpec(shape, lambda i, j: (0,) * len(shape), pipeline_mode=pl.Buffered(1))


def mixer_output_block(x, o_a, o_b, o_c, w_a, w_b, w_c, gates, b_gate, w_o, *, tm, tn):
    m, k = o_a.shape
    nj = w_a.shape[0]
    d = nj * tn
    n_rows = m // tm
    mrow = lambda i: jnp.minimum(i, n_rows - 1)
    mcol = lambda i, j: jnp.where(i < n_rows, j, nj - 1)
    orow = lambda i: jnp.maximum(i - 1, 0)
    ocol = lambda i, j: jnp.where(i > 0, j, 0)
    act = pl.BlockSpec((tm, k), lambda i, j: (mrow(i), 0))
    wgt = _resident((nj, k, tn))
    gate = lambda br: pl.BlockSpec((tm, tn), lambda i, j: (mrow(i), br * nj + mcol(i, j)))
    gbias = lambda br: pl.BlockSpec((1, tn), lambda i, j: (0, br * nj + j))
    return pl.pallas_call(
        functools.partial(_mixer_out_kernel, n_rows=n_rows),
        grid=(n_rows + 1, nj),
        in_specs=[act, act, act, wgt, wgt, wgt, gate(0), gate(1), gate(2), gbias(0), gbias(1), gbias(2),
                  pl.BlockSpec((d, tn), lambda i, j: (0, ocol(i, j))),
                  pl.BlockSpec((tm, tn), lambda i, j: (orow(i), ocol(i, j)))],
        out_specs=pl.BlockSpec((tm, tn), lambda i, j: (orow(i), ocol(i, j))),
        out_shape=jax.ShapeDtypeStruct((m, d), F32),
        scratch_shapes=[pltpu.VMEM((2, nj, tm, tn), BF16)],
        compiler_params=_cparams("arbitrary", "arbitrary"),
        name="mixer_output_block",
    )(o_a, o_b, o_c, w_a, w_b, w_c, gates, gates, gates, *([b_gate.reshape(1, -1)] * 3), w_o, x)


def _cross_block_kernel(x_ref, g_ref, wq_ref, k_ref, v_ref, wo_ref, xr_ref, o_ref, h_ref, ao_ref, *,
                        scale, n_rows):
    i = pl.program_id(0)
    j = pl.program_id(1)
    hd = k_ref.shape[1]
    cur = lax.rem(i, 2)

    @pl.when(i > 0)
    def _():
        acc = xr_ref[...]
        for h in range(X_HEADS):
            acc = acc + jnp.dot(ao_ref[1 - cur, h], wo_ref[h * hd:(h + 1) * hd, :], preferred_element_type=F32)
        o_ref[...] = acc

    @pl.when(jnp.logical_and(j == 0, i < n_rows))
    def _():
        _rmsnorm_rows(x_ref, g_ref, h_ref)

    @pl.when(i < n_rows)
    def _():
        q = jnp.dot(h_ref[...], wq_ref[j], preferred_element_type=F32) * scale
        s = lax.dot_general(q.astype(BF16), k_ref[...], (((1,), (1,)), ((), ())), preferred_element_type=F32)
        m = jnp.max(s, axis=-1, keepdims=True)
        e = jnp.exp(s - m)
        l = jnp.sum(e, axis=-1, keepdims=True)
        o = jnp.dot(e.astype(BF16), v_ref[...], preferred_element_type=F32) / l
        ao_ref[cur, j] = o.astype(ao_ref.dtype)


def cross_attention_block(x, g, w_q, kv, w_o, *, batch, seq, mem_len, tm):
    m, d = x.shape
    hd = d // X_HEADS
    tiles_per_batch = seq // tm
    n_rows = m // tm
    arow = lambda i: jnp.minimum(i, n_rows - 1)
    head = lambda i, j: jnp.where(i < n_rows, j, X_HEADS - 1)
    orow = lambda i: jnp.maximum(i - 1, 0)
    ocol = lambda i, j: jnp.where(i > 0, j, 0)
    return pl.pallas_call(
        functools.partial(_cross_block_kernel, scale=hd ** -0.5, n_rows=n_rows),
        grid=(n_rows + 1, X_HEADS),
        in_specs=[
            pl.BlockSpec((tm, d), lambda i, j: (arow(i), 0)),
            pl.BlockSpec((1, d), lambda i, j: (0, 0)),
            _resident((X_HEADS, d, hd)),
            pl.BlockSpec((mem_len, hd), lambda i, j: (arow(i) // tiles_per_batch, head(i, j))),
            pl.BlockSpec((mem_len, hd), lambda i, j: (arow(i) // tiles_per_batch, X_HEADS + head(i, j))),
            pl.BlockSpec((d, hd), lambda i, j: (0, ocol(i, j))),
            pl.BlockSpec((tm, hd), lambda i, j: (orow(i), ocol(i, j))),
        ],
        out_specs=pl.BlockSpec((tm, hd), lambda i, j: (orow(i), ocol(i, j))),
        out_shape=jax.ShapeDtypeStruct((m, d), F32),
        scratch_shapes=[pltpu.VMEM((tm, d), BF16), pltpu.VMEM((2, X_HEADS, tm, hd), BF16)],
        compiler_params=_cparams("arbitrary", "arbitrary"),
        name="cross_attention_block",
    )(x, g.reshape(1, d), w_q, kv, kv, w_o, x)


def _pack_w_in(w):
    nl, d, _ = w.shape
    sizes = (1024, 1024, 1024, B_DK, B_DK, B_DV, B_GATE_RANK, B_DV, C_Q_RANK, C_KV_RANK, C_ROPE, N_BRANCH * d)
    offs = np.concatenate([[0], np.cumsum(sizes)])
    aq, ak, av, bq, bk, bv, lr, br, cqa, ckva, ckr, gates = [
        w[:, :, offs[i]:offs[i + 1]].astype(BF16) for i in range(len(sizes))]
    zeros = lambda n: jnp.zeros((nl, d, n), BF16)
    used = Z_CKR + LANES
    wz = jnp.concatenate(
        [aq, ak, av, bv, br, cqa, ckva, bq, bk,
         lr, zeros(LANES - B_GATE_RANK), ckr, zeros(LANES - C_ROPE), zeros(Z_WIDTH - used)], axis=2)
    return wz, gates


def _pack_w_qb(w):
    r = w.shape[0]
    w = w.reshape(r, C_HEADS, C_NOPE + C_ROPE)
    w = jnp.pad(w, ((0, 0), (0, 0), (0, C_QW - C_NOPE - C_ROPE)))
    return w.reshape(r, C_HEADS * C_QW).astype(BF16)


def _pack_w_kvb(w):
    r = w.shape[0]
    w = w.reshape(r, C_HEADS, C_NOPE + C_V)
    wk = w[:, :, :C_NOPE].reshape(r, C_HEADS * C_NOPE)
    wv = w[:, :, C_NOPE:].reshape(r, C_HEADS * C_V)
    return wk.astype(BF16), wv.astype(BF16)


def kernel(x, mem, rel_bias, norm_mix, w_in, b_gate, w_alpha, b_alpha, gla_norm, q_a_norm, w_qb, kv_a_norm, w_kvb, w_up_a, w_up_b, w_up_c, w_o, norm_x, norm_mem, w_xq, w_xkv, w_xo, norm_ffn, w_ffn_gate, w_ffn_up, w_ffn_down, norm_final):
    batch, seq, d = x.shape
    mem_len = mem.shape[1]
    depth = w_in.shape[0]
    t = batch * seq
    assert d == 2048 and seq % A_TILE == 0, "tiling is derived for the stated shapes"

    xf = x.reshape(t, d)
    memf = mem.reshape(batch * mem_len, d)
    bias_tab = dilated_bias_table(rel_bias)
    q_tab, k_tab = _rope_tables(seq)
    bf = lambda a: a.astype(BF16)
    w_in_z, w_in_g = _pack_w_in(w_in)

    for l in range(depth):
        z = norm_matmul(xf, norm_mix[l], w_in_z, out_dtype=F32, tm=1024, tn=W_IN_TN, layer=l)
        gates = norm_matmul(xf, norm_mix[l], w_in_g, out_dtype=BF16, tm=1024, tn=W_IN_TN, layer=l)
        o_a = dilated_attention(z, bias_tab, batch=batch, seq=seq)
        w_alpha_p = jnp.pad(w_alpha[l], ((0, LANES - B_GATE_RANK), (0, 0)))
        o_b = gated_linear_attention(z, w_alpha_p, b_alpha[l], gla_norm[l], batch=batch, seq=seq, tc=512)
        wk_p, wv_p = _pack_w_kvb(w_kvb[l])
        cq, ckn, ckr, cv = mla_project(z, q_a_norm[l], kv_a_norm[l], _pack_w_qb(w_qb[l]), wk_p, wv_p,
                                       q_tab, k_tab, seq=seq, tm=512)
        o_c = mla_flash(cq, ckn, ckr, cv, batch=batch, seq=seq, tq=1024)
        ct = lambda w: _column_tiles(w, 512)
        xf = mixer_output_block(xf, o_a, o_b, o_c, ct(w_up_a[l]), ct(w_up_b[l]), ct(w_up_c[l]), gates, b_gate[l],
                                bf(w_o[l]), tm=1024, tn=512)
        xkv = norm_matmul(memf, norm_mem[l], w_xkv, out_dtype=BF16, tm=batch * mem_len, tn=512, layer=l)
        xf = cross_attention_block(xf, norm_x[l], _column_tiles(w_xq[l], d // X_HEADS), xkv,
                                   bf(w_xo[l]),
                                   batch=batch, seq=seq, mem_len=mem_len, tm=1024)
        act = norm_swiglu(xf, norm_ffn[l], w_ffn_gate, w_ffn_up, tm=1024, tn=512, layer=l)
        xf = matmul_residual(act, bf(w_ffn_down[l]), xf, tm=1024, tn=512)
    return rmsnorm_rows(xf, norm_final, tm=512).reshape(batch, seq, d)
```

```python
import functools
import math

import numpy as np
import jax
import jax.numpy as jnp
from jax import lax
from jax.experimental import pallas as pl
from jax.experimental.pallas import tpu as pltpu

F32 = jnp.float32
BF16 = jnp.bfloat16
EPS = 1e-6
NEG = -1e30

HEAD_DIM = 128
A_HEADS = 8
A_DILATIONS = (1, 4, 16)
A_BLOCK = 128
N_BUCKETS = 32
MAX_DISTANCE = 2048
B_HEADS = 4
B_DK = 512
B_DV = 1024
B_GATE_RANK = 16
B_GATE_TAU = 16.0
B_CHUNK = 64
C_HEADS = 8
C_Q_RANK = 512
C_KV_RANK = 512
C_NOPE = 128
C_ROPE = 64
C_V = 128
ROPE_THETA = 10000.0
X_HEADS = 4
N_BRANCH = 3

VMEM_LIMIT_BYTES = 56 * 1024 * 1024
LANES = 128

Z_AQ, Z_AK, Z_AV = 0, 1024, 2048
Z_BV, Z_BR = 3072, 4096
Z_CQA, Z_CKVA = 5120, 5632
Z_BQ, Z_BK = 6144, 6656
Z_LR = 7168
Z_CKR = 7296
Z_WIDTH = 7680
G_WIDTH = 3 * 2048
W_Z_TN = 1920
W_G_TN = 2048
SWIGLU_TM = 2048
A_TILE = 2048


def _cparams(*sem):
    return pltpu.CompilerParams(dimension_semantics=sem, vmem_limit_bytes=VMEM_LIMIT_BYTES)


def _rmsnorm_rows(x_ref, g_ref, h_ref):
    rows = x_ref.shape[0]

    def body(i, carry):
        r = pl.multiple_of(i * 16, 16)
        x = x_ref[pl.ds(r, 16), :]
        ms = jnp.mean(x * x, axis=-1, keepdims=True)
        h_ref[pl.ds(r, 16), :] = (x * lax.rsqrt(ms + EPS) * g_ref[...]).astype(BF16)
        return carry

    lax.fori_loop(0, rows // 16, body, 0, unroll=8)


def _norm_matmul_kernel(x_ref, g_ref, w_ref, o_ref, h_ref, *, scale):
    @pl.when(pl.program_id(1) == 0)
    def _():
        _rmsnorm_rows(x_ref, g_ref, h_ref)

    acc = jnp.dot(h_ref[...], w_ref[...].astype(BF16), preferred_element_type=F32)
    if scale != 1.0:
        acc = acc * scale
    o_ref[...] = acc.astype(o_ref.dtype)


def _weight_spec(w, layer, tn):
    if w.ndim == 2:
        return pl.BlockSpec((w.shape[0], tn), lambda i, j: (0, j))
    return pl.BlockSpec((None, w.shape[1], tn), lambda i, j: (layer, 0, j))


def norm_matmul(x, g, w, *, out_dtype, tm, tn, scale=1.0, layer=None):
    m, k = x.shape
    n = w.shape[-1]
    return pl.pallas_call(
        functools.partial(_norm_matmul_kernel, scale=scale),
        grid=(m // tm, n // tn),
        in_specs=[
            pl.BlockSpec((tm, k), lambda i, j: (i, 0)),
            pl.BlockSpec((1, k), lambda i, j: (0, 0)),
            _weight_spec(w, layer, tn),
        ],
        out_specs=pl.BlockSpec((tm, tn), lambda i, j: (i, j)),
        out_shape=jax.ShapeDtypeStruct((m, n), out_dtype),
        scratch_shapes=[pltpu.VMEM((tm, k), BF16)],
        compiler_params=_cparams("parallel", "arbitrary"),
        name="norm_matmul",
    )(x, g.reshape(1, k), w)


def _inv_cols(inv_ref, n):
    inv = inv_ref[...]
    return jnp.concatenate([inv] * (n // LANES), axis=1)


def _emit_next_norm(o, g_ref, xg_ref, inv_ref, ss_ref, col, ncols):
    d = ncols * o.shape[1]
    xg_ref[...] = (o * g_ref[...]).astype(xg_ref.dtype)
    part = jnp.sum(o * o, axis=-1, keepdims=True)

    @pl.when(col == 0)
    def _():
        ss_ref[...] = jnp.broadcast_to(part, ss_ref.shape)

    @pl.when(col > 0)
    def _():
        ss_ref[...] = ss_ref[...] + part

    @pl.when(col == ncols - 1)
    def _():
        inv_ref[...] = lax.rsqrt(ss_ref[...] * (1.0 / d) + EPS)


def _prenorm_kernel(x_ref, g_ref, xg_ref, inv_ref):
    x = x_ref[...]
    xg_ref[...] = (x * g_ref[...]).astype(xg_ref.dtype)
    ms = jnp.mean(x * x, axis=-1, keepdims=True)
    inv_ref[...] = jnp.broadcast_to(lax.rsqrt(ms + EPS), inv_ref.shape)


def prenorm(x, g, *, tm):
    m, k = x.shape
    return pl.pallas_call(
        _prenorm_kernel,
        grid=(m // tm,),
        in_specs=[pl.BlockSpec((tm, k), lambda i: (i, 0)), pl.BlockSpec((1, k), lambda i: (0, 0))],
        out_specs=[pl.BlockSpec((tm, k), lambda i: (i, 0)), pl.BlockSpec((tm, LANES), lambda i: (i, 0))],
        out_shape=[jax.ShapeDtypeStruct((m, k), BF16), jax.ShapeDtypeStruct((m, LANES), F32)],
        compiler_params=_cparams("parallel"),
        name="prenorm",
    )(x, g.reshape(1, k))


def _scaled_matmul_kernel(xg_ref, inv_ref, w_ref, o_ref):
    acc = jnp.dot(xg_ref[...], w_ref[...].astype(BF16), preferred_element_type=F32)
    o_ref[...] = (acc * _inv_cols(inv_ref, acc.shape[1])).astype(o_ref.dtype)


def scaled_matmul(xg, inv, w, *, out_dtype, tm, tn, layer=None):
    m, k = xg.shape
    n = w.shape[-1]
    return pl.pallas_call(
        _scaled_matmul_kernel,
        grid=(m // tm, n // tn),
        in_specs=[
            pl.BlockSpec((tm, k), lambda i, j: (i, 0)),
            pl.BlockSpec((tm, LANES), lambda i, j: (i, 0)),
            _weight_spec(w, layer, tn),
        ],
        out_specs=pl.BlockSpec((tm, tn), lambda i, j: (i, j)),
        out_shape=jax.ShapeDtypeStruct((m, n), out_dtype),
        compiler_params=_cparams("parallel", "parallel"),
        name="scaled_matmul",
    )(xg, inv, w)


def _scaled_swiglu_kernel(xg_ref, inv_ref, wg_ref, wu_ref, o_ref):
    xg = xg_ref[...]
    inv = _inv_cols(inv_ref, o_ref.shape[1])
    a = jnp.dot(xg, wg_ref[...].astype(BF16), preferred_element_type=F32) * inv
    b = jnp.dot(xg, wu_ref[...].astype(BF16), preferred_element_type=F32) * inv
    o_ref[...] = (a * jax.nn.sigmoid(a) * b).astype(o_ref.dtype)


def scaled_swiglu(xg, inv, wg, wu, *, tm, tn, layer=None):
    m, k = xg.shape
    n = wg.shape[-1]
    return pl.pallas_call(
        _scaled_swiglu_kernel,
        grid=(m // tm, n // tn),
        in_specs=[
            pl.BlockSpec((tm, k), lambda i, j: (i, 0)),
            pl.BlockSpec((tm, LANES), lambda i, j: (i, 0)),
            _weight_spec(wg, layer, tn),
            _weight_spec(wu, layer, tn),
        ],
        out_specs=pl.BlockSpec((tm, tn), lambda i, j: (i, j)),
        out_shape=jax.ShapeDtypeStruct((m, n), BF16),
        compiler_params=_cparams("parallel", "parallel"),
        name="scaled_swiglu",
    )(xg, inv, wg, wu)


def _matmul_residual_kernel(a_ref, w_ref, r_ref, o_ref):
    o_ref[...] = r_ref[...] + jnp.dot(a_ref[...], w_ref[...].astype(BF16), preferred_element_type=F32)


def _matmul_residual_norm_kernel(a_ref, w_ref, r_ref, g_ref, o_ref, xg_ref, inv_ref, ss_ref, *, ncols):
    o = r_ref[...] + jnp.dot(a_ref[...], w_ref[...].astype(BF16), preferred_element_type=F32)
    o_ref[...] = o
    _emit_next_norm(o, g_ref, xg_ref, inv_ref, ss_ref, pl.program_id(1), ncols)


def matmul_residual(a, w, res, *, tm, tn, layer=None, next_gain=None):
    m, k = a.shape
    n = w.shape[-1]
    tile = pl.BlockSpec((tm, tn), lambda i, j: (i, j))
    in_specs = [pl.BlockSpec((tm, k), lambda i, j: (i, 0)), _weight_spec(w, layer, tn), tile]
    if next_gain is None:
        return pl.pallas_call(
            _matmul_residual_kernel,
            grid=(m // tm, n // tn),
            in_specs=in_specs,
            out_specs=tile,
            out_shape=jax.ShapeDtypeStruct((m, n), F32),
            compiler_params=_cparams("parallel", "parallel"),
            name="matmul_residual",
        )(a, w, res)
    return pl.pallas_call(
        functools.partial(_matmul_residual_norm_kernel, ncols=n // tn),
        grid=(m // tm, n // tn),
        in_specs=in_specs + [pl.BlockSpec((1, tn), lambda i, j: (0, j))],
        out_specs=[tile, tile, pl.BlockSpec((tm, LANES), lambda i, j: (i, 0))],
        out_shape=[jax.ShapeDtypeStruct((m, n), F32), jax.ShapeDtypeStruct((m, n), BF16),
                   jax.ShapeDtypeStruct((m, LANES), F32)],
        scratch_shapes=[pltpu.VMEM((tm, LANES), F32)],
        compiler_params=_cparams("parallel", "arbitrary"),
        name="matmul_residual_norm",
    )(a, w, res, next_gain.reshape(1, n))


def _rmsnorm_kernel(x_ref, g_ref, o_ref):
    x = x_ref[...]
    ms = jnp.mean(x * x, axis=-1, keepdims=True)
    o_ref[...] = x * lax.rsqrt(ms + EPS) * g_ref[...]


def rmsnorm_rows(x, g, *, tm):
    m, k = x.shape
    return pl.pallas_call(
        _rmsnorm_kernel,
        grid=(m // tm,),
        in_specs=[pl.BlockSpec((tm, k), lambda i: (i, 0)), pl.BlockSpec((1, k), lambda i: (0, 0))],
        out_specs=pl.BlockSpec((tm, k), lambda i: (i, 0)),
        out_shape=jax.ShapeDtypeStruct((m, k), F32),
        compiler_params=_cparams("parallel"),
        name="final_rmsnorm",
    )(x, g.reshape(1, k))


def _t5_bucket_np(dist):
    max_exact = N_BUCKETS // 2
    n = np.maximum(dist, 1).astype(np.float64)
    large = max_exact + (np.log(n / max_exact) / math.log(MAX_DISTANCE / max_exact)
                         * (N_BUCKETS - max_exact)).astype(np.int32)
    large = np.minimum(large, N_BUCKETS - 1)
    return np.where(dist < max_exact, dist, large).astype(np.int32)


def _dilated_bucket_table():
    qi = np.arange(A_BLOCK)[:, None]
    kj = np.arange(2 * A_BLOCK)[None, :]
    steps_back = qi + A_BLOCK - kj
    valid = (steps_back >= 0) & (steps_back <= A_BLOCK)
    tabs = []
    for dil in A_DILATIONS:
        bucket = _t5_bucket_np(np.clip(steps_back, 0, A_BLOCK) * dil)
        tabs.append(np.where(valid, bucket, -1))
    return np.stack(tabs).astype(np.int32)


def _bias_table_kernel(idx_ref, rb_ref, o_ref):
    h = pl.program_id(1)
    idx = idx_ref[...]
    acc = jnp.full(idx.shape, NEG, F32)
    for b in range(N_BUCKETS):
        acc = jnp.where(idx == b, rb_ref[b, h], acc)
    o_ref[...] = acc


def dilated_bias_table(rel_bias):
    idx = jnp.asarray(_dilated_bucket_table())
    npat = len(A_DILATIONS)
    return pl.pallas_call(
        _bias_table_kernel,
        grid=(npat, A_HEADS),
        in_specs=[
            pl.BlockSpec((None, A_BLOCK, 2 * A_BLOCK), lambda p, h: (p, 0, 0)),
            pl.BlockSpec(memory_space=pltpu.SMEM),
        ],
        out_specs=pl.BlockSpec((None, None, A_BLOCK, 2 * A_BLOCK), lambda p, h: (p, h, 0, 0)),
        out_shape=jax.ShapeDtypeStruct((npat, A_HEADS, A_BLOCK, 2 * A_BLOCK), F32),
        compiler_params=_cparams("parallel", "parallel"),
        name="dilated_bias_table",
    )(idx, rel_bias)


def _dilated_kernel(q_ref, kc_ref, kp_ref, vc_ref, vp_ref, bias_ref, o_ref, bm0, o_scr, lse_scr, *, scale):
    t = pl.program_id(2)
    tile = q_ref.shape[0]
    blk = A_BLOCK
    col = lax.broadcasted_iota(jnp.int32, (blk, 2 * blk), 1)
    no_prev = jnp.logical_and(col < blk, t == 0)
    for p in range(len(A_DILATIONS)):
        bm0[p] = jnp.where(no_prev, NEG, bias_ref[p])

    def rows(start, size, dil):
        return pl.ds(start, size) if dil == 1 else pl.ds(start, size, stride=dil)

    def block(p, dil, r, n):
        q_idx = rows(r + dil * blk * n, blk, dil)
        if n == 0:
            prev = rows(tile - dil * blk + r, blk, dil)
            cur = rows(r, blk, dil)
            k = jnp.concatenate([kp_ref[prev, :], kc_ref[cur, :]], axis=0)
            v = jnp.concatenate([vp_ref[prev, :], vc_ref[cur, :]], axis=0)
            bm = bm0[p]
        else:
            k_idx = rows(r + dil * blk * (n - 1), 2 * blk, dil)
            k, v = kc_ref[k_idx, :], vc_ref[k_idx, :]
            bm = bias_ref[p]
        q = (q_ref[q_idx, :] * scale).astype(BF16)
        s = lax.dot_general(q, k.astype(BF16), (((1,), (1,)), ((), ())), preferred_element_type=F32)
        s = jnp.where(bm > 0.5 * NEG, s + bm, NEG)
        m = jnp.max(s, axis=-1, keepdims=True)
        e = jnp.exp(s - m)
        l = jnp.sum(e, axis=-1, keepdims=True)
        o = jnp.dot(e.astype(BF16), v.astype(BF16), preferred_element_type=F32) / l
        lse = m + jnp.log(l)
        o_scr[p, q_idx, :] = o
        lse_scr[p, q_idx, :] = jnp.broadcast_to(lse, (blk, HEAD_DIM))

    for p, dil in enumerate(A_DILATIONS):
        for r in range(dil):
            for n in range(tile // (blk * dil)):
                block(p, dil, r, n)

    def merge(i, carry):
        r = pl.multiple_of(i * blk, blk)
        rows = pl.ds(r, blk)
        l0, l1, l2 = lse_scr[0, rows, :], lse_scr[1, rows, :], lse_scr[2, rows, :]
        mx = jnp.maximum(jnp.maximum(l0, l1), l2)
        w0, w1, w2 = jnp.exp(l0 - mx), jnp.exp(l1 - mx), jnp.exp(l2 - mx)
        num = w0 * o_scr[0, rows, :] + w1 * o_scr[1, rows, :] + w2 * o_scr[2, rows, :]
        o_ref[rows, :] = (num / (w0 + w1 + w2)).astype(o_ref.dtype)
        return carry

    lax.fori_loop(0, tile // blk, merge, 0)


def dilated_attention(z, bias_tab, *, batch, seq):
    z3 = z.reshape(batch, seq, z.shape[1])
    nt = seq // A_TILE
    hq, hk, hv = Z_AQ // HEAD_DIM, Z_AK // HEAD_DIM, Z_AV // HEAD_DIM
    tile_spec = lambda col0, prev: pl.BlockSpec(
        (None, A_TILE, HEAD_DIM),
        (lambda b, h, t: (b, jnp.maximum(t - 1, 0), col0 + h)) if prev else (lambda b, h, t: (b, t, col0 + h)))
    npat = len(A_DILATIONS)
    out = pl.pallas_call(
        functools.partial(_dilated_kernel, scale=HEAD_DIM ** -0.5),
        grid=(batch, A_HEADS, nt),
        in_specs=[
            tile_spec(hq, False),
            tile_spec(hk, False), tile_spec(hk, True),
            tile_spec(hv, False), tile_spec(hv, True),
            pl.BlockSpec((npat, None, A_BLOCK, 2 * A_BLOCK), lambda b, h, t: (0, h, 0, 0)),
        ],
        out_specs=pl.BlockSpec((None, A_TILE, HEAD_DIM), lambda b, h, t: (b, t, h)),
        out_shape=jax.ShapeDtypeStruct((batch, seq, A_HEADS * HEAD_DIM), BF16),
        scratch_shapes=[
            pltpu.VMEM((npat, A_BLOCK, 2 * A_BLOCK), F32),
            pltpu.VMEM((npat, A_TILE, HEAD_DIM), F32),
            pltpu.VMEM((npat, A_TILE, HEAD_DIM), F32),
        ],
        compiler_params=_cparams("parallel", "parallel", "arbitrary"),
        name="dilated_attention",
    )(z3, z3, z3, z3, z3, bias_tab)
    return out.reshape(batch * seq, A_HEADS * HEAD_DIM)


def _gla_kernel(q_ref, k_ref, v_ref, r_ref, lr_ref, wa_ref, ba_ref, gn_ref, o_ref, st_ref, *, scale):
    @pl.when(pl.program_id(1) == 0)
    def _():
        st_ref[...] = jnp.zeros_like(st_ref)

    c = B_CHUNK
    tc = q_ref.shape[0]
    dk = B_DK // B_HEADS
    dv = B_DV // B_HEADS
    pre = jnp.dot(lr_ref[...], wa_ref[...], precision=lax.Precision.HIGHEST,
                  preferred_element_type=F32) + ba_ref[...]
    log_a = (jnp.minimum(pre, 0.0) - jnp.log(1.0 + jnp.exp(-jnp.abs(pre)))) * (1.0 / B_GATE_TAU)
    tril = lax.broadcasted_iota(jnp.int32, (c, c), 0) >= lax.broadcasted_iota(jnp.int32, (c, c), 1)
    ones_tril = jnp.where(tril, 1.0, 0.0).astype(F32)
    for ci in range(tc // c):
        sl = slice(ci * c, (ci + 1) * c)
        cum_all = jnp.dot(ones_tril, log_a[sl, :], precision=lax.Precision.HIGHEST, preferred_element_type=F32)
        for h in range(B_HEADS):
            kcols = slice(h * dk, (h + 1) * dk)
            vcols = slice(h * dv, (h + 1) * dv)
            cum = cum_all[:, kcols]
            cum_last = cum[c - 1:c, :]
            q = q_ref[sl, kcols] * scale
            k = k_ref[sl, kcols]
            v = v_ref[sl, vcols]
            q_dec = (q * jnp.exp(cum)).astype(BF16)
            k_inv = (k * jnp.exp(-cum)).astype(BF16)
            k_state = (k * jnp.exp(cum_last - cum)).astype(BF16)
            att = lax.dot_general(q_dec, k_inv, (((1,), (1,)), ((), ())), preferred_element_type=F32)
            att = jnp.where(tril, att, 0.0).astype(BF16)
            st = st_ref[h]
            o = (jnp.dot(att, v.astype(BF16), preferred_element_type=F32)
                 + lax.dot_general(q_dec, st.astype(BF16), (((1,), (1,)), ((), ())), preferred_element_type=F32))
            st_ref[h] = st * jnp.exp(cum_last) + jnp.dot(v.T.astype(BF16), k_state, preferred_element_type=F32)
            ms = jnp.mean(o * o, axis=-1, keepdims=True)
            o = o * lax.rsqrt(ms + EPS) * gn_ref[:, vcols]
            r = r_ref[sl, vcols]
            o_ref[sl, vcols] = (o * (r * jax.nn.sigmoid(r))).astype(o_ref.dtype)


def gated_linear_attention(z, w_alpha_p, b_alpha, gla_norm, *, batch, seq, tc):
    dk = B_DK // B_HEADS
    dv = B_DV // B_HEADS
    nt = seq // tc
    row = lambda b, t: b * nt + t
    const = lambda b, t: (0, 0)
    return pl.pallas_call(
        functools.partial(_gla_kernel, scale=dk ** -0.5),
        grid=(batch, nt),
        in_specs=[
            pl.BlockSpec((tc, B_DK), lambda b, t: (row(b, t), Z_BQ // B_DK)),
            pl.BlockSpec((tc, B_DK), lambda b, t: (row(b, t), Z_BK // B_DK)),
            pl.BlockSpec((tc, B_DV), lambda b, t: (row(b, t), Z_BV // B_DV)),
            pl.BlockSpec((tc, B_DV), lambda b, t: (row(b, t), Z_BR // B_DV)),
            pl.BlockSpec((tc, LANES), lambda b, t: (row(b, t), Z_LR // LANES)),
            pl.BlockSpec((LANES, B_DK), const),
            pl.BlockSpec((1, B_DK), const),
            pl.BlockSpec((1, B_DV), const),
        ],
        out_specs=pl.BlockSpec((tc, B_DV), lambda b, t: (row(b, t), 0)),
        out_shape=jax.ShapeDtypeStruct((batch * seq, B_DV), BF16),
        scratch_shapes=[pltpu.VMEM((B_HEADS, dv, dk), F32)],
        compiler_params=_cparams("parallel", "arbitrary"),
        name="gated_linear_attention",
    )(z, z, z, z, z, w_alpha_p, b_alpha.reshape(1, B_DK), gla_norm.reshape(1, B_DV))


C_QW = 2 * C_NOPE


def _rope_tables(seq):
    pos = jnp.arange(seq, dtype=F32)
    inv = ROPE_THETA ** (-jnp.arange(0, C_ROPE, 2, dtype=F32) / C_ROPE)
    ang = pos[:, None] * inv[None, :]
    cos, sin = jnp.cos(ang), jnp.sin(ang)
    half = C_ROPE // 2
    z = lambda w: jnp.zeros((seq, w), F32)
    k_tab = jnp.stack([
        jnp.concatenate([cos, cos, z(LANES - C_ROPE)], axis=1),
        jnp.concatenate([-sin, z(half), z(LANES - C_ROPE)], axis=1),
        jnp.concatenate([z(half), sin, z(LANES - C_ROPE)], axis=1)])
    q_tab = jnp.concatenate([
        jnp.stack([jnp.ones((seq, C_NOPE), F32), z(C_NOPE), z(C_NOPE)]), k_tab], axis=2)
    return q_tab, k_tab


def _mla_proj_kernel(cq_ref, ckv_ref, ckr_ref, gq_ref, gkv_ref, wq_ref, wk_ref, wv_ref, qt_ref, kt_ref,
                     q_out, kn_out, kr_out, v_out, *, scale):
    half = C_ROPE // 2

    def rms(x, g):
        ms = jnp.mean(x * x, axis=-1, keepdims=True)
        return (x * lax.rsqrt(ms + EPS) * g).astype(BF16)

    cq = rms(cq_ref[...], gq_ref[...])
    q = jnp.dot(cq, wq_ref[...], preferred_element_type=F32)
    t0, t1, t2 = qt_ref[0], qt_ref[1], qt_ref[2]
    for h in range(C_HEADS):
        qh = q[:, h * C_QW:(h + 1) * C_QW]
        qh = qh * t0 + pltpu.roll(qh, C_QW - half, axis=1) * t1 + pltpu.roll(qh, half, axis=1) * t2
        q_out[:, h * C_QW:(h + 1) * C_QW] = (qh * scale).astype(q_out.dtype)
    ckv = rms(ckv_ref[...], gkv_ref[...])
    kn_out[...] = jnp.dot(ckv, wk_ref[...], preferred_element_type=F32).astype(kn_out.dtype)
    v_out[...] = jnp.dot(ckv, wv_ref[...], preferred_element_type=F32).astype(v_out.dtype)
    kr = ckr_ref[...]
    kr = kr * kt_ref[0] + pltpu.roll(kr, LANES - half, axis=1) * kt_ref[1] + pltpu.roll(kr, half, axis=1) * kt_ref[2]
    kr_out[...] = kr.astype(kr_out.dtype)


def mla_project(z, q_a_norm, kv_a_norm, wq_p, wk_p, wv_p, q_tab, k_tab, *, seq, tm):
    m = z.shape[0]
    nt = seq // tm
    const = lambda i: (0, 0)
    return pl.pallas_call(
        functools.partial(_mla_proj_kernel, scale=(C_NOPE + C_ROPE) ** -0.5 * math.log2(math.e)),
        grid=(m // tm,),
        in_specs=[
            pl.BlockSpec((tm, C_Q_RANK), lambda i: (i, Z_CQA // C_Q_RANK)),
            pl.BlockSpec((tm, C_KV_RANK), lambda i: (i, Z_CKVA // C_KV_RANK)),
            pl.BlockSpec((tm, LANES), lambda i: (i, Z_CKR // LANES)),
            pl.BlockSpec((1, C_Q_RANK), const),
            pl.BlockSpec((1, C_KV_RANK), const),
            pl.BlockSpec((C_Q_RANK, C_HEADS * C_QW), const),
            pl.BlockSpec((C_KV_RANK, C_HEADS * C_NOPE), const),
            pl.BlockSpec((C_KV_RANK, C_HEADS * C_V), const),
            pl.BlockSpec((3, tm, C_QW), lambda i: (0, i % nt, 0)),
            pl.BlockSpec((3, tm, LANES), lambda i: (0, i % nt, 0)),
        ],
        out_specs=[
            pl.BlockSpec((tm, C_HEADS * C_QW), lambda i: (i, 0)),
            pl.BlockSpec((tm, C_HEADS * C_NOPE), lambda i: (i, 0)),
            pl.BlockSpec((tm, LANES), lambda i: (i, 0)),
            pl.BlockSpec((tm, C_HEADS * C_V), lambda i: (i, 0)),
        ],
        out_shape=[
            jax.ShapeDtypeStruct((m, C_HEADS * C_QW), BF16),
            jax.ShapeDtypeStruct((m, C_HEADS * C_NOPE), BF16),
            jax.ShapeDtypeStruct((m, LANES), BF16),
            jax.ShapeDtypeStruct((m, C_HEADS * C_V), BF16),
        ],
        compiler_params=_cparams("parallel"),
        name="mla_project",
    )(z, z, z, q_a_norm.reshape(1, -1), kv_a_norm.reshape(1, -1), wq_p, wk_p, wv_p, q_tab, k_tab)


def _mla_flash_kernel(q_ref, kn_ref, kr_ref, v_ref, o_ref, m_ref, l_ref, acc_ref):
    tq = m_ref.shape[0]
    tk = tq // 2

    def q_tile(qi, carry):
        q0 = pl.multiple_of(qi * tq, tq)
        m_ref[...] = jnp.full_like(m_ref, NEG)
        l_ref[...] = jnp.zeros_like(l_ref)
        acc_ref[...] = jnp.zeros_like(acc_ref)

        def step(r0, nr, k_start, nk, mask_offset=None):
            rows = slice(r0, r0 + nr)
            ks = pl.ds(pl.multiple_of(k_start, tk), nk)
            k = jnp.concatenate([kn_ref[ks, :], kr_ref[ks, :]], axis=1)
            q = q_ref[pl.ds(pl.multiple_of(q0 + r0, tk), nr), :]
            s = lax.dot_general(q, k, (((1,), (1,)), ((), ())), preferred_element_type=F32)
            if mask_offset is not None:
                visible = (lax.broadcasted_iota(jnp.int32, s.shape, 1)
                           <= lax.broadcasted_iota(jnp.int32, s.shape, 0) + mask_offset)
                s = jnp.where(visible, s, NEG)
            m_prev = m_ref[rows, :]
            m_new = jnp.maximum(m_prev, jnp.max(s, axis=-1, keepdims=True))
            alpha = jnp.exp2(m_prev - m_new)
            p = jnp.exp2(s - jnp.concatenate([m_new] * (nk // LANES), axis=1))
            l_ref[rows, :] = alpha * l_ref[rows, :] + jnp.sum(p, axis=-1, keepdims=True)
            acc_ref[rows, :] = alpha * acc_ref[rows, :] + jnp.dot(p.astype(BF16), v_ref[ks, :],
                                                                  preferred_element_type=F32)
            m_ref[rows, :] = m_new

        def body(kb, c):
            step(0, tq, 2 * kb * tk, tk)
            step(0, tq, (2 * kb + 1) * tk, tk)
            return c

        lax.fori_loop(0, qi, body, 0)
        step(0, tk, q0, tk, 0)
        step(tk, tk, q0, tq, tk)
        o_ref[pl.ds(q0, tq), :] = (acc_ref[...] / l_ref[...]).astype(o_ref.dtype)
        return carry

    lax.fori_loop(0, q_ref.shape[0] // tq, q_tile, 0)


def mla_flash(q, kn, kr, v, *, batch, seq, tq):
    return pl.pallas_call(
        _mla_flash_kernel,
        grid=(batch, C_HEADS),
        in_specs=[
            pl.BlockSpec((seq, C_QW), lambda b, h: (b, h)),
            pl.BlockSpec((seq, C_NOPE), lambda b, h: (b, h)),
            pl.BlockSpec((seq, LANES), lambda b, h: (b, 0)),
            pl.BlockSpec((seq, C_V), lambda b, h: (b, h)),
        ],
        out_specs=pl.BlockSpec((seq, C_V), lambda b, h: (b, h)),
        out_shape=jax.ShapeDtypeStruct((batch * seq, C_HEADS * C_V), BF16),
        scratch_shapes=[
            pltpu.VMEM((tq, LANES), F32),
            pltpu.VMEM((tq, LANES), F32),
            pltpu.VMEM((tq, C_V), F32),
        ],
        compiler_params=_cparams("parallel", "parallel"),
        name="mla_flash",
    )(q, kn, kr, v)


def _mixer_out_kernel(oa_ref, ob_ref, oc_ref, wa_ref, wb_ref, wc_ref, g0_ref, g1_ref, g2_ref,
                      b0_ref, b1_ref, b2_ref, wo_ref, xr_ref, gn_ref, o_ref, xg_ref, inv_ref, mg_ref, ss_ref,
                      *, n_rows):
    i = pl.program_id(0)
    j = pl.program_id(1)
    _, nj, _, tn = mg_ref.shape
    cur = lax.rem(i, 2)

    @pl.when(i > 0)
    def _():
        acc = xr_ref[...]
        for c in range(nj):
            acc = acc + jnp.dot(mg_ref[1 - cur, c], wo_ref[c * tn:(c + 1) * tn, :], preferred_element_type=F32)
        o_ref[...] = acc
        _emit_next_norm(acc, gn_ref, xg_ref, inv_ref, ss_ref, j, nj)

    def branch(o, w, g, b):
        return jax.nn.sigmoid(g[...] + b[...]) * jnp.dot(o[...], w[...], preferred_element_type=F32)

    @pl.when(i < n_rows)
    def _():
        acc = branch(oa_ref, wa_ref, g0_ref, b0_ref)
        acc = acc + branch(ob_ref, wb_ref, g1_ref, b1_ref)
        acc = acc + branch(oc_ref, wc_ref, g2_ref, b2_ref)
        mg_ref[cur, j] = acc.astype(mg_ref.dtype)


def _pipelined_row_maps(n_rows, ncols):
    row1 = lambda i: jnp.minimum(i, n_rows - 1)
    col1 = lambda i, j: jnp.where(i < n_rows, j, ncols - 1)
    row2 = lambda i: jnp.maximum(i - 1, 0)
    col2 = lambda i, j: jnp.where(i > 0, j, 0)
    return row1, col1, row2, col2


def mixer_output_block(x, o_a, o_b, o_c, w_a, w_b, w_c, gates, b_gate, w_o, next_gain, *, tm, tn):
    m, k = o_a.shape
    d = w_a.shape[1]
    nj = d // tn
    n_rows = m // tm
    mrow, mcol, orow, ocol = _pipelined_row_maps(n_rows, nj)
    act = pl.BlockSpec((tm, k), lambda i, j: (mrow(i), 0))
    wgt = pl.BlockSpec((k, tn), lambda i, j: (0, mcol(i, j)))
    gate = lambda br: pl.BlockSpec((tm, tn), lambda i, j: (mrow(i), br * nj + mcol(i, j)))
    gbias = lambda br: pl.BlockSpec((1, tn), lambda i, j: (0, br * nj + j))
    out_tile = pl.BlockSpec((tm, tn), lambda i, j: (orow(i), ocol(i, j)))
    return pl.pallas_call(
        functools.partial(_mixer_out_kernel, n_rows=n_rows),
        grid=(n_rows + 1, nj),
        in_specs=[act, act, act, wgt, wgt, wgt, gate(0), gate(1), gate(2), gbias(0), gbias(1), gbias(2),
                  pl.BlockSpec((d, tn), lambda i, j: (0, ocol(i, j))),
                  out_tile,
                  pl.BlockSpec((1, tn), lambda i, j: (0, ocol(i, j)))],
        out_specs=[out_tile, out_tile, pl.BlockSpec((tm, LANES), lambda i, j: (orow(i), 0))],
        out_shape=[jax.ShapeDtypeStruct((m, d), F32), jax.ShapeDtypeStruct((m, d), BF16),
                   jax.ShapeDtypeStruct((m, LANES), F32)],
        scratch_shapes=[pltpu.VMEM((2, nj, tm, tn), BF16), pltpu.VMEM((tm, LANES), F32)],
        compiler_params=_cparams("arbitrary", "arbitrary"),
        name="mixer_output_block",
    )(o_a, o_b, o_c, w_a, w_b, w_c, gates, gates, gates, *([b_gate.reshape(1, -1)] * 3), w_o, x,
      next_gain.reshape(1, d))


def _cross_block_kernel(xg_ref, inv_ref, wq_ref, k_ref, v_ref, wo_ref, xr_ref, gn_ref,
                        o_ref, xgo_ref, invo_ref, ao_ref, ss_ref, *, scale, n_rows):
    i = pl.program_id(0)
    j = pl.program_id(1)
    hd = k_ref.shape[1]
    cur = lax.rem(i, 2)

    @pl.when(i > 0)
    def _():
        acc = xr_ref[...]
        for h in range(X_HEADS):
            acc = acc + jnp.dot(ao_ref[1 - cur, h], wo_ref[h * hd:(h + 1) * hd, :], preferred_element_type=F32)
        o_ref[...] = acc
        _emit_next_norm(acc, gn_ref, xgo_ref, invo_ref, ss_ref, j, X_HEADS)

    @pl.when(i < n_rows)
    def _():
        q = jnp.dot(xg_ref[...], wq_ref[...], preferred_element_type=F32) * (_inv_cols(inv_ref, hd) * scale)
        s = lax.dot_general(q.astype(BF16), k_ref[...], (((1,), (1,)), ((), ())), preferred_element_type=F32)
        m = jnp.max(s, axis=-1, keepdims=True)
        e = jnp.exp(s - m)
        l = jnp.sum(e, axis=-1, keepdims=True)
        o = jnp.dot(e.astype(BF16), v_ref[...], preferred_element_type=F32) / l
        ao_ref[cur, j] = o.astype(ao_ref.dtype)


def cross_attention_block(x, xg, inv, w_q, kv, w_o, next_gain, *, batch, seq, mem_len, tm):
    m, d = x.shape
    hd = d // X_HEADS
    tiles_per_batch = seq // tm
    n_rows = m // tm
    arow, head, orow, ocol = _pipelined_row_maps(n_rows, X_HEADS)
    out_tile = pl.BlockSpec((tm, hd), lambda i, j: (orow(i), ocol(i, j)))
    return pl.pallas_call(
        functools.partial(_cross_block_kernel, scale=hd ** -0.5, n_rows=n_rows),
        grid=(n_rows + 1, X_HEADS),
        in_specs=[
            pl.BlockSpec((tm, d), lambda i, j: (arow(i), 0)),
            pl.BlockSpec((tm, LANES), lambda i, j: (arow(i), 0)),
            pl.BlockSpec((d, hd), lambda i, j: (0, head(i, j))),
            pl.BlockSpec((mem_len, hd), lambda i, j: (arow(i) // tiles_per_batch, head(i, j))),
            pl.BlockSpec((mem_len, hd), lambda i, j: (arow(i) // tiles_per_batch, X_HEADS + head(i, j))),
            pl.BlockSpec((d, hd), lambda i, j: (0, ocol(i, j))),
            out_tile,
            pl.BlockSpec((1, hd), lambda i, j: (0, ocol(i, j))),
        ],
        out_specs=[out_tile, out_tile, pl.BlockSpec((tm, LANES), lambda i, j: (orow(i), 0))],
        out_shape=[jax.ShapeDtypeStruct((m, d), F32), jax.ShapeDtypeStruct((m, d), BF16),
                   jax.ShapeDtypeStruct((m, LANES), F32)],
        scratch_shapes=[pltpu.VMEM((2, X_HEADS, tm, hd), BF16), pltpu.VMEM((tm, LANES), F32)],
        compiler_params=_cparams("arbitrary", "arbitrary"),
        name="cross_attention_block",
    )(xg, inv, w_q, kv, kv, w_o, x, next_gain.reshape(1, d))


def _pack_w_in(w):
    nl, d, _ = w.shape
    sizes = (1024, 1024, 1024, B_DK, B_DK, B_DV, B_GATE_RANK, B_DV, C_Q_RANK, C_KV_RANK, C_ROPE, N_BRANCH * d)
    offs = np.concatenate([[0], np.cumsum(sizes)])
    aq, ak, av, bq, bk, bv, lr, br, cqa, ckva, ckr, gates = [
        w[:, :, offs[i]:offs[i + 1]].astype(BF16) for i in range(len(sizes))]
    zeros = lambda n: jnp.zeros((nl, d, n), BF16)
    used = Z_CKR + LANES
    wz = jnp.concatenate(
        [aq, ak, av, bv, br, cqa, ckva, bq, bk,
         lr, zeros(LANES - B_GATE_RANK), ckr, zeros(LANES - C_ROPE), zeros(Z_WIDTH - used)], axis=2)
    return wz, gates


def _pack_w_qb(w):
    r = w.shape[0]
    w = w.reshape(r, C_HEADS, C_NOPE + C_ROPE)
    w = jnp.pad(w, ((0, 0), (0, 0), (0, C_QW - C_NOPE - C_ROPE)))
    return w.reshape(r, C_HEADS * C_QW).astype(BF16)


def _pack_w_kvb(w):
    r = w.shape[0]
    w = w.reshape(r, C_HEADS, C_NOPE + C_V)
    wk = w[:, :, :C_NOPE].reshape(r, C_HEADS * C_NOPE)
    wv = w[:, :, C_NOPE:].reshape(r, C_HEADS * C_V)
    return wk.astype(BF16), wv.astype(BF16)


def kernel(x, mem, rel_bias, norm_mix, w_in, b_gate, w_alpha, b_alpha, gla_norm, q_a_norm, w_qb, kv_a_norm, w_kvb, w_up_a, w_up_b, w_up_c, w_o, norm_x, norm_mem, w_xq, w_xkv, w_xo, norm_ffn, w_ffn_gate, w_ffn_up, w_ffn_down, norm_final):
    batch, seq, d = x.shape
    mem_len = mem.shape[1]
    depth = w_in.shape[0]
    t = batch * seq
    assert d == 2048 and seq % A_TILE == 0, "tiling is derived for the stated shapes"

    xf = x.reshape(t, d)
    memf = mem.reshape(batch * mem_len, d)
    bias_tab = dilated_bias_table(rel_bias)
    q_tab, k_tab = _rope_tables(seq)
    bf = lambda a: a.astype(BF16)
    w_in_z, w_in_g = _pack_w_in(w_in)

    xg, inv = prenorm(xf, norm_mix[0], tm=256)
    for l in range(depth):
        z = scaled_matmul(xg, inv, w_in_z, out_dtype=F32, tm=1024, tn=W_Z_TN, layer=l)
        gates = scaled_matmul(xg, inv, w_in_g, out_dtype=BF16, tm=1024, tn=W_G_TN, layer=l)
        o_a = dilated_attention(z, bias_tab, batch=batch, seq=seq)
        w_alpha_p = jnp.pad(w_alpha[l], ((0, LANES - B_GATE_RANK), (0, 0)))
        o_b = gated_linear_attention(z, w_alpha_p, b_alpha[l], gla_norm[l], batch=batch, seq=seq, tc=512)
        wk_p, wv_p = _pack_w_kvb(w_kvb[l])
        cq, ckn, ckr, cv = mla_project(z, q_a_norm[l], kv_a_norm[l], _pack_w_qb(w_qb[l]), wk_p, wv_p,
                                       q_tab, k_tab, seq=seq, tm=512)
        o_c = mla_flash(cq, ckn, ckr, cv, batch=batch, seq=seq, tq=1024)
        xf, xg, inv = mixer_output_block(xf, o_a, o_b, o_c, bf(w_up_a[l]), bf(w_up_b[l]), bf(w_up_c[l]), gates,
                                         b_gate[l], bf(w_o[l]), norm_x[l], tm=1024, tn=512)
        xkv = norm_matmul(memf, norm_mem[l], w_xkv, out_dtype=BF16, tm=batch * mem_len, tn=512, layer=l)
        xf, xg, inv = cross_attention_block(xf, xg, inv, bf(w_xq[l]), xkv, bf(w_xo[l]), norm_ffn[l],
                                            batch=batch, seq=seq, mem_len=mem_len, tm=1024)
        act = scaled_swiglu(xg, inv, w_ffn_gate, w_ffn_up, tm=SWIGLU_TM, tn=512, layer=l)
        if l + 1 < depth:
            xf, xg, inv = matmul_residual(act, bf(w_ffn_down[l]), xf, tm=1024, tn=512, next_gain=norm_mix[l + 1])
        else:
            xf = matmul_residual(act, bf(w_ffn_down[l]), xf, tm=1024, tn=512)
    return rmsnorm_rows(xf, norm_final, tm=512).reshape(batch, seq, d)
```

```python
import functools
import math

import numpy as np
import jax
import jax.numpy as jnp
from jax import lax
from jax.experimental import pallas as pl
from jax.experimental.pallas import tpu as pltpu

F32 = jnp.float32
BF16 = jnp.bfloat16
EPS = 1e-6
NEG = -1e30

HEAD_DIM = 128
A_HEADS = 8
A_DILATIONS = (1, 4, 16)
A_BLOCK = 128
N_BUCKETS = 32
MAX_DISTANCE = 2048
B_HEADS = 4
B_DK = 512
B_DV = 1024
B_GATE_RANK = 16
B_GATE_TAU = 16.0
B_CHUNK = 64
C_HEADS = 8
C_Q_RANK = 512
C_KV_RANK = 512
C_NOPE = 128
C_ROPE = 64
C_V = 128
ROPE_THETA = 10000.0
X_HEADS = 4
N_BRANCH = 3

VMEM_LIMIT_BYTES = 56 * 1024 * 1024
LANES = 128

Z_AQ, Z_AK, Z_AV = 0, 1024, 2048
Z_BV, Z_BR = 3072, 4096
Z_CQA, Z_CKVA = 5120, 5632
Z_BQ, Z_BK = 6144, 6656
Z_LR = 7168
Z_CKR = 7296
Z_WIDTH = 7680
G_WIDTH = 3 * 2048
W_Z_TN = 1536
W_G_TN = 2048
SWIGLU_TM = 2048
A_TILE = 2048


def _cparams(*sem):
    return pltpu.CompilerParams(dimension_semantics=sem, vmem_limit_bytes=VMEM_LIMIT_BYTES)


def _rmsnorm_rows(x_ref, g_ref, h_ref):
    rows = x_ref.shape[0]

    def body(i, carry):
        r = pl.multiple_of(i * 16, 16)
        x = x_ref[pl.ds(r, 16), :]
        ms = jnp.mean(x * x, axis=-1, keepdims=True)
        h_ref[pl.ds(r, 16), :] = (x * lax.rsqrt(ms + EPS) * g_ref[...]).astype(BF16)
        return carry

    lax.fori_loop(0, rows // 16, body, 0, unroll=8)


def _norm_matmul_kernel(x_ref, g_ref, w_ref, o_ref, h_ref, *, scale):
    @pl.when(pl.program_id(1) == 0)
    def _():
        _rmsnorm_rows(x_ref, g_ref, h_ref)

    acc = jnp.dot(h_ref[...], w_ref[...].astype(BF16), preferred_element_type=F32)
    if scale != 1.0:
        acc = acc * scale
    o_ref[...] = acc.astype(o_ref.dtype)


def _weight_spec(w, layer, tn):
    if w.ndim == 2:
        return pl.BlockSpec((w.shape[0], tn), lambda i, j: (0, j))
    return pl.BlockSpec((None, w.shape[1], tn), lambda i, j: (layer, 0, j))


def norm_matmul(x, g, w, *, out_dtype, tm, tn, scale=1.0, layer=None):
    m, k = x.shape
    n = w.shape[-1]
    return pl.pallas_call(
        functools.partial(_norm_matmul_kernel, scale=scale),
        grid=(m // tm, n // tn),
        in_specs=[
            pl.BlockSpec((tm, k), lambda i, j: (i, 0)),
            pl.BlockSpec((1, k), lambda i, j: (0, 0)),
            _weight_spec(w, layer, tn),
        ],
        out_specs=pl.BlockSpec((tm, tn), lambda i, j: (i, j)),
        out_shape=jax.ShapeDtypeStruct((m, n), out_dtype),
        scratch_shapes=[pltpu.VMEM((tm, k), BF16)],
        compiler_params=_cparams("parallel", "arbitrary"),
        name="norm_matmul",
    )(x, g.reshape(1, k), w)


def _inv_cols(inv_ref, n):
    inv = inv_ref[...]
    return jnp.concatenate([inv] * (n // LANES), axis=1)


def _next_norm_tile(o, g_ref, xg_ref, ss_ref):
    xg_ref[...] = (o * g_ref[...]).astype(xg_ref.dtype)
    ss_ref[...] = ss_ref[...] + jnp.sum(o * o, axis=-1, keepdims=True)


def _next_norm_finish(inv_ref, ss_ref, d):
    inv_ref[...] = lax.rsqrt(ss_ref[...] * (1.0 / d) + EPS)


def _prenorm_kernel(x_ref, g_ref, xg_ref, inv_ref):
    x = x_ref[...]
    xg_ref[...] = (x * g_ref[...]).astype(xg_ref.dtype)
    ms = jnp.mean(x * x, axis=-1, keepdims=True)
    inv_ref[...] = jnp.broadcast_to(lax.rsqrt(ms + EPS), inv_ref.shape)


def prenorm(x, g, *, tm):
    m, k = x.shape
    return pl.pallas_call(
        _prenorm_kernel,
        grid=(m // tm,),
        in_specs=[pl.BlockSpec((tm, k), lambda i: (i, 0)), pl.BlockSpec((1, k), lambda i: (0, 0))],
        out_specs=[pl.BlockSpec((tm, k), lambda i: (i, 0)), pl.BlockSpec((tm, LANES), lambda i: (i, 0))],
        out_shape=[jax.ShapeDtypeStruct((m, k), BF16), jax.ShapeDtypeStruct((m, LANES), F32)],
        compiler_params=_cparams("parallel"),
        name="prenorm",
    )(x, g.reshape(1, k))


def _scaled_matmul_kernel(xg_ref, inv_ref, w_ref, o_ref):
    acc = jnp.dot(xg_ref[...], w_ref[...].astype(BF16), preferred_element_type=F32)
    o_ref[...] = (acc * _inv_cols(inv_ref, acc.shape[1])).astype(o_ref.dtype)


def scaled_matmul(xg, inv, w, *, out_dtype, tm, tn, layer=None):
    m, k = xg.shape
    n = w.shape[-1]
    return pl.pallas_call(
        _scaled_matmul_kernel,
        grid=(m // tm, n // tn),
        in_specs=[
            pl.BlockSpec((tm, k), lambda i, j: (i, 0)),
            pl.BlockSpec((tm, LANES), lambda i, j: (i, 0)),
            _weight_spec(w, layer, tn),
        ],
        out_specs=pl.BlockSpec((tm, tn), lambda i, j: (i, j)),
        out_shape=jax.ShapeDtypeStruct((m, n), out_dtype),
        compiler_params=_cparams("parallel", "parallel"),
        name="scaled_matmul",
    )(xg, inv, w)


def _scaled_swiglu_kernel(xg_ref, inv_ref, wg_ref, wu_ref, o_ref):
    xg = xg_ref[...]
    inv = _inv_cols(inv_ref, o_ref.shape[1])
    a = jnp.dot(xg, wg_ref[...].astype(BF16), preferred_element_type=F32) * inv
    b = jnp.dot(xg, wu_ref[...].astype(BF16), preferred_element_type=F32) * inv
    o_ref[...] = (a * jax.nn.sigmoid(a) * b).astype(o_ref.dtype)


def scaled_swiglu(xg, inv, wg, wu, *, tm, tn, layer=None):
    m, k = xg.shape
    n = wg.shape[-1]
    return pl.pallas_call(
        _scaled_swiglu_kernel,
        grid=(m // tm, n // tn),
        in_specs=[
            pl.BlockSpec((tm, k), lambda i, j: (i, 0)),
            pl.BlockSpec((tm, LANES), lambda i, j: (i, 0)),
            _weight_spec(wg, layer, tn),
            _weight_spec(wu, layer, tn),
        ],
        out_specs=pl.BlockSpec((tm, tn), lambda i, j: (i, j)),
        out_shape=jax.ShapeDtypeStruct((m, n), BF16),
        compiler_params=_cparams("parallel", "parallel"),
        name="scaled_swiglu",
    )(xg, inv, wg, wu)


def _matmul_residual_kernel(a_ref, w_ref, r_ref, o_ref):
    o_ref[...] = r_ref[...] + jnp.dot(a_ref[...], w_ref[...].astype(BF16), preferred_element_type=F32)


def _matmul_residual_norm_kernel(a_ref, w_ref, r_ref, g_ref, o_ref, xg_ref, inv_ref, ss_ref, *, ncols):
    j = pl.program_id(1)

    @pl.when(j == 0)
    def _():
        ss_ref[...] = jnp.zeros_like(ss_ref)

    o = r_ref[...] + jnp.dot(a_ref[...], w_ref[...].astype(BF16), preferred_element_type=F32)
    o_ref[...] = o
    _next_norm_tile(o, g_ref, xg_ref, ss_ref)

    @pl.when(j == ncols - 1)
    def _():
        _next_norm_finish(inv_ref, ss_ref, ncols * o.shape[1])


def matmul_residual(a, w, res, *, tm, tn, layer=None, next_gain=None):
    m, k = a.shape
    n = w.shape[-1]
    tile = pl.BlockSpec((tm, tn), lambda i, j: (i, j))
    in_specs = [pl.BlockSpec((tm, k), lambda i, j: (i, 0)), _weight_spec(w, layer, tn), tile]
    if next_gain is None:
        return pl.pallas_call(
            _matmul_residual_kernel,
            grid=(m // tm, n // tn),
            in_specs=in_specs,
            out_specs=tile,
            out_shape=jax.ShapeDtypeStruct((m, n), F32),
            compiler_params=_cparams("parallel", "parallel"),
            name="matmul_residual",
        )(a, w, res)
    return pl.pallas_call(
        functools.partial(_matmul_residual_norm_kernel, ncols=n // tn),
        grid=(m // tm, n // tn),
        in_specs=in_specs + [pl.BlockSpec((1, tn), lambda i, j: (0, j))],
        out_specs=[tile, tile, pl.BlockSpec((tm, LANES), lambda i, j: (i, 0))],
        out_shape=[jax.ShapeDtypeStruct((m, n), F32), jax.ShapeDtypeStruct((m, n), BF16),
                   jax.ShapeDtypeStruct((m, LANES), F32)],
        scratch_shapes=[pltpu.VMEM((tm, LANES), F32)],
        compiler_params=_cparams("parallel", "arbitrary"),
        name="matmul_residual_norm",
    )(a, w, res, next_gain.reshape(1, n))


def _rmsnorm_kernel(x_ref, g_ref, o_ref):
    x = x_ref[...]
    ms = jnp.mean(x * x, axis=-1, keepdims=True)
    o_ref[...] = x * lax.rsqrt(ms + EPS) * g_ref[...]


def rmsnorm_rows(x, g, *, tm):
    m, k = x.shape
    return pl.pallas_call(
        _rmsnorm_kernel,
        grid=(m // tm,),
        in_specs=[pl.BlockSpec((tm, k), lambda i: (i, 0)), pl.BlockSpec((1, k), lambda i: (0, 0))],
        out_specs=pl.BlockSpec((tm, k), lambda i: (i, 0)),
        out_shape=jax.ShapeDtypeStruct((m, k), F32),
        compiler_params=_cparams("parallel"),
        name="final_rmsnorm",
    )(x, g.reshape(1, k))


def _t5_bucket_np(dist):
    max_exact = N_BUCKETS // 2
    n = np.maximum(dist, 1).astype(np.float64)
    large = max_exact + (np.log(n / max_exact) / math.log(MAX_DISTANCE / max_exact)
                         * (N_BUCKETS - max_exact)).astype(np.int32)
    large = np.minimum(large, N_BUCKETS - 1)
    return np.where(dist < max_exact, dist, large).astype(np.int32)


def _dilated_bucket_table():
    qi = np.arange(A_BLOCK)[:, None]
    kj = np.arange(2 * A_BLOCK)[None, :]
    steps_back = qi + A_BLOCK - kj
    valid = (steps_back >= 0) & (steps_back <= A_BLOCK)
    tabs = []
    for dil in A_DILATIONS:
        bucket = _t5_bucket_np(np.clip(steps_back, 0, A_BLOCK) * dil)
        tabs.append(np.where(valid, bucket, -1))
    return np.stack(tabs).astype(np.int32)


def _bias_table_kernel(idx_ref, rb_ref, o_ref):
    h = pl.program_id(1)
    idx = idx_ref[...]
    acc = jnp.full(idx.shape, NEG, F32)
    for b in range(N_BUCKETS):
        acc = jnp.where(idx == b, rb_ref[b, h], acc)
    o_ref[...] = acc


def dilated_bias_table(rel_bias):
    idx = jnp.asarray(_dilated_bucket_table())
    npat = len(A_DILATIONS)
    return pl.pallas_call(
        _bias_table_kernel,
        grid=(npat, A_HEADS),
        in_specs=[
            pl.BlockSpec((None, A_BLOCK, 2 * A_BLOCK), lambda p, h: (p, 0, 0)),
            pl.BlockSpec(memory_space=pltpu.SMEM),
        ],
        out_specs=pl.BlockSpec((None, None, A_BLOCK, 2 * A_BLOCK), lambda p, h: (p, h, 0, 0)),
        out_shape=jax.ShapeDtypeStruct((npat, A_HEADS, A_BLOCK, 2 * A_BLOCK), F32),
        compiler_params=_cparams("parallel", "parallel"),
        name="dilated_bias_table",
    )(idx, rel_bias)


def _dilated_kernel(q_ref, kc_ref, kp_ref, vc_ref, vp_ref, bias_ref, o_ref, bm0, o_scr, lse_scr, *, scale):
    t = pl.program_id(2)
    tile = q_ref.shape[0]
    blk = A_BLOCK
    col = lax.broadcasted_iota(jnp.int32, (blk, 2 * blk), 1)
    no_prev = jnp.logical_and(col < blk, t == 0)
    for p in range(len(A_DILATIONS)):
        bm0[p] = jnp.where(no_prev, NEG, bias_ref[p])

    def rows(start, size, dil):
        return pl.ds(start, size) if dil == 1 else pl.ds(start, size, stride=dil)

    def block(p, dil, r, n):
        q_idx = rows(r + dil * blk * n, blk, dil)
        if n == 0:
            prev = rows(tile - dil * blk + r, blk, dil)
            cur = rows(r, blk, dil)
            k = jnp.concatenate([kp_ref[prev, :], kc_ref[cur, :]], axis=0)
            v = jnp.concatenate([vp_ref[prev, :], vc_ref[cur, :]], axis=0)
            bm = bm0[p]
        else:
            k_idx = rows(r + dil * blk * (n - 1), 2 * blk, dil)
            k, v = kc_ref[k_idx, :], vc_ref[k_idx, :]
            bm = bias_ref[p]
        q = (q_ref[q_idx, :] * scale).astype(BF16)
        s = lax.dot_general(q, k.astype(BF16), (((1,), (1,)), ((), ())), preferred_element_type=F32)
        s = jnp.where(bm > 0.5 * NEG, s + bm, NEG)
        m = jnp.max(s, axis=-1, keepdims=True)
        e = jnp.exp(s - m)
        l = jnp.sum(e, axis=-1, keepdims=True)
        o = jnp.dot(e.astype(BF16), v.astype(BF16), preferred_element_type=F32) / l
        lse = m + jnp.log(l)
        o_scr[p, q_idx, :] = o
        lse_scr[p, q_idx, :] = jnp.broadcast_to(lse, (blk, HEAD_DIM))

    for p, dil in enumerate(A_DILATIONS):
        for r in range(dil):
            for n in range(tile // (blk * dil)):
                block(p, dil, r, n)

    def merge(i, carry):
        r = pl.multiple_of(i * blk, blk)
        rows = pl.ds(r, blk)
        l0, l1, l2 = lse_scr[0, rows, :], lse_scr[1, rows, :], lse_scr[2, rows, :]
        mx = jnp.maximum(jnp.maximum(l0, l1), l2)
        w0, w1, w2 = jnp.exp(l0 - mx), jnp.exp(l1 - mx), jnp.exp(l2 - mx)
        num = w0 * o_scr[0, rows, :] + w1 * o_scr[1, rows, :] + w2 * o_scr[2, rows, :]
        o_ref[rows, :] = (num / (w0 + w1 + w2)).astype(o_ref.dtype)
        return carry

    lax.fori_loop(0, tile // blk, merge, 0)


def dilated_attention(z, bias_tab, *, batch, seq):
    z3 = z.reshape(batch, seq, z.shape[1])
    nt = seq // A_TILE
    hq, hk, hv = Z_AQ // HEAD_DIM, Z_AK // HEAD_DIM, Z_AV // HEAD_DIM
    tile_spec = lambda col0, prev: pl.BlockSpec(
        (None, A_TILE, HEAD_DIM),
        (lambda b, h, t: (b, jnp.maximum(t - 1, 0), col0 + h)) if prev else (lambda b, h, t: (b, t, col0 + h)))
    npat = len(A_DILATIONS)
    out = pl.pallas_call(
        functools.partial(_dilated_kernel, scale=HEAD_DIM ** -0.5),
        grid=(batch, A_HEADS, nt),
        in_specs=[
            tile_spec(hq, False),
            tile_spec(hk, False), tile_spec(hk, True),
            tile_spec(hv, False), tile_spec(hv, True),
            pl.BlockSpec((npat, None, A_BLOCK, 2 * A_BLOCK), lambda b, h, t: (0, h, 0, 0)),
        ],
        out_specs=pl.BlockSpec((None, A_TILE, HEAD_DIM), lambda b, h, t: (b, t, h)),
        out_shape=jax.ShapeDtypeStruct((batch, seq, A_HEADS * HEAD_DIM), BF16),
        scratch_shapes=[
            pltpu.VMEM((npat, A_BLOCK, 2 * A_BLOCK), F32),
            pltpu.VMEM((npat, A_TILE, HEAD_DIM), F32),
            pltpu.VMEM((npat, A_TILE, HEAD_DIM), F32),
        ],
        compiler_params=_cparams("parallel", "parallel", "arbitrary"),
        name="dilated_attention",
    )(z3, z3, z3, z3, z3, bias_tab)
    return out.reshape(batch * seq, A_HEADS * HEAD_DIM)


def _gla_kernel(q_ref, k_ref, v_ref, r_ref, lr_ref, wa_ref, ba_ref, gn_ref, o_ref, st_ref, *, scale):
    @pl.when(pl.program_id(1) == 0)
    def _():
        st_ref[...] = jnp.zeros_like(st_ref)

    c = B_CHUNK
    tc = q_ref.shape[0]
    dk = B_DK // B_HEADS
    dv = B_DV // B_HEADS
    pre = jnp.dot(lr_ref[...], wa_ref[...], precision=lax.Precision.HIGHEST,
                  preferred_element_type=F32) + ba_ref[...]
    log_a = (jnp.minimum(pre, 0.0) - jnp.log(1.0 + jnp.exp(-jnp.abs(pre)))) * (1.0 / B_GATE_TAU)
    tril = lax.broadcasted_iota(jnp.int32, (c, c), 0) >= lax.broadcasted_iota(jnp.int32, (c, c), 1)
    ones_tril = jnp.where(tril, 1.0, 0.0).astype(F32)
    for ci in range(tc // c):
        sl = slice(ci * c, (ci + 1) * c)
        cum_all = jnp.dot(ones_tril, log_a[sl, :], precision=lax.Precision.HIGHEST, preferred_element_type=F32)
        for h in range(B_HEADS):
            kcols = slice(h * dk, (h + 1) * dk)
            vcols = slice(h * dv, (h + 1) * dv)
            cum = cum_all[:, kcols]
            cum_last = cum[c - 1:c, :]
            q = q_ref[sl, kcols] * scale
            k = k_ref[sl, kcols]
            v = v_ref[sl, vcols]
            q_dec = (q * jnp.exp(cum)).astype(BF16)
            k_inv = (k * jnp.exp(-cum)).astype(BF16)
            k_state = (k * jnp.exp(cum_last - cum)).astype(BF16)
            att = lax.dot_general(q_dec, k_inv, (((1,), (1,)), ((), ())), preferred_element_type=F32)
            att = jnp.where(tril, att, 0.0).astype(BF16)
            st = st_ref[h]
            o = (jnp.dot(att, v.astype(BF16), preferred_element_type=F32)
                 + lax.dot_general(q_dec, st.astype(BF16), (((1,), (1,)), ((), ())), preferred_element_type=F32))
            st_ref[h] = st * jnp.exp(cum_last) + jnp.dot(v.T.astype(BF16), k_state, preferred_element_type=F32)
            ms = jnp.mean(o * o, axis=-1, keepdims=True)
            o = o * lax.rsqrt(ms + EPS) * gn_ref[:, vcols]
            r = r_ref[sl, vcols]
            o_ref[sl, vcols] = (o * (r * jax.nn.sigmoid(r))).astype(o_ref.dtype)


def gated_linear_attention(z, w_alpha_p, b_alpha, gla_norm, *, batch, seq, tc):
    dk = B_DK // B_HEADS
    dv = B_DV // B_HEADS
    nt = seq // tc
    row = lambda b, t: b * nt + t
    const = lambda b, t: (0, 0)
    return pl.pallas_call(
        functools.partial(_gla_kernel, scale=dk ** -0.5),
        grid=(batch, nt),
        in_specs=[
            pl.BlockSpec((tc, B_DK), lambda b, t: (row(b, t), Z_BQ // B_DK)),
            pl.BlockSpec((tc, B_DK), lambda b, t: (row(b, t), Z_BK // B_DK)),
            pl.BlockSpec((tc, B_DV), lambda b, t: (row(b, t), Z_BV // B_DV)),
            pl.BlockSpec((tc, B_DV), lambda b, t: (row(b, t), Z_BR // B_DV)),
            pl.BlockSpec((tc, LANES), lambda b, t: (row(b, t), Z_LR // LANES)),
            pl.BlockSpec((LANES, B_DK), const),
            pl.BlockSpec((1, B_DK), const),
            pl.BlockSpec((1, B_DV), const),
        ],
        out_specs=pl.BlockSpec((tc, B_DV), lambda b, t: (row(b, t), 0)),
        out_shape=jax.ShapeDtypeStruct((batch * seq, B_DV), BF16),
        scratch_shapes=[pltpu.VMEM((B_HEADS, dv, dk), F32)],
        compiler_params=_cparams("parallel", "arbitrary"),
        name="gated_linear_attention",
    )(z, z, z, z, z, w_alpha_p, b_alpha.reshape(1, B_DK), gla_norm.reshape(1, B_DV))


C_QW = 2 * C_NOPE


def _rope_tables(seq):
    pos = jnp.arange(seq, dtype=F32)
    inv = ROPE_THETA ** (-jnp.arange(0, C_ROPE, 2, dtype=F32) / C_ROPE)
    ang = pos[:, None] * inv[None, :]
    cos, sin = jnp.cos(ang), jnp.sin(ang)
    half = C_ROPE // 2
    z = lambda w: jnp.zeros((seq, w), F32)
    k_tab = jnp.stack([
        jnp.concatenate([cos, cos, z(LANES - C_ROPE)], axis=1),
        jnp.concatenate([-sin, z(half), z(LANES - C_ROPE)], axis=1),
        jnp.concatenate([z(half), sin, z(LANES - C_ROPE)], axis=1)])
    q_tab = jnp.concatenate([
        jnp.stack([jnp.ones((seq, C_NOPE), F32), z(C_NOPE), z(C_NOPE)]), k_tab], axis=2)
    return q_tab, k_tab


def _mla_proj_kernel(cq_ref, ckv_ref, ckr_ref, gq_ref, gkv_ref, wq_ref, wk_ref, wv_ref, qt_ref, kt_ref,
                     q_out, kn_out, kr_out, v_out, *, scale):
    half = C_ROPE // 2

    def rms(x, g):
        ms = jnp.mean(x * x, axis=-1, keepdims=True)
        return (x * lax.rsqrt(ms + EPS) * g).astype(BF16)

    cq = rms(cq_ref[...], gq_ref[...])
    q = jnp.dot(cq, wq_ref[...], preferred_element_type=F32)
    t0, t1, t2 = qt_ref[0], qt_ref[1], qt_ref[2]
    for h in range(C_HEADS):
        qh = q[:, h * C_QW:(h + 1) * C_QW]
        qh = qh * t0 + pltpu.roll(qh, C_QW - half, axis=1) * t1 + pltpu.roll(qh, half, axis=1) * t2
        q_out[:, h * C_QW:(h + 1) * C_QW] = (qh * scale).astype(q_out.dtype)
    ckv = rms(ckv_ref[...], gkv_ref[...])
    kn_out[...] = jnp.dot(ckv, wk_ref[...], preferred_element_type=F32).astype(kn_out.dtype)
    v_out[...] = jnp.dot(ckv, wv_ref[...], preferred_element_type=F32).astype(v_out.dtype)
    kr = ckr_ref[...]
    kr = kr * kt_ref[0] + pltpu.roll(kr, LANES - half, axis=1) * kt_ref[1] + pltpu.roll(kr, half, axis=1) * kt_ref[2]
    kr_out[...] = kr.astype(kr_out.dtype)


def mla_project(z, q_a_norm, kv_a_norm, wq_p, wk_p, wv_p, q_tab, k_tab, *, seq, tm):
    m = z.shape[0]
    nt = seq // tm
    const = lambda i: (0, 0)
    return pl.pallas_call(
        functools.partial(_mla_proj_kernel, scale=(C_NOPE + C_ROPE) ** -0.5 * math.log2(math.e)),
        grid=(m // tm,),
        in_specs=[
            pl.BlockSpec((tm, C_Q_RANK), lambda i: (i, Z_CQA // C_Q_RANK)),
            pl.BlockSpec((tm, C_KV_RANK), lambda i: (i, Z_CKVA // C_KV_RANK)),
            pl.BlockSpec((tm, LANES), lambda i: (i, Z_CKR // LANES)),
            pl.BlockSpec((1, C_Q_RANK), const),
            pl.BlockSpec((1, C_KV_RANK), const),
            pl.BlockSpec((C_Q_RANK, C_HEADS * C_QW), const),
            pl.BlockSpec((C_KV_RANK, C_HEADS * C_NOPE), const),
            pl.BlockSpec((C_KV_RANK, C_HEADS * C_V), const),
            pl.BlockSpec((3, tm, C_QW), lambda i: (0, i % nt, 0)),
            pl.BlockSpec((3, tm, LANES), lambda i: (0, i % nt, 0)),
        ],
        out_specs=[
            pl.BlockSpec((tm, C_HEADS * C_QW), lambda i: (i, 0)),
            pl.BlockSpec((tm, C_HEADS * C_NOPE), lambda i: (i, 0)),
            pl.BlockSpec((tm, LANES), lambda i: (i, 0)),
            pl.BlockSpec((tm, C_HEADS * C_V), lambda i: (i, 0)),
        ],
        out_shape=[
            jax.ShapeDtypeStruct((m, C_HEADS * C_QW), BF16),
            jax.ShapeDtypeStruct((m, C_HEADS * C_NOPE), BF16),
            jax.ShapeDtypeStruct((m, LANES), BF16),
            jax.ShapeDtypeStruct((m, C_HEADS * C_V), BF16),
        ],
        compiler_params=_cparams("parallel"),
        name="mla_project",
    )(z, z, z, q_a_norm.reshape(1, -1), kv_a_norm.reshape(1, -1), wq_p, wk_p, wv_p, q_tab, k_tab)


def _mla_flash_kernel(q_ref, kn_ref, kr_ref, v_ref, o_ref, m_ref, l_ref, acc_ref):
    tq = m_ref.shape[0]
    tk = tq // 2

    def q_tile(qi, carry):
        q0 = pl.multiple_of(qi * tq, tq)
        m_ref[...] = jnp.full_like(m_ref, NEG)
        l_ref[...] = jnp.zeros_like(l_ref)
        acc_ref[...] = jnp.zeros_like(acc_ref)

        def step(r0, nr, k_start, nk, mask_offset=None):
            rows = slice(r0, r0 + nr)
            ks = pl.ds(pl.multiple_of(k_start, tk), nk)
            k = jnp.concatenate([kn_ref[ks, :], kr_ref[ks, :]], axis=1)
            q = q_ref[pl.ds(pl.multiple_of(q0 + r0, tk), nr), :]
            s = lax.dot_general(q, k, (((1,), (1,)), ((), ())), preferred_element_type=F32)
            if mask_offset is not None:
                visible = (lax.broadcasted_iota(jnp.int32, s.shape, 1)
                           <= lax.broadcasted_iota(jnp.int32, s.shape, 0) + mask_offset)
                s = jnp.where(visible, s, NEG)
            m_prev = m_ref[rows, :]
            m_new = jnp.maximum(m_prev, jnp.max(s, axis=-1, keepdims=True))
            alpha = jnp.exp2(m_prev - m_new)
            p = jnp.exp2(s - jnp.concatenate([m_new] * (nk // LANES), axis=1))
            l_ref[rows, :] = alpha * l_ref[rows, :] + jnp.sum(p, axis=-1, keepdims=True)
            acc_ref[rows, :] = alpha * acc_ref[rows, :] + jnp.dot(p.astype(BF16), v_ref[ks, :],
                                                                  preferred_element_type=F32)
            m_ref[rows, :] = m_new

        def body(kb, c):
            step(0, tq, 2 * kb * tk, tk)
            step(0, tq, (2 * kb + 1) * tk, tk)
            return c

        lax.fori_loop(0, qi, body, 0)
        step(0, tk, q0, tk, 0)
        step(tk, tk, q0, tq, tk)
        o_ref[pl.ds(q0, tq), :] = (acc_ref[...] / l_ref[...]).astype(o_ref.dtype)
        return carry

    lax.fori_loop(0, q_ref.shape[0] // tq, q_tile, 0)


def mla_flash(q, kn, kr, v, *, batch, seq, tq):
    return pl.pallas_call(
        _mla_flash_kernel,
        grid=(batch, C_HEADS),
        in_specs=[
            pl.BlockSpec((seq, C_QW), lambda b, h: (b, h)),
            pl.BlockSpec((seq, C_NOPE), lambda b, h: (b, h)),
            pl.BlockSpec((seq, LANES), lambda b, h: (b, 0)),
            pl.BlockSpec((seq, C_V), lambda b, h: (b, h)),
        ],
        out_specs=pl.BlockSpec((seq, C_V), lambda b, h: (b, h)),
        out_shape=jax.ShapeDtypeStruct((batch * seq, C_HEADS * C_V), BF16),
        scratch_shapes=[
            pltpu.VMEM((tq, LANES), F32),
            pltpu.VMEM((tq, LANES), F32),
            pltpu.VMEM((tq, C_V), F32),
        ],
        compiler_params=_cparams("parallel", "parallel"),
        name="mla_flash",
    )(q, kn, kr, v)


def _mixer_out_kernel(oa_ref, ob_ref, oc_ref, wa_ref, wb_ref, wc_ref, g0_ref, g1_ref, g2_ref,
                      b0_ref, b1_ref, b2_ref, wo_ref, xr_ref, gn_ref, o_ref, xg_ref, inv_ref, mg_ref, ss_ref,
                      *, n_rows):
    i = pl.program_id(0)
    j = pl.program_id(1)
    _, nj, _, tn = mg_ref.shape
    cur = lax.rem(i, 2)

    def project():
        acc = xr_ref[...]
        for c in range(nj):
            acc = acc + jnp.dot(mg_ref[1 - cur, c], wo_ref[c * tn:(c + 1) * tn, :], preferred_element_type=F32)
        o_ref[...] = acc
        _next_norm_tile(acc, gn_ref, xg_ref, ss_ref)

    def branch(o, w, g, b):
        return jax.nn.sigmoid(g[...] + b[...]) * jnp.dot(o[...], w[...], preferred_element_type=F32)

    def merge():
        acc = branch(oa_ref, wa_ref, g0_ref, b0_ref)
        acc = acc + branch(ob_ref, wb_ref, g1_ref, b1_ref)
        acc = acc + branch(oc_ref, wc_ref, g2_ref, b2_ref)
        mg_ref[cur, j] = acc.astype(mg_ref.dtype)

    @pl.when(j == 0)
    def _():
        ss_ref[...] = jnp.zeros_like(ss_ref)

    @pl.when(jnp.logical_and(i > 0, i < n_rows))
    def _():
        project()
        merge()

    @pl.when(i == 0)
    def _():
        merge()

    @pl.when(i == n_rows)
    def _():
        project()

    @pl.when(jnp.logical_and(i > 0, j == nj - 1))
    def _():
        _next_norm_finish(inv_ref, ss_ref, nj * tn)


def _pipelined_row_maps(n_rows, ncols):
    row1 = lambda i: jnp.minimum(i, n_rows - 1)
    col1 = lambda i, j: jnp.where(i < n_rows, j, ncols - 1)
    row2 = lambda i: jnp.maximum(i - 1, 0)
    col2 = lambda i, j: jnp.where(i > 0, j, 0)
    return row1, col1, row2, col2


def mixer_output_block(x, o_a, o_b, o_c, w_a, w_b, w_c, gates, b_gate, w_o, next_gain, *, tm, tn):
    m, k = o_a.shape
    d = w_a.shape[1]
    nj = d // tn
    n_rows = m // tm
    mrow, mcol, orow, ocol = _pipelined_row_maps(n_rows, nj)
    act = pl.BlockSpec((tm, k), lambda i, j: (mrow(i), 0))
    wgt = pl.BlockSpec((k, tn), lambda i, j: (0, mcol(i, j)))
    gate = lambda br: pl.BlockSpec((tm, tn), lambda i, j: (mrow(i), br * nj + mcol(i, j)))
    gbias = lambda br: pl.BlockSpec((1, tn), lambda i, j: (0, br * nj + j))
    out_tile = pl.BlockSpec((tm, tn), lambda i, j: (orow(i), ocol(i, j)))
    return pl.pallas_call(
        functools.partial(_mixer_out_kernel, n_rows=n_rows),
        grid=(n_rows + 1, nj),
        in_specs=[act, act, act, wgt, wgt, wgt, gate(0), gate(1), gate(2), gbias(0), gbias(1), gbias(2),
                  pl.BlockSpec((d, tn), lambda i, j: (0, ocol(i, j))),
                  out_tile,
                  pl.BlockSpec((1, tn), lambda i, j: (0, ocol(i, j)))],
        out_specs=[out_tile, out_tile, pl.BlockSpec((tm, LANES), lambda i, j: (orow(i), 0))],
        out_shape=[jax.ShapeDtypeStruct((m, d), F32), jax.ShapeDtypeStruct((m, d), BF16),
                   jax.ShapeDtypeStruct((m, LANES), F32)],
        scratch_shapes=[pltpu.VMEM((2, nj, tm, tn), BF16), pltpu.VMEM((tm, LANES), F32)],
        compiler_params=_cparams("arbitrary", "arbitrary"),
        name="mixer_output_block",
    )(o_a, o_b, o_c, w_a, w_b, w_c, gates, gates, gates, *([b_gate.reshape(1, -1)] * 3), w_o, x,
      next_gain.reshape(1, d))


def _cross_block_kernel(xg_ref, inv_ref, wq_ref, k_ref, v_ref, wo_ref, xr_ref, gn_ref,
                        o_ref, xgo_ref, invo_ref, ao_ref, ss_ref, *, scale, n_rows):
    i = pl.program_id(0)
    j = pl.program_id(1)
    hd = k_ref.shape[1]
    cur = lax.rem(i, 2)

    def project():
        acc = xr_ref[...]
        for h in range(X_HEADS):
            acc = acc + jnp.dot(ao_ref[1 - cur, h], wo_ref[h * hd:(h + 1) * hd, :], preferred_element_type=F32)
        o_ref[...] = acc
        _next_norm_tile(acc, gn_ref, xgo_ref, ss_ref)

    def attend():
        q = jnp.dot(xg_ref[...], wq_ref[...], preferred_element_type=F32) * (_inv_cols(inv_ref, hd) * scale)
        s = lax.dot_general(q.astype(BF16), k_ref[...], (((1,), (1,)), ((), ())), preferred_element_type=F32)
        m = jnp.max(s, axis=-1, keepdims=True)
        e = jnp.exp(s - m)
        l = jnp.sum(e, axis=-1, keepdims=True)
        o = jnp.dot(e.astype(BF16), v_ref[...], preferred_element_type=F32) / l
        ao_ref[cur, j] = o.astype(ao_ref.dtype)

    @pl.when(j == 0)
    def _():
        ss_ref[...] = jnp.zeros_like(ss_ref)

    @pl.when(jnp.logical_and(i > 0, i < n_rows))
    def _():
        project()
        attend()

    @pl.when(i == 0)
    def _():
        attend()

    @pl.when(i == n_rows)
    def _():
        project()

    @pl.when(jnp.logical_and(i > 0, j == X_HEADS - 1))
    def _():
        _next_norm_finish(invo_ref, ss_ref, X_HEADS * hd)


def cross_attention_block(x, xg, inv, w_q, kv, w_o, next_gain, *, batch, seq, mem_len, tm):
    m, d = x.shape
    hd = d // X_HEADS
    tiles_per_batch = seq // tm
    n_rows = m // tm
    arow, head, orow, ocol = _pipelined_row_maps(n_rows, X_HEADS)
    out_tile = pl.BlockSpec((tm, hd), lambda i, j: (orow(i), ocol(i, j)))
    return pl.pallas_call(
        functools.partial(_cross_block_kernel, scale=hd ** -0.5, n_rows=n_rows),
        grid=(n_rows + 1, X_HEADS),
        in_specs=[
            pl.BlockSpec((tm, d), lambda i, j: (arow(i), 0)),
            pl.BlockSpec((tm, LANES), lambda i, j: (arow(i), 0)),
            pl.BlockSpec((d, hd), lambda i, j: (0, head(i, j))),
            pl.BlockSpec((mem_len, hd), lambda i, j: (arow(i) // tiles_per_batch, head(i, j))),
            pl.BlockSpec((mem_len, hd), lambda i, j: (arow(i) // tiles_per_batch, X_HEADS + head(i, j))),
            pl.BlockSpec((d, hd), lambda i, j: (0, ocol(i, j))),
            out_tile,
            pl.BlockSpec((1, hd), lambda i, j: (0, ocol(i, j))),
        ],
        out_specs=[out_tile, out_tile, pl.BlockSpec((tm, LANES), lambda i, j: (orow(i), 0))],
        out_shape=[jax.ShapeDtypeStruct((m, d), F32), jax.ShapeDtypeStruct((m, d), BF16),
                   jax.ShapeDtypeStruct((m, LANES), F32)],
        scratch_shapes=[pltpu.VMEM((2, X_HEADS, tm, hd), BF16), pltpu.VMEM((tm, LANES), F32)],
        compiler_params=_cparams("arbitrary", "arbitrary"),
        name="cross_attention_block",
    )(xg, inv, w_q, kv, kv, w_o, x, next_gain.reshape(1, d))


def _pack_w_in(w):
    nl, d, _ = w.shape
    sizes = (1024, 1024, 1024, B_DK, B_DK, B_DV, B_GATE_RANK, B_DV, C_Q_RANK, C_KV_RANK, C_ROPE, N_BRANCH * d)
    offs = np.concatenate([[0], np.cumsum(sizes)])
    aq, ak, av, bq, bk, bv, lr, br, cqa, ckva, ckr, gates = [
        w[:, :, offs[i]:offs[i + 1]].astype(BF16) for i in range(len(sizes))]
    zeros = lambda n: jnp.zeros((nl, d, n), BF16)
    used = Z_CKR + LANES
    wz = jnp.concatenate(
        [aq, ak, av, bv, br, cqa, ckva, bq, bk,
         lr, zeros(LANES - B_GATE_RANK), ckr, zeros(LANES - C_ROPE), zeros(Z_WIDTH - used)], axis=2)
    return wz, gates


def _pack_w_qb(w):
    r = w.shape[0]
    w = w.reshape(r, C_HEADS, C_NOPE + C_ROPE)
    w = jnp.pad(w, ((0, 0), (0, 0), (0, C_QW - C_NOPE - C_ROPE)))
    return w.reshape(r, C_HEADS * C_QW).astype(BF16)


def _pack_w_kvb(w):
    r = w.shape[0]
    w = w.reshape(r, C_HEADS, C_NOPE + C_V)
    wk = w[:, :, :C_NOPE].reshape(r, C_HEADS * C_NOPE)
    wv = w[:, :, C_NOPE:].reshape(r, C_HEADS * C_V)
    return wk.astype(BF16), wv.astype(BF16)


def kernel(x, mem, rel_bias, norm_mix, w_in, b_gate, w_alpha, b_alpha, gla_norm, q_a_norm, w_qb, kv_a_norm, w_kvb, w_up_a, w_up_b, w_up_c, w_o, norm_x, norm_mem, w_xq, w_xkv, w_xo, norm_ffn, w_ffn_gate, w_ffn_up, w_ffn_down, norm_final):
    batch, seq, d = x.shape
    mem_len = mem.shape[1]
    depth = w_in.shape[0]
    t = batch * seq
    assert d == 2048 and seq % A_TILE == 0, "tiling is derived for the stated shapes"

    xf = x.reshape(t, d)
    memf = mem.reshape(batch * mem_len, d)
    bias_tab = dilated_bias_table(rel_bias)
    q_tab, k_tab = _rope_tables(seq)
    bf = lambda a: a.astype(BF16)
    w_in_z, w_in_g = _pack_w_in(w_in)

    xg, inv = prenorm(xf, norm_mix[0], tm=1024)
    for l in range(depth):
        z = scaled_matmul(xg, inv, w_in_z, out_dtype=F32, tm=1024, tn=W_Z_TN, layer=l)
        gates = scaled_matmul(xg, inv, w_in_g, out_dtype=BF16, tm=1024, tn=W_G_TN, layer=l)
        o_a = dilated_attention(z, bias_tab, batch=batch, seq=seq)
        w_alpha_p = jnp.pad(w_alpha[l], ((0, LANES - B_GATE_RANK), (0, 0)))
        o_b = gated_linear_attention(z, w_alpha_p, b_alpha[l], gla_norm[l], batch=batch, seq=seq, tc=512)
        wk_p, wv_p = _pack_w_kvb(w_kvb[l])
        cq, ckn, ckr, cv = mla_project(z, q_a_norm[l], kv_a_norm[l], _pack_w_qb(w_qb[l]), wk_p, wv_p,
                                       q_tab, k_tab, seq=seq, tm=512)
        o_c = mla_flash(cq, ckn, ckr, cv, batch=batch, seq=seq, tq=1024)
        xf, xg, inv = mixer_output_block(xf, o_a, o_b, o_c, bf(w_up_a[l]), bf(w_up_b[l]), bf(w_up_c[l]), gates,
                                         b_gate[l], bf(w_o[l]), norm_x[l], tm=1024, tn=512)
        xkv = norm_matmul(memf, norm_mem[l], w_xkv, out_dtype=BF16, tm=batch * mem_len, tn=512, layer=l)
        xf, xg, inv = cross_attention_block(xf, xg, inv, bf(w_xq[l]), xkv, bf(w_xo[l]), norm_ffn[l],
                                            batch=batch, seq=seq, mem_len=mem_len, tm=1024)
        act = scaled_swiglu(xg, inv, w_ffn_gate, w_ffn_up, tm=SWIGLU_TM, tn=512, layer=l)
        if l + 1 < depth:
            xf, xg, inv = matmul_residual(act, bf(w_ffn_down[l]), xf, tm=1024, tn=512, next_gain=norm_mix[l + 1])
        else:
            xf = matmul_residual(act, bf(w_ffn_down[l]), xf, tm=1024, tn=512)
    return rmsnorm_rows(xf, norm_final, tm=512).reshape(batch, seq, d)
```

```python
import functools
import math

import numpy as np
import jax
import jax.numpy as jnp
from jax import lax
from jax.experimental import pallas as pl
from jax.experimental.pallas import tpu as pltpu

F32 = jnp.float32
BF16 = jnp.bfloat16
EPS = 1e-6
NEG = -1e30
LOG2E = math.log2(math.e)

HEAD_DIM = 128
A_HEADS = 8
A_DILATIONS = (1, 4, 16)
A_BLOCK = 128
N_BUCKETS = 32
MAX_DISTANCE = 2048
B_HEADS = 4
B_DK = 512
B_DV = 1024
B_GATE_RANK = 16
B_GATE_TAU = 16.0
B_CHUNK = 64
C_HEADS = 8
C_Q_RANK = 512
C_KV_RANK = 512
C_NOPE = 128
C_ROPE = 64
C_V = 128
ROPE_THETA = 10000.0
X_HEADS = 4
N_BRANCH = 3

VMEM_LIMIT_BYTES = 56 * 1024 * 1024
LANES = 128

Z_AQ, Z_AK, Z_AV = 0, 1024, 2048
Z_BV, Z_BR = 3072, 4096
Z_CQA, Z_CKVA = 5120, 5632
Z_BQ, Z_BK = 6144, 6656
Z_LR = 7168
Z_CKR = 7296
Z_WIDTH = 7680
G_WIDTH = 3 * 2048
W_Z_TN = 1536
W_G_TN = 2048
SWIGLU_TM = 2048
A_TILE = 2048


def _cparams(*sem):
    return pltpu.CompilerParams(dimension_semantics=sem, vmem_limit_bytes=VMEM_LIMIT_BYTES)


def _rmsnorm_rows(x_ref, g_ref, h_ref):
    rows = x_ref.shape[0]

    def body(i, carry):
        r = pl.multiple_of(i * 16, 16)
        x = x_ref[pl.ds(r, 16), :]
        ms = jnp.mean(x * x, axis=-1, keepdims=True)
        h_ref[pl.ds(r, 16), :] = (x * lax.rsqrt(ms + EPS) * g_ref[...]).astype(BF16)
        return carry

    lax.fori_loop(0, rows // 16, body, 0, unroll=8)


def _norm_matmul_kernel(x_ref, g_ref, w_ref, o_ref, h_ref, *, scale):
    @pl.when(pl.program_id(1) == 0)
    def _():
        _rmsnorm_rows(x_ref, g_ref, h_ref)

    acc = jnp.dot(h_ref[...], w_ref[...].astype(BF16), preferred_element_type=F32)
    if scale != 1.0:
        acc = acc * scale
    o_ref[...] = acc.astype(o_ref.dtype)


def _weight_spec(w, layer, tn):
    if w.ndim == 2:
        return pl.BlockSpec((w.shape[0], tn), lambda i, j: (0, j))
    return pl.BlockSpec((None, w.shape[1], tn), lambda i, j: (layer, 0, j))


def norm_matmul(x, g, w, *, out_dtype, tm, tn, scale=1.0, layer=None):
    m, k = x.shape
    n = w.shape[-1]
    return pl.pallas_call(
        functools.partial(_norm_matmul_kernel, scale=scale),
        grid=(m // tm, n // tn),
        in_specs=[
            pl.BlockSpec((tm, k), lambda i, j: (i, 0)),
            pl.BlockSpec((1, k), lambda i, j: (0, 0)),
            _weight_spec(w, layer, tn),
        ],
        out_specs=pl.BlockSpec((tm, tn), lambda i, j: (i, j)),
        out_shape=jax.ShapeDtypeStruct((m, n), out_dtype),
        scratch_shapes=[pltpu.VMEM((tm, k), BF16)],
        compiler_params=_cparams("parallel", "arbitrary"),
        name="norm_matmul",
    )(x, g.reshape(1, k), w)


def _inv_cols(inv_ref, n):
    inv = inv_ref[...]
    return jnp.concatenate([inv] * (n // LANES), axis=1)


def _next_norm_tile(o, g_ref, xg_ref, ss_ref):
    xg_ref[...] = (o * g_ref[...]).astype(xg_ref.dtype)
    ss_ref[...] = ss_ref[...] + jnp.sum(o * o, axis=-1, keepdims=True)


def _next_norm_finish(inv_ref, ss_ref, d):
    inv_ref[...] = lax.rsqrt(ss_ref[...] * (1.0 / d) + EPS)


def _prenorm_kernel(x_ref, g_ref, xg_ref, inv_ref):
    x = x_ref[...]
    xg_ref[...] = (x * g_ref[...]).astype(xg_ref.dtype)
    ms = jnp.mean(x * x, axis=-1, keepdims=True)
    inv_ref[...] = jnp.broadcast_to(lax.rsqrt(ms + EPS), inv_ref.shape)


def prenorm(x, g, *, tm):
    m, k = x.shape
    return pl.pallas_call(
        _prenorm_kernel,
        grid=(m // tm,),
        in_specs=[pl.BlockSpec((tm, k), lambda i: (i, 0)), pl.BlockSpec((1, k), lambda i: (0, 0))],
        out_specs=[pl.BlockSpec((tm, k), lambda i: (i, 0)), pl.BlockSpec((tm, LANES), lambda i: (i, 0))],
        out_shape=[jax.ShapeDtypeStruct((m, k), BF16), jax.ShapeDtypeStruct((m, LANES), F32)],
        compiler_params=_cparams("parallel"),
        name="prenorm",
    )(x, g.reshape(1, k))


def _scaled_matmul_kernel(xg_ref, inv_ref, w_ref, o_ref):
    acc = jnp.dot(xg_ref[...], w_ref[...].astype(BF16), preferred_element_type=F32)
    o_ref[...] = (acc * _inv_cols(inv_ref, acc.shape[1])).astype(o_ref.dtype)


def scaled_matmul(xg, inv, w, *, out_dtype, tm, tn, layer=None):
    m, k = xg.shape
    n = w.shape[-1]
    return pl.pallas_call(
        _scaled_matmul_kernel,
        grid=(m // tm, n // tn),
        in_specs=[
            pl.BlockSpec((tm, k), lambda i, j: (i, 0)),
            pl.BlockSpec((tm, LANES), lambda i, j: (i, 0)),
            _weight_spec(w, layer, tn),
        ],
        out_specs=pl.BlockSpec((tm, tn), lambda i, j: (i, j)),
        out_shape=jax.ShapeDtypeStruct((m, n), out_dtype),
        compiler_params=_cparams("parallel", "parallel"),
        name="scaled_matmul",
    )(xg, inv, w)


def _scaled_swiglu_kernel(xg_ref, inv_ref, wg_ref, wu_ref, o_ref):
    xg = xg_ref[...]
    inv = _inv_cols(inv_ref, o_ref.shape[1])
    a = jnp.dot(xg, wg_ref[...].astype(BF16), preferred_element_type=F32) * inv
    b = jnp.dot(xg, wu_ref[...].astype(BF16), preferred_element_type=F32) * inv
    o_ref[...] = (a * jax.nn.sigmoid(a) * b).astype(o_ref.dtype)


def scaled_swiglu(xg, inv, wg, wu, *, tm, tn, layer=None):
    m, k = xg.shape
    n = wg.shape[-1]
    return pl.pallas_call(
        _scaled_swiglu_kernel,
        grid=(m // tm, n // tn),
        in_specs=[
            pl.BlockSpec((tm, k), lambda i, j: (i, 0)),
            pl.BlockSpec((tm, LANES), lambda i, j: (i, 0)),
            _weight_spec(wg, layer, tn),
            _weight_spec(wu, layer, tn),
        ],
        out_specs=pl.BlockSpec((tm, tn), lambda i, j: (i, j)),
        out_shape=jax.ShapeDtypeStruct((m, n), BF16),
        compiler_params=_cparams("parallel", "parallel"),
        name="scaled_swiglu",
    )(xg, inv, wg, wu)


def _matmul_residual_kernel(a_ref, w_ref, r_ref, o_ref):
    o_ref[...] = r_ref[...] + jnp.dot(a_ref[...], w_ref[...].astype(BF16), preferred_element_type=F32)


def _matmul_residual_norm_kernel(a_ref, w_ref, r_ref, g_ref, o_ref, xg_ref, inv_ref, ss_ref, *, ncols):
    j = pl.program_id(1)

    @pl.when(j == 0)
    def _():
        ss_ref[...] = jnp.zeros_like(ss_ref)

    o = r_ref[...] + jnp.dot(a_ref[...], w_ref[...].astype(BF16), preferred_element_type=F32)
    o_ref[...] = o
    _next_norm_tile(o, g_ref, xg_ref, ss_ref)

    @pl.when(j == ncols - 1)
    def _():
        _next_norm_finish(inv_ref, ss_ref, ncols * o.shape[1])


def matmul_residual(a, w, res, *, tm, tn, layer=None, next_gain=None):
    m, k = a.shape
    n = w.shape[-1]
    tile = pl.BlockSpec((tm, tn), lambda i, j: (i, j))
    in_specs = [pl.BlockSpec((tm, k), lambda i, j: (i, 0)), _weight_spec(w, layer, tn), tile]
    if next_gain is None:
        return pl.pallas_call(
            _matmul_residual_kernel,
            grid=(m // tm, n // tn),
            in_specs=in_specs,
            out_specs=tile,
            out_shape=jax.ShapeDtypeStruct((m, n), F32),
            compiler_params=_cparams("parallel", "parallel"),
            name="matmul_residual",
        )(a, w, res)
    return pl.pallas_call(
        functools.partial(_matmul_residual_norm_kernel, ncols=n // tn),
        grid=(m // tm, n // tn),
        in_specs=in_specs + [pl.BlockSpec((1, tn), lambda i, j: (0, j))],
        out_specs=[tile, tile, pl.BlockSpec((tm, LANES), lambda i, j: (i, 0))],
        out_shape=[jax.ShapeDtypeStruct((m, n), F32), jax.ShapeDtypeStruct((m, n), BF16),
                   jax.ShapeDtypeStruct((m, LANES), F32)],
        scratch_shapes=[pltpu.VMEM((tm, LANES), F32)],
        compiler_params=_cparams("parallel", "arbitrary"),
        name="matmul_residual_norm",
    )(a, w, res, next_gain.reshape(1, n))


def _rmsnorm_kernel(x_ref, g_ref, o_ref):
    x = x_ref[...]
    ms = jnp.mean(x * x, axis=-1, keepdims=True)
    o_ref[...] = x * lax.rsqrt(ms + EPS) * g_ref[...]


def rmsnorm_rows(x, g, *, tm):
    m, k = x.shape
    return pl.pallas_call(
        _rmsnorm_kernel,
        grid=(m // tm,),
        in_specs=[pl.BlockSpec((tm, k), lambda i: (i, 0)), pl.BlockSpec((1, k), lambda i: (0, 0))],
        out_specs=pl.BlockSpec((tm, k), lambda i: (i, 0)),
        out_shape=jax.ShapeDtypeStruct((m, k), F32),
        compiler_params=_cparams("parallel"),
        name="final_rmsnorm",
    )(x, g.reshape(1, k))


def _t5_bucket_np(dist):
    max_exact = N_BUCKETS // 2
    n = np.maximum(dist, 1).astype(np.float64)
    large = max_exact + (np.log(n / max_exact) / math.log(MAX_DISTANCE / max_exact)
                         * (N_BUCKETS - max_exact)).astype(np.int32)
    large = np.minimum(large, N_BUCKETS - 1)
    return np.where(dist < max_exact, dist, large).astype(np.int32)


def _dilated_bucket_table():
    qi = np.arange(A_BLOCK)[:, None]
    kj = np.arange(2 * A_BLOCK)[None, :]
    steps_back = qi + A_BLOCK - kj
    valid = (steps_back >= 0) & (steps_back <= A_BLOCK)
    tabs = []
    for dil in A_DILATIONS:
        bucket = _t5_bucket_np(np.clip(steps_back, 0, A_BLOCK) * dil)
        tabs.append(np.where(valid, bucket, -1))
    return np.stack(tabs).astype(np.int32)


def _bias_table_kernel(idx_ref, rb_ref, o_ref):
    h = pl.program_id(1)
    idx = idx_ref[...]
    acc = jnp.full(idx.shape, NEG, F32)
    for b in range(N_BUCKETS):
        acc = jnp.where(idx == b, rb_ref[b, h] * LOG2E, acc)
    o_ref[...] = acc


def dilated_bias_table(rel_bias):
    idx = jnp.asarray(_dilated_bucket_table())
    npat = len(A_DILATIONS)
    return pl.pallas_call(
        _bias_table_kernel,
        grid=(npat, A_HEADS),
        in_specs=[
            pl.BlockSpec((None, A_BLOCK, 2 * A_BLOCK), lambda p, h: (p, 0, 0)),
            pl.BlockSpec(memory_space=pltpu.SMEM),
        ],
        out_specs=pl.BlockSpec((None, None, A_BLOCK, 2 * A_BLOCK), lambda p, h: (p, h, 0, 0)),
        out_shape=jax.ShapeDtypeStruct((npat, A_HEADS, A_BLOCK, 2 * A_BLOCK), F32),
        compiler_params=_cparams("parallel", "parallel"),
        name="dilated_bias_table",
    )(idx, rel_bias)


def _dilated_kernel(q_ref, kc_ref, kp_ref, vc_ref, vp_ref, bias_ref, o_ref, bm0, o_scr, lse_scr, *, scale):
    t = pl.program_id(2)
    tile = q_ref.shape[0]
    blk = A_BLOCK
    col = lax.broadcasted_iota(jnp.int32, (blk, 2 * blk), 1)
    no_prev = jnp.logical_and(col < blk, t == 0)
    for p in range(len(A_DILATIONS)):
        bm0[p] = jnp.where(no_prev, NEG, bias_ref[p])

    def rows(start, size, dil):
        return pl.ds(start, size) if dil == 1 else pl.ds(start, size, stride=dil)

    def block(p, dil, r, n):
        q_idx = rows(r + dil * blk * n, blk, dil)
        if n == 0:
            prev = rows(tile - dil * blk + r, blk, dil)
            cur = rows(r, blk, dil)
            k = jnp.concatenate([kp_ref[prev, :], kc_ref[cur, :]], axis=0)
            v = jnp.concatenate([vp_ref[prev, :], vc_ref[cur, :]], axis=0)
            bm = bm0[p]
        else:
            k_idx = rows(r + dil * blk * (n - 1), 2 * blk, dil)
            k, v = kc_ref[k_idx, :], vc_ref[k_idx, :]
            bm = bias_ref[p]
        q = (q_ref[q_idx, :] * scale).astype(BF16)
        s = lax.dot_general(q, k.astype(BF16), (((1,), (1,)), ((), ())), preferred_element_type=F32)
        s = jnp.where(bm > 0.5 * NEG, s + bm, NEG)
        m = jnp.max(s, axis=-1, keepdims=True)
        e = jnp.exp2(s - m)
        l = jnp.sum(e, axis=-1, keepdims=True)
        o = jnp.dot(e.astype(BF16), v.astype(BF16), preferred_element_type=F32) / l
        lse = m + jnp.log(l) * LOG2E
        o_scr[p, q_idx, :] = o
        lse_scr[p, q_idx, :] = jnp.broadcast_to(lse, (blk, HEAD_DIM))

    for p, dil in enumerate(A_DILATIONS):
        for r in range(dil):
            for n in range(tile // (blk * dil)):
                block(p, dil, r, n)

    def merge(i, carry):
        r = pl.multiple_of(i * blk, blk)
        rows = pl.ds(r, blk)
        l0, l1, l2 = lse_scr[0, rows, :], lse_scr[1, rows, :], lse_scr[2, rows, :]
        mx = jnp.maximum(jnp.maximum(l0, l1), l2)
        w0, w1, w2 = jnp.exp2(l0 - mx), jnp.exp2(l1 - mx), jnp.exp2(l2 - mx)
        num = w0 * o_scr[0, rows, :] + w1 * o_scr[1, rows, :] + w2 * o_scr[2, rows, :]
        o_ref[rows, :] = (num / (w0 + w1 + w2)).astype(o_ref.dtype)
        return carry

    lax.fori_loop(0, tile // blk, merge, 0)


def dilated_attention(z, bias_tab, *, batch, seq):
    z3 = z.reshape(batch, seq, z.shape[1])
    nt = seq // A_TILE
    hq, hk, hv = Z_AQ // HEAD_DIM, Z_AK // HEAD_DIM, Z_AV // HEAD_DIM
    tile_spec = lambda col0, prev: pl.BlockSpec(
        (None, A_TILE, HEAD_DIM),
        (lambda b, h, t: (b, jnp.maximum(t - 1, 0), col0 + h)) if prev else (lambda b, h, t: (b, t, col0 + h)))
    npat = len(A_DILATIONS)
    out = pl.pallas_call(
        functools.partial(_dilated_kernel, scale=HEAD_DIM ** -0.5 * LOG2E),
        grid=(batch, A_HEADS, nt),
        in_specs=[
            tile_spec(hq, False),
            tile_spec(hk, False), tile_spec(hk, True),
            tile_spec(hv, False), tile_spec(hv, True),
            pl.BlockSpec((npat, None, A_BLOCK, 2 * A_BLOCK), lambda b, h, t: (0, h, 0, 0)),
        ],
        out_specs=pl.BlockSpec((None, A_TILE, HEAD_DIM), lambda b, h, t: (b, t, h)),
        out_shape=jax.ShapeDtypeStruct((batch, seq, A_HEADS * HEAD_DIM), BF16),
        scratch_shapes=[
            pltpu.VMEM((npat, A_BLOCK, 2 * A_BLOCK), F32),
            pltpu.VMEM((npat, A_TILE, HEAD_DIM), F32),
            pltpu.VMEM((npat, A_TILE, HEAD_DIM), F32),
        ],
        compiler_params=_cparams("parallel", "parallel", "arbitrary"),
        name="dilated_attention",
    )(z3, z3, z3, z3, z3, bias_tab)
    return out.reshape(batch * seq, A_HEADS * HEAD_DIM)


def _gla_kernel(q_ref, k_ref, v_ref, r_ref, lr_ref, wa_ref, ba_ref, gn_ref, o_ref, st_ref, *, scale):
    @pl.when(pl.program_id(1) == 0)
    def _():
        st_ref[...] = jnp.zeros_like(st_ref)

    c = B_CHUNK
    tc = q_ref.shape[0]
    dk = B_DK // B_HEADS
    dv = B_DV // B_HEADS
    pre = jnp.dot(lr_ref[...], wa_ref[...], precision=lax.Precision.HIGHEST,
                  preferred_element_type=F32) + ba_ref[...]
    log_a = (jnp.minimum(pre, 0.0) - jnp.log(1.0 + jnp.exp(-jnp.abs(pre)))) * (1.0 / B_GATE_TAU)
    tril = lax.broadcasted_iota(jnp.int32, (c, c), 0) >= lax.broadcasted_iota(jnp.int32, (c, c), 1)
    ones_tril = jnp.where(tril, 1.0, 0.0).astype(F32)
    for ci in range(tc // c):
        sl = slice(ci * c, (ci + 1) * c)
        cum_all = jnp.dot(ones_tril, log_a[sl, :], precision=lax.Precision.HIGHEST, preferred_element_type=F32)
        for h in range(B_HEADS):
            kcols = slice(h * dk, (h + 1) * dk)
            vcols = slice(h * dv, (h + 1) * dv)
            cum = cum_all[:, kcols]
            cum_last = cum[c - 1:c, :]
            q = q_ref[sl, kcols] * scale
            k = k_ref[sl, kcols]
            v = v_ref[sl, vcols]
            q_dec = (q * jnp.exp(cum)).astype(BF16)
            k_inv = (k * jnp.exp(-cum)).astype(BF16)
            k_state = (k * jnp.exp(cum_last - cum)).astype(BF16)
            att = lax.dot_general(q_dec, k_inv, (((1,), (1,)), ((), ())), preferred_element_type=F32)
            att = jnp.where(tril, att, 0.0).astype(BF16)
            st = st_ref[h]
            o = (jnp.dot(att, v.astype(BF16), preferred_element_type=F32)
                 + lax.dot_general(q_dec, st.astype(BF16), (((1,), (1,)), ((), ())), preferred_element_type=F32))
            st_ref[h] = st * jnp.exp(cum_last) + jnp.dot(v.T.astype(BF16), k_state, preferred_element_type=F32)
            ms = jnp.mean(o * o, axis=-1, keepdims=True)
            o = o * lax.rsqrt(ms + EPS) * gn_ref[:, vcols]
            r = r_ref[sl, vcols]
            o_ref[sl, vcols] = (o * (r * jax.nn.sigmoid(r))).astype(o_ref.dtype)


def gated_linear_attention(z, w_alpha_p, b_alpha, gla_norm, *, batch, seq, tc):
    dk = B_DK // B_HEADS
    dv = B_DV // B_HEADS
    nt = seq // tc
    row = lambda b, t: b * nt + t
    const = lambda b, t: (0, 0)
    return pl.pallas_call(
        functools.partial(_gla_kernel, scale=dk ** -0.5),
        grid=(batch, nt),
        in_specs=[
            pl.BlockSpec((tc, B_DK), lambda b, t: (row(b, t), Z_BQ // B_DK)),
            pl.BlockSpec((tc, B_DK), lambda b, t: (row(b, t), Z_BK // B_DK)),
            pl.BlockSpec((tc, B_DV), lambda b, t: (row(b, t), Z_BV // B_DV)),
            pl.BlockSpec((tc, B_DV), lambda b, t: (row(b, t), Z_BR // B_DV)),
            pl.BlockSpec((tc, LANES), lambda b, t: (row(b, t), Z_LR // LANES)),
            pl.BlockSpec((LANES, B_DK), const),
            pl.BlockSpec((1, B_DK), const),
            pl.BlockSpec((1, B_DV), const),
        ],
        out_specs=pl.BlockSpec((tc, B_DV), lambda b, t: (row(b, t), 0)),
        out_shape=jax.ShapeDtypeStruct((batch * seq, B_DV), BF16),
        scratch_shapes=[pltpu.VMEM((B_HEADS, dv, dk), F32)],
        compiler_params=_cparams("parallel", "arbitrary"),
        name="gated_linear_attention",
    )(z, z, z, z, z, w_alpha_p, b_alpha.reshape(1, B_DK), gla_norm.reshape(1, B_DV))


C_QW = 2 * C_NOPE


def _rope_tables(seq):
    pos = jnp.arange(seq, dtype=F32)
    inv = ROPE_THETA ** (-jnp.arange(0, C_ROPE, 2, dtype=F32) / C_ROPE)
    ang = pos[:, None] * inv[None, :]
    cos, sin = jnp.cos(ang), jnp.sin(ang)
    half = C_ROPE // 2
    z = lambda w: jnp.zeros((seq, w), F32)
    return jnp.stack([
        jnp.concatenate([cos, cos, z(LANES - C_ROPE)], axis=1),
        jnp.concatenate([-sin, z(half), z(LANES - C_ROPE)], axis=1),
        jnp.concatenate([z(half), sin, z(LANES - C_ROPE)], axis=1)])


def _mla_proj_kernel(cq_ref, ckv_ref, ckr_ref, gq_ref, gkv_ref, wq_ref, wk_ref, wv_ref, rt_ref,
                     q_out, kn_out, kr_out, v_out, *, scale):
    half = C_ROPE // 2
    t0, t1, t2 = rt_ref[0], rt_ref[1], rt_ref[2]

    def rms(x, g):
        ms = jnp.mean(x * x, axis=-1, keepdims=True)
        return (x * lax.rsqrt(ms + EPS) * g).astype(BF16)

    def rotary(x):
        return x * t0 + pltpu.roll(x, LANES - half, axis=1) * t1 + pltpu.roll(x, half, axis=1) * t2

    cq = rms(cq_ref[...], gq_ref[...])
    q = jnp.dot(cq, wq_ref[...], preferred_element_type=F32)
    for h in range(C_HEADS):
        nope = slice(h * C_QW, h * C_QW + C_NOPE)
        rope = slice(h * C_QW + C_NOPE, (h + 1) * C_QW)
        q_out[:, nope] = (q[:, nope] * scale).astype(q_out.dtype)
        q_out[:, rope] = (rotary(q[:, rope]) * scale).astype(q_out.dtype)
    ckv = rms(ckv_ref[...], gkv_ref[...])
    kn_out[...] = jnp.dot(ckv, wk_ref[...], preferred_element_type=F32).astype(kn_out.dtype)
    v_out[...] = jnp.dot(ckv, wv_ref[...], preferred_element_type=F32).astype(v_out.dtype)
    kr_out[...] = rotary(ckr_ref[...]).astype(kr_out.dtype)


def mla_project(z, q_a_norm, kv_a_norm, wq_p, wk_p, wv_p, rope_tab, *, seq, tm):
    m = z.shape[0]
    nt = seq // tm
    const = lambda i: (0, 0)
    return pl.pallas_call(
        functools.partial(_mla_proj_kernel, scale=(C_NOPE + C_ROPE) ** -0.5 * math.log2(math.e)),
        grid=(m // tm,),
        in_specs=[
            pl.BlockSpec((tm, C_Q_RANK), lambda i: (i, Z_CQA // C_Q_RANK)),
            pl.BlockSpec((tm, C_KV_RANK), lambda i: (i, Z_CKVA // C_KV_RANK)),
            pl.BlockSpec((tm, LANES), lambda i: (i, Z_CKR // LANES)),
            pl.BlockSpec((1, C_Q_RANK), const),
            pl.BlockSpec((1, C_KV_RANK), const),
            pl.BlockSpec((C_Q_RANK, C_HEADS * C_QW), const),
            pl.BlockSpec((C_KV_RANK, C_HEADS * C_NOPE), const),
            pl.BlockSpec((C_KV_RANK, C_HEADS * C_V), const),
            pl.BlockSpec((3, tm, LANES), lambda i: (0, i % nt, 0)),
        ],
        out_specs=[
            pl.BlockSpec((tm, C_HEADS * C_QW), lambda i: (i, 0)),
            pl.BlockSpec((tm, C_HEADS * C_NOPE), lambda i: (i, 0)),
            pl.BlockSpec((tm, LANES), lambda i: (i, 0)),
            pl.BlockSpec((tm, C_HEADS * C_V), lambda i: (i, 0)),
        ],
        out_shape=[
            jax.ShapeDtypeStruct((m, C_HEADS * C_QW), BF16),
            jax.ShapeDtypeStruct((m, C_HEADS * C_NOPE), BF16),
            jax.ShapeDtypeStruct((m, LANES), BF16),
            jax.ShapeDtypeStruct((m, C_HEADS * C_V), BF16),
        ],
        compiler_params=_cparams("parallel"),
        name="mla_project",
    )(z, z, z, q_a_norm.reshape(1, -1), kv_a_norm.reshape(1, -1), wq_p, wk_p, wv_p, rope_tab)


def _mla_flash_kernel(q_ref, kn_ref, kr_ref, v_ref, o_ref, m_ref, l_ref, acc_ref):
    tq = m_ref.shape[0]
    tk = tq // 2

    def q_tile(qi, carry):
        q0 = pl.multiple_of(qi * tq, tq)
        m_ref[...] = jnp.full_like(m_ref, NEG)
        l_ref[...] = jnp.zeros_like(l_ref)
        acc_ref[...] = jnp.zeros_like(acc_ref)

        def step(r0, nr, k_start, nk, mask_offset=None):
            rows = slice(r0, r0 + nr)
            ks = pl.ds(pl.multiple_of(k_start, tk), nk)
            k = jnp.concatenate([kn_ref[ks, :], kr_ref[ks, :]], axis=1)
            q = q_ref[pl.ds(pl.multiple_of(q0 + r0, tk), nr), :]
            s = lax.dot_general(q, k, (((1,), (1,)), ((), ())), preferred_element_type=F32)
            if mask_offset is not None:
                visible = (lax.broadcasted_iota(jnp.int32, s.shape, 1)
                           <= lax.broadcasted_iota(jnp.int32, s.shape, 0) + mask_offset)
                s = jnp.where(visible, s, NEG)
            m_prev = m_ref[rows, :]
            m_new = jnp.maximum(m_prev, jnp.max(s, axis=-1, keepdims=True))
            alpha = jnp.exp2(m_prev - m_new)
            p = jnp.exp2(s - jnp.concatenate([m_new] * (nk // LANES), axis=1))
            l_ref[rows, :] = alpha * l_ref[rows, :] + jnp.sum(p, axis=-1, keepdims=True)
            acc_ref[rows, :] = alpha * acc_ref[rows, :] + jnp.dot(p.astype(BF16), v_ref[ks, :],
                                                                  preferred_element_type=F32)
            m_ref[rows, :] = m_new

        def body(kb, c):
            step(0, tq, 2 * kb * tk, tk)
            step(0, tq, (2 * kb + 1) * tk, tk)
            return c

        lax.fori_loop(0, qi, body, 0)
        step(0, tk, q0, tk, 0)
        step(tk, tk, q0, tq, tk)
        o_ref[pl.ds(q0, tq), :] = (acc_ref[...] / l_ref[...]).astype(o_ref.dtype)
        return carry

    lax.fori_loop(0, q_ref.shape[0] // tq, q_tile, 0)


def mla_flash(q, kn, kr, v, *, batch, seq, tq):
    return pl.pallas_call(
        _mla_flash_kernel,
        grid=(batch, C_HEADS),
        in_specs=[
            pl.BlockSpec((seq, C_QW), lambda b, h: (b, h)),
            pl.BlockSpec((seq, C_NOPE), lambda b, h: (b, h)),
            pl.BlockSpec((seq, LANES), lambda b, h: (b, 0)),
            pl.BlockSpec((seq, C_V), lambda b, h: (b, h)),
        ],
        out_specs=pl.BlockSpec((seq, C_V), lambda b, h: (b, h)),
        out_shape=jax.ShapeDtypeStruct((batch * seq, C_HEADS * C_V), BF16),
        scratch_shapes=[
            pltpu.VMEM((tq, LANES), F32),
            pltpu.VMEM((tq, LANES), F32),
            pltpu.VMEM((tq, C_V), F32),
        ],
        compiler_params=_cparams("parallel", "parallel"),
        name="mla_flash",
    )(q, kn, kr, v)


def _mixer_out_kernel(oa_ref, ob_ref, oc_ref, wa_ref, wb_ref, wc_ref, g0_ref, g1_ref, g2_ref,
                      b0_ref, b1_ref, b2_ref, wo_ref, xr_ref, gn_ref, o_ref, xg_ref, inv_ref, mg_ref, ss_ref,
                      *, n_rows):
    i = pl.program_id(0)
    j = pl.program_id(1)
    _, nj, _, tn = mg_ref.shape
    cur = lax.rem(i, 2)

    def project():
        acc = xr_ref[...]
        for c in range(nj):
            acc = acc + jnp.dot(mg_ref[1 - cur, c], wo_ref[c * tn:(c + 1) * tn, :], preferred_element_type=F32)
        o_ref[...] = acc
        _next_norm_tile(acc, gn_ref, xg_ref, ss_ref)

    def branch(o, w, g, b):
        return jax.nn.sigmoid(g[...] + b[...]) * jnp.dot(o[...], w[...], preferred_element_type=F32)

    def merge():
        acc = branch(oa_ref, wa_ref, g0_ref, b0_ref)
        acc = acc + branch(ob_ref, wb_ref, g1_ref, b1_ref)
        acc = acc + branch(oc_ref, wc_ref, g2_ref, b2_ref)
        mg_ref[cur, j] = acc.astype(mg_ref.dtype)

    @pl.when(j == 0)
    def _():
        ss_ref[...] = jnp.zeros_like(ss_ref)

    @pl.when(jnp.logical_and(i > 0, i < n_rows))
    def _():
        project()
        merge()

    @pl.when(i == 0)
    def _():
        merge()

    @pl.when(i == n_rows)
    def _():
        project()

    @pl.when(jnp.logical_and(i > 0, j == nj - 1))
    def _():
        _next_norm_finish(inv_ref, ss_ref, nj * tn)


def _pipelined_row_maps(n_rows, ncols):
    row1 = lambda i: jnp.minimum(i, n_rows - 1)
    col1 = lambda i, j: jnp.where(i < n_rows, j, ncols - 1)
    row2 = lambda i: jnp.maximum(i - 1, 0)
    col2 = lambda i, j: jnp.where(i > 0, j, 0)
    return row1, col1, row2, col2


def mixer_output_block(x, o_a, o_b, o_c, w_a, w_b, w_c, gates, b_gate, w_o, next_gain, *, tm, tn):
    m, k = o_a.shape
    d = w_a.shape[1]
    nj = d // tn
    n_rows = m // tm
    mrow, mcol, orow, ocol = _pipelined_row_maps(n_rows, nj)
    act = pl.BlockSpec((tm, k), lambda i, j: (mrow(i), 0))
    wgt = pl.BlockSpec((k, tn), lambda i, j: (0, mcol(i, j)))
    gate = lambda br: pl.BlockSpec((tm, tn), lambda i, j: (mrow(i), br * nj + mcol(i, j)))
    gbias = lambda br: pl.BlockSpec((1, tn), lambda i, j: (0, br * nj + j))
    out_tile = pl.BlockSpec((tm, tn), lambda i, j: (orow(i), ocol(i, j)))
    return pl.pallas_call(
        functools.partial(_mixer_out_kernel, n_rows=n_rows),
        grid=(n_rows + 1, nj),
        in_specs=[act, act, act, wgt, wgt, wgt, gate(0), gate(1), gate(2), gbias(0), gbias(1), gbias(2),
                  pl.BlockSpec((d, tn), lambda i, j: (0, ocol(i, j))),
                  out_tile,
                  pl.BlockSpec((1, tn), lambda i, j: (0, ocol(i, j)))],
        out_specs=[out_tile, out_tile, pl.BlockSpec((tm, LANES), lambda i, j: (orow(i), 0))],
        out_shape=[jax.ShapeDtypeStruct((m, d), F32), jax.ShapeDtypeStruct((m, d), BF16),
                   jax.ShapeDtypeStruct((m, LANES), F32)],
        scratch_shapes=[pltpu.VMEM((2, nj, tm, tn), BF16), pltpu.VMEM((tm, LANES), F32)],
        compiler_params=_cparams("arbitrary", "arbitrary"),
        name="mixer_output_block",
    )(o_a, o_b, o_c, w_a, w_b, w_c, gates, gates, gates, *([b_gate.reshape(1, -1)] * 3), w_o, x,
      next_gain.reshape(1, d))


def _cross_block_kernel(xg_ref, inv_ref, wq_ref, k_ref, v_ref, wo_ref, xr_ref, gn_ref,
                        o_ref, xgo_ref, invo_ref, ao_ref, ss_ref, *, scale, n_rows):
    i = pl.program_id(0)
    j = pl.program_id(1)
    hd = k_ref.shape[1]
    cur = lax.rem(i, 2)

    def project():
        acc = xr_ref[...]
        for h in range(X_HEADS):
            acc = acc + jnp.dot(ao_ref[1 - cur, h], wo_ref[h * hd:(h + 1) * hd, :], preferred_element_type=F32)
        o_ref[...] = acc
        _next_norm_tile(acc, gn_ref, xgo_ref, ss_ref)

    def attend():
        q = jnp.dot(xg_ref[...], wq_ref[...], preferred_element_type=F32) * (_inv_cols(inv_ref, hd) * scale)
        s = lax.dot_general(q.astype(BF16), k_ref[...], (((1,), (1,)), ((), ())), preferred_element_type=F32)
        m = jnp.max(s, axis=-1, keepdims=True)
        e = jnp.exp(s - m)
        l = jnp.sum(e, axis=-1, keepdims=True)
        o = jnp.dot(e.astype(BF16), v_ref[...], preferred_element_type=F32) / l
        ao_ref[cur, j] = o.astype(ao_ref.dtype)

    @pl.when(j == 0)
    def _():
        ss_ref[...] = jnp.zeros_like(ss_ref)

    @pl.when(jnp.logical_and(i > 0, i < n_rows))
    def _():
        project()
        attend()

    @pl.when(i == 0)
    def _():
        attend()

    @pl.when(i == n_rows)
    def _():
        project()

    @pl.when(jnp.logical_and(i > 0, j == X_HEADS - 1))
    def _():
        _next_norm_finish(invo_ref, ss_ref, X_HEADS * hd)


def cross_attention_block(x, xg, inv, w_q, kv, w_o, next_gain, *, batch, seq, mem_len, tm):
    m, d = x.shape
    hd = d // X_HEADS
    tiles_per_batch = seq // tm
    n_rows = m // tm
    arow, head, orow, ocol = _pipelined_row_maps(n_rows, X_HEADS)
    out_tile = pl.BlockSpec((tm, hd), lambda i, j: (orow(i), ocol(i, j)))
    return pl.pallas_call(
        functools.partial(_cross_block_kernel, scale=hd ** -0.5, n_rows=n_rows),
        grid=(n_rows + 1, X_HEADS),
        in_specs=[
            pl.BlockSpec((tm, d), lambda i, j: (arow(i), 0)),
            pl.BlockSpec((tm, LANES), lambda i, j: (arow(i), 0)),
            pl.BlockSpec((d, hd), lambda i, j: (0, head(i, j))),
            pl.BlockSpec((mem_len, hd), lambda i, j: (arow(i) // tiles_per_batch, head(i, j))),
            pl.BlockSpec((mem_len, hd), lambda i, j: (arow(i) // tiles_per_batch, X_HEADS + head(i, j))),
            pl.BlockSpec((d, hd), lambda i, j: (0, ocol(i, j))),
            out_tile,
            pl.BlockSpec((1, hd), lambda i, j: (0, ocol(i, j))),
        ],
        out_specs=[out_tile, out_tile, pl.BlockSpec((tm, LANES), lambda i, j: (orow(i), 0))],
        out_shape=[jax.ShapeDtypeStruct((m, d), F32), jax.ShapeDtypeStruct((m, d), BF16),
                   jax.ShapeDtypeStruct((m, LANES), F32)],
        scratch_shapes=[pltpu.VMEM((2, X_HEADS, tm, hd), BF16), pltpu.VMEM((tm, LANES), F32)],
        compiler_params=_cparams("arbitrary", "arbitrary"),
        name="cross_attention_block",
    )(xg, inv, w_q, kv, kv, w_o, x, next_gain.reshape(1, d))


def _pack_w_in(w):
    nl, d, _ = w.shape
    sizes = (1024, 1024, 1024, B_DK, B_DK, B_DV, B_GATE_RANK, B_DV, C_Q_RANK, C_KV_RANK, C_ROPE, N_BRANCH * d)
    offs = np.concatenate([[0], np.cumsum(sizes)])
    aq, ak, av, bq, bk, bv, lr, br, cqa, ckva, ckr, gates = [
        w[:, :, offs[i]:offs[i + 1]].astype(BF16) for i in range(len(sizes))]
    zeros = lambda n: jnp.zeros((nl, d, n), BF16)
    used = Z_CKR + LANES
    wz = jnp.concatenate(
        [aq, ak, av, bv, br, cqa, ckva, bq, bk,
         lr, zeros(LANES - B_GATE_RANK), ckr, zeros(LANES - C_ROPE), zeros(Z_WIDTH - used)], axis=2)
    return wz, gates


def _pack_w_qb(w):
    r = w.shape[0]
    w = w.reshape(r, C_HEADS, C_NOPE + C_ROPE)
    w = jnp.pad(w, ((0, 0), (0, 0), (0, C_QW - C_NOPE - C_ROPE)))
    return w.reshape(r, C_HEADS * C_QW).astype(BF16)


def _pack_w_kvb(w):
    r = w.shape[0]
    w = w.reshape(r, C_HEADS, C_NOPE + C_V)
    wk = w[:, :, :C_NOPE].reshape(r, C_HEADS * C_NOPE)
    wv = w[:, :, C_NOPE:].reshape(r, C_HEADS * C_V)
    return wk.astype(BF16), wv.astype(BF16)


def kernel(x, mem, rel_bias, norm_mix, w_in, b_gate, w_alpha, b_alpha, gla_norm, q_a_norm, w_qb, kv_a_norm, w_kvb, w_up_a, w_up_b, w_up_c, w_o, norm_x, norm_mem, w_xq, w_xkv, w_xo, norm_ffn, w_ffn_gate, w_ffn_up, w_ffn_down, norm_final):
    batch, seq, d = x.shape
    mem_len = mem.shape[1]
    depth = w_in.shape[0]
    t = batch * seq
    assert d == 2048 and seq % A_TILE == 0, "tiling is derived for the stated shapes"

    xf = x.reshape(t, d)
    memf = mem.reshape(batch * mem_len, d)
    bias_tab = dilated_bias_table(rel_bias)
    rope_tab = _rope_tables(seq)
    bf = lambda a: a.astype(BF16)
    w_in_z, w_in_g = _pack_w_in(w_in)

    xg, inv = prenorm(xf, norm_mix[0], tm=1024)
    for l in range(depth):
        z = scaled_matmul(xg, inv, w_in_z, out_dtype=F32, tm=1024, tn=W_Z_TN, layer=l)
        gates = scaled_matmul(xg, inv, w_in_g, out_dtype=BF16, tm=1024, tn=W_G_TN, layer=l)
        o_a = dilated_attention(z, bias_tab, batch=batch, seq=seq)
        w_alpha_p = jnp.pad(w_alpha[l], ((0, LANES - B_GATE_RANK), (0, 0)))
        o_b = gated_linear_attention(z, w_alpha_p, b_alpha[l], gla_norm[l], batch=batch, seq=seq, tc=512)
        wk_p, wv_p = _pack_w_kvb(w_kvb[l])
        cq, ckn, ckr, cv = mla_project(z, q_a_norm[l], kv_a_norm[l], _pack_w_qb(w_qb[l]), wk_p, wv_p,
                                       rope_tab, seq=seq, tm=512)
        o_c = mla_flash(cq, ckn, ckr, cv, batch=batch, seq=seq, tq=1024)
        xf, xg, inv = mixer_output_block(xf, o_a, o_b, o_c, bf(w_up_a[l]), bf(w_up_b[l]), bf(w_up_c[l]), gates,
                                         b_gate[l], bf(w_o[l]), norm_x[l], tm=1024, tn=512)
        xkv = norm_matmul(memf, norm_mem[l], w_xkv, out_dtype=BF16, tm=batch * mem_len, tn=512, layer=l)
        xf, xg, inv = cross_attention_block(xf, xg, inv, bf(w_xq[l]), xkv, bf(w_xo[l]), norm_ffn[l],
                                            batch=batch, seq=seq, mem_len=mem_len, tm=1024)
        act = scaled_swiglu(xg, inv, w_ffn_gate, w_ffn_up, tm=SWIGLU_TM, tn=512, layer=l)
        if l + 1 < depth:
            xf, xg, inv = matmul_residual(act, bf(w_ffn_down[l]), xf, tm=1024, tn=512, next_gain=norm_mix[l + 1])
        else:
            xf = matmul_residual(act, bf(w_ffn_down[l]), xf, tm=1024, tn=512)
    return rmsnorm_rows(xf, norm_final, tm=512).reshape(batch, seq, d)
```

```python
import functools
import math

import numpy as np
import jax
import jax.numpy as jnp
from jax import lax
from jax.experimental import pallas as pl
from jax.experimental.pallas import tpu as pltpu

F32 = jnp.float32
BF16 = jnp.bfloat16
EPS = 1e-6
NEG = -1e30
LOG2E = math.log2(math.e)

HEAD_DIM = 128
A_HEADS = 8
A_DILATIONS = (1, 4, 16)
A_BLOCK = 128
N_BUCKETS = 32
MAX_DISTANCE = 2048
B_HEADS = 4
B_DK = 512
B_DV = 1024
B_GATE_RANK = 16
B_GATE_TAU = 16.0
B_CHUNK = 64
C_HEADS = 8
C_Q_RANK = 512
C_KV_RANK = 512
C_NOPE = 128
C_ROPE = 64
C_V = 128
ROPE_THETA = 10000.0
X_HEADS = 4
N_BRANCH = 3

VMEM_LIMIT_BYTES = 56 * 1024 * 1024
LANES = 128

Z_AQ, Z_AK, Z_AV = 0, 1024, 2048
Z_BV, Z_BR = 3072, 4096
Z_CQA, Z_CKVA = 5120, 5632
Z_BQ, Z_BK = 6144, 6656
Z_LR = 7168
Z_CKR = 7296
Z_WIDTH = 7680
G_WIDTH = 3 * 2048
A_TILE = 2048

TM = 1024
W_Z_TN = 1536
W_G_TN = 2048
TN = 512
SWIGLU_TM = 2048
GLA_TC = 512
FLASH_TQ = 1024


def _cparams(*sem):
    return pltpu.CompilerParams(dimension_semantics=sem, vmem_limit_bytes=VMEM_LIMIT_BYTES)


def _rmsnorm_rows(x_ref, g_ref, h_ref):
    rows = x_ref.shape[0]

    def body(i, carry):
        r = pl.multiple_of(i * 16, 16)
        x = x_ref[pl.ds(r, 16), :]
        ms = jnp.mean(x * x, axis=-1, keepdims=True)
        h_ref[pl.ds(r, 16), :] = (x * lax.rsqrt(ms + EPS) * g_ref[...]).astype(BF16)
        return carry

    lax.fori_loop(0, rows // 16, body, 0, unroll=8)


def _norm_matmul_kernel(x_ref, g_ref, w_ref, o_ref, h_ref, *, scale):
    @pl.when(pl.program_id(1) == 0)
    def _():
        _rmsnorm_rows(x_ref, g_ref, h_ref)

    acc = jnp.dot(h_ref[...], w_ref[...].astype(BF16), preferred_element_type=F32)
    if scale != 1.0:
        acc = acc * scale
    o_ref[...] = acc.astype(o_ref.dtype)


def _weight_spec(w, layer, tn):
    if w.ndim == 2:
        return pl.BlockSpec((w.shape[0], tn), lambda i, j: (0, j))
    return pl.BlockSpec((None, w.shape[1], tn), lambda i, j: (layer, 0, j))


def norm_matmul(x, g, w, *, out_dtype, tm, tn, scale=1.0, layer=None):
    m, k = x.shape
    n = w.shape[-1]
    return pl.pallas_call(
        functools.partial(_norm_matmul_kernel, scale=scale),
        grid=(m // tm, n // tn),
        in_specs=[
            pl.BlockSpec((tm, k), lambda i, j: (i, 0)),
            pl.BlockSpec((1, k), lambda i, j: (0, 0)),
            _weight_spec(w, layer, tn),
        ],
        out_specs=pl.BlockSpec((tm, tn), lambda i, j: (i, j)),
        out_shape=jax.ShapeDtypeStruct((m, n), out_dtype),
        scratch_shapes=[pltpu.VMEM((tm, k), BF16)],
        compiler_params=_cparams("parallel", "arbitrary"),
        name="norm_matmul",
    )(x, g.reshape(1, k), w)


def _inv_cols(inv_ref, n):
    inv = inv_ref[...]
    return jnp.concatenate([inv] * (n // LANES), axis=1)


def _next_norm_tile(o, g_ref, xg_ref, ss_ref):
    xg_ref[...] = (o * g_ref[...]).astype(xg_ref.dtype)
    ss_ref[...] = ss_ref[...] + jnp.sum(o * o, axis=-1, keepdims=True)


def _next_norm_finish(inv_ref, ss_ref, d):
    inv_ref[...] = lax.rsqrt(ss_ref[...] * (1.0 / d) + EPS)


def _prenorm_kernel(x_ref, g_ref, xg_ref, inv_ref):
    x = x_ref[...]
    xg_ref[...] = (x * g_ref[...]).astype(xg_ref.dtype)
    ms = jnp.mean(x * x, axis=-1, keepdims=True)
    inv_ref[...] = jnp.broadcast_to(lax.rsqrt(ms + EPS), inv_ref.shape)


def prenorm(x, g, *, tm):
    m, k = x.shape
    return pl.pallas_call(
        _prenorm_kernel,
        grid=(m // tm,),
        in_specs=[pl.BlockSpec((tm, k), lambda i: (i, 0)), pl.BlockSpec((1, k), lambda i: (0, 0))],
        out_specs=[pl.BlockSpec((tm, k), lambda i: (i, 0)), pl.BlockSpec((tm, LANES), lambda i: (i, 0))],
        out_shape=[jax.ShapeDtypeStruct((m, k), BF16), jax.ShapeDtypeStruct((m, LANES), F32)],
        compiler_params=_cparams("parallel"),
        name="prenorm",
    )(x, g.reshape(1, k))


def _scaled_matmul_kernel(xg_ref, inv_ref, w_ref, o_ref):
    acc = jnp.dot(xg_ref[...], w_ref[...].astype(BF16), preferred_element_type=F32)
    o_ref[...] = (acc * _inv_cols(inv_ref, acc.shape[1])).astype(o_ref.dtype)


def scaled_matmul(xg, inv, w, *, out_dtype, tm, tn, layer=None):
    m, k = xg.shape
    n = w.shape[-1]
    return pl.pallas_call(
        _scaled_matmul_kernel,
        grid=(m // tm, n // tn),
        in_specs=[
            pl.BlockSpec((tm, k), lambda i, j: (i, 0)),
            pl.BlockSpec((tm, LANES), lambda i, j: (i, 0)),
            _weight_spec(w, layer, tn),
        ],
        out_specs=pl.BlockSpec((tm, tn), lambda i, j: (i, j)),
        out_shape=jax.ShapeDtypeStruct((m, n), out_dtype),
        compiler_params=_cparams("parallel", "parallel"),
        name="scaled_matmul",
    )(xg, inv, w)


def _scaled_swiglu_kernel(xg_ref, inv_ref, wg_ref, wu_ref, o_ref):
    xg = xg_ref[...]
    inv = _inv_cols(inv_ref, o_ref.shape[1])
    a = jnp.dot(xg, wg_ref[...].astype(BF16), preferred_element_type=F32) * inv
    b = jnp.dot(xg, wu_ref[...].astype(BF16), preferred_element_type=F32) * inv
    o_ref[...] = (a * jax.nn.sigmoid(a) * b).astype(o_ref.dtype)


def scaled_swiglu(xg, inv, wg, wu, *, tm, tn, layer=None):
    m, k = xg.shape
    n = wg.shape[-1]
    return pl.pallas_call(
        _scaled_swiglu_kernel,
        grid=(m // tm, n // tn),
        in_specs=[
            pl.BlockSpec((tm, k), lambda i, j: (i, 0)),
            pl.BlockSpec((tm, LANES), lambda i, j: (i, 0)),
            _weight_spec(wg, layer, tn),
            _weight_spec(wu, layer, tn),
        ],
        out_specs=pl.BlockSpec((tm, tn), lambda i, j: (i, j)),
        out_shape=jax.ShapeDtypeStruct((m, n), BF16),
        compiler_params=_cparams("parallel", "parallel"),
        name="scaled_swiglu",
    )(xg, inv, wg, wu)


def _matmul_residual_kernel(a_ref, w_ref, r_ref, o_ref):
    o_ref[...] = r_ref[...] + jnp.dot(a_ref[...], w_ref[...].astype(BF16), preferred_element_type=F32)


def _matmul_residual_norm_kernel(a_ref, w_ref, r_ref, g_ref, o_ref, xg_ref, inv_ref, ss_ref, *, ncols):
    j = pl.program_id(1)

    @pl.when(j == 0)
    def _():
        ss_ref[...] = jnp.zeros_like(ss_ref)

    o = r_ref[...] + jnp.dot(a_ref[...], w_ref[...].astype(BF16), preferred_element_type=F32)
    o_ref[...] = o
    _next_norm_tile(o, g_ref, xg_ref, ss_ref)

    @pl.when(j == ncols - 1)
    def _():
        _next_norm_finish(inv_ref, ss_ref, ncols * o.shape[1])


def matmul_residual(a, w, res, *, tm, tn, layer=None, next_gain=None):
    m, k = a.shape
    n = w.shape[-1]
    tile = pl.BlockSpec((tm, tn), lambda i, j: (i, j))
    in_specs = [pl.BlockSpec((tm, k), lambda i, j: (i, 0)), _weight_spec(w, layer, tn), tile]
    if next_gain is None:
        return pl.pallas_call(
            _matmul_residual_kernel,
            grid=(m // tm, n // tn),
            in_specs=in_specs,
            out_specs=tile,
            out_shape=jax.ShapeDtypeStruct((m, n), F32),
            compiler_params=_cparams("parallel", "parallel"),
            name="matmul_residual",
        )(a, w, res)
    return pl.pallas_call(
        functools.partial(_matmul_residual_norm_kernel, ncols=n // tn),
        grid=(m // tm, n // tn),
        in_specs=in_specs + [pl.BlockSpec((1, tn), lambda i, j: (0, j))],
        out_specs=[tile, tile, pl.BlockSpec((tm, LANES), lambda i, j: (i, 0))],
        out_shape=[jax.ShapeDtypeStruct((m, n), F32), jax.ShapeDtypeStruct((m, n), BF16),
                   jax.ShapeDtypeStruct((m, LANES), F32)],
        scratch_shapes=[pltpu.VMEM((tm, LANES), F32)],
        compiler_params=_cparams("parallel", "arbitrary"),
        name="matmul_residual_norm",
    )(a, w, res, next_gain.reshape(1, n))


def _rmsnorm_kernel(x_ref, g_ref, o_ref):
    x = x_ref[...]
    ms = jnp.mean(x * x, axis=-1, keepdims=True)
    o_ref[...] = x * lax.rsqrt(ms + EPS) * g_ref[...]


def rmsnorm_rows(x, g, *, tm):
    m, k = x.shape
    return pl.pallas_call(
        _rmsnorm_kernel,
        grid=(m // tm,),
        in_specs=[pl.BlockSpec((tm, k), lambda i: (i, 0)), pl.BlockSpec((1, k), lambda i: (0, 0))],
        out_specs=pl.BlockSpec((tm, k), lambda i: (i, 0)),
        out_shape=jax.ShapeDtypeStruct((m, k), F32),
        compiler_params=_cparams("parallel"),
        name="final_rmsnorm",
    )(x, g.reshape(1, k))


def _t5_bucket_np(dist):
    max_exact = N_BUCKETS // 2
    n = np.maximum(dist, 1).astype(np.float64)
    large = max_exact + (np.log(n / max_exact) / math.log(MAX_DISTANCE / max_exact)
                         * (N_BUCKETS - max_exact)).astype(np.int32)
    large = np.minimum(large, N_BUCKETS - 1)
    return np.where(dist < max_exact, dist, large).astype(np.int32)


def _dilated_bucket_table():
    qi = np.arange(A_BLOCK)[:, None]
    kj = np.arange(2 * A_BLOCK)[None, :]
    steps_back = qi + A_BLOCK - kj
    valid = (steps_back >= 0) & (steps_back <= A_BLOCK)
    tabs = []
    for dil in A_DILATIONS:
        bucket = _t5_bucket_np(np.clip(steps_back, 0, A_BLOCK) * dil)
        tabs.append(np.where(valid, bucket, -1))
    return np.stack(tabs).astype(np.int32)


def _bias_table_kernel(idx_ref, rb_ref, o_ref):
    h = pl.program_id(1)
    idx = idx_ref[...]
    acc = jnp.full(idx.shape, NEG, F32)
    for b in range(N_BUCKETS):
        acc = jnp.where(idx == b, rb_ref[b, h] * LOG2E, acc)
    o_ref[...] = acc


def dilated_bias_table(rel_bias):
    idx = jnp.asarray(_dilated_bucket_table())
    npat = len(A_DILATIONS)
    return pl.pallas_call(
        _bias_table_kernel,
        grid=(npat, A_HEADS),
        in_specs=[
            pl.BlockSpec((None, A_BLOCK, 2 * A_BLOCK), lambda p, h: (p, 0, 0)),
            pl.BlockSpec(memory_space=pltpu.SMEM),
        ],
        out_specs=pl.BlockSpec((None, None, A_BLOCK, 2 * A_BLOCK), lambda p, h: (p, h, 0, 0)),
        out_shape=jax.ShapeDtypeStruct((npat, A_HEADS, A_BLOCK, 2 * A_BLOCK), F32),
        compiler_params=_cparams("parallel", "parallel"),
        name="dilated_bias_table",
    )(idx, rel_bias)


def _dilated_kernel(q_ref, kc_ref, kp_ref, vc_ref, vp_ref, bias_ref, o_ref, bm0, o_scr, lse_scr, *, scale):
    t = pl.program_id(2)
    tile = q_ref.shape[0]
    blk = A_BLOCK
    col = lax.broadcasted_iota(jnp.int32, (blk, 2 * blk), 1)
    no_prev = jnp.logical_and(col < blk, t == 0)
    for p in range(len(A_DILATIONS)):
        bm0[p] = jnp.where(no_prev, NEG, bias_ref[p])

    def rows(start, size, dil):
        return pl.ds(start, size) if dil == 1 else pl.ds(start, size, stride=dil)

    def block(p, dil, r, n):
        q_idx = rows(r + dil * blk * n, blk, dil)
        if n == 0:
            prev = rows(tile - dil * blk + r, blk, dil)
            cur = rows(r, blk, dil)
            k = jnp.concatenate([kp_ref[prev, :], kc_ref[cur, :]], axis=0)
            v = jnp.concatenate([vp_ref[prev, :], vc_ref[cur, :]], axis=0)
            bm = bm0[p]
        else:
            k_idx = rows(r + dil * blk * (n - 1), 2 * blk, dil)
            k, v = kc_ref[k_idx, :], vc_ref[k_idx, :]
            bm = bias_ref[p]
        q = (q_ref[q_idx, :] * scale).astype(BF16)
        s = lax.dot_general(q, k.astype(BF16), (((1,), (1,)), ((), ())), preferred_element_type=F32)
        s = jnp.where(bm > 0.5 * NEG, s + bm, NEG)
        m = jnp.max(s, axis=-1, keepdims=True)
        e = jnp.exp2(s - m)
        l = jnp.sum(e, axis=-1, keepdims=True)
        o = jnp.dot(e.astype(BF16), v.astype(BF16), preferred_element_type=F32) / l
        lse = m + jnp.log(l) * LOG2E
        o_scr[p, q_idx, :] = o
        lse_scr[p, q_idx, :] = jnp.broadcast_to(lse, (blk, HEAD_DIM))

    for p, dil in enumerate(A_DILATIONS):
        for r in range(dil):
            for n in range(tile // (blk * dil)):
                block(p, dil, r, n)

    def merge(i, carry):
        r = pl.multiple_of(i * blk, blk)
        rows = pl.ds(r, blk)
        l0, l1, l2 = lse_scr[0, rows, :], lse_scr[1, rows, :], lse_scr[2, rows, :]
        mx = jnp.maximum(jnp.maximum(l0, l1), l2)
        w0, w1, w2 = jnp.exp2(l0 - mx), jnp.exp2(l1 - mx), jnp.exp2(l2 - mx)
        num = w0 * o_scr[0, rows, :] + w1 * o_scr[1, rows, :] + w2 * o_scr[2, rows, :]
        o_ref[rows, :] = (num / (w0 + w1 + w2)).astype(o_ref.dtype)
        return carry

    lax.fori_loop(0, tile // blk, merge, 0)


def dilated_attention(z, bias_tab, *, batch, seq):
    z3 = z.reshape(batch, seq, z.shape[1])
    nt = seq // A_TILE
    hq, hk, hv = Z_AQ // HEAD_DIM, Z_AK // HEAD_DIM, Z_AV // HEAD_DIM
    tile_spec = lambda col0, prev: pl.BlockSpec(
        (None, A_TILE, HEAD_DIM),
        (lambda b, h, t: (b, jnp.maximum(t - 1, 0), col0 + h)) if prev else (lambda b, h, t: (b, t, col0 + h)))
    npat = len(A_DILATIONS)
    out = pl.pallas_call(
        functools.partial(_dilated_kernel, scale=HEAD_DIM ** -0.5 * LOG2E),
        grid=(batch, A_HEADS, nt),
        in_specs=[
            tile_spec(hq, False),
            tile_spec(hk, False), tile_spec(hk, True),
            tile_spec(hv, False), tile_spec(hv, True),
            pl.BlockSpec((npat, None, A_BLOCK, 2 * A_BLOCK), lambda b, h, t: (0, h, 0, 0)),
        ],
        out_specs=pl.BlockSpec((None, A_TILE, HEAD_DIM), lambda b, h, t: (b, t, h)),
        out_shape=jax.ShapeDtypeStruct((batch, seq, A_HEADS * HEAD_DIM), BF16),
        scratch_shapes=[
            pltpu.VMEM((npat, A_BLOCK, 2 * A_BLOCK), F32),
            pltpu.VMEM((npat, A_TILE, HEAD_DIM), F32),
            pltpu.VMEM((npat, A_TILE, HEAD_DIM), F32),
        ],
        compiler_params=_cparams("parallel", "parallel", "arbitrary"),
        name="dilated_attention",
    )(z3, z3, z3, z3, z3, bias_tab)
    return out.reshape(batch * seq, A_HEADS * HEAD_DIM)


def _gla_kernel(q_ref, k_ref, v_ref, r_ref, lr_ref, wa_ref, ba_ref, gn_ref, o_ref, st_ref, *, scale):
    @pl.when(pl.program_id(1) == 0)
    def _():
        st_ref[...] = jnp.zeros_like(st_ref)

    c = B_CHUNK
    tc = q_ref.shape[0]
    dk = B_DK // B_HEADS
    dv = B_DV // B_HEADS
    pre = jnp.dot(lr_ref[...], wa_ref[...], precision=lax.Precision.HIGHEST,
                  preferred_element_type=F32) + ba_ref[...]
    log_a = (jnp.minimum(pre, 0.0) - jnp.log(1.0 + jnp.exp(-jnp.abs(pre)))) * (1.0 / B_GATE_TAU)
    tril = lax.broadcasted_iota(jnp.int32, (c, c), 0) >= lax.broadcasted_iota(jnp.int32, (c, c), 1)
    ones_tril = jnp.where(tril, 1.0, 0.0).astype(F32)
    for ci in range(tc // c):
        sl = slice(ci * c, (ci + 1) * c)
        cum_all = jnp.dot(ones_tril, log_a[sl, :], precision=lax.Precision.HIGHEST, preferred_element_type=F32)
        for h in range(B_HEADS):
            kcols = slice(h * dk, (h + 1) * dk)
            vcols = slice(h * dv, (h + 1) * dv)
            cum = cum_all[:, kcols]
            cum_last = cum[c - 1:c, :]
            q = q_ref[sl, kcols] * scale
            k = k_ref[sl, kcols]
            v = v_ref[sl, vcols]
            q_dec = (q * jnp.exp(cum)).astype(BF16)
            k_inv = (k * jnp.exp(-cum)).astype(BF16)
            k_state = (k * jnp.exp(cum_last - cum)).astype(BF16)
            att = lax.dot_general(q_dec, k_inv, (((1,), (1,)), ((), ())), preferred_element_type=F32)
            att = jnp.where(tril, att, 0.0).astype(BF16)
            st = st_ref[h]
            o = (jnp.dot(att, v.astype(BF16), preferred_element_type=F32)
                 + lax.dot_general(q_dec, st.astype(BF16), (((1,), (1,)), ((), ())), preferred_element_type=F32))
            st_ref[h] = st * jnp.exp(cum_last) + jnp.dot(v.T.astype(BF16), k_state, preferred_element_type=F32)
            ms = jnp.mean(o * o, axis=-1, keepdims=True)
            o = o * lax.rsqrt(ms + EPS) * gn_ref[:, vcols]
            r = r_ref[sl, vcols]
            o_ref[sl, vcols] = (o * (r * jax.nn.sigmoid(r))).astype(o_ref.dtype)


def gated_linear_attention(z, w_alpha_p, b_alpha, gla_norm, *, batch, seq, tc):
    dk = B_DK // B_HEADS
    dv = B_DV // B_HEADS
    nt = seq // tc
    row = lambda b, t: b * nt + t
    const = lambda b, t: (0, 0)
    return pl.pallas_call(
        functools.partial(_gla_kernel, scale=dk ** -0.5),
        grid=(batch, nt),
        in_specs=[
            pl.BlockSpec((tc, B_DK), lambda b, t: (row(b, t), Z_BQ // B_DK)),
            pl.BlockSpec((tc, B_DK), lambda b, t: (row(b, t), Z_BK // B_DK)),
            pl.BlockSpec((tc, B_DV), lambda b, t: (row(b, t), Z_BV // B_DV)),
            pl.BlockSpec((tc, B_DV), lambda b, t: (row(b, t), Z_BR // B_DV)),
            pl.BlockSpec((tc, LANES), lambda b, t: (row(b, t), Z_LR // LANES)),
            pl.BlockSpec((LANES, B_DK), const),
            pl.BlockSpec((1, B_DK), const),
            pl.BlockSpec((1, B_DV), const),
        ],
        out_specs=pl.BlockSpec((tc, B_DV), lambda b, t: (row(b, t), 0)),
        out_shape=jax.ShapeDtypeStruct((batch * seq, B_DV), BF16),
        scratch_shapes=[pltpu.VMEM((B_HEADS, dv, dk), F32)],
        compiler_params=_cparams("parallel", "arbitrary"),
        name="gated_linear_attention",
    )(z, z, z, z, z, w_alpha_p, b_alpha.reshape(1, B_DK), gla_norm.reshape(1, B_DV))


C_QW = 2 * C_NOPE


def _rope_tables(seq):
    pos = jnp.arange(seq, dtype=F32)
    inv = ROPE_THETA ** (-jnp.arange(0, C_ROPE, 2, dtype=F32) / C_ROPE)
    ang = pos[:, None] * inv[None, :]
    cos, sin = jnp.cos(ang), jnp.sin(ang)
    half = C_ROPE // 2
    z = lambda w: jnp.zeros((seq, w), F32)
    return jnp.stack([
        jnp.concatenate([cos, cos, z(LANES - C_ROPE)], axis=1),
        jnp.concatenate([-sin, z(half), z(LANES - C_ROPE)], axis=1),
        jnp.concatenate([z(half), sin, z(LANES - C_ROPE)], axis=1)])


def _mla_proj_kernel(cq_ref, ckv_ref, ckr_ref, gq_ref, gkv_ref, wq_ref, wk_ref, wv_ref, rt_ref,
                     q_out, kn_out, kr_out, v_out, *, scale):
    half = C_ROPE // 2
    t0, t1, t2 = rt_ref[0], rt_ref[1], rt_ref[2]

    def rms(x, g):
        ms = jnp.mean(x * x, axis=-1, keepdims=True)
        return (x * lax.rsqrt(ms + EPS) * g).astype(BF16)

    def rotary(x):
        return x * t0 + pltpu.roll(x, LANES - half, axis=1) * t1 + pltpu.roll(x, half, axis=1) * t2

    cq = rms(cq_ref[...], gq_ref[...])
    q = jnp.dot(cq, wq_ref[...], preferred_element_type=F32)
    for h in range(C_HEADS):
        nope = slice(h * C_QW, h * C_QW + C_NOPE)
        rope = slice(h * C_QW + C_NOPE, (h + 1) * C_QW)
        q_out[:, nope] = (q[:, nope] * scale).astype(q_out.dtype)
        q_out[:, rope] = (rotary(q[:, rope]) * scale).astype(q_out.dtype)
    ckv = rms(ckv_ref[...], gkv_ref[...])
    kn_out[...] = jnp.dot(ckv, wk_ref[...], preferred_element_type=F32).astype(kn_out.dtype)
    v_out[...] = jnp.dot(ckv, wv_ref[...], preferred_element_type=F32).astype(v_out.dtype)
    kr_out[...] = rotary(ckr_ref[...]).astype(kr_out.dtype)


def mla_project(z, q_a_norm, kv_a_norm, wq_p, wk_p, wv_p, rope_tab, *, seq, tm):
    m = z.shape[0]
    nt = seq // tm
    const = lambda i: (0, 0)
    return pl.pallas_call(
        functools.partial(_mla_proj_kernel, scale=(C_NOPE + C_ROPE) ** -0.5 * math.log2(math.e)),
        grid=(m // tm,),
        in_specs=[
            pl.BlockSpec((tm, C_Q_RANK), lambda i: (i, Z_CQA // C_Q_RANK)),
            pl.BlockSpec((tm, C_KV_RANK), lambda i: (i, Z_CKVA // C_KV_RANK)),
            pl.BlockSpec((tm, LANES), lambda i: (i, Z_CKR // LANES)),
            pl.BlockSpec((1, C_Q_RANK), const),
            pl.BlockSpec((1, C_KV_RANK), const),
            pl.BlockSpec((C_Q_RANK, C_HEADS * C_QW), const),
            pl.BlockSpec((C_KV_RANK, C_HEADS * C_NOPE), const),
            pl.BlockSpec((C_KV_RANK, C_HEADS * C_V), const),
            pl.BlockSpec((3, tm, LANES), lambda i: (0, i % nt, 0)),
        ],
        out_specs=[
            pl.BlockSpec((tm, C_HEADS * C_QW), lambda i: (i, 0)),
            pl.BlockSpec((tm, C_HEADS * C_NOPE), lambda i: (i, 0)),
            pl.BlockSpec((tm, LANES), lambda i: (i, 0)),
            pl.BlockSpec((tm, C_HEADS * C_V), lambda i: (i, 0)),
        ],
        out_shape=[
            jax.ShapeDtypeStruct((m, C_HEADS * C_QW), BF16),
            jax.ShapeDtypeStruct((m, C_HEADS * C_NOPE), BF16),
            jax.ShapeDtypeStruct((m, LANES), BF16),
            jax.ShapeDtypeStruct((m, C_HEADS * C_V), BF16),
        ],
        compiler_params=_cparams("parallel"),
        name="mla_project",
    )(z, z, z, q_a_norm.reshape(1, -1), kv_a_norm.reshape(1, -1), wq_p, wk_p, wv_p, rope_tab)


def _mla_flash_kernel(q_ref, kn_ref, kr_ref, v_ref, o_ref, m_ref, l_ref, acc_ref):
    tq = m_ref.shape[0]
    tk = tq // 2

    def q_tile(qi, carry):
        q0 = pl.multiple_of(qi * tq, tq)
        m_ref[...] = jnp.full_like(m_ref, NEG)
        l_ref[...] = jnp.zeros_like(l_ref)
        acc_ref[...] = jnp.zeros_like(acc_ref)

        def step(r0, nr, k_start, nk, mask_offset=None):
            rows = slice(r0, r0 + nr)
            ks = pl.ds(pl.multiple_of(k_start, tk), nk)
            k = jnp.concatenate([kn_ref[ks, :], kr_ref[ks, :]], axis=1)
            q = q_ref[pl.ds(pl.multiple_of(q0 + r0, tk), nr), :]
            s = lax.dot_general(q, k, (((1,), (1,)), ((), ())), preferred_element_type=F32)
            if mask_offset is not None:
                visible = (lax.broadcasted_iota(jnp.int32, s.shape, 1)
                           <= lax.broadcasted_iota(jnp.int32, s.shape, 0) + mask_offset)
                s = jnp.where(visible, s, NEG)
            m_prev = m_ref[rows, :]
            m_new = jnp.maximum(m_prev, jnp.max(s, axis=-1, keepdims=True))
            alpha = jnp.exp2(m_prev - m_new)
            p = jnp.exp2(s - jnp.concatenate([m_new] * (nk // LANES), axis=1))
            l_ref[rows, :] = alpha * l_ref[rows, :] + jnp.sum(p, axis=-1, keepdims=True)
            acc_ref[rows, :] = alpha * acc_ref[rows, :] + jnp.dot(p.astype(BF16), v_ref[ks, :],
                                                                  preferred_element_type=F32)
            m_ref[rows, :] = m_new

        def body(kb, c):
            for u in range(4):
                step(0, tq, (4 * kb + u) * tk, tk)
            return c

        lax.fori_loop(0, qi // 2, body, 0)

        @pl.when(qi % 2 == 1)
        def _():
            step(0, tq, (2 * qi - 2) * tk, tk)
            step(0, tq, (2 * qi - 1) * tk, tk)
        step(0, tk, q0, tk, 0)
        step(tk, tk, q0, tq, tk)
        o_ref[pl.ds(q0, tq), :] = (acc_ref[...] / l_ref[...]).astype(o_ref.dtype)
        return carry

    lax.fori_loop(0, q_ref.shape[0] // tq, q_tile, 0)


def mla_flash(q, kn, kr, v, *, batch, seq, tq):
    return pl.pallas_call(
        _mla_flash_kernel,
        grid=(batch, C_HEADS),
        in_specs=[
            pl.BlockSpec((seq, C_QW), lambda b, h: (b, h)),
            pl.BlockSpec((seq, C_NOPE), lambda b, h: (b, h)),
            pl.BlockSpec((seq, LANES), lambda b, h: (b, 0)),
            pl.BlockSpec((seq, C_V), lambda b, h: (b, h)),
        ],
        out_specs=pl.BlockSpec((seq, C_V), lambda b, h: (b, h)),
        out_shape=jax.ShapeDtypeStruct((batch * seq, C_HEADS * C_V), BF16),
        scratch_shapes=[
            pltpu.VMEM((tq, LANES), F32),
            pltpu.VMEM((tq, LANES), F32),
            pltpu.VMEM((tq, C_V), F32),
        ],
        compiler_params=_cparams("parallel", "parallel"),
        name="mla_flash",
    )(q, kn, kr, v)


def _mixer_out_kernel(oa_ref, ob_ref, oc_ref, wa_ref, wb_ref, wc_ref, g0_ref, g1_ref, g2_ref,
                      b0_ref, b1_ref, b2_ref, wo_ref, xr_ref, gn_ref, o_ref, xg_ref, inv_ref, mg_ref, ss_ref,
                      *, n_rows):
    i = pl.program_id(0)
    j = pl.program_id(1)
    _, nj, _, tn = mg_ref.shape
    cur = lax.rem(i, 2)

    def project():
        acc = xr_ref[...]
        for c in range(nj):
            acc = acc + jnp.dot(mg_ref[1 - cur, c], wo_ref[c * tn:(c + 1) * tn, :], preferred_element_type=F32)
        o_ref[...] = acc
        _next_norm_tile(acc, gn_ref, xg_ref, ss_ref)

    def branch(o, w, g, b):
        return jax.nn.sigmoid(g[...] + b[...]) * jnp.dot(o[...], w[...], preferred_element_type=F32)

    def merge():
        acc = branch(oa_ref, wa_ref, g0_ref, b0_ref)
        acc = acc + branch(ob_ref, wb_ref, g1_ref, b1_ref)
        acc = acc + branch(oc_ref, wc_ref, g2_ref, b2_ref)
        mg_ref[cur, j] = acc.astype(mg_ref.dtype)

    @pl.when(j == 0)
    def _():
        ss_ref[...] = jnp.zeros_like(ss_ref)

    @pl.when(jnp.logical_and(i > 0, i < n_rows))
    def _():
        project()
        merge()

    @pl.when(i == 0)
    def _():
        merge()

    @pl.when(i == n_rows)
    def _():
        project()

    @pl.when(jnp.logical_and(i > 0, j == nj - 1))
    def _():
        _next_norm_finish(inv_ref, ss_ref, nj * tn)


def _pipelined_row_maps(n_rows, ncols):
    row1 = lambda i: jnp.minimum(i, n_rows - 1)
    col1 = lambda i, j: jnp.where(i < n_rows, j, ncols - 1)
    row2 = lambda i: jnp.maximum(i - 1, 0)
    col2 = lambda i, j: jnp.where(i > 0, j, 0)
    return row1, col1, row2, col2


def mixer_output_block(x, o_a, o_b, o_c, w_a, w_b, w_c, gates, b_gate, w_o, next_gain, *, tm, tn):
    m, k = o_a.shape
    d = w_a.shape[1]
    nj = d // tn
    n_rows = m // tm
    mrow, mcol, orow, ocol = _pipelined_row_maps(n_rows, nj)
    act = pl.BlockSpec((tm, k), lambda i, j: (mrow(i), 0))
    wgt = pl.BlockSpec((k, tn), lambda i, j: (0, mcol(i, j)))
    gate = lambda br: pl.BlockSpec((tm, tn), lambda i, j: (mrow(i), br * nj + mcol(i, j)))
    gbias = lambda br: pl.BlockSpec((1, tn), lambda i, j: (0, br * nj + j))
    out_tile = pl.BlockSpec((tm, tn), lambda i, j: (orow(i), ocol(i, j)))
    return pl.pallas_call(
        functools.partial(_mixer_out_kernel, n_rows=n_rows),
        grid=(n_rows + 1, nj),
        in_specs=[act, act, act, wgt, wgt, wgt, gate(0), gate(1), gate(2), gbias(0), gbias(1), gbias(2),
                  pl.BlockSpec((d, tn), lambda i, j: (0, ocol(i, j))),
                  out_tile,
                  pl.BlockSpec((1, tn), lambda i, j: (0, ocol(i, j)))],
        out_specs=[out_tile, out_tile, pl.BlockSpec((tm, LANES), lambda i, j: (orow(i), 0))],
        out_shape=[jax.ShapeDtypeStruct((m, d), F32), jax.ShapeDtypeStruct((m, d), BF16),
                   jax.ShapeDtypeStruct((m, LANES), F32)],
        scratch_shapes=[pltpu.VMEM((2, nj, tm, tn), BF16), pltpu.VMEM((tm, LANES), F32)],
        compiler_params=_cparams("arbitrary", "arbitrary"),
        name="mixer_output_block",
    )(o_a, o_b, o_c, w_a, w_b, w_c, gates, gates, gates, *([b_gate.reshape(1, -1)] * 3), w_o, x,
      next_gain.reshape(1, d))


def _cross_block_kernel(xg_ref, inv_ref, wq_ref, k_ref, v_ref, wo_ref, xr_ref, gn_ref,
                        o_ref, xgo_ref, invo_ref, ao_ref, ss_ref, *, scale, n_rows):
    i = pl.program_id(0)
    j = pl.program_id(1)
    hd = k_ref.shape[1]
    cur = lax.rem(i, 2)

    def project():
        acc = xr_ref[...]
        for h in range(X_HEADS):
            acc = acc + jnp.dot(ao_ref[1 - cur, h], wo_ref[h * hd:(h + 1) * hd, :], preferred_element_type=F32)
        o_ref[...] = acc
        _next_norm_tile(acc, gn_ref, xgo_ref, ss_ref)

    def attend():
        q = jnp.dot(xg_ref[...], wq_ref[...], preferred_element_type=F32) * (_inv_cols(inv_ref, hd) * scale)
        s = lax.dot_general(q.astype(BF16), k_ref[...], (((1,), (1,)), ((), ())), preferred_element_type=F32)
        m = jnp.max(s, axis=-1, keepdims=True)
        e = jnp.exp(s - m)
        l = jnp.sum(e, axis=-1, keepdims=True)
        o = jnp.dot(e.astype(BF16), v_ref[...], preferred_element_type=F32) / l
        ao_ref[cur, j] = o.astype(ao_ref.dtype)

    @pl.when(j == 0)
    def _():
        ss_ref[...] = jnp.zeros_like(ss_ref)

    @pl.when(jnp.logical_and(i > 0, i < n_rows))
    def _():
        project()
        attend()

    @pl.when(i == 0)
    def _():
        attend()

    @pl.when(i == n_rows)
    def _():
        project()

    @pl.when(jnp.logical_and(i > 0, j == X_HEADS - 1))
    def _():
        _next_norm_finish(invo_ref, ss_ref, X_HEADS * hd)


def cross_attention_block(x, xg, inv, w_q, kv, w_o, next_gain, *, batch, seq, mem_len, tm):
    m, d = x.shape
    hd = d // X_HEADS
    tiles_per_batch = seq // tm
    n_rows = m // tm
    arow, head, orow, ocol = _pipelined_row_maps(n_rows, X_HEADS)
    out_tile = pl.BlockSpec((tm, hd), lambda i, j: (orow(i), ocol(i, j)))
    return pl.pallas_call(
        functools.partial(_cross_block_kernel, scale=hd ** -0.5, n_rows=n_rows),
        grid=(n_rows + 1, X_HEADS),
        in_specs=[
            pl.BlockSpec((tm, d), lambda i, j: (arow(i), 0)),
            pl.BlockSpec((tm, LANES), lambda i, j: (arow(i), 0)),
            pl.BlockSpec((d, hd), lambda i, j: (0, head(i, j))),
            pl.BlockSpec((mem_len, hd), lambda i, j: (arow(i) // tiles_per_batch, head(i, j))),
            pl.BlockSpec((mem_len, hd), lambda i, j: (arow(i) // tiles_per_batch, X_HEADS + head(i, j))),
            pl.BlockSpec((d, hd), lambda i, j: (0, ocol(i, j))),
            out_tile,
            pl.BlockSpec((1, hd), lambda i, j: (0, ocol(i, j))),
        ],
        out_specs=[out_tile, out_tile, pl.BlockSpec((tm, LANES), lambda i, j: (orow(i), 0))],
        out_shape=[jax.ShapeDtypeStruct((m, d), F32), jax.ShapeDtypeStruct((m, d), BF16),
                   jax.ShapeDtypeStruct((m, LANES), F32)],
        scratch_shapes=[pltpu.VMEM((2, X_HEADS, tm, hd), BF16), pltpu.VMEM((tm, LANES), F32)],
        compiler_params=_cparams("arbitrary", "arbitrary"),
        name="cross_attention_block",
    )(xg, inv, w_q, kv, kv, w_o, x, next_gain.reshape(1, d))


def _pack_w_in(w):
    nl, d, _ = w.shape
    sizes = (1024, 1024, 1024, B_DK, B_DK, B_DV, B_GATE_RANK, B_DV, C_Q_RANK, C_KV_RANK, C_ROPE, N_BRANCH * d)
    offs = np.concatenate([[0], np.cumsum(sizes)])
    aq, ak, av, bq, bk, bv, lr, br, cqa, ckva, ckr, gates = [
        w[:, :, offs[i]:offs[i + 1]].astype(BF16) for i in range(len(sizes))]
    zeros = lambda n: jnp.zeros((nl, d, n), BF16)
    used = Z_CKR + LANES
    wz = jnp.concatenate(
        [aq, ak, av, bv, br, cqa, ckva, bq, bk,
         lr, zeros(LANES - B_GATE_RANK), ckr, zeros(LANES - C_ROPE), zeros(Z_WIDTH - used)], axis=2)
    return wz, gates


def _pack_w_qb(w):
    r = w.shape[0]
    w = w.reshape(r, C_HEADS, C_NOPE + C_ROPE)
    w = jnp.pad(w, ((0, 0), (0, 0), (0, C_QW - C_NOPE - C_ROPE)))
    return w.reshape(r, C_HEADS * C_QW).astype(BF16)


def _pack_w_kvb(w):
    r = w.shape[0]
    w = w.reshape(r, C_HEADS, C_NOPE + C_V)
    wk = w[:, :, :C_NOPE].reshape(r, C_HEADS * C_NOPE)
    wv = w[:, :, C_NOPE:].reshape(r, C_HEADS * C_V)
    return wk.astype(BF16), wv.astype(BF16)


def kernel(x, mem, rel_bias, norm_mix, w_in, b_gate, w_alpha, b_alpha, gla_norm, q_a_norm, w_qb, kv_a_norm, w_kvb, w_up_a, w_up_b, w_up_c, w_o, norm_x, norm_mem, w_xq, w_xkv, w_xo, norm_ffn, w_ffn_gate, w_ffn_up, w_ffn_down, norm_final):
    batch, seq, d = x.shape
    mem_len = mem.shape[1]
    depth = w_in.shape[0]
    t = batch * seq
    assert (batch, seq, d, mem_len) == (4, 4096, 2048, 256), "tile sizes are derived for the stated shapes"
    assert d // X_HEADS == TN and t % SWIGLU_TM == 0 and seq % A_TILE == 0

    xf = x.reshape(t, d)
    memf = mem.reshape(batch * mem_len, d)
    bias_tab = dilated_bias_table(rel_bias)
    rope_tab = _rope_tables(seq)
    bf = lambda a: a.astype(BF16)
    w_in_z, w_in_g = _pack_w_in(w_in)

    xg, inv = prenorm(xf, norm_mix[0], tm=TM)
    for l in range(depth):
        z = scaled_matmul(xg, inv, w_in_z, out_dtype=F32, tm=TM, tn=W_Z_TN, layer=l)
        gates = scaled_matmul(xg, inv, w_in_g, out_dtype=BF16, tm=TM, tn=W_G_TN, layer=l)
        o_a = dilated_attention(z, bias_tab, batch=batch, seq=seq)
        w_alpha_p = jnp.pad(w_alpha[l], ((0, LANES - B_GATE_RANK), (0, 0)))
        o_b = gated_linear_attention(z, w_alpha_p, b_alpha[l], gla_norm[l], batch=batch, seq=seq, tc=GLA_TC)
        wk_p, wv_p = _pack_w_kvb(w_kvb[l])
        cq, ckn, ckr, cv = mla_project(z, q_a_norm[l], kv_a_norm[l], _pack_w_qb(w_qb[l]), wk_p, wv_p,
                                       rope_tab, seq=seq, tm=TM)
        o_c = mla_flash(cq, ckn, ckr, cv, batch=batch, seq=seq, tq=FLASH_TQ)
        xf, xg, inv = mixer_output_block(xf, o_a, o_b, o_c, bf(w_up_a[l]), bf(w_up_b[l]), bf(w_up_c[l]), gates,
                                         b_gate[l], bf(w_o[l]), norm_x[l], tm=TM, tn=TN)
        xkv = norm_matmul(memf, norm_mem[l], w_xkv, out_dtype=BF16, tm=batch * mem_len, tn=TN, layer=l)
        xf, xg, inv = cross_attention_block(xf, xg, inv, bf(w_xq[l]), xkv, bf(w_xo[l]), norm_ffn[l],
                                            batch=batch, seq=seq, mem_len=mem_len, tm=TM)
        act = scaled_swiglu(xg, inv, w_ffn_gate, w_ffn_up, tm=SWIGLU_TM, tn=TN, layer=l)
        if l + 1 < depth:
            xf, xg, inv = matmul_residual(act, bf(w_ffn_down[l]), xf, tm=TM, tn=TN, next_gain=norm_mix[l + 1])
        else:
            xf = matmul_residual(act, bf(w_ffn_down[l]), xf, tm=TM, tn=TN)
    return rmsnorm_rows(xf, norm_final, tm=TM).reshape(batch, seq, d)
```

```python
import functools
import math

import numpy as np
import jax
import jax.numpy as jnp
from jax import lax
from jax.experimental import pallas as pl
from jax.experimental.pallas import tpu as pltpu

F32 = jnp.float32
BF16 = jnp.bfloat16
EPS = 1e-6
NEG = -1e30
LOG2E = math.log2(math.e)

HEAD_DIM = 128
A_HEADS = 8
A_DILATIONS = (1, 4, 16)
A_BLOCK = 128
N_BUCKETS = 32
MAX_DISTANCE = 2048
B_HEADS = 4
B_DK = 512
B_DV = 1024
B_GATE_RANK = 16
B_GATE_TAU = 16.0
B_CHUNK = 64
C_HEADS = 8
C_Q_RANK = 512
C_KV_RANK = 512
C_NOPE = 128
C_ROPE = 64
C_V = 128
ROPE_THETA = 10000.0
X_HEADS = 4
N_BRANCH = 3

VMEM_LIMIT_BYTES = 56 * 1024 * 1024
LANES = 128

Z_AQ, Z_AK, Z_AV = 0, 1024, 2048
Z_BV, Z_BR = 3072, 4096
Z_CQA, Z_CKVA = 5120, 5632
Z_BQ, Z_BK = 6144, 6656
Z_LR = 7168
Z_CKR = 7296
Z_WIDTH = 7680
G_WIDTH = 3 * 2048
A_TILE = 2048

TM = 1024
W_Z_TN = 1536
W_G_TN = 2048
TN = 512
SWIGLU_TM = 2048
GLA_TC = 1024
FLASH_TQ = 1024


def _cparams(*sem):
    return pltpu.CompilerParams(dimension_semantics=sem, vmem_limit_bytes=VMEM_LIMIT_BYTES)


def _rmsnorm_rows(x_ref, g_ref, h_ref):
    rows = x_ref.shape[0]

    def body(i, carry):
        r = pl.multiple_of(i * 16, 16)
        x = x_ref[pl.ds(r, 16), :]
        ms = jnp.mean(x * x, axis=-1, keepdims=True)
        h_ref[pl.ds(r, 16), :] = (x * lax.rsqrt(ms + EPS) * g_ref[...]).astype(BF16)
        return carry

    lax.fori_loop(0, rows // 16, body, 0, unroll=8)


def _norm_matmul_kernel(x_ref, g_ref, w_ref, o_ref, h_ref, *, scale):
    @pl.when(pl.program_id(1) == 0)
    def _():
        _rmsnorm_rows(x_ref, g_ref, h_ref)

    acc = jnp.dot(h_ref[...], w_ref[...].astype(BF16), preferred_element_type=F32)
    if scale != 1.0:
        acc = acc * scale
    o_ref[...] = acc.astype(o_ref.dtype)


def _weight_spec(w, layer, tn):
    if w.ndim == 2:
        return pl.BlockSpec((w.shape[0], tn), lambda i, j: (0, j))
    return pl.BlockSpec((None, w.shape[1], tn), lambda i, j: (layer, 0, j))


def norm_matmul(x, g, w, *, out_dtype, tm, tn, scale=1.0, layer=None):
    m, k = x.shape
    n = w.shape[-1]
    return pl.pallas_call(
        functools.partial(_norm_matmul_kernel, scale=scale),
        grid=(m // tm, n // tn),
        in_specs=[
            pl.BlockSpec((tm, k), lambda i, j: (i, 0)),
            pl.BlockSpec((1, k), lambda i, j: (0, 0)),
            _weight_spec(w, layer, tn),
        ],
        out_specs=pl.BlockSpec((tm, tn), lambda i, j: (i, j)),
        out_shape=jax.ShapeDtypeStruct((m, n), out_dtype),
        scratch_shapes=[pltpu.VMEM((tm, k), BF16)],
        compiler_params=_cparams("parallel", "arbitrary"),
        name="norm_matmul",
    )(x, g.reshape(1, k), w)


def _inv_cols(inv_ref, n):
    inv = inv_ref[...]
    return jnp.concatenate([inv] * (n // LANES), axis=1)


def _next_norm_tile(o, g_ref, xg_ref, ss_ref):
    xg_ref[...] = (o * g_ref[...]).astype(xg_ref.dtype)
    ss_ref[...] = ss_ref[...] + jnp.sum(o * o, axis=-1, keepdims=True)


def _next_norm_finish(inv_ref, ss_ref, d):
    inv_ref[...] = lax.rsqrt(ss_ref[...] * (1.0 / d) + EPS)


def _prenorm_kernel(x_ref, g_ref, xg_ref, inv_ref):
    x = x_ref[...]
    xg_ref[...] = (x * g_ref[...]).astype(xg_ref.dtype)
    ms = jnp.mean(x * x, axis=-1, keepdims=True)
    inv_ref[...] = jnp.broadcast_to(lax.rsqrt(ms + EPS), inv_ref.shape)


def prenorm(x, g, *, tm):
    m, k = x.shape
    return pl.pallas_call(
        _prenorm_kernel,
        grid=(m // tm,),
        in_specs=[pl.BlockSpec((tm, k), lambda i: (i, 0)), pl.BlockSpec((1, k), lambda i: (0, 0))],
        out_specs=[pl.BlockSpec((tm, k), lambda i: (i, 0)), pl.BlockSpec((tm, LANES), lambda i: (i, 0))],
        out_shape=[jax.ShapeDtypeStruct((m, k), BF16), jax.ShapeDtypeStruct((m, LANES), F32)],
        compiler_params=_cparams("parallel"),
        name="prenorm",
    )(x, g.reshape(1, k))


def _scaled_matmul_kernel(xg_ref, inv_ref, w_ref, o_ref):
    acc = jnp.dot(xg_ref[...], w_ref[...].astype(BF16), preferred_element_type=F32)
    o_ref[...] = (acc * _inv_cols(inv_ref, acc.shape[1])).astype(o_ref.dtype)


def scaled_matmul(xg, inv, w, *, out_dtype, tm, tn, layer=None):
    m, k = xg.shape
    n = w.shape[-1]
    return pl.pallas_call(
        _scaled_matmul_kernel,
        grid=(m // tm, n // tn),
        in_specs=[
            pl.BlockSpec((tm, k), lambda i, j: (i, 0)),
            pl.BlockSpec((tm, LANES), lambda i, j: (i, 0)),
            _weight_spec(w, layer, tn),
        ],
        out_specs=pl.BlockSpec((tm, tn), lambda i, j: (i, j)),
        out_shape=jax.ShapeDtypeStruct((m, n), out_dtype),
        compiler_params=_cparams("parallel", "parallel"),
        name="scaled_matmul",
    )(xg, inv, w)


def _scaled_swiglu_kernel(xg_ref, inv_ref, wg_ref, wu_ref, o_ref):
    xg = xg_ref[...]
    inv = _inv_cols(inv_ref, o_ref.shape[1])
    a = jnp.dot(xg, wg_ref[...].astype(BF16), preferred_element_type=F32) * inv
    b = jnp.dot(xg, wu_ref[...].astype(BF16), preferred_element_type=F32) * inv
    o_ref[...] = (a * jax.nn.sigmoid(a) * b).astype(o_ref.dtype)


def scaled_swiglu(xg, inv, wg, wu, *, tm, tn, layer=None):
    m, k = xg.shape
    n = wg.shape[-1]
    return pl.pallas_call(
        _scaled_swiglu_kernel,
        grid=(m // tm, n // tn),
        in_specs=[
            pl.BlockSpec((tm, k), lambda i, j: (i, 0)),
            pl.BlockSpec((tm, LANES), lambda i, j: (i, 0)),
            _weight_spec(wg, layer, tn),
            _weight_spec(wu, layer, tn),
        ],
        out_specs=pl.BlockSpec((tm, tn), lambda i, j: (i, j)),
        out_shape=jax.ShapeDtypeStruct((m, n), BF16),
        compiler_params=_cparams("parallel", "parallel"),
        name="scaled_swiglu",
    )(xg, inv, wg, wu)


def _matmul_residual_kernel(a_ref, w_ref, r_ref, o_ref):
    o_ref[...] = r_ref[...] + jnp.dot(a_ref[...], w_ref[...].astype(BF16), preferred_element_type=F32)


def _matmul_residual_norm_kernel(a_ref, w_ref, r_ref, g_ref, o_ref, xg_ref, inv_ref, ss_ref, *, ncols):
    j = pl.program_id(1)

    @pl.when(j == 0)
    def _():
        ss_ref[...] = jnp.zeros_like(ss_ref)

    o = r_ref[...] + jnp.dot(a_ref[...], w_ref[...].astype(BF16), preferred_element_type=F32)
    o_ref[...] = o
    _next_norm_tile(o, g_ref, xg_ref, ss_ref)

    @pl.when(j == ncols - 1)
    def _():
        _next_norm_finish(inv_ref, ss_ref, ncols * o.shape[1])


def matmul_residual(a, w, res, *, tm, tn, layer=None, next_gain=None):
    m, k = a.shape
    n = w.shape[-1]
    tile = pl.BlockSpec((tm, tn), lambda i, j: (i, j))
    in_specs = [pl.BlockSpec((tm, k), lambda i, j: (i, 0)), _weight_spec(w, layer, tn), tile]
    if next_gain is None:
        return pl.pallas_call(
            _matmul_residual_kernel,
            grid=(m // tm, n // tn),
            in_specs=in_specs,
            out_specs=tile,
            out_shape=jax.ShapeDtypeStruct((m, n), F32),
            compiler_params=_cparams("parallel", "parallel"),
            name="matmul_residual",
        )(a, w, res)
    return pl.pallas_call(
        functools.partial(_matmul_residual_norm_kernel, ncols=n // tn),
        grid=(m // tm, n // tn),
        in_specs=in_specs + [pl.BlockSpec((1, tn), lambda i, j: (0, j))],
        out_specs=[tile, tile, pl.BlockSpec((tm, LANES), lambda i, j: (i, 0))],
        out_shape=[jax.ShapeDtypeStruct((m, n), F32), jax.ShapeDtypeStruct((m, n), BF16),
                   jax.ShapeDtypeStruct((m, LANES), F32)],
        scratch_shapes=[pltpu.VMEM((tm, LANES), F32)],
        compiler_params=_cparams("parallel", "arbitrary"),
        name="matmul_residual_norm",
    )(a, w, res, next_gain.reshape(1, n))


def _rmsnorm_kernel(x_ref, g_ref, o_ref):
    x = x_ref[...]
    ms = jnp.mean(x * x, axis=-1, keepdims=True)
    o_ref[...] = x * lax.rsqrt(ms + EPS) * g_ref[...]


def rmsnorm_rows(x, g, *, tm):
    m, k = x.shape
    return pl.pallas_call(
        _rmsnorm_kernel,
        grid=(m // tm,),
        in_specs=[pl.BlockSpec((tm, k), lambda i: (i, 0)), pl.BlockSpec((1, k), lambda i: (0, 0))],
        out_specs=pl.BlockSpec((tm, k), lambda i: (i, 0)),
        out_shape=jax.ShapeDtypeStruct((m, k), F32),
        compiler_params=_cparams("parallel"),
        name="final_rmsnorm",
    )(x, g.reshape(1, k))


def _t5_bucket_np(dist):
    max_exact = N_BUCKETS // 2
    n = np.maximum(dist, 1).astype(np.float64)
    large = max_exact + (np.log(n / max_exact) / math.log(MAX_DISTANCE / max_exact)
                         * (N_BUCKETS - max_exact)).astype(np.int32)
    large = np.minimum(large, N_BUCKETS - 1)
    return np.where(dist < max_exact, dist, large).astype(np.int32)


def _dilated_bucket_table():
    qi = np.arange(A_BLOCK)[:, None]
    kj = np.arange(2 * A_BLOCK)[None, :]
    steps_back = qi + A_BLOCK - kj
    valid = (steps_back >= 0) & (steps_back <= A_BLOCK)
    tabs = []
    for dil in A_DILATIONS:
        bucket = _t5_bucket_np(np.clip(steps_back, 0, A_BLOCK) * dil)
        tabs.append(np.where(valid, bucket, -1))
    return np.stack(tabs).astype(np.int32)


def _bias_table_kernel(idx_ref, rb_ref, o_ref):
    h = pl.program_id(1)
    idx = idx_ref[...]
    acc = jnp.full(idx.shape, NEG, F32)
    for b in range(N_BUCKETS):
        acc = jnp.where(idx == b, rb_ref[b, h] * LOG2E, acc)
    o_ref[...] = acc


def dilated_bias_table(rel_bias):
    idx = jnp.asarray(_dilated_bucket_table())
    npat = len(A_DILATIONS)
    return pl.pallas_call(
        _bias_table_kernel,
        grid=(npat, A_HEADS),
        in_specs=[
            pl.BlockSpec((None, A_BLOCK, 2 * A_BLOCK), lambda p, h: (p, 0, 0)),
            pl.BlockSpec(memory_space=pltpu.SMEM),
        ],
        out_specs=pl.BlockSpec((None, None, A_BLOCK, 2 * A_BLOCK), lambda p, h: (p, h, 0, 0)),
        out_shape=jax.ShapeDtypeStruct((npat, A_HEADS, A_BLOCK, 2 * A_BLOCK), F32),
        compiler_params=_cparams("parallel", "parallel"),
        name="dilated_bias_table",
    )(idx, rel_bias)


def _dilated_kernel(q_ref, kc_ref, kp_ref, vc_ref, vp_ref, bias_ref, o_ref, bm0, o_scr, lse_scr, *, scale):
    t = pl.program_id(2)
    tile = q_ref.shape[0]
    blk = A_BLOCK
    col = lax.broadcasted_iota(jnp.int32, (blk, 2 * blk), 1)
    no_prev = jnp.logical_and(col < blk, t == 0)
    for p in range(len(A_DILATIONS)):
        bm0[p] = jnp.where(no_prev, NEG, bias_ref[p])

    def rows(start, size, dil):
        return pl.ds(start, size) if dil == 1 else pl.ds(start, size, stride=dil)

    def block(p, dil, r, n):
        q_idx = rows(r + dil * blk * n, blk, dil)
        if n == 0:
            prev = rows(tile - dil * blk + r, blk, dil)
            cur = rows(r, blk, dil)
            k = jnp.concatenate([kp_ref[prev, :], kc_ref[cur, :]], axis=0)
            v = jnp.concatenate([vp_ref[prev, :], vc_ref[cur, :]], axis=0)
            bm = bm0[p]
        else:
            k_idx = rows(r + dil * blk * (n - 1), 2 * blk, dil)
            k, v = kc_ref[k_idx, :], vc_ref[k_idx, :]
            bm = bias_ref[p]
        q = (q_ref[q_idx, :] * scale).astype(BF16)
        s = lax.dot_general(q, k.astype(BF16), (((1,), (1,)), ((), ())), preferred_element_type=F32)
        s = jnp.where(bm > 0.5 * NEG, s + bm, NEG)
        m = jnp.max(s, axis=-1, keepdims=True)
        e = jnp.exp2(s - m)
        l = jnp.sum(e, axis=-1, keepdims=True)
        o = jnp.dot(e.astype(BF16), v.astype(BF16), preferred_element_type=F32) / l
        lse = m + jnp.log(l) * LOG2E
        o_scr[p, q_idx, :] = o
        lse_scr[p, q_idx, :] = jnp.broadcast_to(lse, (blk, HEAD_DIM))

    for p, dil in enumerate(A_DILATIONS):
        for r in range(dil):
            for n in range(tile // (blk * dil)):
                block(p, dil, r, n)

    def merge(i, carry):
        r = pl.multiple_of(i * blk, blk)
        rows = pl.ds(r, blk)
        l0, l1, l2 = lse_scr[0, rows, :], lse_scr[1, rows, :], lse_scr[2, rows, :]
        mx = jnp.maximum(jnp.maximum(l0, l1), l2)
        w0, w1, w2 = jnp.exp2(l0 - mx), jnp.exp2(l1 - mx), jnp.exp2(l2 - mx)
        num = w0 * o_scr[0, rows, :] + w1 * o_scr[1, rows, :] + w2 * o_scr[2, rows, :]
        o_ref[rows, :] = (num / (w0 + w1 + w2)).astype(o_ref.dtype)
        return carry

    lax.fori_loop(0, tile // blk, merge, 0)


def dilated_attention(z, bias_tab, *, batch, seq):
    z3 = z.reshape(batch, seq, z.shape[1])
    nt = seq // A_TILE
    hq, hk, hv = Z_AQ // HEAD_DIM, Z_AK // HEAD_DIM, Z_AV // HEAD_DIM
    tile_spec = lambda col0, prev: pl.BlockSpec(
        (None, A_TILE, HEAD_DIM),
        (lambda b, h, t: (b, jnp.maximum(t - 1, 0), col0 + h)) if prev else (lambda b, h, t: (b, t, col0 + h)))
    npat = len(A_DILATIONS)
    out = pl.pallas_call(
        functools.partial(_dilated_kernel, scale=HEAD_DIM ** -0.5 * LOG2E),
        grid=(batch, A_HEADS, nt),
        in_specs=[
            tile_spec(hq, False),
            tile_spec(hk, False), tile_spec(hk, True),
            tile_spec(hv, False), tile_spec(hv, True),
            pl.BlockSpec((npat, None, A_BLOCK, 2 * A_BLOCK), lambda b, h, t: (0, h, 0, 0)),
        ],
        out_specs=pl.BlockSpec((None, A_TILE, HEAD_DIM), lambda b, h, t: (b, t, h)),
        out_shape=jax.ShapeDtypeStruct((batch, seq, A_HEADS * HEAD_DIM), BF16),
        scratch_shapes=[
            pltpu.VMEM((npat, A_BLOCK, 2 * A_BLOCK), F32),
            pltpu.VMEM((npat, A_TILE, HEAD_DIM), F32),
            pltpu.VMEM((npat, A_TILE, HEAD_DIM), F32),
        ],
        compiler_params=_cparams("parallel", "parallel", "arbitrary"),
        name="dilated_attention",
    )(z3, z3, z3, z3, z3, bias_tab)
    return out.reshape(batch * seq, A_HEADS * HEAD_DIM)


def _split_bf16(a):
    hi = a.astype(BF16)
    return hi, (a - hi.astype(F32)).astype(BF16)


def _gla_kernel(q_ref, k_ref, v_ref, r_ref, lr_ref, wa_ref, ba_ref, gn_ref, o_ref, st_ref, *, scale):
    @pl.when(pl.program_id(1) == 0)
    def _():
        st_ref[...] = jnp.zeros_like(st_ref)

    c = B_CHUNK
    tc = q_ref.shape[0]
    dk = B_DK // B_HEADS
    dv = B_DV // B_HEADS
    dot = functools.partial(jnp.dot, preferred_element_type=F32)
    lr_hi, lr_lo = _split_bf16(lr_ref[...])
    wa_hi, wa_lo = _split_bf16(wa_ref[...])
    pre = dot(lr_hi, wa_hi) + dot(lr_hi, wa_lo) + dot(lr_lo, wa_hi) + ba_ref[...]
    log_a = (jnp.minimum(pre, 0.0) - jnp.log(1.0 + jnp.exp(-jnp.abs(pre)))) * (1.0 / B_GATE_TAU)
    tril = lax.broadcasted_iota(jnp.int32, (c, c), 0) >= lax.broadcasted_iota(jnp.int32, (c, c), 1)
    ones_tril = jnp.where(tril, 1.0, 0.0).astype(BF16)
    for ci in range(tc // c):
        sl = slice(ci * c, (ci + 1) * c)
        la_hi, la_lo = _split_bf16(log_a[sl, :])
        cum_all = dot(ones_tril, la_hi) + dot(ones_tril, la_lo)
        for h in range(B_HEADS):
            kcols = slice(h * dk, (h + 1) * dk)
            vcols = slice(h * dv, (h + 1) * dv)
            cum = cum_all[:, kcols]
            cum_last = cum[c - 1:c, :]
            q = q_ref[sl, kcols] * scale
            k = k_ref[sl, kcols]
            v = v_ref[sl, vcols]
            q_dec = (q * jnp.exp(cum)).astype(BF16)
            k_inv = (k * jnp.exp(-cum)).astype(BF16)
            k_state = (k * jnp.exp(cum_last - cum)).astype(BF16)
            att = lax.dot_general(q_dec, k_inv, (((1,), (1,)), ((), ())), preferred_element_type=F32)
            att = jnp.where(tril, att, 0.0).astype(BF16)
            st = st_ref[h]
            o = (jnp.dot(att, v.astype(BF16), preferred_element_type=F32)
                 + lax.dot_general(q_dec, st.astype(BF16), (((1,), (1,)), ((), ())), preferred_element_type=F32))
            st_ref[h] = st * jnp.exp(cum_last) + jnp.dot(v.T.astype(BF16), k_state, preferred_element_type=F32)
            ms = jnp.mean(o * o, axis=-1, keepdims=True)
            o = o * lax.rsqrt(ms + EPS) * gn_ref[:, vcols]
            r = r_ref[sl, vcols]
            o_ref[sl, vcols] = (o * (r * jax.nn.sigmoid(r))).astype(o_ref.dtype)


def gated_linear_attention(z, w_alpha_p, b_alpha, gla_norm, *, batch, seq, tc):
    dk = B_DK // B_HEADS
    dv = B_DV // B_HEADS
    nt = seq // tc
    row = lambda b, t: b * nt + t
    const = lambda b, t: (0, 0)
    return pl.pallas_call(
        functools.partial(_gla_kernel, scale=dk ** -0.5),
        grid=(batch, nt),
        in_specs=[
            pl.BlockSpec((tc, B_DK), lambda b, t: (row(b, t), Z_BQ // B_DK)),
            pl.BlockSpec((tc, B_DK), lambda b, t: (row(b, t), Z_BK // B_DK)),
            pl.BlockSpec((tc, B_DV), lambda b, t: (row(b, t), Z_BV // B_DV)),
            pl.BlockSpec((tc, B_DV), lambda b, t: (row(b, t), Z_BR // B_DV)),
            pl.BlockSpec((tc, LANES), lambda b, t: (row(b, t), Z_LR // LANES)),
            pl.BlockSpec((LANES, B_DK), const),
            pl.BlockSpec((1, B_DK), const),
            pl.BlockSpec((1, B_DV), const),
        ],
        out_specs=pl.BlockSpec((tc, B_DV), lambda b, t: (row(b, t), 0)),
        out_shape=jax.ShapeDtypeStruct((batch * seq, B_DV), BF16),
        scratch_shapes=[pltpu.VMEM((B_HEADS, dv, dk), F32)],
        compiler_params=_cparams("parallel", "arbitrary"),
        name="gated_linear_attention",
    )(z, z, z, z, z, w_alpha_p, b_alpha.reshape(1, B_DK), gla_norm.reshape(1, B_DV))


C_QW = 2 * C_NOPE


def _rope_tables(seq):
    pos = jnp.arange(seq, dtype=F32)
    inv = ROPE_THETA ** (-jnp.arange(0, C_ROPE, 2, dtype=F32) / C_ROPE)
    ang = pos[:, None] * inv[None, :]
    cos, sin = jnp.cos(ang), jnp.sin(ang)
    half = C_ROPE // 2
    z = lambda w: jnp.zeros((seq, w), F32)
    return jnp.stack([
        jnp.concatenate([cos, cos, z(LANES - C_ROPE)], axis=1),
        jnp.concatenate([-sin, z(half), z(LANES - C_ROPE)], axis=1),
        jnp.concatenate([z(half), sin, z(LANES - C_ROPE)], axis=1)])


def _mla_proj_kernel(cq_ref, ckv_ref, ckr_ref, gq_ref, gkv_ref, wq_ref, wk_ref, wv_ref, rt_ref,
                     q_out, kn_out, kr_out, v_out, *, scale):
    half = C_ROPE // 2
    t0, t1, t2 = rt_ref[0], rt_ref[1], rt_ref[2]

    def rms(x, g):
        ms = jnp.mean(x * x, axis=-1, keepdims=True)
        return (x * lax.rsqrt(ms + EPS) * g).astype(BF16)

    def rotary(x):
        return x * t0 + pltpu.roll(x, LANES - half, axis=1) * t1 + pltpu.roll(x, half, axis=1) * t2

    cq = rms(cq_ref[...], gq_ref[...])
    q = jnp.dot(cq, wq_ref[...], preferred_element_type=F32)
    for h in range(C_HEADS):
        nope = slice(h * C_QW, h * C_QW + C_NOPE)
        rope = slice(h * C_QW + C_NOPE, (h + 1) * C_QW)
        q_out[:, nope] = (q[:, nope] * scale).astype(q_out.dtype)
        q_out[:, rope] = (rotary(q[:, rope]) * scale).astype(q_out.dtype)
    ckv = rms(ckv_ref[...], gkv_ref[...])
    kn_out[...] = jnp.dot(ckv, wk_ref[...], preferred_element_type=F32).astype(kn_out.dtype)
    v_out[...] = jnp.dot(ckv, wv_ref[...], preferred_element_type=F32).astype(v_out.dtype)
    kr_out[...] = rotary(ckr_ref[...]).astype(kr_out.dtype)


def mla_project(z, q_a_norm, kv_a_norm, wq_p, wk_p, wv_p, rope_tab, *, seq, tm):
    m = z.shape[0]
    nt = seq // tm
    const = lambda i: (0, 0)
    return pl.pallas_call(
        functools.partial(_mla_proj_kernel, scale=(C_NOPE + C_ROPE) ** -0.5 * math.log2(math.e)),
        grid=(m // tm,),
        in_specs=[
            pl.BlockSpec((tm, C_Q_RANK), lambda i: (i, Z_CQA // C_Q_RANK)),
            pl.BlockSpec((tm, C_KV_RANK), lambda i: (i, Z_CKVA // C_KV_RANK)),
            pl.BlockSpec((tm, LANES), lambda i: (i, Z_CKR // LANES)),
            pl.BlockSpec((1, C_Q_RANK), const),
            pl.BlockSpec((1, C_KV_RANK), const),
            pl.BlockSpec((C_Q_RANK, C_HEADS * C_QW), const),
            pl.BlockSpec((C_KV_RANK, C_HEADS * C_NOPE), const),
            pl.BlockSpec((C_KV_RANK, C_HEADS * C_V), const),
            pl.BlockSpec((3, tm, LANES), lambda i: (0, i % nt, 0)),
        ],
        out_specs=[
            pl.BlockSpec((tm, C_HEADS * C_QW), lambda i: (i, 0)),
            pl.BlockSpec((tm, C_HEADS * C_NOPE), lambda i: (i, 0)),
            pl.BlockSpec((tm, LANES), lambda i: (i, 0)),
            pl.BlockSpec((tm, C_HEADS * C_V), lambda i: (i, 0)),
        ],
        out_shape=[
            jax.ShapeDtypeStruct((m, C_HEADS * C_QW), BF16),
            jax.ShapeDtypeStruct((m, C_HEADS * C_NOPE), BF16),
            jax.ShapeDtypeStruct((m, LANES), BF16),
            jax.ShapeDtypeStruct((m, C_HEADS * C_V), BF16),
        ],
        compiler_params=_cparams("parallel"),
        name="mla_project",
    )(z, z, z, q_a_norm.reshape(1, -1), kv_a_norm.reshape(1, -1), wq_p, wk_p, wv_p, rope_tab)


def _mla_flash_kernel(q_ref, kn_ref, kr_ref, v_ref, o_ref, m_ref, l_ref, acc_ref):
    tq = m_ref.shape[0]
    tk = tq // 2

    def q_tile(qi, carry):
        q0 = pl.multiple_of(qi * tq, tq)
        m_ref[...] = jnp.full_like(m_ref, NEG)
        l_ref[...] = jnp.zeros_like(l_ref)
        acc_ref[...] = jnp.zeros_like(acc_ref)

        def step(r0, nr, k_start, nk, mask_offset=None):
            rows = slice(r0, r0 + nr)
            ks = pl.ds(pl.multiple_of(k_start, tk), nk)
            k = jnp.concatenate([kn_ref[ks, :], kr_ref[ks, :]], axis=1)
            q = q_ref[pl.ds(pl.multiple_of(q0 + r0, tk), nr), :]
            s = lax.dot_general(q, k, (((1,), (1,)), ((), ())), preferred_element_type=F32)
            if mask_offset is not None:
                visible = (lax.broadcasted_iota(jnp.int32, s.shape, 1)
                           <= lax.broadcasted_iota(jnp.int32, s.shape, 0) + mask_offset)
                s = jnp.where(visible, s, NEG)
            m_prev = m_ref[rows, :]
            m_new = jnp.maximum(m_prev, jnp.max(s, axis=-1, keepdims=True))
            alpha = jnp.exp2(m_prev - m_new)
            p = jnp.exp2(s - jnp.concatenate([m_new] * (nk // LANES), axis=1))
            l_ref[rows, :] = alpha * l_ref[rows, :] + jnp.sum(p, axis=-1, keepdims=True)
            acc_ref[rows, :] = alpha * acc_ref[rows, :] + jnp.dot(p.astype(BF16), v_ref[ks, :],
                                                                  preferred_element_type=F32)
            m_ref[rows, :] = m_new

        def body(kb, c):
            for u in range(4):
                step(0, tq, (4 * kb + u) * tk, tk)
            return c

        lax.fori_loop(0, qi // 2, body, 0)

        @pl.when(qi % 2 == 1)
        def _():
            step(0, tq, (2 * qi - 2) * tk, tk)
            step(0, tq, (2 * qi - 1) * tk, tk)
        step(0, tk, q0, tk, 0)
        step(tk, tk, q0, tq, tk)
        o_ref[pl.ds(q0, tq), :] = (acc_ref[...] / l_ref[...]).astype(o_ref.dtype)
        return carry

    lax.fori_loop(0, q_ref.shape[0] // tq, q_tile, 0)


def mla_flash(q, kn, kr, v, *, batch, seq, tq):
    return pl.pallas_call(
        _mla_flash_kernel,
        grid=(batch, C_HEADS),
        in_specs=[
            pl.BlockSpec((seq, C_QW), lambda b, h: (b, h)),
            pl.BlockSpec((seq, C_NOPE), lambda b, h: (b, h)),
            pl.BlockSpec((seq, LANES), lambda b, h: (b, 0)),
            pl.BlockSpec((seq, C_V), lambda b, h: (b, h)),
        ],
        out_specs=pl.BlockSpec((seq, C_V), lambda b, h: (b, h)),
        out_shape=jax.ShapeDtypeStruct((batch * seq, C_HEADS * C_V), BF16),
        scratch_shapes=[
            pltpu.VMEM((tq, LANES), F32),
            pltpu.VMEM((tq, LANES), F32),
            pltpu.VMEM((tq, C_V), F32),
        ],
        compiler_params=_cparams("parallel", "parallel"),
        name="mla_flash",
    )(q, kn, kr, v)


def _mixer_out_kernel(oa_ref, ob_ref, oc_ref, wa_ref, wb_ref, wc_ref, g0_ref, g1_ref, g2_ref,
                      b0_ref, b1_ref, b2_ref, wo_ref, xr_ref, gn_ref, o_ref, xg_ref, inv_ref, mg_ref, ss_ref,
                      *, n_rows):
    i = pl.program_id(0)
    j = pl.program_id(1)
    _, nj, _, tn = mg_ref.shape
    cur = lax.rem(i, 2)

    def project():
        acc = xr_ref[...]
        for c in range(nj):
            acc = acc + jnp.dot(mg_ref[1 - cur, c], wo_ref[c * tn:(c + 1) * tn, :], preferred_element_type=F32)
        o_ref[...] = acc
        _next_norm_tile(acc, gn_ref, xg_ref, ss_ref)

    def branch(o, w, g, b):
        return jax.nn.sigmoid(g[...] + b[...]) * jnp.dot(o[...], w[...], preferred_element_type=F32)

    def merge():
        acc = branch(oa_ref, wa_ref, g0_ref, b0_ref)
        acc = acc + branch(ob_ref, wb_ref, g1_ref, b1_ref)
        acc = acc + branch(oc_ref, wc_ref, g2_ref, b2_ref)
        mg_ref[cur, j] = acc.astype(mg_ref.dtype)

    @pl.when(j == 0)
    def _():
        ss_ref[...] = jnp.zeros_like(ss_ref)

    @pl.when(jnp.logical_and(i > 0, i < n_rows))
    def _():
        project()
        merge()

    @pl.when(i == 0)
    def _():
        merge()

    @pl.when(i == n_rows)
    def _():
        project()

    @pl.when(jnp.logical_and(i > 0, j == nj - 1))
    def _():
        _next_norm_finish(inv_ref, ss_ref, nj * tn)


def _pipelined_row_maps(n_rows, ncols):
    row1 = lambda i: jnp.minimum(i, n_rows - 1)
    col1 = lambda i, j: jnp.where(i < n_rows, j, ncols - 1)
    row2 = lambda i: jnp.maximum(i - 1, 0)
    col2 = lambda i, j: jnp.where(i > 0, j, 0)
    return row1, col1, row2, col2


def mixer_output_block(x, o_a, o_b, o_c, w_a, w_b, w_c, gates, b_gate, w_o, next_gain, *, tm, tn):
    m, k = o_a.shape
    d = w_a.shape[1]
    nj = d // tn
    n_rows = m // tm
    mrow, mcol, orow, ocol = _pipelined_row_maps(n_rows, nj)
    act = pl.BlockSpec((tm, k), lambda i, j: (mrow(i), 0))
    wgt = pl.BlockSpec((k, tn), lambda i, j: (0, mcol(i, j)))
    gate = lambda br: pl.BlockSpec((tm, tn), lambda i, j: (mrow(i), br * nj + mcol(i, j)))
    gbias = lambda br: pl.BlockSpec((1, tn), lambda i, j: (0, br * nj + j))
    out_tile = pl.BlockSpec((tm, tn), lambda i, j: (orow(i), ocol(i, j)))
    return pl.pallas_call(
        functools.partial(_mixer_out_kernel, n_rows=n_rows),
        grid=(n_rows + 1, nj),
        in_specs=[act, act, act, wgt, wgt, wgt, gate(0), gate(1), gate(2), gbias(0), gbias(1), gbias(2),
                  pl.BlockSpec((d, tn), lambda i, j: (0, ocol(i, j))),
                  out_tile,
                  pl.BlockSpec((1, tn), lambda i, j: (0, ocol(i, j)))],
        out_specs=[out_tile, out_tile, pl.BlockSpec((tm, LANES), lambda i, j: (orow(i), 0))],
        out_shape=[jax.ShapeDtypeStruct((m, d), F32), jax.ShapeDtypeStruct((m, d), BF16),
                   jax.ShapeDtypeStruct((m, LANES), F32)],
        scratch_shapes=[pltpu.VMEM((2, nj, tm, tn), BF16), pltpu.VMEM((tm, LANES), F32)],
        compiler_params=_cparams("arbitrary", "arbitrary"),
        name="mixer_output_block",
    )(o_a, o_b, o_c, w_a, w_b, w_c, gates, gates, gates, *([b_gate.reshape(1, -1)] * 3), w_o, x,
      next_gain.reshape(1, d))


def _cross_block_kernel(xg_ref, inv_ref, wq_ref, k_ref, v_ref, wo_ref, xr_ref, gn_ref,
                        o_ref, xgo_ref, invo_ref, ao_ref, ss_ref, *, scale, n_rows):
    i = pl.program_id(0)
    j = pl.program_id(1)
    hd = k_ref.shape[1]
    cur = lax.rem(i, 2)

    def project():
        acc = xr_ref[...]
        for h in range(X_HEADS):
            acc = acc + jnp.dot(ao_ref[1 - cur, h], wo_ref[h * hd:(h + 1) * hd, :], preferred_element_type=F32)
        o_ref[...] = acc
        _next_norm_tile(acc, gn_ref, xgo_ref, ss_ref)

    def attend():
        q = jnp.dot(xg_ref[...], wq_ref[...], preferred_element_type=F32) * (_inv_cols(inv_ref, hd) * scale)
        s = lax.dot_general(q.astype(BF16), k_ref[...], (((1,), (1,)), ((), ())), preferred_element_type=F32)
        m = jnp.max(s, axis=-1, keepdims=True)
        e = jnp.exp(s - m)
        l = jnp.sum(e, axis=-1, keepdims=True)
        o = jnp.dot(e.astype(BF16), v_ref[...], preferred_element_type=F32) / l
        ao_ref[cur, j] = o.astype(ao_ref.dtype)

    @pl.when(j == 0)
    def _():
        ss_ref[...] = jnp.zeros_like(ss_ref)

    @pl.when(jnp.logical_and(i > 0, i < n_rows))
    def _():
        project()
        attend()

    @pl.when(i == 0)
    def _():
        attend()

    @pl.when(i == n_rows)
    def _():
        project()

    @pl.when(jnp.logical_and(i > 0, j == X_HEADS - 1))
    def _():
        _next_norm_finish(invo_ref, ss_ref, X_HEADS * hd)


def cross_attention_block(x, xg, inv, w_q, kv, w_o, next_gain, *, batch, seq, mem_len, tm):
    m, d = x.shape
    hd = d // X_HEADS
    tiles_per_batch = seq // tm
    n_rows = m // tm
    arow, head, orow, ocol = _pipelined_row_maps(n_rows, X_HEADS)
    out_tile = pl.BlockSpec((tm, hd), lambda i, j: (orow(i), ocol(i, j)))
    return pl.pallas_call(
        functools.partial(_cross_block_kernel, scale=hd ** -0.5, n_rows=n_rows),
        grid=(n_rows + 1, X_HEADS),
        in_specs=[
            pl.BlockSpec((tm, d), lambda i, j: (arow(i), 0)),
            pl.BlockSpec((tm, LANES), lambda i, j: (arow(i), 0)),
            pl.BlockSpec((d, hd), lambda i, j: (0, head(i, j))),
            pl.BlockSpec((mem_len, hd), lambda i, j: (arow(i) // tiles_per_batch, head(i, j))),
            pl.BlockSpec((mem_len, hd), lambda i, j: (arow(i) // tiles_per_batch, X_HEADS + head(i, j))),
            pl.BlockSpec((d, hd), lambda i, j: (0, ocol(i, j))),
            out_tile,
            pl.BlockSpec((1, hd), lambda i, j: (0, ocol(i, j))),
        ],
        out_specs=[out_tile, out_tile, pl.BlockSpec((tm, LANES), lambda i, j: (orow(i), 0))],
        out_shape=[jax.ShapeDtypeStruct((m, d), F32), jax.ShapeDtypeStruct((m, d), BF16),
                   jax.ShapeDtypeStruct((m, LANES), F32)],
        scratch_shapes=[pltpu.VMEM((2, X_HEADS, tm, hd), BF16), pltpu.VMEM((tm, LANES), F32)],
        compiler_params=_cparams("arbitrary", "arbitrary"),
        name="cross_attention_block",
    )(xg, inv, w_q, kv, kv, w_o, x, next_gain.reshape(1, d))


def _pack_w_in(w):
    nl, d, _ = w.shape
    sizes = (1024, 1024, 1024, B_DK, B_DK, B_DV, B_GATE_RANK, B_DV, C_Q_RANK, C_KV_RANK, C_ROPE, N_BRANCH * d)
    offs = np.concatenate([[0], np.cumsum(sizes)])
    aq, ak, av, bq, bk, bv, lr, br, cqa, ckva, ckr, gates = [
        w[:, :, offs[i]:offs[i + 1]].astype(BF16) for i in range(len(sizes))]
    zeros = lambda n: jnp.zeros((nl, d, n), BF16)
    used = Z_CKR + LANES
    wz = jnp.concatenate(
        [aq, ak, av, bv, br, cqa, ckva, bq, bk,
         lr, zeros(LANES - B_GATE_RANK), ckr, zeros(LANES - C_ROPE), zeros(Z_WIDTH - used)], axis=2)
    return wz, gates


def _pack_w_qb(w):
    r = w.shape[0]
    w = w.reshape(r, C_HEADS, C_NOPE + C_ROPE)
    w = jnp.pad(w, ((0, 0), (0, 0), (0, C_QW - C_NOPE - C_ROPE)))
    return w.reshape(r, C_HEADS * C_QW).astype(BF16)


def _pack_w_kvb(w):
    r = w.shape[0]
    w = w.reshape(r, C_HEADS, C_NOPE + C_V)
    wk = w[:, :, :C_NOPE].reshape(r, C_HEADS * C_NOPE)
    wv = w[:, :, C_NOPE:].reshape(r, C_HEADS * C_V)
    return wk.astype(BF16), wv.astype(BF16)


def kernel(x, mem, rel_bias, norm_mix, w_in, b_gate, w_alpha, b_alpha, gla_norm, q_a_norm, w_qb, kv_a_norm, w_kvb, w_up_a, w_up_b, w_up_c, w_o, norm_x, norm_mem, w_xq, w_xkv, w_xo, norm_ffn, w_ffn_gate, w_ffn_up, w_ffn_down, norm_final):
    batch, seq, d = x.shape
    mem_len = mem.shape[1]
    depth = w_in.shape[0]
    t = batch * seq
    assert (batch, seq, d, mem_len) == (4, 4096, 2048, 256), "tile sizes are derived for the stated shapes"
    assert d // X_HEADS == TN and t % SWIGLU_TM == 0 and seq % A_TILE == 0

    xf = x.reshape(t, d)
    memf = mem.reshape(batch * mem_len, d)
    bias_tab = dilated_bias_table(rel_bias)
    rope_tab = _rope_tables(seq)
    bf = lambda a: a.astype(BF16)
    w_in_z, w_in_g = _pack_w_in(w_in)

    xg, inv = prenorm(xf, norm_mix[0], tm=TM)
    for l in range(depth):
        z = scaled_matmul(xg, inv, w_in_z, out_dtype=F32, tm=TM, tn=W_Z_TN, layer=l)
        gates = scaled_matmul(xg, inv, w_in_g, out_dtype=BF16, tm=TM, tn=W_G_TN, layer=l)
        o_a = dilated_attention(z, bias_tab, batch=batch, seq=seq)
        w_alpha_p = jnp.pad(w_alpha[l], ((0, LANES - B_GATE_RANK), (0, 0)))
        o_b = gated_linear_attention(z, w_alpha_p, b_alpha[l], gla_norm[l], batch=batch, seq=seq, tc=GLA_TC)
        wk_p, wv_p = _pack_w_kvb(w_kvb[l])
        cq, ckn, ckr, cv = mla_project(z, q_a_norm[l], kv_a_norm[l], _pack_w_qb(w_qb[l]), wk_p, wv_p,
                                       rope_tab, seq=seq, tm=TM)
        o_c = mla_flash(cq, ckn, ckr, cv, batch=batch, seq=seq, tq=FLASH_TQ)
        xf, xg, inv = mixer_output_block(xf, o_a, o_b, o_c, bf(w_up_a[l]), bf(w_up_b[l]), bf(w_up_c[l]), gates,
                                         b_gate[l], bf(w_o[l]), norm_x[l], tm=TM, tn=TN)
        xkv = norm_matmul(memf, norm_mem[l], w_xkv, out_dtype=BF16, tm=batch * mem_len, tn=TN, layer=l)
        xf, xg, inv = cross_attention_block(xf, xg, inv, bf(w_xq[l]), xkv, bf(w_xo[l]), norm_ffn[l],
                                            batch=batch, seq=seq, mem_len=mem_len, tm=TM)
        act = scaled_swiglu(xg, inv, w_ffn_gate, w_ffn_up, tm=SWIGLU_TM, tn=TN, layer=l)
        if l + 1 < depth:
            xf, xg, inv = matmul_residual(act, bf(w_ffn_down[l]), xf, tm=TM, tn=TN, next_gain=norm_mix[l + 1])
        else:
            xf = matmul_residual(act, bf(w_ffn_down[l]), xf, tm=TM, tn=TN)
    return rmsnorm_rows(xf, norm_final, tm=TM).reshape(batch, seq, d)
```

```python
import functools
import math

import numpy as np
import jax
import jax.numpy as jnp
from jax import lax
from jax.experimental import pallas as pl
from jax.experimental.pallas import tpu as pltpu

F32 = jnp.float32
BF16 = jnp.bfloat16
EPS = 1e-6
NEG = -1e30
LOG2E = math.log2(math.e)

HEAD_DIM = 128
A_HEADS = 8
A_DILATIONS = (1, 4, 16)
A_BLOCK = 128
N_BUCKETS = 32
MAX_DISTANCE = 2048
B_HEADS = 4
B_DK = 512
B_DV = 1024
B_GATE_RANK = 16
B_GATE_TAU = 16.0
B_CHUNK = 64
C_HEADS = 8
C_Q_RANK = 512
C_KV_RANK = 512
C_NOPE = 128
C_ROPE = 64
C_V = 128
ROPE_THETA = 10000.0
X_HEADS = 4
N_BRANCH = 3

VMEM_LIMIT_BYTES = 56 * 1024 * 1024
LANES = 128

Z_AQ, Z_AK, Z_AV = 0, 1024, 2048
Z_BV, Z_BR = 3072, 4096
Z_CQA, Z_CKVA = 5120, 5632
Z_BQ, Z_BK = 6144, 6656
Z_LR = 7168
Z_CKR = 7296
Z_WIDTH = 7680
G_WIDTH = 3 * 2048
A_TILE = 2048

TM = 1024
W_Z_TN = 1536
W_G_TN = 2048
TN = 512
SWIGLU_TM = 2048
GLA_TC = 1024
FLASH_TQ = 1024


def _cparams(*sem):
    return pltpu.CompilerParams(dimension_semantics=sem, vmem_limit_bytes=VMEM_LIMIT_BYTES)


def _rmsnorm_rows(x_ref, g_ref, h_ref):
    rows = x_ref.shape[0]

    def body(i, carry):
        r = pl.multiple_of(i * 16, 16)
        x = x_ref[pl.ds(r, 16), :]
        ms = jnp.mean(x * x, axis=-1, keepdims=True)
        h_ref[pl.ds(r, 16), :] = (x * lax.rsqrt(ms + EPS) * g_ref[...]).astype(BF16)
        return carry

    lax.fori_loop(0, rows // 16, body, 0, unroll=8)


def _norm_matmul_kernel(x_ref, g_ref, w_ref, o_ref, h_ref, *, scale):
    @pl.when(pl.program_id(1) == 0)
    def _():
        _rmsnorm_rows(x_ref, g_ref, h_ref)

    acc = jnp.dot(h_ref[...], w_ref[...].astype(BF16), preferred_element_type=F32)
    if scale != 1.0:
        acc = acc * scale
    o_ref[...] = acc.astype(o_ref.dtype)


def _weight_spec(w, layer, tn):
    if w.ndim == 2:
        return pl.BlockSpec((w.shape[0], tn), lambda i, j: (0, j))
    return pl.BlockSpec((None, w.shape[1], tn), lambda i, j: (layer, 0, j))


def norm_matmul(x, g, w, *, out_dtype, tm, tn, scale=1.0, layer=None):
    m, k = x.shape
    n = w.shape[-1]
    return pl.pallas_call(
        functools.partial(_norm_matmul_kernel, scale=scale),
        grid=(m // tm, n // tn),
        in_specs=[
            pl.BlockSpec((tm, k), lambda i, j: (i, 0)),
            pl.BlockSpec((1, k), lambda i, j: (0, 0)),
            _weight_spec(w, layer, tn),
        ],
        out_specs=pl.BlockSpec((tm, tn), lambda i, j: (i, j)),
        out_shape=jax.ShapeDtypeStruct((m, n), out_dtype),
        scratch_shapes=[pltpu.VMEM((tm, k), BF16)],
        compiler_params=_cparams("parallel", "arbitrary"),
        name="norm_matmul",
    )(x, g.reshape(1, k), w)


def _inv_cols(inv_ref, n):
    inv = inv_ref[...]
    return jnp.concatenate([inv] * (n // LANES), axis=1)


def _next_norm_tile(o, g_ref, xg_ref, ss_ref):
    xg_ref[...] = (o * g_ref[...]).astype(xg_ref.dtype)
    ss_ref[...] = ss_ref[...] + jnp.sum(o * o, axis=-1, keepdims=True)


def _next_norm_finish(inv_ref, ss_ref, d):
    inv_ref[...] = lax.rsqrt(ss_ref[...] * (1.0 / d) + EPS)


def _prenorm_kernel(x_ref, g_ref, xg_ref, inv_ref):
    x = x_ref[...]
    xg_ref[...] = (x * g_ref[...]).astype(xg_ref.dtype)
    ms = jnp.mean(x * x, axis=-1, keepdims=True)
    inv_ref[...] = jnp.broadcast_to(lax.rsqrt(ms + EPS), inv_ref.shape)


def prenorm(x, g, *, tm):
    m, k = x.shape
    return pl.pallas_call(
        _prenorm_kernel,
        grid=(m // tm,),
        in_specs=[pl.BlockSpec((tm, k), lambda i: (i, 0)), pl.BlockSpec((1, k), lambda i: (0, 0))],
        out_specs=[pl.BlockSpec((tm, k), lambda i: (i, 0)), pl.BlockSpec((tm, LANES), lambda i: (i, 0))],
        out_shape=[jax.ShapeDtypeStruct((m, k), BF16), jax.ShapeDtypeStruct((m, LANES), F32)],
        compiler_params=_cparams("parallel"),
        name="prenorm",
    )(x, g.reshape(1, k))


def _scaled_matmul_kernel(xg_ref, inv_ref, w_ref, o_ref):
    acc = jnp.dot(xg_ref[...], w_ref[...].astype(BF16), preferred_element_type=F32)
    o_ref[...] = (acc * _inv_cols(inv_ref, acc.shape[1])).astype(o_ref.dtype)


def scaled_matmul(xg, inv, w, *, out_dtype, tm, tn, layer=None):
    m, k = xg.shape
    n = w.shape[-1]
    return pl.pallas_call(
        _scaled_matmul_kernel,
        grid=(m // tm, n // tn),
        in_specs=[
            pl.BlockSpec((tm, k), lambda i, j: (i, 0)),
            pl.BlockSpec((tm, LANES), lambda i, j: (i, 0)),
            _weight_spec(w, layer, tn),
        ],
        out_specs=pl.BlockSpec((tm, tn), lambda i, j: (i, j)),
        out_shape=jax.ShapeDtypeStruct((m, n), out_dtype),
        compiler_params=_cparams("parallel", "parallel"),
        name="scaled_matmul",
    )(xg, inv, w)


def _scaled_swiglu_kernel(xg_ref, inv_ref, wg_ref, wu_ref, o_ref):
    wg = wg_ref[...].astype(BF16)
    wu = wu_ref[...].astype(BF16)
    half = xg_ref.shape[0] // 2
    for r in range(2):
        rows = slice(r * half, (r + 1) * half)
        xg = xg_ref[rows, :]
        inv = jnp.concatenate([inv_ref[rows, :]] * (o_ref.shape[1] // LANES), axis=1)
        a = jnp.dot(xg, wg, preferred_element_type=F32) * inv
        b = jnp.dot(xg, wu, preferred_element_type=F32) * inv
        o_ref[rows, :] = (a * jax.nn.sigmoid(a) * b).astype(o_ref.dtype)


def scaled_swiglu(xg, inv, wg, wu, *, tm, tn, layer=None):
    m, k = xg.shape
    n = wg.shape[-1]
    return pl.pallas_call(
        _scaled_swiglu_kernel,
        grid=(m // tm, n // tn),
        in_specs=[
            pl.BlockSpec((tm, k), lambda i, j: (i, 0)),
            pl.BlockSpec((tm, LANES), lambda i, j: (i, 0)),
            _weight_spec(wg, layer, tn),
            _weight_spec(wu, layer, tn),
        ],
        out_specs=pl.BlockSpec((tm, tn), lambda i, j: (i, j)),
        out_shape=jax.ShapeDtypeStruct((m, n), BF16),
        compiler_params=_cparams("parallel", "parallel"),
        name="scaled_swiglu",
    )(xg, inv, wg, wu)


def _matmul_residual_kernel(a_ref, w_ref, r_ref, o_ref):
    o_ref[...] = r_ref[...] + jnp.dot(a_ref[...], w_ref[...].astype(BF16), preferred_element_type=F32)


def _matmul_residual_norm_kernel(a_ref, w_ref, r_ref, g_ref, o_ref, xg_ref, inv_ref, ss_ref, *, ncols):
    j = pl.program_id(1)

    @pl.when(j == 0)
    def _():
        ss_ref[...] = jnp.zeros_like(ss_ref)

    o = r_ref[...] + jnp.dot(a_ref[...], w_ref[...].astype(BF16), preferred_element_type=F32)
    o_ref[...] = o
    _next_norm_tile(o, g_ref, xg_ref, ss_ref)

    @pl.when(j == ncols - 1)
    def _():
        _next_norm_finish(inv_ref, ss_ref, ncols * o.shape[1])


def matmul_residual(a, w, res, *, tm, tn, layer=None, next_gain=None):
    m, k = a.shape
    n = w.shape[-1]
    tile = pl.BlockSpec((tm, tn), lambda i, j: (i, j))
    in_specs = [pl.BlockSpec((tm, k), lambda i, j: (i, 0)), _weight_spec(w, layer, tn), tile]
    if next_gain is None:
        return pl.pallas_call(
            _matmul_residual_kernel,
            grid=(m // tm, n // tn),
            in_specs=in_specs,
            out_specs=tile,
            out_shape=jax.ShapeDtypeStruct((m, n), F32),
            compiler_params=_cparams("parallel", "parallel"),
            name="matmul_residual",
        )(a, w, res)
    return pl.pallas_call(
        functools.partial(_matmul_residual_norm_kernel, ncols=n // tn),
        grid=(m // tm, n // tn),
        in_specs=in_specs + [pl.BlockSpec((1, tn), lambda i, j: (0, j))],
        out_specs=[tile, tile, pl.BlockSpec((tm, LANES), lambda i, j: (i, 0))],
        out_shape=[jax.ShapeDtypeStruct((m, n), F32), jax.ShapeDtypeStruct((m, n), BF16),
                   jax.ShapeDtypeStruct((m, LANES), F32)],
        scratch_shapes=[pltpu.VMEM((tm, LANES), F32)],
        compiler_params=_cparams("parallel", "arbitrary"),
        name="matmul_residual_norm",
    )(a, w, res, next_gain.reshape(1, n))


def _rmsnorm_kernel(x_ref, g_ref, o_ref):
    x = x_ref[...]
    ms = jnp.mean(x * x, axis=-1, keepdims=True)
    o_ref[...] = x * lax.rsqrt(ms + EPS) * g_ref[...]


def rmsnorm_rows(x, g, *, tm):
    m, k = x.shape
    return pl.pallas_call(
        _rmsnorm_kernel,
        grid=(m // tm,),
        in_specs=[pl.BlockSpec((tm, k), lambda i: (i, 0)), pl.BlockSpec((1, k), lambda i: (0, 0))],
        out_specs=pl.BlockSpec((tm, k), lambda i: (i, 0)),
        out_shape=jax.ShapeDtypeStruct((m, k), F32),
        compiler_params=_cparams("parallel"),
        name="final_rmsnorm",
    )(x, g.reshape(1, k))


def _t5_bucket_np(dist):
    max_exact = N_BUCKETS // 2
    n = np.maximum(dist, 1).astype(np.float64)
    large = max_exact + (np.log(n / max_exact) / math.log(MAX_DISTANCE / max_exact)
                         * (N_BUCKETS - max_exact)).astype(np.int32)
    large = np.minimum(large, N_BUCKETS - 1)
    return np.where(dist < max_exact, dist, large).astype(np.int32)


def _dilated_bucket_table():
    qi = np.arange(A_BLOCK)[:, None]
    kj = np.arange(2 * A_BLOCK)[None, :]
    steps_back = qi + A_BLOCK - kj
    valid = (steps_back >= 0) & (steps_back <= A_BLOCK)
    tabs = []
    for dil in A_DILATIONS:
        bucket = _t5_bucket_np(np.clip(steps_back, 0, A_BLOCK) * dil)
        tabs.append(np.where(valid, bucket, -1))
    return np.stack(tabs).astype(np.int32)


def _bias_table_kernel(idx_ref, rb_ref, o_ref):
    h = pl.program_id(1)
    idx = idx_ref[...]
    acc = jnp.full(idx.shape, NEG, F32)
    for b in range(N_BUCKETS):
        acc = jnp.where(idx == b, rb_ref[b, h] * LOG2E, acc)
    o_ref[...] = acc


def dilated_bias_table(rel_bias):
    idx = jnp.asarray(_dilated_bucket_table())
    npat = len(A_DILATIONS)
    return pl.pallas_call(
        _bias_table_kernel,
        grid=(npat, A_HEADS),
        in_specs=[
            pl.BlockSpec((None, A_BLOCK, 2 * A_BLOCK), lambda p, h: (p, 0, 0)),
            pl.BlockSpec(memory_space=pltpu.SMEM),
        ],
        out_specs=pl.BlockSpec((None, None, A_BLOCK, 2 * A_BLOCK), lambda p, h: (p, h, 0, 0)),
        out_shape=jax.ShapeDtypeStruct((npat, A_HEADS, A_BLOCK, 2 * A_BLOCK), F32),
        compiler_params=_cparams("parallel", "parallel"),
        name="dilated_bias_table",
    )(idx, rel_bias)


def _dilated_kernel(q_ref, kc_ref, kp_ref, vc_ref, vp_ref, bias_ref, o_ref, bm0, o_scr, lse_scr, *, scale):
    t = pl.program_id(2)
    tile = q_ref.shape[0]
    blk = A_BLOCK
    col = lax.broadcasted_iota(jnp.int32, (blk, 2 * blk), 1)
    no_prev = jnp.logical_and(col < blk, t == 0)
    for p in range(len(A_DILATIONS)):
        bm0[p] = jnp.where(no_prev, NEG, bias_ref[p])

    def rows(start, size, dil):
        return pl.ds(start, size) if dil == 1 else pl.ds(start, size, stride=dil)

    def block(p, dil, r, n):
        q_idx = rows(r + dil * blk * n, blk, dil)
        if n == 0:
            prev = rows(tile - dil * blk + r, blk, dil)
            cur = rows(r, blk, dil)
            k = jnp.concatenate([kp_ref[prev, :], kc_ref[cur, :]], axis=0)
            v = jnp.concatenate([vp_ref[prev, :], vc_ref[cur, :]], axis=0)
            bm = bm0[p]
        else:
            k_idx = rows(r + dil * blk * (n - 1), 2 * blk, dil)
            k, v = kc_ref[k_idx, :], vc_ref[k_idx, :]
            bm = bias_ref[p]
        q = (q_ref[q_idx, :] * scale).astype(BF16)
        s = lax.dot_general(q, k.astype(BF16), (((1,), (1,)), ((), ())), preferred_element_type=F32)
        s = jnp.where(bm > 0.5 * NEG, s + bm, NEG)
        m = jnp.max(s, axis=-1, keepdims=True)
        e = jnp.exp2(s - m)
        l = jnp.sum(e, axis=-1, keepdims=True)
        o = jnp.dot(e.astype(BF16), v.astype(BF16), preferred_element_type=F32) / l
        lse = m + jnp.log(l) * LOG2E
        o_scr[p, q_idx, :] = o
        lse_scr[p, q_idx, :] = jnp.broadcast_to(lse, (blk, HEAD_DIM))

    for p, dil in enumerate(A_DILATIONS):
        for r in range(dil):
            for n in range(tile // (blk * dil)):
                block(p, dil, r, n)

    def merge(i, carry):
        r = pl.multiple_of(i * blk, blk)
        rows = pl.ds(r, blk)
        l0, l1, l2 = lse_scr[0, rows, :], lse_scr[1, rows, :], lse_scr[2, rows, :]
        mx = jnp.maximum(jnp.maximum(l0, l1), l2)
        w0, w1, w2 = jnp.exp2(l0 - mx), jnp.exp2(l1 - mx), jnp.exp2(l2 - mx)
        num = w0 * o_scr[0, rows, :] + w1 * o_scr[1, rows, :] + w2 * o_scr[2, rows, :]
        o_ref[rows, :] = (num / (w0 + w1 + w2)).astype(o_ref.dtype)
        return carry

    lax.fori_loop(0, tile // blk, merge, 0)


def dilated_attention(z, bias_tab, *, batch, seq):
    z3 = z.reshape(batch, seq, z.shape[1])
    nt = seq // A_TILE
    hq, hk, hv = Z_AQ // HEAD_DIM, Z_AK // HEAD_DIM, Z_AV // HEAD_DIM
    tile_spec = lambda col0, prev: pl.BlockSpec(
        (None, A_TILE, HEAD_DIM),
        (lambda b, h, t: (b, jnp.maximum(t - 1, 0), col0 + h)) if prev else (lambda b, h, t: (b, t, col0 + h)))
    npat = len(A_DILATIONS)
    out = pl.pallas_call(
        functools.partial(_dilated_kernel, scale=HEAD_DIM ** -0.5 * LOG2E),
        grid=(batch, A_HEADS, nt),
        in_specs=[
            tile_spec(hq, False),
            tile_spec(hk, False), tile_spec(hk, True),
            tile_spec(hv, False), tile_spec(hv, True),
            pl.BlockSpec((npat, None, A_BLOCK, 2 * A_BLOCK), lambda b, h, t: (0, h, 0, 0)),
        ],
        out_specs=pl.BlockSpec((None, A_TILE, HEAD_DIM), lambda b, h, t: (b, t, h)),
        out_shape=jax.ShapeDtypeStruct((batch, seq, A_HEADS * HEAD_DIM), BF16),
        scratch_shapes=[
            pltpu.VMEM((npat, A_BLOCK, 2 * A_BLOCK), F32),
            pltpu.VMEM((npat, A_TILE, HEAD_DIM), F32),
            pltpu.VMEM((npat, A_TILE, HEAD_DIM), F32),
        ],
        compiler_params=_cparams("parallel", "parallel", "arbitrary"),
        name="dilated_attention",
    )(z3, z3, z3, z3, z3, bias_tab)
    return out.reshape(batch * seq, A_HEADS * HEAD_DIM)


def _split_bf16(a):
    hi = a.astype(BF16)
    return hi, (a - hi.astype(F32)).astype(BF16)


def _gla_kernel(q_ref, k_ref, v_ref, r_ref, lr_ref, wa_ref, ba_ref, gn_ref, o_ref, st_ref, *, scale):
    @pl.when(pl.program_id(1) == 0)
    def _():
        st_ref[...] = jnp.zeros_like(st_ref)

    c = B_CHUNK
    tc = q_ref.shape[0]
    dk = B_DK // B_HEADS
    dv = B_DV // B_HEADS
    dot = functools.partial(jnp.dot, preferred_element_type=F32)
    lr_hi, lr_lo = _split_bf16(lr_ref[...])
    wa_hi, wa_lo = _split_bf16(wa_ref[...])
    pre = dot(lr_hi, wa_hi) + dot(lr_hi, wa_lo) + dot(lr_lo, wa_hi) + ba_ref[...]
    log_a = (jnp.minimum(pre, 0.0) - jnp.log(1.0 + jnp.exp(-jnp.abs(pre)))) * (1.0 / B_GATE_TAU)
    tril = lax.broadcasted_iota(jnp.int32, (c, c), 0) >= lax.broadcasted_iota(jnp.int32, (c, c), 1)
    ones_tril = jnp.where(tril, 1.0, 0.0).astype(BF16)
    for ci in range(tc // c):
        sl = slice(ci * c, (ci + 1) * c)
        la_hi, la_lo = _split_bf16(log_a[sl, :])
        cum_all = dot(ones_tril, la_hi) + dot(ones_tril, la_lo)
        for h in range(B_HEADS):
            kcols = slice(h * dk, (h + 1) * dk)
            vcols = slice(h * dv, (h + 1) * dv)
            cum = cum_all[:, kcols]
            cum_last = cum[c - 1:c, :]
            q = q_ref[sl, kcols] * scale
            k = k_ref[sl, kcols]
            v = v_ref[sl, vcols]
            q_dec = (q * jnp.exp(cum)).astype(BF16)
            k_inv = (k * jnp.exp(-cum)).astype(BF16)
            k_state = (k * jnp.exp(cum_last - cum)).astype(BF16)
            att = lax.dot_general(q_dec, k_inv, (((1,), (1,)), ((), ())), preferred_element_type=F32)
            att = jnp.where(tril, att, 0.0).astype(BF16)
            st = st_ref[h]
            o = (jnp.dot(att, v.astype(BF16), preferred_element_type=F32)
                 + lax.dot_general(q_dec, st.astype(BF16), (((1,), (1,)), ((), ())), preferred_element_type=F32))
            st_ref[h] = st * jnp.exp(cum_last) + jnp.dot(v.T.astype(BF16), k_state, preferred_element_type=F32)
            ms = jnp.mean(o * o, axis=-1, keepdims=True)
            o = o * lax.rsqrt(ms + EPS) * gn_ref[:, vcols]
            r = r_ref[sl, vcols]
            o_ref[sl, vcols] = (o * (r * jax.nn.sigmoid(r))).astype(o_ref.dtype)


def gated_linear_attention(z, w_alpha_p, b_alpha, gla_norm, *, batch, seq, tc):
    dk = B_DK // B_HEADS
    dv = B_DV // B_HEADS
    nt = seq // tc
    row = lambda b, t: b * nt + t
    const = lambda b, t: (0, 0)
    return pl.pallas_call(
        functools.partial(_gla_kernel, scale=dk ** -0.5),
        grid=(batch, nt),
        in_specs=[
            pl.BlockSpec((tc, B_DK), lambda b, t: (row(b, t), Z_BQ // B_DK)),
            pl.BlockSpec((tc, B_DK), lambda b, t: (row(b, t), Z_BK // B_DK)),
            pl.BlockSpec((tc, B_DV), lambda b, t: (row(b, t), Z_BV // B_DV)),
            pl.BlockSpec((tc, B_DV), lambda b, t: (row(b, t), Z_BR // B_DV)),
            pl.BlockSpec((tc, LANES), lambda b, t: (row(b, t), Z_LR // LANES)),
            pl.BlockSpec((LANES, B_DK), const),
            pl.BlockSpec((1, B_DK), const),
            pl.BlockSpec((1, B_DV), const),
        ],
        out_specs=pl.BlockSpec((tc, B_DV), lambda b, t: (row(b, t), 0)),
        out_shape=jax.ShapeDtypeStruct((batch * seq, B_DV), BF16),
        scratch_shapes=[pltpu.VMEM((B_HEADS, dv, dk), F32)],
        compiler_params=_cparams("parallel", "arbitrary"),
        name="gated_linear_attention",
    )(z, z, z, z, z, w_alpha_p, b_alpha.reshape(1, B_DK), gla_norm.reshape(1, B_DV))


C_QW = 2 * C_NOPE


def _rope_tables(seq):
    pos = jnp.arange(seq, dtype=F32)
    inv = ROPE_THETA ** (-jnp.arange(0, C_ROPE, 2, dtype=F32) / C_ROPE)
    ang = pos[:, None] * inv[None, :]
    cos, sin = jnp.cos(ang), jnp.sin(ang)
    half = C_ROPE // 2
    z = lambda w: jnp.zeros((seq, w), F32)
    return jnp.stack([
        jnp.concatenate([cos, cos, z(LANES - C_ROPE)], axis=1),
        jnp.concatenate([-sin, z(half), z(LANES - C_ROPE)], axis=1),
        jnp.concatenate([z(half), sin, z(LANES - C_ROPE)], axis=1)])


def _mla_proj_kernel(cq_ref, ckv_ref, ckr_ref, gq_ref, gkv_ref, wq_ref, wk_ref, wv_ref, rt_ref,
                     q_out, kn_out, kr_out, v_out, *, scale):
    half = C_ROPE // 2
    t0, t1, t2 = rt_ref[0], rt_ref[1], rt_ref[2]

    def rms(x, g):
        ms = jnp.mean(x * x, axis=-1, keepdims=True)
        return (x * lax.rsqrt(ms + EPS) * g).astype(BF16)

    def rotary(x):
        return x * t0 + pltpu.roll(x, LANES - half, axis=1) * t1 + pltpu.roll(x, half, axis=1) * t2

    cq = rms(cq_ref[...], gq_ref[...])
    q = jnp.dot(cq, wq_ref[...], preferred_element_type=F32)
    for h in range(C_HEADS):
        nope = slice(h * C_QW, h * C_QW + C_NOPE)
        rope = slice(h * C_QW + C_NOPE, (h + 1) * C_QW)
        q_out[:, nope] = (q[:, nope] * scale).astype(q_out.dtype)
        q_out[:, rope] = (rotary(q[:, rope]) * scale).astype(q_out.dtype)
    ckv = rms(ckv_ref[...], gkv_ref[...])
    kn_out[...] = jnp.dot(ckv, wk_ref[...], preferred_element_type=F32).astype(kn_out.dtype)
    v_out[...] = jnp.dot(ckv, wv_ref[...], preferred_element_type=F32).astype(v_out.dtype)
    kr_out[...] = rotary(ckr_ref[...]).astype(kr_out.dtype)


def mla_project(z, q_a_norm, kv_a_norm, wq_p, wk_p, wv_p, rope_tab, *, seq, tm):
    m = z.shape[0]
    nt = seq // tm
    const = lambda i: (0, 0)
    return pl.pallas_call(
        functools.partial(_mla_proj_kernel, scale=(C_NOPE + C_ROPE) ** -0.5 * math.log2(math.e)),
        grid=(m // tm,),
        in_specs=[
            pl.BlockSpec((tm, C_Q_RANK), lambda i: (i, Z_CQA // C_Q_RANK)),
            pl.BlockSpec((tm, C_KV_RANK), lambda i: (i, Z_CKVA // C_KV_RANK)),
            pl.BlockSpec((tm, LANES), lambda i: (i, Z_CKR // LANES)),
            pl.BlockSpec((1, C_Q_RANK), const),
            pl.BlockSpec((1, C_KV_RANK), const),
            pl.BlockSpec((C_Q_RANK, C_HEADS * C_QW), const),
            pl.BlockSpec((C_KV_RANK, C_HEADS * C_NOPE), const),
            pl.BlockSpec((C_KV_RANK, C_HEADS * C_V), const),
            pl.BlockSpec((3, tm, LANES), lambda i: (0, i % nt, 0)),
        ],
        out_specs=[
            pl.BlockSpec((tm, C_HEADS * C_QW), lambda i: (i, 0)),
            pl.BlockSpec((tm, C_HEADS * C_NOPE), lambda i: (i, 0)),
            pl.BlockSpec((tm, LANES), lambda i: (i, 0)),
            pl.BlockSpec((tm, C_HEADS * C_V), lambda i: (i, 0)),
        ],
        out_shape=[
            jax.ShapeDtypeStruct((m, C_HEADS * C_QW), BF16),
            jax.ShapeDtypeStruct((m, C_HEADS * C_NOPE), BF16),
            jax.ShapeDtypeStruct((m, LANES), BF16),
            jax.ShapeDtypeStruct((m, C_HEADS * C_V), BF16),
        ],
        compiler_params=_cparams("parallel"),
        name="mla_project",
    )(z, z, z, q_a_norm.reshape(1, -1), kv_a_norm.reshape(1, -1), wq_p, wk_p, wv_p, rope_tab)


def _mla_flash_kernel(q_ref, kn_ref, kr_ref, v_ref, o_ref, m_ref, l_ref, acc_ref):
    tq = m_ref.shape[0]
    tk = tq // 2

    def q_tile(qi, carry):
        q0 = pl.multiple_of(qi * tq, tq)
        m_ref[...] = jnp.full_like(m_ref, NEG)
        l_ref[...] = jnp.zeros_like(l_ref)
        acc_ref[...] = jnp.zeros_like(acc_ref)

        def step(r0, nr, k_start, nk, mask_offset=None):
            rows = slice(r0, r0 + nr)
            ks = pl.ds(pl.multiple_of(k_start, tk), nk)
            k = jnp.concatenate([kn_ref[ks, :], kr_ref[ks, :]], axis=1)
            q = q_ref[pl.ds(pl.multiple_of(q0 + r0, tk), nr), :]
            s = lax.dot_general(q, k, (((1,), (1,)), ((), ())), preferred_element_type=F32)
            if mask_offset is not None:
                visible = (lax.broadcasted_iota(jnp.int32, s.shape, 1)
                           <= lax.broadcasted_iota(jnp.int32, s.shape, 0) + mask_offset)
                s = jnp.where(visible, s, NEG)
            m_prev = m_ref[rows, :]
            m_new = jnp.maximum(m_prev, jnp.max(s, axis=-1, keepdims=True))
            alpha = jnp.exp2(m_prev - m_new)
            p = jnp.exp2(s - jnp.concatenate([m_new] * (nk // LANES), axis=1))
            l_ref[rows, :] = alpha * l_ref[rows, :] + jnp.sum(p, axis=-1, keepdims=True)
            acc_ref[rows, :] = alpha * acc_ref[rows, :] + jnp.dot(p.astype(BF16), v_ref[ks, :],
                                                                  preferred_element_type=F32)
            m_ref[rows, :] = m_new

        def body(kb, c):
            for u in range(4):
                step(0, tq, (4 * kb + u) * tk, tk)
            return c

        lax.fori_loop(0, qi // 2, body, 0)

        @pl.when(qi % 2 == 1)
        def _():
            step(0, tq, (2 * qi - 2) * tk, tk)
            step(0, tq, (2 * qi - 1) * tk, tk)
        step(0, tk, q0, tk, 0)
        step(tk, tk, q0, tq, tk)
        o_ref[pl.ds(q0, tq), :] = (acc_ref[...] / l_ref[...]).astype(o_ref.dtype)
        return carry

    lax.fori_loop(0, q_ref.shape[0] // tq, q_tile, 0)


def mla_flash(q, kn, kr, v, *, batch, seq, tq):
    return pl.pallas_call(
        _mla_flash_kernel,
        grid=(batch, C_HEADS),
        in_specs=[
            pl.BlockSpec((seq, C_QW), lambda b, h: (b, h)),
            pl.BlockSpec((seq, C_NOPE), lambda b, h: (b, h)),
            pl.BlockSpec((seq, LANES), lambda b, h: (b, 0)),
            pl.BlockSpec((seq, C_V), lambda b, h: (b, h)),
        ],
        out_specs=pl.BlockSpec((seq, C_V), lambda b, h: (b, h)),
        out_shape=jax.ShapeDtypeStruct((batch * seq, C_HEADS * C_V), BF16),
        scratch_shapes=[
            pltpu.VMEM((tq, LANES), F32),
            pltpu.VMEM((tq, LANES), F32),
            pltpu.VMEM((tq, C_V), F32),
        ],
        compiler_params=_cparams("parallel", "parallel"),
        name="mla_flash",
    )(q, kn, kr, v)


def _mixer_out_kernel(oa_ref, ob_ref, oc_ref, wa_ref, wb_ref, wc_ref, g0_ref, g1_ref, g2_ref,
                      b0_ref, b1_ref, b2_ref, wo_ref, xr_ref, gn_ref, o_ref, xg_ref, inv_ref, mg_ref, ss_ref,
                      *, n_rows):
    i = pl.program_id(0)
    j = pl.program_id(1)
    _, nj, _, tn = mg_ref.shape
    cur = lax.rem(i, 2)

    def project():
        acc = xr_ref[...]
        for c in range(nj):
            acc = acc + jnp.dot(mg_ref[1 - cur, c], wo_ref[c * tn:(c + 1) * tn, :], preferred_element_type=F32)
        o_ref[...] = acc
        _next_norm_tile(acc, gn_ref, xg_ref, ss_ref)

    def branch(o, w, g, b):
        return jax.nn.sigmoid(g[...] + b[...]) * jnp.dot(o[...], w[...], preferred_element_type=F32)

    def merge():
        acc = branch(oa_ref, wa_ref, g0_ref, b0_ref)
        acc = acc + branch(ob_ref, wb_ref, g1_ref, b1_ref)
        acc = acc + branch(oc_ref, wc_ref, g2_ref, b2_ref)
        mg_ref[cur, j] = acc.astype(mg_ref.dtype)

    @pl.when(j == 0)
    def _():
        ss_ref[...] = jnp.zeros_like(ss_ref)

    @pl.when(jnp.logical_and(i > 0, i < n_rows))
    def _():
        project()
        merge()

    @pl.when(i == 0)
    def _():
        merge()

    @pl.when(i == n_rows)
    def _():
        project()

    @pl.when(jnp.logical_and(i > 0, j == nj - 1))
    def _():
        _next_norm_finish(inv_ref, ss_ref, nj * tn)


def _pipelined_row_maps(n_rows, ncols):
    row1 = lambda i: jnp.minimum(i, n_rows - 1)
    col1 = lambda i, j: jnp.where(i < n_rows, j, ncols - 1)
    row2 = lambda i: jnp.maximum(i - 1, 0)
    col2 = lambda i, j: jnp.where(i > 0, j, 0)
    return row1, col1, row2, col2


def mixer_output_block(x, o_a, o_b, o_c, w_a, w_b, w_c, gates, b_gate, w_o, next_gain, *, tm, tn):
    m, k = o_a.shape
    d = w_a.shape[1]
    nj = d // tn
    n_rows = m // tm
    mrow, mcol, orow, ocol = _pipelined_row_maps(n_rows, nj)
    act = pl.BlockSpec((tm, k), lambda i, j: (mrow(i), 0))
    wgt = pl.BlockSpec((k, tn), lambda i, j: (0, mcol(i, j)))
    gate = lambda br: pl.BlockSpec((tm, tn), lambda i, j: (mrow(i), br * nj + mcol(i, j)))
    gbias = lambda br: pl.BlockSpec((1, tn), lambda i, j: (0, br * nj + j))
    out_tile = pl.BlockSpec((tm, tn), lambda i, j: (orow(i), ocol(i, j)))
    return pl.pallas_call(
        functools.partial(_mixer_out_kernel, n_rows=n_rows),
        grid=(n_rows + 1, nj),
        in_specs=[act, act, act, wgt, wgt, wgt, gate(0), gate(1), gate(2), gbias(0), gbias(1), gbias(2),
                  pl.BlockSpec((d, tn), lambda i, j: (0, ocol(i, j))),
                  out_tile,
                  pl.BlockSpec((1, tn), lambda i, j: (0, ocol(i, j)))],
        out_specs=[out_tile, out_tile, pl.BlockSpec((tm, LANES), lambda i, j: (orow(i), 0))],
        out_shape=[jax.ShapeDtypeStruct((m, d), F32), jax.ShapeDtypeStruct((m, d), BF16),
                   jax.ShapeDtypeStruct((m, LANES), F32)],
        scratch_shapes=[pltpu.VMEM((2, nj, tm, tn), BF16), pltpu.VMEM((tm, LANES), F32)],
        compiler_params=_cparams("arbitrary", "arbitrary"),
        name="mixer_output_block",
    )(o_a, o_b, o_c, w_a, w_b, w_c, gates, gates, gates, *([b_gate.reshape(1, -1)] * 3), w_o, x,
      next_gain.reshape(1, d))


def _cross_block_kernel(xg_ref, inv_ref, wq_ref, k_ref, v_ref, wo_ref, xr_ref, gn_ref,
                        o_ref, xgo_ref, invo_ref, ao_ref, ss_ref, *, scale, n_rows):
    i = pl.program_id(0)
    j = pl.program_id(1)
    hd = k_ref.shape[1]
    cur = lax.rem(i, 2)

    def project():
        acc = xr_ref[...]
        for h in range(X_HEADS):
            acc = acc + jnp.dot(ao_ref[1 - cur, h], wo_ref[h * hd:(h + 1) * hd, :], preferred_element_type=F32)
        o_ref[...] = acc
        _next_norm_tile(acc, gn_ref, xgo_ref, ss_ref)

    def attend():
        q = jnp.dot(xg_ref[...], wq_ref[...], preferred_element_type=F32) * (_inv_cols(inv_ref, hd) * scale)
        s = lax.dot_general(q.astype(BF16), k_ref[...], (((1,), (1,)), ((), ())), preferred_element_type=F32)
        m = jnp.max(s, axis=-1, keepdims=True)
        e = jnp.exp(s - m)
        l = jnp.sum(e, axis=-1, keepdims=True)
        o = jnp.dot(e.astype(BF16), v_ref[...], preferred_element_type=F32) / l
        ao_ref[cur, j] = o.astype(ao_ref.dtype)

    @pl.when(j == 0)
    def _():
        ss_ref[...] = jnp.zeros_like(ss_ref)

    @pl.when(jnp.logical_and(i > 0, i < n_rows))
    def _():
        project()
        attend()

    @pl.when(i == 0)
    def _():
        attend()

    @pl.when(i == n_rows)
    def _():
        project()

    @pl.when(jnp.logical_and(i > 0, j == X_HEADS - 1))
    def _():
        _next_norm_finish(invo_ref, ss_ref, X_HEADS * hd)


def cross_attention_block(x, xg, inv, w_q, kv, w_o, next_gain, *, batch, seq, mem_len, tm):
    m, d = x.shape
    hd = d // X_HEADS
    tiles_per_batch = seq // tm
    n_rows = m // tm
    arow, head, orow, ocol = _pipelined_row_maps(n_rows, X_HEADS)
    out_tile = pl.BlockSpec((tm, hd), lambda i, j: (orow(i), ocol(i, j)))
    return pl.pallas_call(
        functools.partial(_cross_block_kernel, scale=hd ** -0.5, n_rows=n_rows),
        grid=(n_rows + 1, X_HEADS),
        in_specs=[
            pl.BlockSpec((tm, d), lambda i, j: (arow(i), 0)),
            pl.BlockSpec((tm, LANES), lambda i, j: (arow(i), 0)),
            pl.BlockSpec((d, hd), lambda i, j: (0, head(i, j))),
            pl.BlockSpec((mem_len, hd), lambda i, j: (arow(i) // tiles_per_batch, head(i, j))),
            pl.BlockSpec((mem_len, hd), lambda i, j: (arow(i) // tiles_per_batch, X_HEADS + head(i, j))),
            pl.BlockSpec((d, hd), lambda i, j: (0, ocol(i, j))),
            out_tile,
            pl.BlockSpec((1, hd), lambda i, j: (0, ocol(i, j))),
        ],
        out_specs=[out_tile, out_tile, pl.BlockSpec((tm, LANES), lambda i, j: (orow(i), 0))],
        out_shape=[jax.ShapeDtypeStruct((m, d), F32), jax.ShapeDtypeStruct((m, d), BF16),
                   jax.ShapeDtypeStruct((m, LANES), F32)],
        scratch_shapes=[pltpu.VMEM((2, X_HEADS, tm, hd), BF16), pltpu.VMEM((tm, LANES), F32)],
        compiler_params=_cparams("arbitrary", "arbitrary"),
        name="cross_attention_block",
    )(xg, inv, w_q, kv, kv, w_o, x, next_gain.reshape(1, d))


def _pack_w_in(w):
    nl, d, _ = w.shape
    sizes = (1024, 1024, 1024, B_DK, B_DK, B_DV, B_GATE_RANK, B_DV, C_Q_RANK, C_KV_RANK, C_ROPE, N_BRANCH * d)
    offs = np.concatenate([[0], np.cumsum(sizes)])
    aq, ak, av, bq, bk, bv, lr, br, cqa, ckva, ckr, gates = [
        w[:, :, offs[i]:offs[i + 1]].astype(BF16) for i in range(len(sizes))]
    zeros = lambda n: jnp.zeros((nl, d, n), BF16)
    used = Z_CKR + LANES
    wz = jnp.concatenate(
        [aq, ak, av, bv, br, cqa, ckva, bq, bk,
         lr, zeros(LANES - B_GATE_RANK), ckr, zeros(LANES - C_ROPE), zeros(Z_WIDTH - used)], axis=2)
    return wz, gates


def _pack_w_qb(w):
    r = w.shape[0]
    w = w.reshape(r, C_HEADS, C_NOPE + C_ROPE)
    w = jnp.pad(w, ((0, 0), (0, 0), (0, C_QW - C_NOPE - C_ROPE)))
    return w.reshape(r, C_HEADS * C_QW).astype(BF16)


def _pack_w_kvb(w):
    r = w.shape[0]
    w = w.reshape(r, C_HEADS, C_NOPE + C_V)
    wk = w[:, :, :C_NOPE].reshape(r, C_HEADS * C_NOPE)
    wv = w[:, :, C_NOPE:].reshape(r, C_HEADS * C_V)
    return wk.astype(BF16), wv.astype(BF16)


def kernel(x, mem, rel_bias, norm_mix, w_in, b_gate, w_alpha, b_alpha, gla_norm, q_a_norm, w_qb, kv_a_norm, w_kvb, w_up_a, w_up_b, w_up_c, w_o, norm_x, norm_mem, w_xq, w_xkv, w_xo, norm_ffn, w_ffn_gate, w_ffn_up, w_ffn_down, norm_final):
    batch, seq, d = x.shape
    mem_len = mem.shape[1]
    depth = w_in.shape[0]
    t = batch * seq
    assert (batch, seq, d, mem_len) == (4, 4096, 2048, 256), "tile sizes are derived for the stated shapes"
    assert d // X_HEADS == TN and t % SWIGLU_TM == 0 and seq % A_TILE == 0

    xf = x.reshape(t, d)
    memf = mem.reshape(batch * mem_len, d)
    bias_tab = dilated_bias_table(rel_bias)
    rope_tab = _rope_tables(seq)
    bf = lambda a: a.astype(BF16)
    w_in_z, w_in_g = _pack_w_in(w_in)

    xg, inv = prenorm(xf, norm_mix[0], tm=TM)
    for l in range(depth):
        z = scaled_matmul(xg, inv, w_in_z, out_dtype=F32, tm=TM, tn=W_Z_TN, layer=l)
        gates = scaled_matmul(xg, inv, w_in_g, out_dtype=BF16, tm=TM, tn=W_G_TN, layer=l)
        o_a = dilated_attention(z, bias_tab, batch=batch, seq=seq)
        w_alpha_p = jnp.pad(w_alpha[l], ((0, LANES - B_GATE_RANK), (0, 0)))
        o_b = gated_linear_attention(z, w_alpha_p, b_alpha[l], gla_norm[l], batch=batch, seq=seq, tc=GLA_TC)
        wk_p, wv_p = _pack_w_kvb(w_kvb[l])
        cq, ckn, ckr, cv = mla_project(z, q_a_norm[l], kv_a_norm[l], _pack_w_qb(w_qb[l]), wk_p, wv_p,
                                       rope_tab, seq=seq, tm=TM)
        o_c = mla_flash(cq, ckn, ckr, cv, batch=batch, seq=seq, tq=FLASH_TQ)
        xf, xg, inv = mixer_output_block(xf, o_a, o_b, o_c, bf(w_up_a[l]), bf(w_up_b[l]), bf(w_up_c[l]), gates,
                                         b_gate[l], bf(w_o[l]), norm_x[l], tm=TM, tn=TN)
        xkv = norm_matmul(memf, norm_mem[l], w_xkv, out_dtype=BF16, tm=batch * mem_len, tn=TN, layer=l)
        xf, xg, inv = cross_attention_block(xf, xg, inv, bf(w_xq[l]), xkv, bf(w_xo[l]), norm_ffn[l],
                                            batch=batch, seq=seq, mem_len=mem_len, tm=TM)
        act = scaled_swiglu(xg, inv, w_ffn_gate, w_ffn_up, tm=SWIGLU_TM, tn=TN, layer=l)
        if l + 1 < depth:
            xf, xg, inv = matmul_residual(act, bf(w_ffn_down[l]), xf, tm=TM, tn=TN, next_gain=norm_mix[l + 1])
        else:
            xf = matmul_residual(act, bf(w_ffn_down[l]), xf, tm=TM, tn=TN)
    return rmsnorm_rows(xf, norm_final, tm=TM).reshape(batch, seq, d)
```

```python
import functools
import math

import numpy as np
import jax
import jax.numpy as jnp
from jax import lax
from jax.experimental import pallas as pl
from jax.experimental.pallas import tpu as pltpu

F32 = jnp.float32
BF16 = jnp.bfloat16
EPS = 1e-6
NEG = -1e30
LOG2E = math.log2(math.e)

HEAD_DIM = 128
A_HEADS = 8
A_DILATIONS = (1, 4, 16)
A_BLOCK = 128
N_BUCKETS = 32
MAX_DISTANCE = 2048
B_HEADS = 4
B_DK = 512
B_DV = 1024
B_GATE_RANK = 16
B_GATE_TAU = 16.0
B_CHUNK = 64
C_HEADS = 8
C_Q_RANK = 512
C_KV_RANK = 512
C_NOPE = 128
C_ROPE = 64
C_V = 128
ROPE_THETA = 10000.0
X_HEADS = 4
N_BRANCH = 3

VMEM_LIMIT_BYTES = 56 * 1024 * 1024
LANES = 128

Z_AQ, Z_AK, Z_AV = 0, 1024, 2048
Z_BV, Z_BR = 3072, 4096
Z_CQA, Z_CKVA = 5120, 5632
Z_BQ, Z_BK = 6144, 6656
Z_LR = 7168
Z_CKR = 7296
Z_WIDTH = 7680
G_WIDTH = 3 * 2048
A_TILE = 2048

TM = 1024
W_Z_TN = 1536
W_G_TN = 2048
TN = 512
SWIGLU_TM = 2048
GLA_TC = 1024
FLASH_TQ = 1024


def _cparams(*sem):
    return pltpu.CompilerParams(dimension_semantics=sem, vmem_limit_bytes=VMEM_LIMIT_BYTES)


def _rmsnorm_rows(x_ref, g_ref, h_ref):
    rows = x_ref.shape[0]

    def body(i, carry):
        r = pl.multiple_of(i * 16, 16)
        x = x_ref[pl.ds(r, 16), :]
        ms = jnp.mean(x * x, axis=-1, keepdims=True)
        h_ref[pl.ds(r, 16), :] = (x * lax.rsqrt(ms + EPS) * g_ref[...]).astype(BF16)
        return carry

    lax.fori_loop(0, rows // 16, body, 0, unroll=8)


def _norm_matmul_kernel(x_ref, g_ref, w_ref, o_ref, h_ref, *, scale):
    @pl.when(pl.program_id(1) == 0)
    def _():
        _rmsnorm_rows(x_ref, g_ref, h_ref)

    acc = jnp.dot(h_ref[...], w_ref[...].astype(BF16), preferred_element_type=F32)
    if scale != 1.0:
        acc = acc * scale
    o_ref[...] = acc.astype(o_ref.dtype)


def _weight_spec(w, layer, tn):
    if w.ndim == 2:
        return pl.BlockSpec((w.shape[0], tn), lambda i, j: (0, j))
    return pl.BlockSpec((None, w.shape[1], tn), lambda i, j: (layer, 0, j))


def norm_matmul(x, g, w, *, out_dtype, tm, tn, scale=1.0, layer=None):
    m, k = x.shape
    n = w.shape[-1]
    return pl.pallas_call(
        functools.partial(_norm_matmul_kernel, scale=scale),
        grid=(m // tm, n // tn),
        in_specs=[
            pl.BlockSpec((tm, k), lambda i, j: (i, 0)),
            pl.BlockSpec((1, k), lambda i, j: (0, 0)),
            _weight_spec(w, layer, tn),
        ],
        out_specs=pl.BlockSpec((tm, tn), lambda i, j: (i, j)),
        out_shape=jax.ShapeDtypeStruct((m, n), out_dtype),
        scratch_shapes=[pltpu.VMEM((tm, k), BF16)],
        compiler_params=_cparams("parallel", "arbitrary"),
        name="norm_matmul",
    )(x, g.reshape(1, k), w)


def _inv_cols(inv_ref, n):
    inv = inv_ref[...]
    return jnp.concatenate([inv] * (n // LANES), axis=1)


def _next_norm_tile(o, g_ref, xg_ref, ss_ref):
    xg_ref[...] = (o * g_ref[...]).astype(xg_ref.dtype)
    ss_ref[...] = ss_ref[...] + jnp.sum(o * o, axis=-1, keepdims=True)


def _next_norm_finish(inv_ref, ss_ref, d):
    inv_ref[...] = lax.rsqrt(ss_ref[...] * (1.0 / d) + EPS)


def _prenorm_kernel(x_ref, g_ref, xg_ref, inv_ref):
    x = x_ref[...]
    xg_ref[...] = (x * g_ref[...]).astype(xg_ref.dtype)
    ms = jnp.mean(x * x, axis=-1, keepdims=True)
    inv_ref[...] = jnp.broadcast_to(lax.rsqrt(ms + EPS), inv_ref.shape)


def prenorm(x, g, *, tm):
    m, k = x.shape
    return pl.pallas_call(
        _prenorm_kernel,
        grid=(m // tm,),
        in_specs=[pl.BlockSpec((tm, k), lambda i: (i, 0)), pl.BlockSpec((1, k), lambda i: (0, 0))],
        out_specs=[pl.BlockSpec((tm, k), lambda i: (i, 0)), pl.BlockSpec((tm, LANES), lambda i: (i, 0))],
        out_shape=[jax.ShapeDtypeStruct((m, k), BF16), jax.ShapeDtypeStruct((m, LANES), F32)],
        compiler_params=_cparams("parallel"),
        name="prenorm",
    )(x, g.reshape(1, k))


def _scaled_matmul_kernel(xg_ref, inv_ref, w_ref, o_ref):
    acc = jnp.dot(xg_ref[...], w_ref[...].astype(BF16), preferred_element_type=F32)
    o_ref[...] = (acc * _inv_cols(inv_ref, acc.shape[1])).astype(o_ref.dtype)


def scaled_matmul(xg, inv, w, *, out_dtype, tm, tn, layer=None):
    m, k = xg.shape
    n = w.shape[-1]
    return pl.pallas_call(
        _scaled_matmul_kernel,
        grid=(m // tm, n // tn),
        in_specs=[
            pl.BlockSpec((tm, k), lambda i, j: (i, 0)),
            pl.BlockSpec((tm, LANES), lambda i, j: (i, 0)),
            _weight_spec(w, layer, tn),
        ],
        out_specs=pl.BlockSpec((tm, tn), lambda i, j: (i, j)),
        out_shape=jax.ShapeDtypeStruct((m, n), out_dtype),
        compiler_params=_cparams("parallel", "parallel"),
        name="scaled_matmul",
    )(xg, inv, w)


def _scaled_swiglu_kernel(xg_ref, inv_ref, wg_ref, wu_ref, o_ref):
    wg = wg_ref[...].astype(BF16)
    wu = wu_ref[...].astype(BF16)
    half = xg_ref.shape[0] // 2
    for r in range(2):
        rows = slice(r * half, (r + 1) * half)
        xg = xg_ref[rows, :]
        inv = jnp.concatenate([inv_ref[rows, :]] * (o_ref.shape[1] // LANES), axis=1)
        a = jnp.dot(xg, wg, preferred_element_type=F32) * inv
        b = jnp.dot(xg, wu, preferred_element_type=F32) * inv
        o_ref[rows, :] = (a * jax.nn.sigmoid(a) * b).astype(o_ref.dtype)


def scaled_swiglu(xg, inv, wg, wu, *, tm, tn, layer=None):
    m, k = xg.shape
    n = wg.shape[-1]
    return pl.pallas_call(
        _scaled_swiglu_kernel,
        grid=(m // tm, n // tn),
        in_specs=[
            pl.BlockSpec((tm, k), lambda i, j: (i, 0)),
            pl.BlockSpec((tm, LANES), lambda i, j: (i, 0)),
            _weight_spec(wg, layer, tn),
            _weight_spec(wu, layer, tn),
        ],
        out_specs=pl.BlockSpec((tm, tn), lambda i, j: (i, j)),
        out_shape=jax.ShapeDtypeStruct((m, n), BF16),
        compiler_params=_cparams("parallel", "parallel"),
        name="scaled_swiglu",
    )(xg, inv, wg, wu)


def _matmul_residual_kernel(a_ref, w_ref, r_ref, o_ref):
    o_ref[...] = r_ref[...] + jnp.dot(a_ref[...], w_ref[...].astype(BF16), preferred_element_type=F32)


def _matmul_residual_norm_kernel(a_ref, w_ref, r_ref, g_ref, o_ref, xg_ref, inv_ref, ss_ref, *, ncols):
    j = pl.program_id(1)

    @pl.when(j == 0)
    def _():
        ss_ref[...] = jnp.zeros_like(ss_ref)

    o = r_ref[...] + jnp.dot(a_ref[...], w_ref[...].astype(BF16), preferred_element_type=F32)
    o_ref[...] = o
    _next_norm_tile(o, g_ref, xg_ref, ss_ref)

    @pl.when(j == ncols - 1)
    def _():
        _next_norm_finish(inv_ref, ss_ref, ncols * o.shape[1])


def matmul_residual(a, w, res, *, tm, tn, layer=None, next_gain=None):
    m, k = a.shape
    n = w.shape[-1]
    tile = pl.BlockSpec((tm, tn), lambda i, j: (i, j))
    in_specs = [pl.BlockSpec((tm, k), lambda i, j: (i, 0)), _weight_spec(w, layer, tn), tile]
    if next_gain is None:
        return pl.pallas_call(
            _matmul_residual_kernel,
            grid=(m // tm, n // tn),
            in_specs=in_specs,
            out_specs=tile,
            out_shape=jax.ShapeDtypeStruct((m, n), F32),
            compiler_params=_cparams("parallel", "parallel"),
            name="matmul_residual",
        )(a, w, res)
    return pl.pallas_call(
        functools.partial(_matmul_residual_norm_kernel, ncols=n // tn),
        grid=(m // tm, n // tn),
        in_specs=in_specs + [pl.BlockSpec((1, tn), lambda i, j: (0, j))],
        out_specs=[tile, tile, pl.BlockSpec((tm, LANES), lambda i, j: (i, 0))],
        out_shape=[jax.ShapeDtypeStruct((m, n), F32), jax.ShapeDtypeStruct((m, n), BF16),
                   jax.ShapeDtypeStruct((m, LANES), F32)],
        scratch_shapes=[pltpu.VMEM((tm, LANES), F32)],
        compiler_params=_cparams("parallel", "arbitrary"),
        name="matmul_residual_norm",
    )(a, w, res, next_gain.reshape(1, n))


def _rmsnorm_kernel(x_ref, g_ref, o_ref):
    x = x_ref[...]
    ms = jnp.mean(x * x, axis=-1, keepdims=True)
    o_ref[...] = x * lax.rsqrt(ms + EPS) * g_ref[...]


def rmsnorm_rows(x, g, *, tm):
    m, k = x.shape
    return pl.pallas_call(
        _rmsnorm_kernel,
        grid=(m // tm,),
        in_specs=[pl.BlockSpec((tm, k), lambda i: (i, 0)), pl.BlockSpec((1, k), lambda i: (0, 0))],
        out_specs=pl.BlockSpec((tm, k), lambda i: (i, 0)),
        out_shape=jax.ShapeDtypeStruct((m, k), F32),
        compiler_params=_cparams("parallel"),
        name="final_rmsnorm",
    )(x, g.reshape(1, k))


def _t5_bucket_np(dist):
    max_exact = N_BUCKETS // 2
    n = np.maximum(dist, 1).astype(np.float64)
    large = max_exact + (np.log(n / max_exact) / math.log(MAX_DISTANCE / max_exact)
                         * (N_BUCKETS - max_exact)).astype(np.int32)
    large = np.minimum(large, N_BUCKETS - 1)
    return np.where(dist < max_exact, dist, large).astype(np.int32)


def _dilated_bucket_table():
    qi = np.arange(A_BLOCK)[:, None]
    kj = np.arange(2 * A_BLOCK)[None, :]
    steps_back = qi + A_BLOCK - kj
    valid = (steps_back >= 0) & (steps_back <= A_BLOCK)
    tabs = []
    for dil in A_DILATIONS:
        bucket = _t5_bucket_np(np.clip(steps_back, 0, A_BLOCK) * dil)
        tabs.append(np.where(valid, bucket, -1))
    return np.stack(tabs).astype(np.int32)


def _bias_table_kernel(idx_ref, rb_ref, o_ref):
    h = pl.program_id(1)
    idx = idx_ref[...]
    acc = jnp.full(idx.shape, NEG, F32)
    for b in range(N_BUCKETS):
        acc = jnp.where(idx == b, rb_ref[b, h] * LOG2E, acc)
    o_ref[...] = acc


def dilated_bias_table(rel_bias):
    idx = jnp.asarray(_dilated_bucket_table())
    npat = len(A_DILATIONS)
    return pl.pallas_call(
        _bias_table_kernel,
        grid=(npat, A_HEADS),
        in_specs=[
            pl.BlockSpec((None, A_BLOCK, 2 * A_BLOCK), lambda p, h: (p, 0, 0)),
            pl.BlockSpec(memory_space=pltpu.SMEM),
        ],
        out_specs=pl.BlockSpec((None, None, A_BLOCK, 2 * A_BLOCK), lambda p, h: (p, h, 0, 0)),
        out_shape=jax.ShapeDtypeStruct((npat, A_HEADS, A_BLOCK, 2 * A_BLOCK), F32),
        compiler_params=_cparams("parallel", "parallel"),
        name="dilated_bias_table",
    )(idx, rel_bias)


def _dilated_kernel(q_ref, kc_ref, kp_ref, vc_ref, vp_ref, bias_ref, o_ref, bm0, o_scr, lse_scr, *, scale):
    t = pl.program_id(2)
    tile = q_ref.shape[0]
    blk = A_BLOCK
    col = lax.broadcasted_iota(jnp.int32, (blk, 2 * blk), 1)
    no_prev = jnp.logical_and(col < blk, t == 0)
    for p in range(len(A_DILATIONS)):
        bm0[p] = jnp.where(no_prev, NEG, bias_ref[p])

    def rows(start, size, dil):
        return pl.ds(start, size) if dil == 1 else pl.ds(start, size, stride=dil)

    def block(p, dil, r, n):
        q_idx = rows(r + dil * blk * n, blk, dil)
        if n == 0:
            prev = rows(tile - dil * blk + r, blk, dil)
            cur = rows(r, blk, dil)
            k = jnp.concatenate([kp_ref[prev, :], kc_ref[cur, :]], axis=0)
            v = jnp.concatenate([vp_ref[prev, :], vc_ref[cur, :]], axis=0)
            bm = bm0[p]
        else:
            k_idx = rows(r + dil * blk * (n - 1), 2 * blk, dil)
            k, v = kc_ref[k_idx, :], vc_ref[k_idx, :]
            bm = bias_ref[p]
        q = (q_ref[q_idx, :] * scale).astype(BF16)
        s = lax.dot_general(q, k.astype(BF16), (((1,), (1,)), ((), ())), preferred_element_type=F32)
        s = jnp.where(bm > 0.5 * NEG, s + bm, NEG)
        m = jnp.max(s, axis=-1, keepdims=True)
        e = jnp.exp2(s - m)
        l = jnp.sum(e, axis=-1, keepdims=True)
        o = jnp.dot(e.astype(BF16), v.astype(BF16), preferred_element_type=F32) / l
        lse = m + jnp.log(l) * LOG2E
        o_scr[p, q_idx, :] = o
        lse_scr[p, q_idx, :] = jnp.broadcast_to(lse, (blk, HEAD_DIM))

    for p, dil in enumerate(A_DILATIONS):
        for r in range(dil):
            for n in range(tile // (blk * dil)):
                block(p, dil, r, n)

    def merge(i, carry):
        r = pl.multiple_of(i * blk, blk)
        rows = pl.ds(r, blk)
        l0, l1, l2 = lse_scr[0, rows, :], lse_scr[1, rows, :], lse_scr[2, rows, :]
        mx = jnp.maximum(jnp.maximum(l0, l1), l2)
        w0, w1, w2 = jnp.exp2(l0 - mx), jnp.exp2(l1 - mx), jnp.exp2(l2 - mx)
        num = w0 * o_scr[0, rows, :] + w1 * o_scr[1, rows, :] + w2 * o_scr[2, rows, :]
        o_ref[rows, :] = (num / (w0 + w1 + w2)).astype(o_ref.dtype)
        return carry

    lax.fori_loop(0, tile // blk, merge, 0)


def dilated_attention(z, bias_tab, *, batch, seq):
    z3 = z.reshape(batch, seq, z.shape[1])
    nt = seq // A_TILE
    hq, hk, hv = Z_AQ // HEAD_DIM, Z_AK // HEAD_DIM, Z_AV // HEAD_DIM
    tile_spec = lambda col0, prev: pl.BlockSpec(
        (None, A_TILE, HEAD_DIM),
        (lambda b, h, t: (b, jnp.maximum(t - 1, 0), col0 + h)) if prev else (lambda b, h, t: (b, t, col0 + h)))
    npat = len(A_DILATIONS)
    out = pl.pallas_call(
        functools.partial(_dilated_kernel, scale=HEAD_DIM ** -0.5 * LOG2E),
        grid=(batch, A_HEADS, nt),
        in_specs=[
            tile_spec(hq, False),
            tile_spec(hk, False), tile_spec(hk, True),
            tile_spec(hv, False), tile_spec(hv, True),
            pl.BlockSpec((npat, None, A_BLOCK, 2 * A_BLOCK), lambda b, h, t: (0, h, 0, 0)),
        ],
        out_specs=pl.BlockSpec((None, A_TILE, HEAD_DIM), lambda b, h, t: (b, t, h)),
        out_shape=jax.ShapeDtypeStruct((batch, seq, A_HEADS * HEAD_DIM), BF16),
        scratch_shapes=[
            pltpu.VMEM((npat, A_BLOCK, 2 * A_BLOCK), F32),
            pltpu.VMEM((npat, A_TILE, HEAD_DIM), F32),
            pltpu.VMEM((npat, A_TILE, HEAD_DIM), F32),
        ],
        compiler_params=_cparams("parallel", "parallel", "arbitrary"),
        name="dilated_attention",
    )(z3, z3, z3, z3, z3, bias_tab)
    return out.reshape(batch * seq, A_HEADS * HEAD_DIM)


def _split_bf16(a):
    hi = a.astype(BF16)
    return hi, (a - hi.astype(F32)).astype(BF16)


def _gla_kernel(q_ref, k_ref, v_ref, r_ref, lr_ref, wa_ref, ba_ref, gn_ref, o_ref, st_ref, *, scale):
    @pl.when(pl.program_id(1) == 0)
    def _():
        st_ref[...] = jnp.zeros_like(st_ref)

    c = B_CHUNK
    tc = q_ref.shape[0]
    dk = B_DK // B_HEADS
    dv = B_DV // B_HEADS
    dot = functools.partial(jnp.dot, preferred_element_type=F32)
    lr_hi, lr_lo = _split_bf16(lr_ref[...])
    wa_hi, wa_lo = _split_bf16(wa_ref[...])
    pre = dot(lr_hi, wa_hi) + dot(lr_hi, wa_lo) + dot(lr_lo, wa_hi) + ba_ref[...]
    log_a = (jnp.minimum(pre, 0.0) - jnp.log(1.0 + jnp.exp(-jnp.abs(pre)))) * (1.0 / B_GATE_TAU)
    tril = lax.broadcasted_iota(jnp.int32, (c, c), 0) >= lax.broadcasted_iota(jnp.int32, (c, c), 1)
    ones_tril = jnp.where(tril, 1.0, 0.0).astype(BF16)
    for ci in range(tc // c):
        sl = slice(ci * c, (ci + 1) * c)
        la_hi, la_lo = _split_bf16(log_a[sl, :])
        cum_all = dot(ones_tril, la_hi) + dot(ones_tril, la_lo)
        for h in range(B_HEADS):
            kcols = slice(h * dk, (h + 1) * dk)
            vcols = slice(h * dv, (h + 1) * dv)
            cum = cum_all[:, kcols]
            cum_last = cum[c - 1:c, :]
            q = q_ref[sl, kcols] * scale
            k = k_ref[sl, kcols]
            v = v_ref[sl, vcols]
            q_dec = (q * jnp.exp(cum)).astype(BF16)
            k_inv = (k * jnp.exp(-cum)).astype(BF16)
            k_state = (k * jnp.exp(cum_last - cum)).astype(BF16)
            att = lax.dot_general(q_dec, k_inv, (((1,), (1,)), ((), ())), preferred_element_type=F32)
            att = jnp.where(tril, att, 0.0).astype(BF16)
            st = st_ref[h]
            o = (jnp.dot(att, v.astype(BF16), preferred_element_type=F32)
                 + lax.dot_general(q_dec, st.astype(BF16), (((1,), (1,)), ((), ())), preferred_element_type=F32))
            st_ref[h] = st * jnp.exp(cum_last) + jnp.dot(v.T.astype(BF16), k_state, preferred_element_type=F32)
            ms = jnp.mean(o * o, axis=-1, keepdims=True)
            o = o * lax.rsqrt(ms + EPS) * gn_ref[:, vcols]
            r = r_ref[sl, vcols]
            o_ref[sl, vcols] = (o * (r * jax.nn.sigmoid(r))).astype(o_ref.dtype)


def gated_linear_attention(z, w_alpha_p, b_alpha, gla_norm, *, batch, seq, tc):
    dk = B_DK // B_HEADS
    dv = B_DV // B_HEADS
    nt = seq // tc
    row = lambda b, t: b * nt + t
    const = lambda b, t: (0, 0)
    return pl.pallas_call(
        functools.partial(_gla_kernel, scale=dk ** -0.5),
        grid=(batch, nt),
        in_specs=[
            pl.BlockSpec((tc, B_DK), lambda b, t: (row(b, t), Z_BQ // B_DK)),
            pl.BlockSpec((tc, B_DK), lambda b, t: (row(b, t), Z_BK // B_DK)),
            pl.BlockSpec((tc, B_DV), lambda b, t: (row(b, t), Z_BV // B_DV)),
            pl.BlockSpec((tc, B_DV), lambda b, t: (row(b, t), Z_BR // B_DV)),
            pl.BlockSpec((tc, LANES), lambda b, t: (row(b, t), Z_LR // LANES)),
            pl.BlockSpec((LANES, B_DK), const),
            pl.BlockSpec((1, B_DK), const),
            pl.BlockSpec((1, B_DV), const),
        ],
        out_specs=pl.BlockSpec((tc, B_DV), lambda b, t: (row(b, t), 0)),
        out_shape=jax.ShapeDtypeStruct((batch * seq, B_DV), BF16),
        scratch_shapes=[pltpu.VMEM((B_HEADS, dv, dk), F32)],
        compiler_params=_cparams("parallel", "arbitrary"),
        name="gated_linear_attention",
    )(z, z, z, z, z, w_alpha_p, b_alpha.reshape(1, B_DK), gla_norm.reshape(1, B_DV))


C_QW = 2 * C_NOPE


def _rope_tables(seq):
    pos = jnp.arange(seq, dtype=F32)
    inv = ROPE_THETA ** (-jnp.arange(0, C_ROPE, 2, dtype=F32) / C_ROPE)
    ang = pos[:, None] * inv[None, :]
    cos, sin = jnp.cos(ang), jnp.sin(ang)
    half = C_ROPE // 2
    z = lambda w: jnp.zeros((seq, w), F32)
    return jnp.stack([
        jnp.concatenate([cos, cos, z(LANES - C_ROPE)], axis=1),
        jnp.concatenate([-sin, z(half), z(LANES - C_ROPE)], axis=1),
        jnp.concatenate([z(half), sin, z(LANES - C_ROPE)], axis=1)])


def _mla_proj_kernel(cq_ref, ckv_ref, ckr_ref, gq_ref, gkv_ref, wq_ref, wk_ref, wv_ref, rt_ref,
                     q_out, kn_out, kr_out, v_out, *, scale):
    half = C_ROPE // 2
    t0, t1, t2 = rt_ref[0], rt_ref[1], rt_ref[2]

    def rms(x, g):
        ms = jnp.mean(x * x, axis=-1, keepdims=True)
        return (x * lax.rsqrt(ms + EPS) * g).astype(BF16)

    def rotary(x):
        return x * t0 + pltpu.roll(x, LANES - half, axis=1) * t1 + pltpu.roll(x, half, axis=1) * t2

    cq = rms(cq_ref[...], gq_ref[...])
    q = jnp.dot(cq, wq_ref[...], preferred_element_type=F32)
    for h in range(C_HEADS):
        nope = slice(h * C_QW, h * C_QW + C_NOPE)
        rope = slice(h * C_QW + C_NOPE, (h + 1) * C_QW)
        q_out[:, nope] = (q[:, nope] * scale).astype(q_out.dtype)
        q_out[:, rope] = (rotary(q[:, rope]) * scale).astype(q_out.dtype)
    ckv = rms(ckv_ref[...], gkv_ref[...])
    kn_out[...] = jnp.dot(ckv, wk_ref[...], preferred_element_type=F32).astype(kn_out.dtype)
    v_out[...] = jnp.dot(ckv, wv_ref[...], preferred_element_type=F32).astype(v_out.dtype)
    kr_out[...] = rotary(ckr_ref[...]).astype(kr_out.dtype)


def mla_project(z, q_a_norm, kv_a_norm, wq_p, wk_p, wv_p, rope_tab, *, seq, tm):
    m = z.shape[0]
    nt = seq // tm
    const = lambda i: (0, 0)
    return pl.pallas_call(
        functools.partial(_mla_proj_kernel, scale=(C_NOPE + C_ROPE) ** -0.5 * math.log2(math.e)),
        grid=(m // tm,),
        in_specs=[
            pl.BlockSpec((tm, C_Q_RANK), lambda i: (i, Z_CQA // C_Q_RANK)),
            pl.BlockSpec((tm, C_KV_RANK), lambda i: (i, Z_CKVA // C_KV_RANK)),
            pl.BlockSpec((tm, LANES), lambda i: (i, Z_CKR // LANES)),
            pl.BlockSpec((1, C_Q_RANK), const),
            pl.BlockSpec((1, C_KV_RANK), const),
            pl.BlockSpec((C_Q_RANK, C_HEADS * C_QW), const),
            pl.BlockSpec((C_KV_RANK, C_HEADS * C_NOPE), const),
            pl.BlockSpec((C_KV_RANK, C_HEADS * C_V), const),
            pl.BlockSpec((3, tm, LANES), lambda i: (0, i % nt, 0)),
        ],
        out_specs=[
            pl.BlockSpec((tm, C_HEADS * C_QW), lambda i: (i, 0)),
            pl.BlockSpec((tm, C_HEADS * C_NOPE), lambda i: (i, 0)),
            pl.BlockSpec((tm, LANES), lambda i: (i, 0)),
            pl.BlockSpec((tm, C_HEADS * C_V), lambda i: (i, 0)),
        ],
        out_shape=[
            jax.ShapeDtypeStruct((m, C_HEADS * C_QW), BF16),
            jax.ShapeDtypeStruct((m, C_HEADS * C_NOPE), BF16),
            jax.ShapeDtypeStruct((m, LANES), BF16),
            jax.ShapeDtypeStruct((m, C_HEADS * C_V), BF16),
        ],
        compiler_params=_cparams("parallel"),
        name="mla_project",
    )(z, z, z, q_a_norm.reshape(1, -1), kv_a_norm.reshape(1, -1), wq_p, wk_p, wv_p, rope_tab)


def _mla_flash_kernel(q_ref, kn_ref, kr_ref, v_ref, o_ref, m_ref, l_ref, acc_ref):
    tq = m_ref.shape[0]
    tk = tq // 2

    def q_tile(qi, carry):
        q0 = pl.multiple_of(qi * tq, tq)
        m_ref[...] = jnp.full_like(m_ref, NEG)
        l_ref[...] = jnp.zeros_like(l_ref)
        acc_ref[...] = jnp.zeros_like(acc_ref)

        def step(r0, nr, k_start, nk, mask_offset=None):
            rows = slice(r0, r0 + nr)
            ks = pl.ds(pl.multiple_of(k_start, tk), nk)
            k = jnp.concatenate([kn_ref[ks, :], kr_ref[ks, :]], axis=1)
            q = q_ref[pl.ds(pl.multiple_of(q0 + r0, tk), nr), :]
            s = lax.dot_general(q, k, (((1,), (1,)), ((), ())), preferred_element_type=F32)
            if mask_offset is not None:
                visible = (lax.broadcasted_iota(jnp.int32, s.shape, 1)
                           <= lax.broadcasted_iota(jnp.int32, s.shape, 0) + mask_offset)
                s = jnp.where(visible, s, NEG)
            m_prev = m_ref[rows, :]
            m_new = jnp.maximum(m_prev, jnp.max(s, axis=-1, keepdims=True))
            alpha = jnp.exp2(m_prev - m_new)
            p = jnp.exp2(s - jnp.concatenate([m_new] * (nk // LANES), axis=1))
            l_ref[rows, :] = alpha * l_ref[rows, :] + jnp.sum(p, axis=-1, keepdims=True)
            acc_ref[rows, :] = alpha * acc_ref[rows, :] + jnp.dot(p.astype(BF16), v_ref[ks, :],
                                                                  preferred_element_type=F32)
            m_ref[rows, :] = m_new

        def body(kb, c):
            for u in range(4):
                step(0, tq, (4 * kb + u) * tk, tk)
            return c

        lax.fori_loop(0, qi // 2, body, 0)

        @pl.when(qi % 2 == 1)
        def _():
            step(0, tq, (2 * qi - 2) * tk, tk)
            step(0, tq, (2 * qi - 1) * tk, tk)
        step(0, tk, q0, tk, 0)
        step(tk, tk, q0, tq, tk)
        o_ref[pl.ds(q0, tq), :] = (acc_ref[...] / l_ref[...]).astype(o_ref.dtype)
        return carry

    lax.fori_loop(0, q_ref.shape[0] // tq, q_tile, 0)


def mla_flash(q, kn, kr, v, *, batch, seq, tq):
    return pl.pallas_call(
        _mla_flash_kernel,
        grid=(batch, C_HEADS),
        in_specs=[
            pl.BlockSpec((seq, C_QW), lambda b, h: (b, h)),
            pl.BlockSpec((seq, C_NOPE), lambda b, h: (b, h)),
            pl.BlockSpec((seq, LANES), lambda b, h: (b, 0)),
            pl.BlockSpec((seq, C_V), lambda b, h: (b, h)),
        ],
        out_specs=pl.BlockSpec((seq, C_V), lambda b, h: (b, h)),
        out_shape=jax.ShapeDtypeStruct((batch * seq, C_HEADS * C_V), BF16),
        scratch_shapes=[
            pltpu.VMEM((tq, LANES), F32),
            pltpu.VMEM((tq, LANES), F32),
            pltpu.VMEM((tq, C_V), F32),
        ],
        compiler_params=_cparams("parallel", "parallel"),
        name="mla_flash",
    )(q, kn, kr, v)


def _mixer_out_kernel(oa_ref, ob_ref, oc_ref, wa_ref, wb_ref, wc_ref, g0_ref, g1_ref, g2_ref,
                      b0_ref, b1_ref, b2_ref, wo_ref, xr_ref, gn_ref, o_ref, xg_ref, inv_ref, mg_ref, ss_ref,
                      *, n_rows):
    i = pl.program_id(0)
    j = pl.program_id(1)
    _, nj, _, tn = mg_ref.shape
    cur = lax.rem(i, 2)

    def project():
        acc = xr_ref[...]
        for c in range(nj):
            acc = acc + jnp.dot(mg_ref[1 - cur, c], wo_ref[c * tn:(c + 1) * tn, :], preferred_element_type=F32)
        o_ref[...] = acc
        _next_norm_tile(acc, gn_ref, xg_ref, ss_ref)

    def branch(o, w, g, b):
        return jax.nn.sigmoid(g[...] + b[...]) * jnp.dot(o[...], w[...], preferred_element_type=F32)

    def merge():
        acc = branch(oa_ref, wa_ref, g0_ref, b0_ref)
        acc = acc + branch(ob_ref, wb_ref, g1_ref, b1_ref)
        acc = acc + branch(oc_ref, wc_ref, g2_ref, b2_ref)
        mg_ref[cur, j] = acc.astype(mg_ref.dtype)

    @pl.when(j == 0)
    def _():
        ss_ref[...] = jnp.zeros_like(ss_ref)

    @pl.when(jnp.logical_and(i > 0, i < n_rows))
    def _():
        project()
        merge()

    @pl.when(i == 0)
    def _():
        merge()

    @pl.when(i == n_rows)
    def _():
        project()

    @pl.when(jnp.logical_and(i > 0, j == nj - 1))
    def _():
        _next_norm_finish(inv_ref, ss_ref, nj * tn)


def _pipelined_row_maps(n_rows, ncols):
    row1 = lambda i: jnp.minimum(i, n_rows - 1)
    col1 = lambda i, j: jnp.where(i < n_rows, j, ncols - 1)
    row2 = lambda i: jnp.maximum(i - 1, 0)
    col2 = lambda i, j: jnp.where(i > 0, j, 0)
    return row1, col1, row2, col2


def mixer_output_block(x, o_a, o_b, o_c, w_a, w_b, w_c, gates, b_gate, w_o, next_gain, *, tm, tn):
    m, k = o_a.shape
    d = w_a.shape[1]
    nj = d // tn
    n_rows = m // tm
    mrow, mcol, orow, ocol = _pipelined_row_maps(n_rows, nj)
    act = pl.BlockSpec((tm, k), lambda i, j: (mrow(i), 0))
    wgt = pl.BlockSpec((k, tn), lambda i, j: (0, mcol(i, j)))
    gate = lambda br: pl.BlockSpec((tm, tn), lambda i, j: (mrow(i), br * nj + mcol(i, j)))
    gbias = lambda br: pl.BlockSpec((1, tn), lambda i, j: (0, br * nj + j))
    out_tile = pl.BlockSpec((tm, tn), lambda i, j: (orow(i), ocol(i, j)))
    return pl.pallas_call(
        functools.partial(_mixer_out_kernel, n_rows=n_rows),
        grid=(n_rows + 1, nj),
        in_specs=[act, act, act, wgt, wgt, wgt, gate(0), gate(1), gate(2), gbias(0), gbias(1), gbias(2),
                  pl.BlockSpec((d, tn), lambda i, j: (0, ocol(i, j))),
                  out_tile,
                  pl.BlockSpec((1, tn), lambda i, j: (0, ocol(i, j)))],
        out_specs=[out_tile, out_tile, pl.BlockSpec((tm, LANES), lambda i, j: (orow(i), 0))],
        out_shape=[jax.ShapeDtypeStruct((m, d), F32), jax.ShapeDtypeStruct((m, d), BF16),
                   jax.ShapeDtypeStruct((m, LANES), F32)],
        scratch_shapes=[pltpu.VMEM((2, nj, tm, tn), BF16), pltpu.VMEM((tm, LANES), F32)],
        compiler_params=_cparams("arbitrary", "arbitrary"),
        name="mixer_output_block",
    )(o_a, o_b, o_c, w_a, w_b, w_c, gates, gates, gates, *([b_gate.reshape(1, -1)] * 3), w_o, x,
      next_gain.reshape(1, d))


def _cross_block_kernel(xg_ref, inv_ref, wq_ref, k_ref, v_ref, wo_ref, xr_ref, gn_ref,
                        o_ref, xgo_ref, invo_ref, ao_ref, ss_ref, *, scale, n_rows):
    i = pl.program_id(0)
    j = pl.program_id(1)
    hd = k_ref.shape[1]
    cur = lax.rem(i, 2)

    def project():
        acc = xr_ref[...]
        for h in range(X_HEADS):
            acc = acc + jnp.dot(ao_ref[1 - cur, h], wo_ref[h * hd:(h + 1) * hd, :].astype(BF16),
                                preferred_element_type=F32)
        o_ref[...] = acc
        _next_norm_tile(acc, gn_ref, xgo_ref, ss_ref)

    def attend():
        q = (jnp.dot(xg_ref[...], wq_ref[...].astype(BF16), preferred_element_type=F32)
             * (_inv_cols(inv_ref, hd) * scale))
        s = lax.dot_general(q.astype(BF16), k_ref[...], (((1,), (1,)), ((), ())), preferred_element_type=F32)
        m = jnp.max(s, axis=-1, keepdims=True)
        e = jnp.exp(s - m)
        l = jnp.sum(e, axis=-1, keepdims=True)
        o = jnp.dot(e.astype(BF16), v_ref[...], preferred_element_type=F32) / l
        ao_ref[cur, j] = o.astype(ao_ref.dtype)

    @pl.when(j == 0)
    def _():
        ss_ref[...] = jnp.zeros_like(ss_ref)

    @pl.when(jnp.logical_and(i > 0, i < n_rows))
    def _():
        project()
        attend()

    @pl.when(i == 0)
    def _():
        attend()

    @pl.when(i == n_rows)
    def _():
        project()

    @pl.when(jnp.logical_and(i > 0, j == X_HEADS - 1))
    def _():
        _next_norm_finish(invo_ref, ss_ref, X_HEADS * hd)


def cross_attention_block(x, xg, inv, w_q, kv, w_o, next_gain, *, batch, seq, mem_len, tm, layer):
    m, d = x.shape
    hd = d // X_HEADS
    tiles_per_batch = seq // tm
    n_rows = m // tm
    arow, head, orow, ocol = _pipelined_row_maps(n_rows, X_HEADS)
    out_tile = pl.BlockSpec((tm, hd), lambda i, j: (orow(i), ocol(i, j)))
    return pl.pallas_call(
        functools.partial(_cross_block_kernel, scale=hd ** -0.5, n_rows=n_rows),
        grid=(n_rows + 1, X_HEADS),
        in_specs=[
            pl.BlockSpec((tm, d), lambda i, j: (arow(i), 0)),
            pl.BlockSpec((tm, LANES), lambda i, j: (arow(i), 0)),
            pl.BlockSpec((None, d, hd), lambda i, j: (layer, 0, head(i, j))),
            pl.BlockSpec((mem_len, hd), lambda i, j: (arow(i) // tiles_per_batch, head(i, j))),
            pl.BlockSpec((mem_len, hd), lambda i, j: (arow(i) // tiles_per_batch, X_HEADS + head(i, j))),
            pl.BlockSpec((None, d, hd), lambda i, j: (layer, 0, ocol(i, j))),
            out_tile,
            pl.BlockSpec((1, hd), lambda i, j: (0, ocol(i, j))),
        ],
        out_specs=[out_tile, out_tile, pl.BlockSpec((tm, LANES), lambda i, j: (orow(i), 0))],
        out_shape=[jax.ShapeDtypeStruct((m, d), F32), jax.ShapeDtypeStruct((m, d), BF16),
                   jax.ShapeDtypeStruct((m, LANES), F32)],
        scratch_shapes=[pltpu.VMEM((2, X_HEADS, tm, hd), BF16), pltpu.VMEM((tm, LANES), F32)],
        compiler_params=_cparams("arbitrary", "arbitrary"),
        name="cross_attention_block",
    )(xg, inv, w_q, kv, kv, w_o, x, next_gain.reshape(1, d))


def _pack_w_in(w):
    nl, d, _ = w.shape
    sizes = (1024, 1024, 1024, B_DK, B_DK, B_DV, B_GATE_RANK, B_DV, C_Q_RANK, C_KV_RANK, C_ROPE, N_BRANCH * d)
    offs = np.concatenate([[0], np.cumsum(sizes)])
    aq, ak, av, bq, bk, bv, lr, br, cqa, ckva, ckr, gates = [
        w[:, :, offs[i]:offs[i + 1]].astype(BF16) for i in range(len(sizes))]
    zeros = lambda n: jnp.zeros((nl, d, n), BF16)
    used = Z_CKR + LANES
    wz = jnp.concatenate(
        [aq, ak, av, bv, br, cqa, ckva, bq, bk,
         lr, zeros(LANES - B_GATE_RANK), ckr, zeros(LANES - C_ROPE), zeros(Z_WIDTH - used)], axis=2)
    return wz, gates


def _pack_w_qb(w):
    r = w.shape[0]
    w = w.reshape(r, C_HEADS, C_NOPE + C_ROPE)
    w = jnp.pad(w, ((0, 0), (0, 0), (0, C_QW - C_NOPE - C_ROPE)))
    return w.reshape(r, C_HEADS * C_QW).astype(BF16)


def _pack_w_kvb(w):
    r = w.shape[0]
    w = w.reshape(r, C_HEADS, C_NOPE + C_V)
    wk = w[:, :, :C_NOPE].reshape(r, C_HEADS * C_NOPE)
    wv = w[:, :, C_NOPE:].reshape(r, C_HEADS * C_V)
    return wk.astype(BF16), wv.astype(BF16)


def kernel(x, mem, rel_bias, norm_mix, w_in, b_gate, w_alpha, b_alpha, gla_norm, q_a_norm, w_qb, kv_a_norm, w_kvb, w_up_a, w_up_b, w_up_c, w_o, norm_x, norm_mem, w_xq, w_xkv, w_xo, norm_ffn, w_ffn_gate, w_ffn_up, w_ffn_down, norm_final):
    batch, seq, d = x.shape
    mem_len = mem.shape[1]
    depth = w_in.shape[0]
    t = batch * seq
    assert (batch, seq, d, mem_len) == (4, 4096, 2048, 256), "tile sizes are derived for the stated shapes"
    assert d // X_HEADS == TN and t % SWIGLU_TM == 0 and seq % A_TILE == 0

    xf = x.reshape(t, d)
    memf = mem.reshape(batch * mem_len, d)
    bias_tab = dilated_bias_table(rel_bias)
    rope_tab = _rope_tables(seq)
    bf = lambda a: a.astype(BF16)
    w_in_z, w_in_g = _pack_w_in(w_in)

    xg, inv = prenorm(xf, norm_mix[0], tm=TM)
    for l in range(depth):
        z = scaled_matmul(xg, inv, w_in_z, out_dtype=F32, tm=TM, tn=W_Z_TN, layer=l)
        gates = scaled_matmul(xg, inv, w_in_g, out_dtype=BF16, tm=TM, tn=W_G_TN, layer=l)
        o_a = dilated_attention(z, bias_tab, batch=batch, seq=seq)
        w_alpha_p = jnp.pad(w_alpha[l], ((0, LANES - B_GATE_RANK), (0, 0)))
        o_b = gated_linear_attention(z, w_alpha_p, b_alpha[l], gla_norm[l], batch=batch, seq=seq, tc=GLA_TC)
        wk_p, wv_p = _pack_w_kvb(w_kvb[l])
        cq, ckn, ckr, cv = mla_project(z, q_a_norm[l], kv_a_norm[l], _pack_w_qb(w_qb[l]), wk_p, wv_p,
                                       rope_tab, seq=seq, tm=TM)
        o_c = mla_flash(cq, ckn, ckr, cv, batch=batch, seq=seq, tq=FLASH_TQ)
        xf, xg, inv = mixer_output_block(xf, o_a, o_b, o_c, bf(w_up_a[l]), bf(w_up_b[l]), bf(w_up_c[l]), gates,
                                         b_gate[l], bf(w_o[l]), norm_x[l], tm=TM, tn=TN)
        xkv = norm_matmul(memf, norm_mem[l], w_xkv, out_dtype=BF16, tm=batch * mem_len, tn=TN, layer=l)
        xf, xg, inv = cross_attention_block(xf, xg, inv, w_xq, xkv, w_xo, norm_ffn[l],
                                            batch=batch, seq=seq, mem_len=mem_len, tm=TM, layer=l)
        act = scaled_swiglu(xg, inv, w_ffn_gate, w_ffn_up, tm=SWIGLU_TM, tn=TN, layer=l)
        if l + 1 < depth:
            xf, xg, inv = matmul_residual(act, bf(w_ffn_down[l]), xf, tm=TM, tn=TN, next_gain=norm_mix[l + 1])
        else:
            xf = matmul_residual(act, bf(w_ffn_down[l]), xf, tm=TM, tn=TN)
    return rmsnorm_rows(xf, norm_final, tm=TM).reshape(batch, seq, d)
```

```python
import functools
import math

import numpy as np
import jax
import jax.numpy as jnp
from jax import lax
from jax.experimental import pallas as pl
from jax.experimental.pallas import tpu as pltpu

F32 = jnp.float32
BF16 = jnp.bfloat16
EPS = 1e-6
NEG = -1e30
LOG2E = math.log2(math.e)

HEAD_DIM = 128
A_HEADS = 8
A_DILATIONS = (1, 4, 16)
A_BLOCK = 128
N_BUCKETS = 32
MAX_DISTANCE = 2048
B_HEADS = 4
B_DK = 512
B_DV = 1024
B_GATE_RANK = 16
B_GATE_TAU = 16.0
B_CHUNK = 64
C_HEADS = 8
C_Q_RANK = 512
C_KV_RANK = 512
C_NOPE = 128
C_ROPE = 64
C_V = 128
ROPE_THETA = 10000.0
X_HEADS = 4
N_BRANCH = 3

VMEM_LIMIT_BYTES = 56 * 1024 * 1024
LANES = 128

Z_AQ, Z_AK, Z_AV = 0, 1024, 2048
Z_BV, Z_BR = 3072, 4096
Z_CQA, Z_CKVA = 5120, 5632
Z_BQ, Z_BK = 6144, 6656
Z_LR = 7168
Z_CKR = 7296
Z_WIDTH = 7680
G_WIDTH = 3 * 2048
A_TILE = 2048

TM = 1024
W_TM = 2048
W_Z_TN = 768
W_G_TN = 1024
TN = 512
SWIGLU_TM = 2048
GLA_TC = 1024
FLASH_TQ = 1024


def _cparams(*sem):
    return pltpu.CompilerParams(dimension_semantics=sem, vmem_limit_bytes=VMEM_LIMIT_BYTES)


def _rmsnorm_rows(x_ref, g_ref, h_ref):
    rows = x_ref.shape[0]

    def body(i, carry):
        r = pl.multiple_of(i * 16, 16)
        x = x_ref[pl.ds(r, 16), :]
        ms = jnp.mean(x * x, axis=-1, keepdims=True)
        h_ref[pl.ds(r, 16), :] = (x * lax.rsqrt(ms + EPS) * g_ref[...]).astype(BF16)
        return carry

    lax.fori_loop(0, rows // 16, body, 0, unroll=8)


def _norm_matmul_kernel(x_ref, g_ref, w_ref, o_ref, h_ref, *, scale):
    @pl.when(pl.program_id(1) == 0)
    def _():
        _rmsnorm_rows(x_ref, g_ref, h_ref)

    acc = jnp.dot(h_ref[...], w_ref[...].astype(BF16), preferred_element_type=F32)
    if scale != 1.0:
        acc = acc * scale
    o_ref[...] = acc.astype(o_ref.dtype)


def _weight_spec(w, layer, tn):
    if w.ndim == 2:
        return pl.BlockSpec((w.shape[0], tn), lambda i, j: (0, j))
    return pl.BlockSpec((None, w.shape[1], tn), lambda i, j: (layer, 0, j))


def norm_matmul(x, g, w, *, out_dtype, tm, tn, scale=1.0, layer=None):
    m, k = x.shape
    n = w.shape[-1]
    return pl.pallas_call(
        functools.partial(_norm_matmul_kernel, scale=scale),
        grid=(m // tm, n // tn),
        in_specs=[
            pl.BlockSpec((tm, k), lambda i, j: (i, 0)),
            pl.BlockSpec((1, k), lambda i, j: (0, 0)),
            _weight_spec(w, layer, tn),
        ],
        out_specs=pl.BlockSpec((tm, tn), lambda i, j: (i, j)),
        out_shape=jax.ShapeDtypeStruct((m, n), out_dtype),
        scratch_shapes=[pltpu.VMEM((tm, k), BF16)],
        compiler_params=_cparams("parallel", "arbitrary"),
        name="norm_matmul",
    )(x, g.reshape(1, k), w)


def _inv_cols(inv_ref, n):
    inv = inv_ref[...]
    return jnp.concatenate([inv] * (n // LANES), axis=1)


def _next_norm_tile(o, g_ref, xg_ref, ss_ref):
    xg_ref[...] = (o * g_ref[...]).astype(xg_ref.dtype)
    ss_ref[...] = ss_ref[...] + jnp.sum(o * o, axis=-1, keepdims=True)


def _next_norm_finish(inv_ref, ss_ref, d):
    inv_ref[...] = lax.rsqrt(ss_ref[...] * (1.0 / d) + EPS)


def _prenorm_kernel(x_ref, g_ref, xg_ref, inv_ref):
    x = x_ref[...]
    xg_ref[...] = (x * g_ref[...]).astype(xg_ref.dtype)
    ms = jnp.mean(x * x, axis=-1, keepdims=True)
    inv_ref[...] = jnp.broadcast_to(lax.rsqrt(ms + EPS), inv_ref.shape)


def prenorm(x, g, *, tm):
    m, k = x.shape
    return pl.pallas_call(
        _prenorm_kernel,
        grid=(m // tm,),
        in_specs=[pl.BlockSpec((tm, k), lambda i: (i, 0)), pl.BlockSpec((1, k), lambda i: (0, 0))],
        out_specs=[pl.BlockSpec((tm, k), lambda i: (i, 0)), pl.BlockSpec((tm, LANES), lambda i: (i, 0))],
        out_shape=[jax.ShapeDtypeStruct((m, k), BF16), jax.ShapeDtypeStruct((m, LANES), F32)],
        compiler_params=_cparams("parallel"),
        name="prenorm",
    )(x, g.reshape(1, k))


def _scaled_matmul_kernel(xg_ref, inv_ref, w_ref, o_ref):
    acc = jnp.dot(xg_ref[...], w_ref[...].astype(BF16), preferred_element_type=F32)
    o_ref[...] = (acc * _inv_cols(inv_ref, acc.shape[1])).astype(o_ref.dtype)


def scaled_matmul(xg, inv, w, *, out_dtype, tm, tn, layer=None):
    m, k = xg.shape
    n = w.shape[-1]
    return pl.pallas_call(
        _scaled_matmul_kernel,
        grid=(m // tm, n // tn),
        in_specs=[
            pl.BlockSpec((tm, k), lambda i, j: (i, 0)),
            pl.BlockSpec((tm, LANES), lambda i, j: (i, 0)),
            _weight_spec(w, layer, tn),
        ],
        out_specs=pl.BlockSpec((tm, tn), lambda i, j: (i, j)),
        out_shape=jax.ShapeDtypeStruct((m, n), out_dtype),
        compiler_params=_cparams("parallel", "parallel"),
        name="scaled_matmul",
    )(xg, inv, w)


def _scaled_swiglu_kernel(xg_ref, inv_ref, wg_ref, wu_ref, o_ref):
    wg = wg_ref[...].astype(BF16)
    wu = wu_ref[...].astype(BF16)
    half = xg_ref.shape[0] // 2
    for r in range(2):
        rows = slice(r * half, (r + 1) * half)
        xg = xg_ref[rows, :]
        inv = jnp.concatenate([inv_ref[rows, :]] * (o_ref.shape[1] // LANES), axis=1)
        a = jnp.dot(xg, wg, preferred_element_type=F32) * inv
        b = jnp.dot(xg, wu, preferred_element_type=F32) * inv
        o_ref[rows, :] = (a * jax.nn.sigmoid(a) * b).astype(o_ref.dtype)


def scaled_swiglu(xg, inv, wg, wu, *, tm, tn, layer=None):
    m, k = xg.shape
    n = wg.shape[-1]
    return pl.pallas_call(
        _scaled_swiglu_kernel,
        grid=(m // tm, n // tn),
        in_specs=[
            pl.BlockSpec((tm, k), lambda i, j: (i, 0)),
            pl.BlockSpec((tm, LANES), lambda i, j: (i, 0)),
            _weight_spec(wg, layer, tn),
            _weight_spec(wu, layer, tn),
        ],
        out_specs=pl.BlockSpec((tm, tn), lambda i, j: (i, j)),
        out_shape=jax.ShapeDtypeStruct((m, n), BF16),
        compiler_params=_cparams("parallel", "parallel"),
        name="scaled_swiglu",
    )(xg, inv, wg, wu)


def _matmul_residual_kernel(a_ref, w_ref, r_ref, o_ref):
    o_ref[...] = r_ref[...] + jnp.dot(a_ref[...], w_ref[...].astype(BF16), preferred_element_type=F32)


def _matmul_residual_norm_kernel(a_ref, w_ref, r_ref, g_ref, o_ref, xg_ref, inv_ref, ss_ref, *, ncols):
    j = pl.program_id(1)

    @pl.when(j == 0)
    def _():
        ss_ref[...] = jnp.zeros_like(ss_ref)

    o = r_ref[...] + jnp.dot(a_ref[...], w_ref[...].astype(BF16), preferred_element_type=F32)
    o_ref[...] = o
    _next_norm_tile(o, g_ref, xg_ref, ss_ref)

    @pl.when(j == ncols - 1)
    def _():
        _next_norm_finish(inv_ref, ss_ref, ncols * o.shape[1])


def matmul_residual(a, w, res, *, tm, tn, layer=None, next_gain=None):
    m, k = a.shape
    n = w.shape[-1]
    tile = pl.BlockSpec((tm, tn), lambda i, j: (i, j))
    in_specs = [pl.BlockSpec((tm, k), lambda i, j: (i, 0)), _weight_spec(w, layer, tn), tile]
    if next_gain is None:
        return pl.pallas_call(
            _matmul_residual_kernel,
            grid=(m // tm, n // tn),
            in_specs=in_specs,
            out_specs=tile,
            out_shape=jax.ShapeDtypeStruct((m, n), F32),
            compiler_params=_cparams("parallel", "parallel"),
            name="matmul_residual",
        )(a, w, res)
    return pl.pallas_call(
        functools.partial(_matmul_residual_norm_kernel, ncols=n // tn),
        grid=(m // tm, n // tn),
        in_specs=in_specs + [pl.BlockSpec((1, tn), lambda i, j: (0, j))],
        out_specs=[tile, tile, pl.BlockSpec((tm, LANES), lambda i, j: (i, 0))],
        out_shape=[jax.ShapeDtypeStruct((m, n), F32), jax.ShapeDtypeStruct((m, n), BF16),
                   jax.ShapeDtypeStruct((m, LANES), F32)],
        scratch_shapes=[pltpu.VMEM((tm, LANES), F32)],
        compiler_params=_cparams("parallel", "arbitrary"),
        name="matmul_residual_norm",
    )(a, w, res, next_gain.reshape(1, n))


def _rmsnorm_kernel(x_ref, g_ref, o_ref):
    x = x_ref[...]
    ms = jnp.mean(x * x, axis=-1, keepdims=True)
    o_ref[...] = x * lax.rsqrt(ms + EPS) * g_ref[...]


def rmsnorm_rows(x, g, *, tm):
    m, k = x.shape
    return pl.pallas_call(
        _rmsnorm_kernel,
        grid=(m // tm,),
        in_specs=[pl.BlockSpec((tm, k), lambda i: (i, 0)), pl.BlockSpec((1, k), lambda i: (0, 0))],
        out_specs=pl.BlockSpec((tm, k), lambda i: (i, 0)),
        out_shape=jax.ShapeDtypeStruct((m, k), F32),
        compiler_params=_cparams("parallel"),
        name="final_rmsnorm",
    )(x, g.reshape(1, k))


def _t5_bucket_np(dist):
    max_exact = N_BUCKETS // 2
    n = np.maximum(dist, 1).astype(np.float64)
    large = max_exact + (np.log(n / max_exact) / math.log(MAX_DISTANCE / max_exact)
                         * (N_BUCKETS - max_exact)).astype(np.int32)
    large = np.minimum(large, N_BUCKETS - 1)
    return np.where(dist < max_exact, dist, large).astype(np.int32)


def _dilated_bucket_table():
    qi = np.arange(A_BLOCK)[:, None]
    kj = np.arange(2 * A_BLOCK)[None, :]
    steps_back = qi + A_BLOCK - kj
    valid = (steps_back >= 0) & (steps_back <= A_BLOCK)
    tabs = []
    for dil in A_DILATIONS:
        bucket = _t5_bucket_np(np.clip(steps_back, 0, A_BLOCK) * dil)
        tabs.append(np.where(valid, bucket, -1))
    return np.stack(tabs).astype(np.int32)


def _bias_table_kernel(idx_ref, rb_ref, o_ref):
    h = pl.program_id(1)
    idx = idx_ref[...]
    acc = jnp.full(idx.shape, NEG, F32)
    for b in range(N_BUCKETS):
        acc = jnp.where(idx == b, rb_ref[b, h] * LOG2E, acc)
    o_ref[...] = acc


def dilated_bias_table(rel_bias):
    idx = jnp.asarray(_dilated_bucket_table())
    npat = len(A_DILATIONS)
    return pl.pallas_call(
        _bias_table_kernel,
        grid=(npat, A_HEADS),
        in_specs=[
            pl.BlockSpec((None, A_BLOCK, 2 * A_BLOCK), lambda p, h: (p, 0, 0)),
            pl.BlockSpec(memory_space=pltpu.SMEM),
        ],
        out_specs=pl.BlockSpec((None, None, A_BLOCK, 2 * A_BLOCK), lambda p, h: (p, h, 0, 0)),
        out_shape=jax.ShapeDtypeStruct((npat, A_HEADS, A_BLOCK, 2 * A_BLOCK), F32),
        compiler_params=_cparams("parallel", "parallel"),
        name="dilated_bias_table",
    )(idx, rel_bias)


def _dilated_kernel(q_ref, kc_ref, kp_ref, vc_ref, vp_ref, bias_ref, o_ref, bm0, o_scr, lse_scr, *, scale):
    t = pl.program_id(2)
    tile = q_ref.shape[0]
    blk = A_BLOCK
    col = lax.broadcasted_iota(jnp.int32, (blk, 2 * blk), 1)
    no_prev = jnp.logical_and(col < blk, t == 0)
    for p in range(len(A_DILATIONS)):
        bm0[p] = jnp.where(no_prev, NEG, bias_ref[p])

    def rows(start, size, dil):
        return pl.ds(start, size) if dil == 1 else pl.ds(start, size, stride=dil)

    def block(p, dil, r, n):
        q_idx = rows(r + dil * blk * n, blk, dil)
        if n == 0:
            prev = rows(tile - dil * blk + r, blk, dil)
            cur = rows(r, blk, dil)
            k = jnp.concatenate([kp_ref[prev, :], kc_ref[cur, :]], axis=0)
            v = jnp.concatenate([vp_ref[prev, :], vc_ref[cur, :]], axis=0)
            bm = bm0[p]
        else:
            k_idx = rows(r + dil * blk * (n - 1), 2 * blk, dil)
            k, v = kc_ref[k_idx, :], vc_ref[k_idx, :]
            bm = bias_ref[p]
        q = (q_ref[q_idx, :] * scale).astype(BF16)
        s = lax.dot_general(q, k.astype(BF16), (((1,), (1,)), ((), ())), preferred_element_type=F32)
        s = jnp.where(bm > 0.5 * NEG, s + bm, NEG)
        m = jnp.max(s, axis=-1, keepdims=True)
        e = jnp.exp2(s - m)
        l = jnp.sum(e, axis=-1, keepdims=True)
        o = jnp.dot(e.astype(BF16), v.astype(BF16), preferred_element_type=F32) / l
        lse = m + jnp.log(l) * LOG2E
        o_scr[p, q_idx, :] = o
        lse_scr[p, q_idx, :] = jnp.broadcast_to(lse, (blk, HEAD_DIM))

    for p, dil in enumerate(A_DILATIONS):
        for r in range(dil):
            for n in range(tile // (blk * dil)):
                block(p, dil, r, n)

    def merge(i, carry):
        r = pl.multiple_of(i * blk, blk)
        rows = pl.ds(r, blk)
        l0, l1, l2 = lse_scr[0, rows, :], lse_scr[1, rows, :], lse_scr[2, rows, :]
        mx = jnp.maximum(jnp.maximum(l0, l1), l2)
        w0, w1, w2 = jnp.exp2(l0 - mx), jnp.exp2(l1 - mx), jnp.exp2(l2 - mx)
        num = w0 * o_scr[0, rows, :] + w1 * o_scr[1, rows, :] + w2 * o_scr[2, rows, :]
        o_ref[rows, :] = (num / (w0 + w1 + w2)).astype(o_ref.dtype)
        return carry

    lax.fori_loop(0, tile // blk, merge, 0)


def dilated_attention(z, bias_tab, *, batch, seq):
    z3 = z.reshape(batch, seq, z.shape[1])
    nt = seq // A_TILE
    hq, hk, hv = Z_AQ // HEAD_DIM, Z_AK // HEAD_DIM, Z_AV // HEAD_DIM
    tile_spec = lambda col0, prev: pl.BlockSpec(
        (None, A_TILE, HEAD_DIM),
        (lambda b, h, t: (b, jnp.maximum(t - 1, 0), col0 + h)) if prev else (lambda b, h, t: (b, t, col0 + h)))
    npat = len(A_DILATIONS)
    out = pl.pallas_call(
        functools.partial(_dilated_kernel, scale=HEAD_DIM ** -0.5 * LOG2E),
        grid=(batch, A_HEADS, nt),
        in_specs=[
            tile_spec(hq, False),
            tile_spec(hk, False), tile_spec(hk, True),
            tile_spec(hv, False), tile_spec(hv, True),
            pl.BlockSpec((npat, None, A_BLOCK, 2 * A_BLOCK), lambda b, h, t: (0, h, 0, 0)),
        ],
        out_specs=pl.BlockSpec((None, A_TILE, HEAD_DIM), lambda b, h, t: (b, t, h)),
        out_shape=jax.ShapeDtypeStruct((batch, seq, A_HEADS * HEAD_DIM), BF16),
        scratch_shapes=[
            pltpu.VMEM((npat, A_BLOCK, 2 * A_BLOCK), F32),
            pltpu.VMEM((npat, A_TILE, HEAD_DIM), F32),
            pltpu.VMEM((npat, A_TILE, HEAD_DIM), F32),
        ],
        compiler_params=_cparams("parallel", "parallel", "arbitrary"),
        name="dilated_attention",
    )(z3, z3, z3, z3, z3, bias_tab)
    return out.reshape(batch * seq, A_HEADS * HEAD_DIM)


def _split_bf16(a):
    hi = a.astype(BF16)
    return hi, (a - hi.astype(F32)).astype(BF16)


def _gla_kernel(q_ref, k_ref, v_ref, r_ref, lr_ref, wa_ref, ba_ref, gn_ref, o_ref, st_ref, *, scale):
    @pl.when(pl.program_id(1) == 0)
    def _():
        st_ref[...] = jnp.zeros_like(st_ref)

    c = B_CHUNK
    tc = q_ref.shape[0]
    dk = B_DK // B_HEADS
    dv = B_DV // B_HEADS
    dot = functools.partial(jnp.dot, preferred_element_type=F32)
    lr_hi, lr_lo = _split_bf16(lr_ref[...])
    wa_hi, wa_lo = _split_bf16(wa_ref[...])
    pre = dot(lr_hi, wa_hi) + dot(lr_hi, wa_lo) + dot(lr_lo, wa_hi) + ba_ref[...]
    log_a = (jnp.minimum(pre, 0.0) - jnp.log(1.0 + jnp.exp(-jnp.abs(pre)))) * (1.0 / B_GATE_TAU)
    tril = lax.broadcasted_iota(jnp.int32, (c, c), 0) >= lax.broadcasted_iota(jnp.int32, (c, c), 1)
    ones_tril = jnp.where(tril, 1.0, 0.0).astype(BF16)
    for ci in range(tc // c):
        sl = slice(ci * c, (ci + 1) * c)
        la_hi, la_lo = _split_bf16(log_a[sl, :])
        cum_all = dot(ones_tril, la_hi) + dot(ones_tril, la_lo)
        for h in range(B_HEADS):
            kcols = slice(h * dk, (h + 1) * dk)
            vcols = slice(h * dv, (h + 1) * dv)
            cum = cum_all[:, kcols]
            cum_last = cum[c - 1:c, :]
            q = q_ref[sl, kcols] * scale
            k = k_ref[sl, kcols]
            v = v_ref[sl, vcols]
            q_dec = (q * jnp.exp(cum)).astype(BF16)
            k_inv = (k * jnp.exp(-cum)).astype(BF16)
            k_state = (k * jnp.exp(cum_last - cum)).astype(BF16)
            att = lax.dot_general(q_dec, k_inv, (((1,), (1,)), ((), ())), preferred_element_type=F32)
            att = jnp.where(tril, att, 0.0).astype(BF16)
            st = st_ref[h]
            o = (jnp.dot(att, v.astype(BF16), preferred_element_type=F32)
                 + lax.dot_general(q_dec, st.astype(BF16), (((1,), (1,)), ((), ())), preferred_element_type=F32))
            st_ref[h] = st * jnp.exp(cum_last) + jnp.dot(v.T.astype(BF16), k_state, preferred_element_type=F32)
            ms = jnp.mean(o * o, axis=-1, keepdims=True)
            o = o * lax.rsqrt(ms + EPS) * gn_ref[:, vcols]
            r = r_ref[sl, vcols]
            o_ref[sl, vcols] = (o * (r * jax.nn.sigmoid(r))).astype(o_ref.dtype)


def gated_linear_attention(z, w_alpha_p, b_alpha, gla_norm, *, batch, seq, tc):
    dk = B_DK // B_HEADS
    dv = B_DV // B_HEADS
    nt = seq // tc
    row = lambda b, t: b * nt + t
    const = lambda b, t: (0, 0)
    return pl.pallas_call(
        functools.partial(_gla_kernel, scale=dk ** -0.5),
        grid=(batch, nt),
        in_specs=[
            pl.BlockSpec((tc, B_DK), lambda b, t: (row(b, t), Z_BQ // B_DK)),
            pl.BlockSpec((tc, B_DK), lambda b, t: (row(b, t), Z_BK // B_DK)),
            pl.BlockSpec((tc, B_DV), lambda b, t: (row(b, t), Z_BV // B_DV)),
            pl.BlockSpec((tc, B_DV), lambda b, t: (row(b, t), Z_BR // B_DV)),
            pl.BlockSpec((tc, LANES), lambda b, t: (row(b, t), Z_LR // LANES)),
            pl.BlockSpec((LANES, B_DK), const),
            pl.BlockSpec((1, B_DK), const),
            pl.BlockSpec((1, B_DV), const),
        ],
        out_specs=pl.BlockSpec((tc, B_DV), lambda b, t: (row(b, t), 0)),
        out_shape=jax.ShapeDtypeStruct((batch * seq, B_DV), BF16),
        scratch_shapes=[pltpu.VMEM((B_HEADS, dv, dk), F32)],
        compiler_params=_cparams("parallel", "arbitrary"),
        name="gated_linear_attention",
    )(z, z, z, z, z, w_alpha_p, b_alpha.reshape(1, B_DK), gla_norm.reshape(1, B_DV))


C_QW = 2 * C_NOPE


def _rope_tables(seq):
    pos = jnp.arange(seq, dtype=F32)
    inv = ROPE_THETA ** (-jnp.arange(0, C_ROPE, 2, dtype=F32) / C_ROPE)
    ang = pos[:, None] * inv[None, :]
    cos, sin = jnp.cos(ang), jnp.sin(ang)
    half = C_ROPE // 2
    z = lambda w: jnp.zeros((seq, w), F32)
    return jnp.stack([
        jnp.concatenate([cos, cos, z(LANES - C_ROPE)], axis=1),
        jnp.concatenate([-sin, z(half), z(LANES - C_ROPE)], axis=1),
        jnp.concatenate([z(half), sin, z(LANES - C_ROPE)], axis=1)])


def _mla_proj_kernel(cq_ref, ckv_ref, ckr_ref, gq_ref, gkv_ref, wq_ref, wk_ref, wv_ref, rt_ref,
                     q_out, kn_out, kr_out, v_out, *, scale):
    half = C_ROPE // 2
    t0, t1, t2 = rt_ref[0], rt_ref[1], rt_ref[2]

    def rms(x, g):
        ms = jnp.mean(x * x, axis=-1, keepdims=True)
        return (x * lax.rsqrt(ms + EPS) * g).astype(BF16)

    def rotary(x):
        return x * t0 + pltpu.roll(x, LANES - half, axis=1) * t1 + pltpu.roll(x, half, axis=1) * t2

    cq = rms(cq_ref[...], gq_ref[...])
    q = jnp.dot(cq, wq_ref[...], preferred_element_type=F32)
    for h in range(C_HEADS):
        nope = slice(h * C_QW, h * C_QW + C_NOPE)
        rope = slice(h * C_QW + C_NOPE, (h + 1) * C_QW)
        q_out[:, nope] = (q[:, nope] * scale).astype(q_out.dtype)
        q_out[:, rope] = (rotary(q[:, rope]) * scale).astype(q_out.dtype)
    ckv = rms(ckv_ref[...], gkv_ref[...])
    kn_out[...] = jnp.dot(ckv, wk_ref[...], preferred_element_type=F32).astype(kn_out.dtype)
    v_out[...] = jnp.dot(ckv, wv_ref[...], preferred_element_type=F32).astype(v_out.dtype)
    kr_out[...] = rotary(ckr_ref[...]).astype(kr_out.dtype)


def mla_project(z, q_a_norm, kv_a_norm, wq_p, wk_p, wv_p, rope_tab, *, seq, tm):
    m = z.shape[0]
    nt = seq // tm
    const = lambda i: (0, 0)
    return pl.pallas_call(
        functools.partial(_mla_proj_kernel, scale=(C_NOPE + C_ROPE) ** -0.5 * math.log2(math.e)),
        grid=(m // tm,),
        in_specs=[
            pl.BlockSpec((tm, C_Q_RANK), lambda i: (i, Z_CQA // C_Q_RANK)),
            pl.BlockSpec((tm, C_KV_RANK), lambda i: (i, Z_CKVA // C_KV_RANK)),
            pl.BlockSpec((tm, LANES), lambda i: (i, Z_CKR // LANES)),
            pl.BlockSpec((1, C_Q_RANK), const),
            pl.BlockSpec((1, C_KV_RANK), const),
            pl.BlockSpec((C_Q_RANK, C_HEADS * C_QW), const),
            pl.BlockSpec((C_KV_RANK, C_HEADS * C_NOPE), const),
            pl.BlockSpec((C_KV_RANK, C_HEADS * C_V), const),
            pl.BlockSpec((3, tm, LANES), lambda i: (0, i % nt, 0)),
        ],
        out_specs=[
            pl.BlockSpec((tm, C_HEADS * C_QW), lambda i: (i, 0)),
            pl.BlockSpec((tm, C_HEADS * C_NOPE), lambda i: (i, 0)),
            pl.BlockSpec((tm, LANES), lambda i: (i, 0)),
            pl.BlockSpec((tm, C_HEADS * C_V), lambda i: (i, 0)),
        ],
        out_shape=[
            jax.ShapeDtypeStruct((m, C_HEADS * C_QW), BF16),
            jax.ShapeDtypeStruct((m, C_HEADS * C_NOPE), BF16),
            jax.ShapeDtypeStruct((m, LANES), BF16),
            jax.ShapeDtypeStruct((m, C_HEADS * C_V), BF16),
        ],
        compiler_params=_cparams("parallel"),
        name="mla_project",
    )(z, z, z, q_a_norm.reshape(1, -1), kv_a_norm.reshape(1, -1), wq_p, wk_p, wv_p, rope_tab)


def _mla_flash_kernel(q_ref, kn_ref, kr_ref, v_ref, o_ref, m_ref, l_ref, acc_ref):
    tq = m_ref.shape[0]
    tk = tq // 2

    def q_tile(qi, carry):
        q0 = pl.multiple_of(qi * tq, tq)
        m_ref[...] = jnp.full_like(m_ref, NEG)
        l_ref[...] = jnp.zeros_like(l_ref)
        acc_ref[...] = jnp.zeros_like(acc_ref)

        def step(r0, nr, k_start, nk, mask_offset=None):
            rows = slice(r0, r0 + nr)
            ks = pl.ds(pl.multiple_of(k_start, tk), nk)
            k = jnp.concatenate([kn_ref[ks, :], kr_ref[ks, :]], axis=1)
            q = q_ref[pl.ds(pl.multiple_of(q0 + r0, tk), nr), :]
            s = lax.dot_general(q, k, (((1,), (1,)), ((), ())), preferred_element_type=F32)
            if mask_offset is not None:
                visible = (lax.broadcasted_iota(jnp.int32, s.shape, 1)
                           <= lax.broadcasted_iota(jnp.int32, s.shape, 0) + mask_offset)
                s = jnp.where(visible, s, NEG)
            m_prev = m_ref[rows, :]
            m_new = jnp.maximum(m_prev, jnp.max(s, axis=-1, keepdims=True))
            alpha = jnp.exp2(m_prev - m_new)
            p = jnp.exp2(s - jnp.concatenate([m_new] * (nk // LANES), axis=1))
            l_ref[rows, :] = alpha * l_ref[rows, :] + jnp.sum(p, axis=-1, keepdims=True)
            acc_ref[rows, :] = alpha * acc_ref[rows, :] + jnp.dot(p.astype(BF16), v_ref[ks, :],
                                                                  preferred_element_type=F32)
            m_ref[rows, :] = m_new

        def body(kb, c):
            for u in range(4):
                step(0, tq, (4 * kb + u) * tk, tk)
            return c

        lax.fori_loop(0, qi // 2, body, 0)

        @pl.when(qi % 2 == 1)
        def _():
            step(0, tq, (2 * qi - 2) * tk, tk)
            step(0, tq, (2 * qi - 1) * tk, tk)
        step(0, tk, q0, tk, 0)
        step(tk, tk, q0, tq, tk)
        o_ref[pl.ds(q0, tq), :] = (acc_ref[...] / l_ref[...]).astype(o_ref.dtype)
        return carry

    lax.fori_loop(0, q_ref.shape[0] // tq, q_tile, 0)


def mla_flash(q, kn, kr, v, *, batch, seq, tq):
    return pl.pallas_call(
        _mla_flash_kernel,
        grid=(batch, C_HEADS),
        in_specs=[
            pl.BlockSpec((seq, C_QW), lambda b, h: (b, h)),
            pl.BlockSpec((seq, C_NOPE), lambda b, h: (b, h)),
            pl.BlockSpec((seq, LANES), lambda b, h: (b, 0)),
            pl.BlockSpec((seq, C_V), lambda b, h: (b, h)),
        ],
        out_specs=pl.BlockSpec((seq, C_V), lambda b, h: (b, h)),
        out_shape=jax.ShapeDtypeStruct((batch * seq, C_HEADS * C_V), BF16),
        scratch_shapes=[
            pltpu.VMEM((tq, LANES), F32),
            pltpu.VMEM((tq, LANES), F32),
            pltpu.VMEM((tq, C_V), F32),
        ],
        compiler_params=_cparams("parallel", "parallel"),
        name="mla_flash",
    )(q, kn, kr, v)


def _mixer_out_kernel(oa_ref, ob_ref, oc_ref, wa_ref, wb_ref, wc_ref, g0_ref, g1_ref, g2_ref,
                      b0_ref, b1_ref, b2_ref, wo_ref, xr_ref, gn_ref, o_ref, xg_ref, inv_ref, mg_ref, ss_ref,
                      *, n_rows):
    i = pl.program_id(0)
    j = pl.program_id(1)
    _, nj, _, tn = mg_ref.shape
    cur = lax.rem(i, 2)

    def project():
        acc = xr_ref[...]
        for c in range(nj):
            acc = acc + jnp.dot(mg_ref[1 - cur, c], wo_ref[c * tn:(c + 1) * tn, :], preferred_element_type=F32)
        o_ref[...] = acc
        _next_norm_tile(acc, gn_ref, xg_ref, ss_ref)

    def branch(o, w, g, b):
        return jax.nn.sigmoid(g[...] + b[...]) * jnp.dot(o[...], w[...], preferred_element_type=F32)

    def merge():
        acc = branch(oa_ref, wa_ref, g0_ref, b0_ref)
        acc = acc + branch(ob_ref, wb_ref, g1_ref, b1_ref)
        acc = acc + branch(oc_ref, wc_ref, g2_ref, b2_ref)
        mg_ref[cur, j] = acc.astype(mg_ref.dtype)

    @pl.when(j == 0)
    def _():
        ss_ref[...] = jnp.zeros_like(ss_ref)

    @pl.when(jnp.logical_and(i > 0, i < n_rows))
    def _():
        project()
        merge()

    @pl.when(i == 0)
    def _():
        merge()

    @pl.when(i == n_rows)
    def _():
        project()

    @pl.when(jnp.logical_and(i > 0, j == nj - 1))
    def _():
        _next_norm_finish(inv_ref, ss_ref, nj * tn)


def _pipelined_row_maps(n_rows, ncols):
    row1 = lambda i: jnp.minimum(i, n_rows - 1)
    col1 = lambda i, j: jnp.where(i < n_rows, j, ncols - 1)
    row2 = lambda i: jnp.maximum(i - 1, 0)
    col2 = lambda i, j: jnp.where(i > 0, j, 0)
    return row1, col1, row2, col2


def mixer_output_block(x, o_a, o_b, o_c, w_a, w_b, w_c, gates, b_gate, w_o, next_gain, *, tm, tn):
    m, k = o_a.shape
    d = w_a.shape[1]
    nj = d // tn
    n_rows = m // tm
    mrow, mcol, orow, ocol = _pipelined_row_maps(n_rows, nj)
    act = pl.BlockSpec((tm, k), lambda i, j: (mrow(i), 0))
    wgt = pl.BlockSpec((k, tn), lambda i, j: (0, mcol(i, j)))
    gate = lambda br: pl.BlockSpec((tm, tn), lambda i, j: (mrow(i), br * nj + mcol(i, j)))
    gbias = lambda br: pl.BlockSpec((1, tn), lambda i, j: (0, br * nj + j))
    out_tile = pl.BlockSpec((tm, tn), lambda i, j: (orow(i), ocol(i, j)))
    return pl.pallas_call(
        functools.partial(_mixer_out_kernel, n_rows=n_rows),
        grid=(n_rows + 1, nj),
        in_specs=[act, act, act, wgt, wgt, wgt, gate(0), gate(1), gate(2), gbias(0), gbias(1), gbias(2),
                  pl.BlockSpec((d, tn), lambda i, j: (0, ocol(i, j))),
                  out_tile,
                  pl.BlockSpec((1, tn), lambda i, j: (0, ocol(i, j)))],
        out_specs=[out_tile, out_tile, pl.BlockSpec((tm, LANES), lambda i, j: (orow(i), 0))],
        out_shape=[jax.ShapeDtypeStruct((m, d), F32), jax.ShapeDtypeStruct((m, d), BF16),
                   jax.ShapeDtypeStruct((m, LANES), F32)],
        scratch_shapes=[pltpu.VMEM((2, nj, tm, tn), BF16), pltpu.VMEM((tm, LANES), F32)],
        compiler_params=_cparams("arbitrary", "arbitrary"),
        name="mixer_output_block",
    )(o_a, o_b, o_c, w_a, w_b, w_c, gates, gates, gates, *([b_gate.reshape(1, -1)] * 3), w_o, x,
      next_gain.reshape(1, d))


def _cross_block_kernel(xg_ref, inv_ref, wq_ref, k_ref, v_ref, wo_ref, xr_ref, gn_ref,
                        o_ref, xgo_ref, invo_ref, ao_ref, ss_ref, *, scale, n_rows):
    i = pl.program_id(0)
    j = pl.program_id(1)
    hd = k_ref.shape[1]
    cur = lax.rem(i, 2)

    def project():
        acc = xr_ref[...]
        for h in range(X_HEADS):
            acc = acc + jnp.dot(ao_ref[1 - cur, h], wo_ref[h * hd:(h + 1) * hd, :], preferred_element_type=F32)
        o_ref[...] = acc
        _next_norm_tile(acc, gn_ref, xgo_ref, ss_ref)

    def attend():
        q = jnp.dot(xg_ref[...], wq_ref[...], preferred_element_type=F32) * (_inv_cols(inv_ref, hd) * scale)
        s = lax.dot_general(q.astype(BF16), k_ref[...], (((1,), (1,)), ((), ())), preferred_element_type=F32)
        m = jnp.max(s, axis=-1, keepdims=True)
        e = jnp.exp(s - m)
        l = jnp.sum(e, axis=-1, keepdims=True)
        o = jnp.dot(e.astype(BF16), v_ref[...], preferred_element_type=F32) / l
        ao_ref[cur, j] = o.astype(ao_ref.dtype)

    @pl.when(j == 0)
    def _():
        ss_ref[...] = jnp.zeros_like(ss_ref)

    @pl.when(jnp.logical_and(i > 0, i < n_rows))
    def _():
        project()
        attend()

    @pl.when(i == 0)
    def _():
        attend()

    @pl.when(i == n_rows)
    def _():
        project()

    @pl.when(jnp.logical_and(i > 0, j == X_HEADS - 1))
    def _():
        _next_norm_finish(invo_ref, ss_ref, X_HEADS * hd)


def cross_attention_block(x, xg, inv, w_q, kv, w_o, next_gain, *, batch, seq, mem_len, tm):
    m, d = x.shape
    hd = d // X_HEADS
    tiles_per_batch = seq // tm
    n_rows = m // tm
    arow, head, orow, ocol = _pipelined_row_maps(n_rows, X_HEADS)
    out_tile = pl.BlockSpec((tm, hd), lambda i, j: (orow(i), ocol(i, j)))
    return pl.pallas_call(
        functools.partial(_cross_block_kernel, scale=hd ** -0.5, n_rows=n_rows),
        grid=(n_rows + 1, X_HEADS),
        in_specs=[
            pl.BlockSpec((tm, d), lambda i, j: (arow(i), 0)),
            pl.BlockSpec((tm, LANES), lambda i, j: (arow(i), 0)),
            pl.BlockSpec((d, hd), lambda i, j: (0, head(i, j))),
            pl.BlockSpec((mem_len, hd), lambda i, j: (arow(i) // tiles_per_batch, head(i, j))),
            pl.BlockSpec((mem_len, hd), lambda i, j: (arow(i) // tiles_per_batch, X_HEADS + head(i, j))),
            pl.BlockSpec((d, hd), lambda i, j: (0, ocol(i, j))),
            out_tile,
            pl.BlockSpec((1, hd), lambda i, j: (0, ocol(i, j))),
        ],
        out_specs=[out_tile, out_tile, pl.BlockSpec((tm, LANES), lambda i, j: (orow(i), 0))],
        out_shape=[jax.ShapeDtypeStruct((m, d), F32), jax.ShapeDtypeStruct((m, d), BF16),
                   jax.ShapeDtypeStruct((m, LANES), F32)],
        scratch_shapes=[pltpu.VMEM((2, X_HEADS, tm, hd), BF16), pltpu.VMEM((tm, LANES), F32)],
        compiler_params=_cparams("arbitrary", "arbitrary"),
        name="cross_attention_block",
    )(xg, inv, w_q, kv, kv, w_o, x, next_gain.reshape(1, d))


def _pack_w_in(w):
    nl, d, _ = w.shape
    sizes = (1024, 1024, 1024, B_DK, B_DK, B_DV, B_GATE_RANK, B_DV, C_Q_RANK, C_KV_RANK, C_ROPE, N_BRANCH * d)
    offs = np.concatenate([[0], np.cumsum(sizes)])
    aq, ak, av, bq, bk, bv, lr, br, cqa, ckva, ckr, gates = [
        w[:, :, offs[i]:offs[i + 1]].astype(BF16) for i in range(len(sizes))]
    zeros = lambda n: jnp.zeros((nl, d, n), BF16)
    used = Z_CKR + LANES
    wz = jnp.concatenate(
        [aq, ak, av, bv, br, cqa, ckva, bq, bk,
         lr, zeros(LANES - B_GATE_RANK), ckr, zeros(LANES - C_ROPE), zeros(Z_WIDTH - used)], axis=2)
    return wz, gates


def _pack_w_qb(w):
    r = w.shape[0]
    w = w.reshape(r, C_HEADS, C_NOPE + C_ROPE)
    w = jnp.pad(w, ((0, 0), (0, 0), (0, C_QW - C_NOPE - C_ROPE)))
    return w.reshape(r, C_HEADS * C_QW).astype(BF16)


def _pack_w_kvb(w):
    r = w.shape[0]
    w = w.reshape(r, C_HEADS, C_NOPE + C_V)
    wk = w[:, :, :C_NOPE].reshape(r, C_HEADS * C_NOPE)
    wv = w[:, :, C_NOPE:].reshape(r, C_HEADS * C_V)
    return wk.astype(BF16), wv.astype(BF16)


def kernel(x, mem, rel_bias, norm_mix, w_in, b_gate, w_alpha, b_alpha, gla_norm, q_a_norm, w_qb, kv_a_norm, w_kvb, w_up_a, w_up_b, w_up_c, w_o, norm_x, norm_mem, w_xq, w_xkv, w_xo, norm_ffn, w_ffn_gate, w_ffn_up, w_ffn_down, norm_final):
    batch, seq, d = x.shape
    mem_len = mem.shape[1]
    depth = w_in.shape[0]
    t = batch * seq
    assert (batch, seq, d, mem_len) == (4, 4096, 2048, 256), "tile sizes are derived for the stated shapes"
    assert d // X_HEADS == TN and t % SWIGLU_TM == 0 and seq % A_TILE == 0

    xf = x.reshape(t, d)
    memf = mem.reshape(batch * mem_len, d)
    bias_tab = dilated_bias_table(rel_bias)
    rope_tab = _rope_tables(seq)
    bf = lambda a: a.astype(BF16)
    w_in_z, w_in_g = _pack_w_in(w_in)

    xg, inv = prenorm(xf, norm_mix[0], tm=TM)
    for l in range(depth):
        z = scaled_matmul(xg, inv, w_in_z, out_dtype=F32, tm=W_TM, tn=W_Z_TN, layer=l)
        gates = scaled_matmul(xg, inv, w_in_g, out_dtype=BF16, tm=W_TM, tn=W_G_TN, layer=l)
        o_a = dilated_attention(z, bias_tab, batch=batch, seq=seq)
        w_alpha_p = jnp.pad(w_alpha[l], ((0, LANES - B_GATE_RANK), (0, 0)))
        o_b = gated_linear_attention(z, w_alpha_p, b_alpha[l], gla_norm[l], batch=batch, seq=seq, tc=GLA_TC)
        wk_p, wv_p = _pack_w_kvb(w_kvb[l])
        cq, ckn, ckr, cv = mla_project(z, q_a_norm[l], kv_a_norm[l], _pack_w_qb(w_qb[l]), wk_p, wv_p,
                                       rope_tab, seq=seq, tm=TM)
        o_c = mla_flash(cq, ckn, ckr, cv, batch=batch, seq=seq, tq=FLASH_TQ)
        xf, xg, inv = mixer_output_block(xf, o_a, o_b, o_c, bf(w_up_a[l]), bf(w_up_b[l]), bf(w_up_c[l]), gates,
                                         b_gate[l], bf(w_o[l]), norm_x[l], tm=TM, tn=TN)
        xkv = norm_matmul(memf, norm_mem[l], w_xkv, out_dtype=BF16, tm=batch * mem_len, tn=TN, layer=l)
        xf, xg, inv = cross_attention_block(xf, xg, inv, bf(w_xq[l]), xkv, bf(w_xo[l]), norm_ffn[l],
                                            batch=batch, seq=seq, mem_len=mem_len, tm=TM)
        act = scaled_swiglu(xg, inv, w_ffn_gate, w_ffn_up, tm=SWIGLU_TM, tn=TN, layer=l)
        if l + 1 < depth:
            xf, xg, inv = matmul_residual(act, bf(w_ffn_down[l]), xf, tm=TM, tn=TN, next_gain=norm_mix[l + 1])
        else:
            xf = matmul_residual(act, bf(w_ffn_down[l]), xf, tm=TM, tn=TN)
    return rmsnorm_rows(xf, norm_final, tm=TM).reshape(batch, seq, d)
```
